```python
import math
import jax
import jax.numpy as jnp
from jax import lax
import numpy as np

D_MODEL = 1024
BATCH = 8
SEQ = 2048
DEPTH = 4

GRID_W = 64
CTX_LEN = 256
N_MIXERS = 3
S5_GROUP = 16
S5_GROUPS = D_MODEL // S5_GROUP
S5_STATE = 64
S5_DT_MIN = 1e-3
S5_DT_MAX = 1e-1
CONV_WIDTH = 31
CONV_PAD = CONV_WIDTH // 2
RWKV_HEAD = 64
RWKV_HEADS = D_MODEL // RWKV_HEAD
DECAY_LORA = max(32, int(round(1.8 * D_MODEL ** 0.5 / 32)) * 32)
ICLR_LORA = DECAY_LORA
GATE_LORA = max(32, int(round(0.6 * D_MODEL ** 0.8 / 32)) * 32)
FFN_HIDDEN = D_MODEL * 7 // 2
N_EXPERTS = 8
TOP_K = 2
RMS_EPS = 1e-6
LN_EPS = 1e-5
GN_EPS = 64e-5
KK_EPS = 1e-12
LAMBDA_RE_MAX = -1e-4

kernel_name = 'hybrid_s5_conformer_rwkv7_moe_dit'


def _rmsnorm(x, g):
    x32 = x.astype(jnp.float32)
    y = x32 * lax.rsqrt(jnp.mean(x32 * x32, axis=-1, keepdims=True) + RMS_EPS)
    return y.astype(x.dtype) * g


def _layernorm(x, g, b):
    x32 = x.astype(jnp.float32)
    mu = jnp.mean(x32, axis=-1, keepdims=True)
    xc = x32 - mu
    y = xc * lax.rsqrt(jnp.mean(xc * xc, axis=-1, keepdims=True) + LN_EPS)
    return y.astype(x.dtype) * g + b


def _modulate(h, shift, scale):
    return h * (1 + scale) + shift


def _flip_parts(t, lc):
    return jnp.concatenate([jnp.flip(t[:, :lc], axis=1), jnp.flip(t[:, lc:], axis=1)], axis=1)


def _restore_order(t, lc, start):
    return _flip_parts(t, lc) if start == 0 else jnp.flip(t, axis=1)


def _centred_shift(t):
    zero = jnp.zeros_like(t[:, :1])
    prev = jnp.concatenate([zero, t[:, :-1]], axis=1)
    nxt = jnp.concatenate([t[:, 1:], zero], axis=1)
    return 0.5 * (prev + nxt)


def _complex_affine_combine(e1, e2):
    a1r, a1i, b1r, b1i = e1
    a2r, a2i, b2r, b2i = e2
    return (a1r * a2r - a1i * a2i, a1r * a2i + a1i * a2r,
            a2r * b1r - a2i * b1i + b2r, a2r * b1i + a2i * b1r + b2i)


def _s5_states(ug, lam_re, lam_im, log_dt, b_re, b_im):
    lam_re = jnp.minimum(lam_re.astype(jnp.float32), LAMBDA_RE_MAX)
    lam_im = lam_im.astype(jnp.float32)
    dt = jnp.exp(log_dt.astype(jnp.float32))[:, None]
    mag = jnp.exp(lam_re * dt)
    abar_re = mag * jnp.cos(lam_im * dt)
    abar_im = mag * jnp.sin(lam_im * dt)
    inv_den = 1.0 / (lam_re * lam_re + lam_im * lam_im)
    gam_re = ((abar_re - 1.0) * lam_re + abar_im * lam_im) * inv_den
    gam_im = (abar_im * lam_re - (abar_re - 1.0) * lam_im) * inv_den
    b_re = b_re.astype(jnp.float32)
    b_im = b_im.astype(jnp.float32)
    bb_re = gam_re[..., None] * b_re - gam_im[..., None] * b_im
    bb_im = gam_re[..., None] * b_im + gam_im[..., None] * b_re
    bu_re = jnp.einsum('btgp,gnp->btgn', ug, bb_re)
    bu_im = jnp.einsum('btgp,gnp->btgn', ug, bb_im)
    shape = (1, ug.shape[1]) + abar_re.shape
    elems = (jnp.broadcast_to(abar_re, shape), jnp.broadcast_to(abar_im, shape), bu_re, bu_im)
    _, _, h_re, h_im = lax.associative_scan(_complex_affine_combine, elems, axis=1)
    return h_re, h_im


def _s5_readout(h_re, h_im, c_re, c_im):
    return (jnp.einsum('btgn,gpn->btgp', h_re, c_re.astype(jnp.float32))
            - jnp.einsum('btgn,gpn->btgp', h_im, c_im.astype(jnp.float32)))


def _s5_mixer(h_ctx, h_lat, lam_re, lam_im, log_dt, b_re, b_im, c_re, c_im, d_skip, glu_w, glu_b, ctx_out):
    lc = h_ctx.shape[1]
    start = 0 if ctx_out else lc
    u = jnp.concatenate([h_ctx, h_lat], axis=1)
    bsz, t_all, d = u.shape
    ug = u.reshape(bsz, t_all, S5_GROUPS, S5_GROUP).astype(jnp.float32)
    h_re, h_im = _s5_states(ug, lam_re[0], lam_im[0], log_dt[0], b_re[0], b_im[0])
    y = _s5_readout(h_re[:, start:], h_im[:, start:], c_re[0], c_im[0])
    h_re, h_im = _s5_states(_flip_parts(ug, lc), lam_re[1], lam_im[1], log_dt[1], b_re[1], b_im[1])
    y = y + _restore_order(_s5_readout(h_re[:, start:], h_im[:, start:], c_re[1], c_im[1]), lc, start)
    y = y + d_skip.astype(jnp.float32).reshape(S5_GROUPS, S5_GROUP) * ug[:, start:]
    z = jax.nn.gelu(y.reshape(bsz, t_all - start, d)).astype(h_lat.dtype)
    zz = z @ glu_w + glu_b
    out = zz[..., :d] * jax.nn.sigmoid(zz[..., d:])
    return (out[:, :lc] if ctx_out else None), out[:, lc - start:]


def _conv_module(h, pw1_w, pw1_b, dw_w, dw_b, ln_g, ln_b, pw2_w, pw2_b):
    d = h.shape[-1]
    u = h @ pw1_w + pw1_b
    u = u[..., :d] * jax.nn.sigmoid(u[..., d:])
    u = lax.conv_general_dilated(u, dw_w[:, None, :], window_strides=(1,),
                                 padding=((CONV_PAD, CONV_PAD),),
                                 dimension_numbers=('NWC', 'WIO', 'NWC'),
                                 feature_group_count=d) + dw_b
    u = jax.nn.silu(_layernorm(u, ln_g, ln_b))
    return u @ pw2_w + pw2_b


def _conformer_mixer(h_ctx, h_lat, rows, pw1_w, pw1_b, dw_w, dw_b, ln_g, ln_b, pw2_w, pw2_b, ctx_out):
    bsz, seq, d = h_lat.shape
    grid = h_lat.reshape(bsz * rows, GRID_W, d)
    y_lat = _conv_module(grid, pw1_w, pw1_b, dw_w, dw_b, ln_g, ln_b, pw2_w, pw2_b).reshape(bsz, seq, d)
    y_ctx = _conv_module(h_ctx, pw1_w, pw1_b, dw_w, dw_b, ln_g, ln_b, pw2_w, pw2_b) if ctx_out else None
    return y_ctx, y_lat


def _wkv7_scan(r, w, k, v, a, b):
    bsz, _, nh, n = r.shape

    def step(s, inp):
        r_t, w_t, k_t, v_t, a_t, b_t = inp
        sa = jnp.einsum('bhij,bhj->bhi', s, a_t)
        s = s * w_t[:, :, None, :] + sa[..., None] * b_t[:, :, None, :] + v_t[..., None] * k_t[:, :, None, :]
        return s, jnp.einsum('bhij,bhj->bhi', s, r_t)

    xs = tuple(jnp.moveaxis(z.astype(jnp.float32), 1, 0) for z in (r, w, k, v, a, b))
    s0 = jnp.zeros((bsz, nh, n, n), jnp.float32)
    _, y = lax.scan(step, s0, xs)
    return jnp.moveaxis(y, 0, 1)


def _rwkv7_mixer(h_ctx, h_lat, mu, w_r, w_k, w_v, w_o, w0, w1, w2, a0, a1, a2, g1, g2,
                 k_k, k_a, r_k, lnx_g, lnx_b, ctx_out):
    lc = h_ctx.shape[1]
    start = 0 if ctx_out else lc
    hcat = jnp.concatenate([h_ctx, h_lat], axis=1)
    bsz, t_all, d = hcat.shape
    t_out = t_all - start
    xx = jnp.concatenate([_centred_shift(h_ctx), _centred_shift(h_lat)], axis=1) - hcat
    xr, xw, xk, xv, xa, xg = (hcat + xx * mu[j] for j in range(6))

    def heads(z):
        return z.reshape(z.shape[0], z.shape[1], RWKV_HEADS, RWKV_HEAD)

    r = heads(xr @ w_r)
    k = heads(xk @ w_k)
    v = heads(xv @ w_v)
    kk = (k * k_k.reshape(RWKV_HEADS, RWKV_HEAD)).astype(jnp.float32)
    kk = kk * lax.rsqrt(jnp.sum(kk * kk, axis=-1, keepdims=True) + KK_EPS)
    k_a_h = k_a.reshape(RWKV_HEADS, RWKV_HEAD)
    ys = []
    keys = []
    for direction in range(2):
        log_w = -jax.nn.softplus(-(w0[direction] + jnp.tanh(xw @ w1[direction]) @ w2[direction])) - 0.5
        decay = heads(jnp.exp(-jnp.exp(log_w.astype(jnp.float32))))
        a = heads(jax.nn.sigmoid(a0[direction] + (xa @ a1[direction]) @ a2[direction]))
        k_dir = k * (1 + (a - 1) * k_a_h)
        seq = (r, decay, k_dir, v, -kk, kk * a)
        if direction == 1:
            seq = tuple(_flip_parts(z, lc) for z in seq)
        y_dir = _wkv7_scan(*seq)[:, start:]
        ys.append(_restore_order(y_dir, lc, start) if direction == 1 else y_dir)
        keys.append(k_dir[:, start:])
    y = ys[0] + ys[1]
    mean = jnp.mean(y, axis=-1, keepdims=True)
    yc = y - mean
    y = (yc * lax.rsqrt(jnp.mean(yc * yc, axis=-1, keepdims=True) + GN_EPS)).reshape(bsz, t_out, d)
    y = y.astype(hcat.dtype) * lnx_g + lnx_b
    k_mean = 0.5 * (keys[0] + keys[1])
    bonus = jnp.sum(r[:, start:] * k_mean * r_k, axis=-1, keepdims=True) * v[:, start:]
    g = jax.nn.sigmoid(xg[:, start:] @ g1) @ g2
    out = ((y + bonus.reshape(bsz, t_out, d)) * g) @ w_o
    return (out[:, :lc] if ctx_out else None), out[:, lc - start:]


def _swiglu(h, w1, w3, w2):
    return (jax.nn.silu(h @ w1) * (h @ w3)) @ w2


def _moe_swiglu(h, router, w1, w3, w2):
    logits = jnp.einsum('...d,de->...e', h, router).astype(jnp.float32)
    top_logits, top_idx = lax.top_k(logits, TOP_K)
    top_p = jax.nn.softmax(top_logits, axis=-1)
    gates = jnp.einsum('...k,...ke->...e', top_p,
                       jax.nn.one_hot(top_idx, N_EXPERTS, dtype=jnp.float32)).astype(h.dtype)
    out = gates[..., 0:1] * _swiglu(h, w1[0], w3[0], w2[0])
    for e in range(1, N_EXPERTS):
        out = out + gates[..., e:e + 1] * _swiglu(h, w1[e], w3[e], w2[e])
    return out


def setup_inputs(seed: int = 0) -> dict:
    key = jax.random.key(seed)
    ks = iter(jax.random.split(key, 64))
    f32 = jnp.float32

    def nrm(shape, scale):
        return scale * jax.random.normal(next(ks), shape, f32)

    def unif(shape, lo, hi):
        return jax.random.uniform(next(ks), shape, f32, lo, hi)

    d, f, e = D_MODEL, FFN_HIDDEN, N_EXPERTS
    g, p, n = S5_GROUPS, S5_GROUP, S5_STATE
    n_a = (DEPTH + 2) // 3
    n_b = (DEPTH + 1) // 3
    n_c = DEPTH // 3
    n_dense = (DEPTH + 1) // 2
    n_moe = DEPTH // 2
    decay_speed = -7.0 + 5.0 * (jnp.arange(d, dtype=f32) / (d - 1)) ** 0.85
    return {
        'x': nrm((BATCH, SEQ, d), 1.0),
        'c': nrm((BATCH, d), 1.0),
        'ctx': nrm((BATCH, CTX_LEN, d), 1.0),
        'c_ctx': nrm((d,), 1.0),
        'ada_w': nrm((DEPTH, d, 6 * d), 0.5 * d ** -0.5),
        'ada_b': nrm((DEPTH, 6 * d), 0.02),
        'norm_mix_g': 1.0 + nrm((DEPTH, d), 0.02),
        'norm_ffn_g': 1.0 + nrm((DEPTH, d), 0.02),
        'final_g': 1.0 + nrm((d,), 0.02),
        's5_lambda_re': -0.5 * (1.0 + nrm((n_a, 2, g, n), 0.02)),
        's5_lambda_im': math.pi * jnp.arange(n, dtype=f32) + nrm((n_a, 2, g, n), 0.02),
        's5_log_dt': unif((n_a, 2, g), math.log(S5_DT_MIN), math.log(S5_DT_MAX)),
        's5_b_re': nrm((n_a, 2, g, n, p), (2 * p) ** -0.5),
        's5_b_im': nrm((n_a, 2, g, n, p), (2 * p) ** -0.5),
        's5_c_re': nrm((n_a, 2, g, p, n), n ** -0.5),
        's5_c_im': nrm((n_a, 2, g, p, n), n ** -0.5),
        's5_d': nrm((n_a, d), 1.0),
        's5_glu_w': nrm((n_a, d, 2 * d), d ** -0.5),
        's5_glu_b': nrm((n_a, 2 * d), 0.02),
        'cv_pw1_w': nrm((n_b, d, 2 * d), d ** -0.5),
        'cv_pw1_b': nrm((n_b, 2 * d), 0.02),
        'cv_dw_w': nrm((n_b, CONV_WIDTH, d), CONV_WIDTH ** -0.5),
        'cv_dw_b': nrm((n_b, d), 0.02),
        'cv_ln_g': 1.0 + nrm((n_b, d), 0.02),
        'cv_ln_b': nrm((n_b, d), 0.02),
        'cv_pw2_w': nrm((n_b, d, d), d ** -0.5),
        'cv_pw2_b': nrm((n_b, d), 0.02),
        'rw_mu': unif((n_c, 6, d), 0.0, 1.0),
        'rw_w_r': nrm((n_c, d, d), d ** -0.5),
        'rw_w_k': nrm((n_c, d, d), d ** -0.5),
        'rw_w_v': nrm((n_c, d, d), d ** -0.5),
        'rw_w_o': nrm((n_c, d, d), d ** -0.5),
        'rw_w0': decay_speed + 0.5 + nrm((n_c, 2, d), 0.1),
        'rw_w1': nrm((n_c, 2, d, DECAY_LORA), d ** -0.5),
        'rw_w2': nrm((n_c, 2, DECAY_LORA, d), 0.1 * DECAY_LORA ** -0.5),
        'rw_a0': nrm((n_c, 2, d), 0.1),
        'rw_a1': nrm((n_c, 2, d, ICLR_LORA), d ** -0.5),
        'rw_a2': nrm((n_c, 2, ICLR_LORA, d), 0.1 * ICLR_LORA ** -0.5),
        'rw_g1': nrm((n_c, d, GATE_LORA), d ** -0.5),
        'rw_g2': nrm((n_c, GATE_LORA, d), GATE_LORA ** -0.5),
        'rw_k_k': 0.85 + nrm((n_c, d), 0.02),
        'rw_k_a': 1.0 + nrm((n_c, d), 0.02),
        'rw_r_k': nrm((n_c, RWKV_HEADS, RWKV_HEAD), 0.1),
        'rw_lnx_g': 1.0 + nrm((n_c, d), 0.02),
        'rw_lnx_b': nrm((n_c, d), 0.02),
        'ff_w1': nrm((n_dense, d, f), d ** -0.5),
        'ff_w3': nrm((n_dense, d, f), d ** -0.5),
        'ff_w2': nrm((n_dense, f, d), f ** -0.5),
        'moe_router': nrm((n_moe, d, e), d ** -0.5),
        'moe_w1': nrm((n_moe, e, d, f), d ** -0.5),
        'moe_w3': nrm((n_moe, e, d, f), d ** -0.5),
        'moe_w2': nrm((n_moe, e, f, d), f ** -0.5),
    }


def reference(x, c, ctx, c_ctx, ada_w, ada_b, norm_mix_g, norm_ffn_g, final_g,
              s5_lambda_re, s5_lambda_im, s5_log_dt, s5_b_re, s5_b_im, s5_c_re, s5_c_im,
              s5_d, s5_glu_w, s5_glu_b,
              cv_pw1_w, cv_pw1_b, cv_dw_w, cv_dw_b, cv_ln_g, cv_ln_b, cv_pw2_w, cv_pw2_b,
              rw_mu, rw_w_r, rw_w_k, rw_w_v, rw_w_o, rw_w0, rw_w1, rw_w2, rw_a0, rw_a1, rw_a2,
              rw_g1, rw_g2, rw_k_k, rw_k_a, rw_r_k, rw_lnx_g, rw_lnx_b,
              ff_w1, ff_w3, ff_w2, moe_router, moe_w1, moe_w3, moe_w2):
    rows = x.shape[1] // GRID_W
    silu_c = jax.nn.silu(c)
    silu_cc = jax.nn.silu(c_ctx)
    for i in range(DEPTH):
        last = i == DEPTH - 1
        m_lat = jnp.split((silu_c @ ada_w[i] + ada_b[i])[:, None, :], 6, axis=-1)
        m_ctx = jnp.split(silu_cc @ ada_w[i] + ada_b[i], 6, axis=-1)
        h_lat = _modulate(_rmsnorm(x, norm_mix_g[i]), m_lat[0], m_lat[1])
        h_ctx = _modulate(_rmsnorm(ctx, norm_mix_g[i]), m_ctx[0], m_ctx[1])
        j = i // N_MIXERS
        if i % N_MIXERS == 0:
            y_ctx, y_lat = _s5_mixer(h_ctx, h_lat, s5_lambda_re[j], s5_lambda_im[j], s5_log_dt[j],
                                     s5_b_re[j], s5_b_im[j], s5_c_re[j], s5_c_im[j], s5_d[j],
                                     s5_glu_w[j], s5_glu_b[j], not last)
        elif i % N_MIXERS == 1:
            y_ctx, y_lat = _conformer_mixer(h_ctx, h_lat, rows, cv_pw1_w[j], cv_pw1_b[j], cv_dw_w[j],
                                            cv_dw_b[j], cv_ln_g[j], cv_ln_b[j], cv_pw2_w[j], cv_pw2_b[j],
                                            not last)
        else:
            y_ctx, y_lat = _rwkv7_mixer(h_ctx, h_lat, rw_mu[j], rw_w_r[j], rw_w_k[j], rw_w_v[j], rw_w_o[j],
                                        rw_w0[j], rw_w1[j], rw_w2[j], rw_a0[j], rw_a1[j], rw_a2[j],
                                        rw_g1[j], rw_g2[j], rw_k_k[j], rw_k_a[j], rw_r_k[j],
                                        rw_lnx_g[j], rw_lnx_b[j], not last)
        x = x + m_lat[2] * y_lat
        if not last:
            ctx = ctx + m_ctx[2] * y_ctx
        fi = i // 2
        if i % 2 == 0:
            def ffn(h):
                return _swiglu(h, ff_w1[fi], ff_w3[fi], ff_w2[fi])
        else:
            def ffn(h):
                return _moe_swiglu(h, moe_router[fi], moe_w1[fi], moe_w3[fi], moe_w2[fi])
        x = x + m_lat[5] * ffn(_modulate(_rmsnorm(x, norm_ffn_g[i]), m_lat[3], m_lat[4]))
        if not last:
            ctx = ctx + m_ctx[5] * ffn(_modulate(_rmsnorm(ctx, norm_ffn_g[i]), m_ctx[3], m_ctx[4]))
    return _rmsnorm(x, final_g)
```

```python
import functools
import math

import jax
import jax.numpy as jnp
from jax import lax
from jax.experimental import pallas as pl
from jax.experimental.pallas import tpu as pltpu

F32 = jnp.float32
BF16 = jnp.bfloat16

NB = 8
LANES = 128
TM = 512
TT = TM // NB
GRID_W = 64
CONV_WIDTH = 31
CONV_PAD = CONV_WIDTH // 2
HALO_ROWS = 128
S5_GROUP = 16
S5_STATE = 64
S5_JBLK = 8
RWKV_HEAD = 64
WKV_L = 32
RW_TM = 256
RMS_EPS = 1e-6
LN_EPS = 1e-5
GN_EPS = 64e-5
KK_EPS = 1e-12
LAMBDA_RE_MAX = -1e-4
VMEM_LIMIT = 56 * 1024 * 1024


def _cparams(sem):
    return pltpu.CompilerParams(dimension_semantics=sem, vmem_limit_bytes=VMEM_LIMIT)


def _dot(a, b):
    return jnp.dot(a, b, preferred_element_type=F32)


def _dot_nt(a, b):
    return lax.dot_general(a, b, (((1,), (1,)), ((), ())), preferred_element_type=F32)


def _split3(x):
    h1 = x.astype(BF16)
    r1 = x - h1.astype(F32)
    h2 = r1.astype(BF16)
    h3 = (r1 - h2.astype(F32)).astype(BF16)
    return h1, h2, h3


def _dot_exact_rhs(x, m_bf16):
    h1, h2, h3 = _split3(x)
    return _dot(h1, m_bf16) + _dot(h2, m_bf16) + _dot(h3, m_bf16)


def _dot_exact_lhs(m_bf16, x):
    h1, h2, h3 = _split3(x)
    return _dot(m_bf16, h1) + _dot(m_bf16, h2) + _dot(m_bf16, h3)


def _dot_x3(a, b):
    a1 = a.astype(BF16)
    a2 = (a - a1.astype(F32)).astype(BF16)
    b1 = b.astype(BF16)
    b2 = (b - b1.astype(F32)).astype(BF16)
    return _dot(a1, b1) + _dot(a1, b2) + _dot(a2, b1)


def _sigmoid(x):
    return jax.nn.sigmoid(x)


def _norm_mod(x, g, shift, scale):
    rows, d = x.shape
    ms = jnp.mean(x * x, axis=-1, keepdims=True)
    y = x * lax.rsqrt(ms + RMS_EPS) * g
    y3 = y.reshape(rows // NB, NB, d)
    return (y3 * (1.0 + scale)[None] + shift[None]).reshape(rows, d)


def _residual(x, gate, y):
    rows, d = x.shape
    y3 = y.reshape(rows // NB, NB, d) * gate[None]
    return x + y3.reshape(rows, d)


def _is_lat(i, nct):
    return jnp.where(i >= nct, 1, 0).astype(jnp.int32)


def _ada_kernel(s_ref, w_ref, b_ref, o_ref):
    s = s_ref[...]
    s = s * _sigmoid(s)
    o_ref[0] = _dot(s.astype(BF16), w_ref[0].astype(BF16)) + b_ref[0]


def _ada_call(cvec, ada_w, ada_b):
    depth, d, d6 = ada_w.shape
    tn = 1536
    return pl.pallas_call(
        _ada_kernel,
        grid=(depth, d6 // tn),
        in_specs=[
            pl.BlockSpec((2 * NB, d), lambda l, j: (0, 0)),
            pl.BlockSpec((1, d, tn), lambda l, j: (l, 0, j)),
            pl.BlockSpec((1, 1, tn), lambda l, j: (l, 0, j)),
        ],
        out_specs=pl.BlockSpec((1, 2 * NB, tn), lambda l, j: (l, 0, j)),
        out_shape=jax.ShapeDtypeStruct((depth, 2 * NB, d6), F32),
        compiler_params=_cparams(("arbitrary", "arbitrary")),
    )(cvec, ada_w, ada_b.reshape(depth, 1, d6))


def _ffn_kernel(x_ref, mod_ref, g_ref, router_ref, w1_ref, w3_ref, w2_ref, o_ref,
                hb_ref, acc_ref, gate_ref, *, n_e, n_k, moe):
    e = pl.program_id(1)
    k = pl.program_id(2)

    @pl.when((e == 0) & (k == 0))
    def _():
        h = _norm_mod(x_ref[...], g_ref[...], mod_ref[0, 3], mod_ref[0, 4])
        hb_ref[...] = h.astype(BF16)
        acc_ref[...] = jnp.zeros_like(acc_ref)
        if moe:
            logits = _dot_x3(h, router_ref[...])
            lane = lax.broadcasted_iota(jnp.int32, logits.shape, 1).astype(F32)
            neg = jnp.float32(-3.0e38)
            lg = jnp.where(lane < n_e, logits, neg)
            m1 = jnp.max(lg, axis=-1, keepdims=True)
            i1 = jnp.min(jnp.where(lg == m1, lane, 1.0e9), axis=-1, keepdims=True)
            lg2 = jnp.where(lane == i1, neg, lg)
            m2 = jnp.max(lg2, axis=-1, keepdims=True)
            i2 = jnp.min(jnp.where(lg2 == m2, lane, 1.0e9), axis=-1, keepdims=True)
            e2 = jnp.exp(m2 - m1)
            den = 1.0 + e2
            gate_ref[...] = (jnp.where(lane == i1, 1.0 / den, 0.0)
                             + jnp.where(lane == i2, e2 / den, 0.0))

    hb = hb_ref[...]
    a = _dot(hb, w1_ref[0])
    b = _dot(hb, w3_ref[0])
    act = a * _sigmoid(a) * b
    if moe:
        lane = lax.broadcasted_iota(jnp.int32, gate_ref.shape, 1)
        ge = jnp.sum(jnp.where(lane == e, gate_ref[...], 0.0), axis=-1, keepdims=True)
        act = act * ge
    acc_ref[...] += _dot(act.astype(BF16), w2_ref[0])

    @pl.when((e == n_e - 1) & (k == n_k - 1))
    def _():
        o_ref[...] = _residual(x_ref[...], mod_ref[0, 5], acc_ref[...])


def _ffn_call(x, mod, g, router, w1, w3, w2, *, nct, tile0, moe):
    n, d = x.shape
    n_e, _, f = w1.shape
    tf = 512
    n_k = f // tf
    nt = n // TM
    kern = functools.partial(_ffn_kernel, n_e=n_e, n_k=n_k, moe=moe)
    return pl.pallas_call(
        kern,
        grid=(nt - tile0, n_e, n_k),
        in_specs=[
            pl.BlockSpec((TM, d), lambda i, e, k: (i + tile0, 0)),
            pl.BlockSpec((1, 6, NB, d), lambda i, e, k: (_is_lat(i + tile0, nct), 0, 0, 0)),
            pl.BlockSpec((1, d), lambda i, e, k: (0, 0)),
            pl.BlockSpec((d, LANES), lambda i, e, k: (0, 0)),
            pl.BlockSpec((1, d, tf), lambda i, e, k: (e, 0, k)),
            pl.BlockSpec((1, d, tf), lambda i, e, k: (e, 0, k)),
            pl.BlockSpec((1, tf, d), lambda i, e, k: (e, k, 0)),
        ],
        out_specs=pl.BlockSpec((TM, d), lambda i, e, k: (i + tile0, 0)),
        out_shape=jax.ShapeDtypeStruct((n, d), F32),
        scratch_shapes=[
            pltpu.VMEM((TM, d), BF16),
            pltpu.VMEM((TM, d), F32),
            pltpu.VMEM((TM, LANES), F32),
        ],
        compiler_params=_cparams(("arbitrary", "arbitrary", "arbitrary")),
    )(x, mod, g, router, w1, w3, w2)


def _dir_tile(d, s, nct, nt):
    bwd = jnp.where(s < nct, nct - 1 - s, nt - 1 + nct - s)
    return jnp.where(d == 0, s, bwd)


def _s5_scan_kernel(x_ref, mod_ref, g_ref, bb_ref, cc_ref, ab_ref, y_ref,
                    hb_ref, bu_ref, st_ref):
    d = pl.program_id(0)
    s = pl.program_id(1)
    half = S5_JBLK * S5_STATE

    @pl.when(s == 0)
    def _():
        st_ref[...] = jnp.zeros_like(st_ref)

    h = _norm_mod(x_ref[...], g_ref[...], mod_ref[0, 0], mod_ref[0, 1])
    hb_ref[...] = h.astype(BF16)

    for j in range(S5_JBLK):
        bu_ref[...] = _dot(hb_ref[:, j * LANES:(j + 1) * LANES], bb_ref[0, j])
        ar = ab_ref[0, j, 0]
        ai = ab_ref[0, j, 1]

        def body(t, carry):
            re, im = carry
            tt = t + d * (TT - 1 - 2 * t)
            r0 = pl.multiple_of(tt * NB, NB)
            bre = bu_ref[pl.ds(r0, NB), 0:half]
            bim = bu_ref[pl.ds(r0, NB), half:2 * half]
            nre = ar * re - ai * im + bre
            nim = ar * im + ai * re + bim
            bu_ref[pl.ds(r0, NB), 0:half] = nre
            bu_ref[pl.ds(r0, NB), half:2 * half] = nim
            return nre, nim

        re, im = lax.fori_loop(0, TT, body, (st_ref[j, 0], st_ref[j, 1]), unroll=4)
        st_ref[j, 0] = re
        st_ref[j, 1] = im
        y_ref[0, j] = _dot(bu_ref[...].astype(BF16), cc_ref[0, j])


def _s5_scan_call(x, mod, g, bb, cc, ab, *, nct):
    n, d = x.shape
    nt = n // TM
    half = S5_JBLK * S5_STATE
    tile = lambda dd, s: _dir_tile(dd, s, nct, nt)
    return pl.pallas_call(
        _s5_scan_kernel,
        grid=(2, nt),
        in_specs=[
            pl.BlockSpec((TM, d), lambda dd, s: (tile(dd, s), 0)),
            pl.BlockSpec((1, 6, NB, d), lambda dd, s: (_is_lat(tile(dd, s), nct), 0, 0, 0)),
            pl.BlockSpec((1, d), lambda dd, s: (0, 0)),
            pl.BlockSpec((1, S5_JBLK, LANES, 2 * half), lambda dd, s: (dd, 0, 0, 0)),
            pl.BlockSpec((1, S5_JBLK, 2 * half, LANES), lambda dd, s: (dd, 0, 0, 0)),
            pl.BlockSpec((1, S5_JBLK, 2, NB, half), lambda dd, s: (dd, 0, 0, 0, 0)),
        ],
        out_specs=pl.BlockSpec((1, S5_JBLK, TM, LANES), lambda dd, s: (dd, 0, tile(dd, s), 0)),
        out_shape=jax.ShapeDtypeStruct((2, S5_JBLK, n, LANES), F32),
        scratch_shapes=[
            pltpu.VMEM((TM, d), BF16),
            pltpu.VMEM((TM, 2 * half), F32),
            pltpu.VMEM((S5_JBLK, 2, NB, half), F32),
        ],
        compiler_params=_cparams(("arbitrary", "arbitrary")),
    )(x, mod, g, bb, cc, ab)


def _gelu_tanh(x):
    c = math.sqrt(2.0 / math.pi)
    return 0.5 * x * (1.0 + jnp.tanh(c * (x + 0.044715 * (x * x * x))))


def _s5_glu_kernel(x_ref, mod_ref, g_ref, y_ref, dsk_ref, w_ref, b_ref, o_ref):
    x = x_ref[...]
    d = x.shape[1]
    h = _norm_mod(x, g_ref[...], mod_ref[0, 0], mod_ref[0, 1])
    y = jnp.concatenate([y_ref[0, j] + y_ref[1, j] for j in range(S5_JBLK)], axis=1)
    z = _gelu_tanh(y + dsk_ref[...] * h).astype(BF16)
    zz = _dot(z, w_ref[...]) + b_ref[...]
    out = zz[:, :d] * _sigmoid(zz[:, d:])
    o_ref[...] = _residual(x, mod_ref[0, 2], out)


def _s5_glu_call(x, mod, g, y, dsk, w, b, *, nct, tile0):
    n, d = x.shape
    nt = n // TM
    return pl.pallas_call(
        _s5_glu_kernel,
        grid=(nt - tile0,),
        in_specs=[
            pl.BlockSpec((TM, d), lambda i: (i + tile0, 0)),
            pl.BlockSpec((1, 6, NB, d), lambda i: (_is_lat(i + tile0, nct), 0, 0, 0)),
            pl.BlockSpec((1, d), lambda i: (0, 0)),
            pl.BlockSpec((2, S5_JBLK, TM, LANES), lambda i: (0, 0, i + tile0, 0)),
            pl.BlockSpec((1, d), lambda i: (0, 0)),
            pl.BlockSpec((d, 2 * d), lambda i: (0, 0)),
            pl.BlockSpec((1, 2 * d), lambda i: (0, 0)),
        ],
        out_specs=pl.BlockSpec((TM, d), lambda i: (i + tile0, 0)),
        out_shape=jax.ShapeDtypeStruct((n, d), F32),
        compiler_params=_cparams(("arbitrary",)),
    )(x, mod, g, y, dsk, w, b)


def _s5_params(lam_re, lam_im, log_dt, b_re, b_im, c_re, c_im):
    lam_re = jnp.minimum(lam_re.astype(F32), LAMBDA_RE_MAX)
    lam_im = lam_im.astype(F32)
    dt = jnp.exp(log_dt.astype(F32))[..., None]
    mag = jnp.exp(lam_re * dt)
    abar_re = mag * jnp.cos(lam_im * dt)
    abar_im = mag * jnp.sin(lam_im * dt)
    inv_den = 1.0 / (lam_re * lam_re + lam_im * lam_im)
    gam_re = ((abar_re - 1.0) * lam_re + abar_im * lam_im) * inv_den
    gam_im = (abar_im * lam_re - (abar_re - 1.0) * lam_im) * inv_den
    b_re = b_re.astype(F32)
    b_im = b_im.astype(F32)
    bb_re = gam_re[..., None] * b_re - gam_im[..., None] * b_im
    bb_im = gam_re[..., None] * b_im + gam_im[..., None] * b_re
    g_all = lam_re.shape[1]
    nj = g_all // S5_JBLK
    eye = jnp.eye(S5_JBLK, dtype=F32)

    def in_mat(m):
        m = m.reshape(2, nj, S5_JBLK, S5_STATE, S5_GROUP)
        m = jnp.einsum('djgnp,gh->djgphn', m, eye)
        return m.reshape(2, nj, S5_JBLK * S5_GROUP, S5_JBLK * S5_STATE)

    def out_mat(m):
        m = m.reshape(2, nj, S5_JBLK, S5_GROUP, S5_STATE)
        m = jnp.einsum('djgpn,gh->djgnhp', m, eye)
        return m.reshape(2, nj, S5_JBLK * S5_STATE, S5_JBLK * S5_GROUP)

    bb = jnp.concatenate([in_mat(bb_re), in_mat(bb_im)], axis=-1).astype(BF16)
    cc = jnp.concatenate([out_mat(c_re.astype(F32)), -out_mat(c_im.astype(F32))], axis=-2).astype(BF16)
    half = S5_JBLK * S5_STATE
    ab = jnp.stack([abar_re.reshape(2, nj, half), abar_im.reshape(2, nj, half)], axis=2)
    ab = jnp.broadcast_to(ab[:, :, :, None, :], (2, nj, 2, NB, half))
    return bb, cc, ab


def _cv_glu_kernel(x_ref, mod_ref, g_ref, w_ref, b_ref, o_ref):
    x = x_ref[...]
    d = x.shape[1]
    h = _norm_mod(x, g_ref[...], mod_ref[0, 0], mod_ref[0, 1]).astype(BF16)
    u = _dot(h, w_ref[...]) + b_ref[...]
    o_ref[...] = u[:, :d] * _sigmoid(u[:, d:])


def _cv_glu_call(x, mod, g, w, b, *, nct):
    n, d = x.shape
    nt = n // TM
    return pl.pallas_call(
        _cv_glu_kernel,
        grid=(nt,),
        in_specs=[
            pl.BlockSpec((TM, d), lambda i: (i, 0)),
            pl.BlockSpec((1, 6, NB, d), lambda i: (_is_lat(i, nct), 0, 0, 0)),
            pl.BlockSpec((1, d), lambda i: (0, 0)),
            pl.BlockSpec((d, 2 * d), lambda i: (0, 0)),
            pl.BlockSpec((1, 2 * d), lambda i: (0, 0)),
        ],
        out_specs=pl.BlockSpec((TM, d), lambda i: (i, 0)),
        out_shape=jax.ShapeDtypeStruct((n, d), F32),
        compiler_params=_cparams(("arbitrary",)),
    )(x, mod, g, w, b)


def _cv_conv_kernel(u_ref, up_ref, un_ref, x_ref, mod_ref, dw_ref, dwb_ref, lng_ref, lnb_ref,
                    w_ref, b_ref, o_ref, pad_ref, cv_ref, *, nct):
    i = pl.program_id(0)
    halo = CONV_PAD * NB
    prev_ok = (i > 0) & (i < nct)
    next_ok = i < nct - 1
    pad_ref[0:halo] = jnp.where(prev_ok, up_ref[HALO_ROWS - halo:HALO_ROWS], 0.0)
    pad_ref[halo:halo + TM] = u_ref[...]
    pad_ref[halo + TM:halo + TM + halo] = jnp.where(next_ok, un_ref[0:halo], 0.0)

    rb = 32

    def body(c, _):
        r0 = pl.multiple_of(c * rb, rb)
        acc = jnp.broadcast_to(dwb_ref[...], (rb, dwb_ref.shape[1]))
        for k in range(CONV_WIDTH):
            acc = acc + dw_ref[k] * pad_ref[pl.ds(r0 + k * NB, rb)]
        cv_ref[pl.ds(r0, rb)] = acc
        return 0

    lax.fori_loop(0, TM // rb, body, 0)

    c = cv_ref[...]
    mu = jnp.mean(c, axis=-1, keepdims=True)
    cc = c - mu
    var = jnp.mean(cc * cc, axis=-1, keepdims=True)
    y = cc * lax.rsqrt(var + LN_EPS) * lng_ref[...] + lnb_ref[...]
    y = (y * _sigmoid(y)).astype(BF16)
    out = _dot(y, w_ref[...]) + b_ref[...]
    o_ref[...] = _residual(x_ref[...], mod_ref[0, 2], out)


def _cv_conv_call(u, x, mod, dw_w, dw_b, ln_g, ln_b, w, b, *, nct):
    n, d = x.shape
    nt = n // TM
    hb = TM // HALO_ROWS
    nhb = n // HALO_ROWS
    kern = functools.partial(_cv_conv_kernel, nct=nct)
    row = lambda i: (0, 0)
    return pl.pallas_call(
        kern,
        grid=(nt,),
        in_specs=[
            pl.BlockSpec((TM, d), lambda i: (i, 0)),
            pl.BlockSpec((HALO_ROWS, d), lambda i: (jnp.maximum(i * hb - 1, 0), 0)),
            pl.BlockSpec((HALO_ROWS, d), lambda i: (jnp.minimum((i + 1) * hb, nhb - 1), 0)),
            pl.BlockSpec((TM, d), lambda i: (i, 0)),
            pl.BlockSpec((1, 6, NB, d), lambda i: (_is_lat(i, nct), 0, 0, 0)),
            pl.BlockSpec((CONV_WIDTH, 1, d), lambda i: (0, 0, 0)),
            pl.BlockSpec((1, d), row),
            pl.BlockSpec((1, d), row),
            pl.BlockSpec((1, d), row),
            pl.BlockSpec((d, d), row),
            pl.BlockSpec((1, d), row),
        ],
        out_specs=pl.BlockSpec((TM, d), lambda i: (i, 0)),
        out_shape=jax.ShapeDtypeStruct((n, d), F32),
        scratch_shapes=[
            pltpu.VMEM((TM + 2 * CONV_PAD * NB, d), F32),
            pltpu.VMEM((TM, d), F32),
        ],
        compiler_params=_cparams(("arbitrary",)),
    )(u, u, u, x, mod, dw_w.reshape(CONV_WIDTH, 1, d), dw_b, ln_g, ln_b, w, b)


def _seg_mats(d):
    ch = lax.broadcasted_iota(jnp.int32, (d, LANES), 0) // RWKV_HEAD
    hd = lax.broadcasted_iota(jnp.int32, (d, LANES), 1)
    seg = jnp.where(ch == hd, 1.0, 0.0).astype(BF16)
    hd_t = lax.broadcasted_iota(jnp.int32, (LANES, d), 0)
    ch_t = lax.broadcasted_iota(jnp.int32, (LANES, d), 1) // RWKV_HEAD
    seg_t = jnp.where(ch_t == hd_t, 1.0, 0.0).astype(BF16)
    return seg, seg_t


def _head_sum_bcast(x, seg, seg_t):
    return _dot_exact_rhs(_dot_exact_rhs(x, seg), seg_t)


def _rw_proj_kernel(x_ref, xp_ref, xn_ref, mod_ref, g_ref, mu_ref, wr_ref, wk_ref, wv_ref,
                    w1_ref, w2_ref, w0_ref, a1_ref, a2_ref, a0_ref, g1_ref, g2_ref, kkw_ref,
                    r_ref, k_ref, v_ref, kk_ref, ew_ref, as_ref, gg_ref, hp_ref, *, nct, nt):
    i = pl.program_id(0)
    tm, d = x_ref.shape
    g = g_ref[...]
    sh = mod_ref[0, 0]
    sc = mod_ref[0, 1]
    h = _norm_mod(x_ref[...], g, sh, sc)
    hp = _norm_mod(xp_ref[...], g, sh, sc)
    hn = _norm_mod(xn_ref[...], g, sh, sc)
    prev_ok = (i != 0) & (i != nct)
    next_ok = (i != nct - 1) & (i != nt - 1)
    hp_ref[0:NB] = jnp.where(prev_ok, hp, 0.0)
    hp_ref[NB:NB + tm] = h
    hp_ref[NB + tm:NB + tm + NB] = jnp.where(next_ok, hn, 0.0)
    xx = 0.5 * (hp_ref[0:tm] + hp_ref[2 * NB:2 * NB + tm]) - h

    def mix(j):
        return (h + xx * mu_ref[j]).astype(BF16)

    r = _dot(mix(0), wr_ref[...])
    k = _dot(mix(2), wk_ref[...])
    v = _dot(mix(3), wv_ref[...])
    zw = w0_ref[...] + _dot(jnp.tanh(_dot(mix(1), w1_ref[...])).astype(BF16), w2_ref[...])
    ew = _sigmoid(zw) * math.exp(-0.5)
    asg = _sigmoid(a0_ref[...] + _dot(_dot(mix(4), a1_ref[...]).astype(BF16), a2_ref[...]))
    gg_ref[...] = _dot(_sigmoid(_dot(mix(5), g1_ref[...])).astype(BF16), g2_ref[...])

    seg, seg_t = _seg_mats(d)
    kk = k * kkw_ref[...]
    kk = kk * lax.rsqrt(_head_sum_bcast(kk * kk, seg, seg_t) + KK_EPS)

    for p in range(d // LANES):
        sl = slice(p * LANES, (p + 1) * LANES)
        r_ref[p] = r[:, sl]
        k_ref[p] = k[:, sl]
        v_ref[p] = v[:, sl]
        kk_ref[p] = kk[:, sl]
        for dr in range(2):
            sl2 = slice(dr * d + p * LANES, dr * d + (p + 1) * LANES)
            ew_ref[dr, p] = ew[:, sl2]
            as_ref[dr, p] = asg[:, sl2]


def _rw_proj_call(x, mod, g, mu, wr, wk, wv, w1, w2, w0, a1, a2, a0, g1, g2, kkw, *, nct):
    n, d = x.shape
    tm = RW_TM
    nt = n // tm
    nct = nct * (TM // tm)
    npair = d // LANES
    tb = tm // NB
    nb8 = n // NB
    kern = functools.partial(_rw_proj_kernel, nct=nct, nt=nt)
    full2 = lambda a: pl.BlockSpec(a.shape, lambda i: (0, 0))
    slab = pl.BlockSpec((npair, tm, LANES), lambda i: (0, i, 0))
    slab2 = pl.BlockSpec((2, npair, tm, LANES), lambda i: (0, 0, i, 0))
    slab_shape = jax.ShapeDtypeStruct((npair, n, LANES), F32)
    slab2_shape = jax.ShapeDtypeStruct((2, npair, n, LANES), F32)
    mu3 = mu.reshape(6, 1, d)
    return pl.pallas_call(
        kern,
        grid=(nt,),
        in_specs=[
            pl.BlockSpec((tm, d), lambda i: (i, 0)),
            pl.BlockSpec((NB, d), lambda i: (jnp.maximum(i * tb - 1, 0), 0)),
            pl.BlockSpec((NB, d), lambda i: (jnp.minimum((i + 1) * tb, nb8 - 1), 0)),
            pl.BlockSpec((1, 6, NB, d), lambda i: (_is_lat(i, nct), 0, 0, 0)),
            pl.BlockSpec((1, d), lambda i: (0, 0)),
            pl.BlockSpec((6, 1, d), lambda i: (0, 0, 0)),
            full2(wr), full2(wk), full2(wv),
            full2(w1), full2(w2), full2(w0), full2(a1), full2(a2), full2(a0),
            full2(g1), full2(g2), full2(kkw),
        ],
        out_specs=[slab, slab, slab, slab, slab2, slab2, pl.BlockSpec((tm, d), lambda i: (i, 0))],
        out_shape=[slab_shape, slab_shape, slab_shape, slab_shape, slab2_shape, slab2_shape,
                   jax.ShapeDtypeStruct((n, d), F32)],
        scratch_shapes=[pltpu.VMEM((tm + 2 * NB, d), F32)],
        compiler_params=_cparams(("arbitrary",)),
    )(x, x, x, mod, g, mu3, wr, wk, wv, w1, w2, w0, a1, a2, a0, g1, g2, kkw)


def _wkv_kernel(r_ref, k_ref, v_ref, kk_ref, ew_ref, as_ref, ka_ref, y_ref, st_ref, *, L, npair):
    d = pl.program_id(0)
    c = pl.program_id(1)

    @pl.when(c == 0)
    def _():
        st_ref[...] = jnp.zeros_like(st_ref)

    sgn = 1 - 2 * d
    l2 = 2 * L
    rr = lax.broadcasted_iota(jnp.int32, (l2, l2), 0)
    cc = lax.broadcasted_iota(jnp.int32, (l2, l2), 1)
    same = (rr >= L) == (cc >= L)
    diff = (rr - cc) * sgn
    strict_f = jnp.where(same & (diff > 0), 1.0, 0.0)
    incl_f = jnp.where(same & (diff >= 0), 1.0, 0.0)
    r1 = lax.broadcasted_iota(jnp.int32, (L, L), 0)
    c1 = lax.broadcasted_iota(jnp.int32, (L, L), 1)
    mcum = jnp.where((r1 - c1) * sgn >= 0, 1.0, 0.0).astype(BF16)
    head0 = lax.broadcasted_iota(jnp.int32, (L, LANES), 1) < RWKV_HEAD
    sr = lax.broadcasted_iota(jnp.int32, (LANES, LANES), 0) < RWKV_HEAD
    scl = lax.broadcasted_iota(jnp.int32, (LANES, LANES), 1) < RWKV_HEAD
    blockdiag = sr == scl
    n_dbl = int(round(math.log2(L)))

    def stack2(x):
        return jnp.concatenate([jnp.where(head0, x, 0.0), jnp.where(head0, 0.0, x)], axis=0)

    def inst(idx, carry):
        p = idx // NB
        b = idx % NB
        rows = pl.ds(b, L, stride=NB)
        r = r_ref[p, rows, :]
        k = k_ref[p, rows, :]
        v = v_ref[p, rows, :]
        kk = kk_ref[p, rows, :]
        ew = ew_ref[0, p, rows, :]
        asg = as_ref[0, p, rows, :]
        ka = ka_ref[p]

        ldec = -ew
        cl = _dot_exact_lhs(mcum, ldec)
        ce = cl - ldec
        ecl = jnp.exp(cl)
        encl = jnp.exp(-cl)
        etot = jnp.exp(jnp.sum(ldec, axis=0, keepdims=True))
        at = -kk * jnp.exp(ce)
        rt = r * ecl
        bt = kk * asg * encl
        kt = k * (1.0 + (asg - 1.0) * ka) * encl

        lhs = jnp.concatenate([stack2(at), stack2(rt)], axis=0).astype(BF16)
        pb = _dot_nt(lhs, stack2(bt).astype(BF16))
        pk = _dot_nt(lhs, stack2(kt).astype(BF16))
        nab = pb[0:l2] * strict_f
        nak = pk[0:l2] * strict_f
        mrb = pb[l2:2 * l2] * incl_f
        mrk = pk[l2:2 * l2] * incl_f

        s0 = st_ref[idx]
        ah = _dot_nt(jnp.concatenate([at, rt], axis=0).astype(BF16), s0.astype(BF16))
        vs = stack2(v)
        vsb = vs.astype(BF16)
        xs = stack2(ah[0:L]) + _dot(nak.astype(BF16), vsb)
        npow = nab
        for it in range(n_dbl):
            xs = xs + _dot_x3(npow, xs)
            if it + 1 < n_dbl:
                npow = _dot_x3(npow, npow)
        ys = _dot(mrb.astype(BF16), xs.astype(BF16)) + _dot(mrk.astype(BF16), vsb)
        y_ref[0, p, rows, :] = ah[L:l2] + ys[0:L] + ys[L:l2]

        u = xs[0:L] + xs[L:l2]
        uv = jnp.concatenate([u, v], axis=0)
        bk = jnp.concatenate([bt, kt], axis=0)
        sadd = _dot(uv.T.astype(BF16), bk.astype(BF16))
        st_ref[idx] = (s0 + jnp.where(blockdiag, sadd, 0.0)) * etot
        return carry

    lax.fori_loop(0, npair * NB, inst, 0)


def _wkv_call(r, k, v, kk, ew, asg, ka, *, nct_rows):
    npair, n, _ = r.shape
    L = WKV_L
    rows = L * NB
    nc = n // rows
    ncc = nct_rows // rows
    kern = functools.partial(_wkv_kernel, L=L, npair=npair)

    def chunk(dd, c):
        bwd = jnp.where(c < ncc, ncc - 1 - c, nc - 1 + ncc - c)
        return jnp.where(dd == 0, c, bwd)

    slab = pl.BlockSpec((npair, rows, LANES), lambda dd, c: (0, chunk(dd, c), 0))
    slab2 = pl.BlockSpec((1, npair, rows, LANES), lambda dd, c: (dd, 0, chunk(dd, c), 0))
    return pl.pallas_call(
        kern,
        grid=(2, nc),
        in_specs=[slab, slab, slab, slab, slab2, slab2,
                  pl.BlockSpec((npair, 1, LANES), lambda dd, c: (0, 0, 0))],
        out_specs=slab2,
        out_shape=jax.ShapeDtypeStruct((2, npair, n, LANES), F32),
        scratch_shapes=[pltpu.VMEM((npair * NB, LANES, LANES), F32)],
        compiler_params=_cparams(("arbitrary", "arbitrary")),
    )(r, k, v, kk, ew, asg, ka)


def _rw_out_kernel(x_ref, mod_ref, y_ref, r_ref, k_ref, v_ref, as_ref, gg_ref, lng_ref, lnb_ref,
                   ka_ref, rk_ref, wo_ref, o_ref):
    x = x_ref[...]
    d = x.shape[1]
    npair = d // LANES
    cat = lambda f: jnp.concatenate([f(p) for p in range(npair)], axis=1)
    y = cat(lambda p: y_ref[0, p] + y_ref[1, p])
    r = cat(lambda p: r_ref[p])
    k = cat(lambda p: k_ref[p])
    v = cat(lambda p: v_ref[p])
    am = cat(lambda p: 0.5 * (as_ref[0, p] + as_ref[1, p]))
    seg, seg_t = _seg_mats(d)
    inv = 1.0 / RWKV_HEAD
    mean = _head_sum_bcast(y, seg, seg_t) * inv
    yc = y - mean
    var = _head_sum_bcast(yc * yc, seg, seg_t) * inv
    yn = yc * lax.rsqrt(var + GN_EPS) * lng_ref[...] + lnb_ref[...]
    k_mean = k * (1.0 + (am - 1.0) * ka_ref[...])
    bonus = _head_sum_bcast(r * k_mean * rk_ref[...], seg, seg_t) * v
    z = ((yn + bonus) * gg_ref[...]).astype(BF16)
    o_ref[...] = _residual(x, mod_ref[0, 2], _dot(z, wo_ref[...]))


def _rw_out_call(x, mod, y, r, k, v, asg, gg, lng, lnb, ka, rk, wo, *, nct):
    n, d = x.shape
    nt = n // TM
    npair = d // LANES
    slab = pl.BlockSpec((npair, TM, LANES), lambda i: (0, i, 0))
    slab2 = pl.BlockSpec((2, npair, TM, LANES), lambda i: (0, 0, i, 0))
    row = pl.BlockSpec((1, d), lambda i: (0, 0))
    return pl.pallas_call(
        _rw_out_kernel,
        grid=(nt,),
        in_specs=[
            pl.BlockSpec((TM, d), lambda i: (i, 0)),
            pl.BlockSpec((1, 6, NB, d), lambda i: (_is_lat(i, nct), 0, 0, 0)),
            slab2, slab, slab, slab, slab2,
            pl.BlockSpec((TM, d), lambda i: (i, 0)),
            row, row, row, row,
            pl.BlockSpec((d, d), lambda i: (0, 0)),
        ],
        out_specs=pl.BlockSpec((TM, d), lambda i: (i, 0)),
        out_shape=jax.ShapeDtypeStruct((n, d), F32),
        compiler_params=_cparams(("arbitrary",)),
    )(x, mod, y, r, k, v, asg, gg, lng, lnb, ka, rk, wo)


def _final_kernel(x_ref, g_ref, o_ref):
    x = x_ref[...]
    ms = jnp.mean(x * x, axis=-1, keepdims=True)
    o_ref[...] = x * lax.rsqrt(ms + RMS_EPS) * g_ref[...]


def _final_call(x, g, *, tile0):
    n, d = x.shape
    nt = n // TM
    return pl.pallas_call(
        _final_kernel,
        grid=(nt - tile0,),
        in_specs=[pl.BlockSpec((TM, d), lambda i: (i + tile0, 0)),
                  pl.BlockSpec((1, d), lambda i: (0, 0))],
        out_specs=pl.BlockSpec((TM, d), lambda i: (i, 0)),
        out_shape=jax.ShapeDtypeStruct((n - tile0 * TM, d), F32),
        compiler_params=_cparams(("arbitrary",)),
    )(x, g)


def _row(v):
    return v.reshape(1, -1).astype(F32)


def _s5_layer(x, mod, g, p, *, nct, tile0):
    bb, cc, ab = _s5_params(p['lam_re'], p['lam_im'], p['log_dt'], p['b_re'], p['b_im'],
                            p['c_re'], p['c_im'])
    y = _s5_scan_call(x, mod, g, bb, cc, ab, nct=nct)
    return _s5_glu_call(x, mod, g, y, _row(p['d']), p['glu_w'].astype(BF16), _row(p['glu_b']),
                        nct=nct, tile0=tile0)


def _conformer_layer(x, mod, g, p, *, nct):
    u = _cv_glu_call(x, mod, g, p['pw1_w'].astype(BF16), _row(p['pw1_b']), nct=nct)
    return _cv_conv_call(u, x, mod, p['dw_w'].astype(F32), _row(p['dw_b']), _row(p['ln_g']),
                         _row(p['ln_b']), p['pw2_w'].astype(BF16), _row(p['pw2_b']), nct=nct)


def _blockdiag2(m):
    z = jnp.zeros_like(m[0])
    return jnp.concatenate([jnp.concatenate([m[0], z], axis=1),
                            jnp.concatenate([z, m[1]], axis=1)], axis=0)


def _rwkv_layer(x, mod, g, p, *, nct):
    n, d = x.shape
    npair = d // LANES
    w1 = jnp.concatenate([p['w1'][0], p['w1'][1]], axis=1).astype(BF16)
    w2 = _blockdiag2(p['w2']).astype(BF16)
    a1 = jnp.concatenate([p['a1'][0], p['a1'][1]], axis=1).astype(BF16)
    a2 = _blockdiag2(p['a2']).astype(BF16)
    gl = p['g1'].shape[1]
    glp = -(-gl // LANES) * LANES
    g1 = jnp.pad(p['g1'], ((0, 0), (0, glp - gl))).astype(BF16)
    g2 = jnp.pad(p['g2'], ((0, glp - gl), (0, 0))).astype(BF16)
    r, k, v, kk, ew, asg, gg = _rw_proj_call(
        x, mod, g, p['mu'].astype(F32), p['w_r'].astype(BF16), p['w_k'].astype(BF16),
        p['w_v'].astype(BF16), w1, w2, _row(p['w0']), a1, a2, _row(p['a0']), g1, g2,
        _row(p['k_k']), nct=nct)
    ka_slab = p['k_a'].astype(F32).reshape(npair, 1, LANES)
    y = _wkv_call(r, k, v, kk, ew, asg, ka_slab, nct_rows=nct * TM)
    return _rw_out_call(x, mod, y, r, k, v, asg, gg, _row(p['lnx_g']), _row(p['lnx_b']),
                        _row(p['k_a']), _row(p['r_k']), p['w_o'].astype(BF16), nct=nct)


def _forward(x, c, ctx, c_ctx, ada_w, ada_b, norm_mix_g, norm_ffn_g, final_g, s5, cv, rw,
             ff_w1, ff_w3, ff_w2, moe_router, moe_w1, moe_w3, moe_w2):
    bsz, seq, d = x.shape
    lc = ctx.shape[1]
    depth = ada_w.shape[0]
    assert bsz == NB and lc % TT == 0 and seq % TT == 0
    assert TT == GRID_W, "one token tile must hold exactly one latent raster row per batch entry"
    nct = lc // TT
    t_all = lc + seq
    n = t_all * NB

    xs = jnp.concatenate([jnp.swapaxes(ctx, 0, 1), jnp.swapaxes(x, 0, 1)], axis=0)
    xs = xs.reshape(n, d).astype(F32)

    cvec = jnp.zeros((2 * NB, d), F32).at[:NB].set(c).at[NB].set(c_ctx)
    m_all = _ada_call(cvec, ada_w.astype(F32), ada_b.astype(F32))
    m_all = m_all.reshape(depth, 2 * NB, 6, d)
    m_lat = jnp.swapaxes(m_all[:, :NB], 1, 2)
    m_ctx = jnp.broadcast_to(m_all[:, NB][:, :, None, :], (depth, 6, NB, d))
    mods = jnp.stack([m_ctx, m_lat], axis=1)

    zero_router = jnp.zeros((d, LANES), F32)
    for i in range(depth):
        last = i == depth - 1
        tile0 = nct if last else 0
        mod = mods[i]
        g_mix = _row(norm_mix_g[i])
        j = i // 3
        if i % 3 == 0:
            xs = _s5_layer(xs, mod, g_mix, {k_: v_[j] for k_, v_ in s5.items()}, nct=nct, tile0=tile0)
        elif i % 3 == 1:
            xs = _conformer_layer(xs, mod, g_mix, {k_: v_[j] for k_, v_ in cv.items()}, nct=nct)
        else:
            xs = _rwkv_layer(xs, mod, g_mix, {k_: v_[j] for k_, v_ in rw.items()}, nct=nct)
        fi = i // 2
        g_ffn = _row(norm_ffn_g[i])
        if i % 2 == 0:
            xs = _ffn_call(xs, mod, g_ffn, zero_router, ff_w1[fi][None].astype(BF16),
                           ff_w3[fi][None].astype(BF16), ff_w2[fi][None].astype(BF16),
                           nct=nct, tile0=tile0, moe=False)
        else:
            n_e = moe_router.shape[-1]
            router = jnp.pad(moe_router[fi].astype(F32), ((0, 0), (0, LANES - n_e)))
            xs = _ffn_call(xs, mod, g_ffn, router, moe_w1[fi].astype(BF16), moe_w3[fi].astype(BF16),
                           moe_w2[fi].astype(BF16), nct=nct, tile0=tile0, moe=True)

    out = _final_call(xs, _row(final_g), tile0=nct)
    return jnp.swapaxes(out.reshape(seq, NB, d), 0, 1)


def kernel(x, c, ctx, c_ctx, ada_w, ada_b, norm_mix_g, norm_ffn_g, final_g, s5_lambda_re, s5_lambda_im, s5_log_dt, s5_b_re, s5_b_im, s5_c_re, s5_c_im, s5_d, s5_glu_w, s5_glu_b, cv_pw1_w, cv_pw1_b, cv_dw_w, cv_dw_b, cv_ln_g, cv_ln_b, cv_pw2_w, cv_pw2_b, rw_mu, rw_w_r, rw_w_k, rw_w_v, rw_w_o, rw_w0, rw_w1, rw_w2, rw_a0, rw_a1, rw_a2, rw_g1, rw_g2, rw_k_k, rw_k_a, rw_r_k, rw_lnx_g, rw_lnx_b, ff_w1, ff_w3, ff_w2, moe_router, moe_w1, moe_w3, moe_w2):
    s5 = dict(lam_re=s5_lambda_re, lam_im=s5_lambda_im, log_dt=s5_log_dt, b_re=s5_b_re, b_im=s5_b_im,
              c_re=s5_c_re, c_im=s5_c_im, d=s5_d, glu_w=s5_glu_w, glu_b=s5_glu_b)
    cv = dict(pw1_w=cv_pw1_w, pw1_b=cv_pw1_b, dw_w=cv_dw_w, dw_b=cv_dw_b, ln_g=cv_ln_g, ln_b=cv_ln_b,
              pw2_w=cv_pw2_w, pw2_b=cv_pw2_b)
    rw = dict(mu=rw_mu, w_r=rw_w_r, w_k=rw_w_k, w_v=rw_w_v, w_o=rw_w_o, w0=rw_w0, w1=rw_w1, w2=rw_w2,
              a0=rw_a0, a1=rw_a1, a2=rw_a2, g1=rw_g1, g2=rw_g2, k_k=rw_k_k, k_a=rw_k_a, r_k=rw_r_k,
              lnx_g=rw_lnx_g, lnx_b=rw_lnx_b)
    return _forward(x, c, ctx, c_ctx, ada_w, ada_b, norm_mix_g, norm_ffn_g, final_g, s5, cv, rw,
                    ff_w1, ff_w3, ff_w2, moe_router, moe_w1, moe_w3, moe_w2)
```

```python
import functools
import math

import jax
import jax.numpy as jnp
from jax import lax
from jax.experimental import pallas as pl
from jax.experimental.pallas import tpu as pltpu

F32 = jnp.float32
BF16 = jnp.bfloat16

NB = 8
LANES = 128
TM = 512
TT = TM // NB
GRID_W = 64
CONV_WIDTH = 31
CONV_PAD = CONV_WIDTH // 2
HALO_ROWS = 128
S5_GROUP = 16
S5_STATE = 64
S5_JBLK = 8
RWKV_HEAD = 64
WKV_L = 64
WKV_GROUP = 8
RW_TM = 256
RMS_EPS = 1e-6
LN_EPS = 1e-5
GN_EPS = 64e-5
KK_EPS = 1e-12
LAMBDA_RE_MAX = -1e-4
VMEM_LIMIT = 56 * 1024 * 1024


def _cparams(sem):
    return pltpu.CompilerParams(dimension_semantics=sem, vmem_limit_bytes=VMEM_LIMIT)


def _dot(a, b):
    return jnp.dot(a, b, preferred_element_type=F32)


def _dot_nt(a, b):
    return lax.dot_general(a, b, (((1,), (1,)), ((), ())), preferred_element_type=F32)


def _split3(x):
    h1 = x.astype(BF16)
    r1 = x - h1.astype(F32)
    h2 = r1.astype(BF16)
    h3 = (r1 - h2.astype(F32)).astype(BF16)
    return h1, h2, h3


def _dot_exact_rhs(x, m_bf16):
    h1, h2, h3 = _split3(x)
    return _dot(h1, m_bf16) + _dot(h2, m_bf16) + _dot(h3, m_bf16)


def _dot_exact_lhs(m_bf16, x):
    h1, h2, h3 = _split3(x)
    return _dot(m_bf16, h1) + _dot(m_bf16, h2) + _dot(m_bf16, h3)


def _dot_x3(a, b):
    a1 = a.astype(BF16)
    a2 = (a - a1.astype(F32)).astype(BF16)
    b1 = b.astype(BF16)
    b2 = (b - b1.astype(F32)).astype(BF16)
    return _dot(a1, b1) + _dot(a1, b2) + _dot(a2, b1)


def _sigmoid(x):
    return jax.nn.sigmoid(x)


def _norm_mod(x, g, shift, scale):
    rows, d = x.shape
    ms = jnp.mean(x * x, axis=-1, keepdims=True)
    y = x * lax.rsqrt(ms + RMS_EPS) * g
    y3 = y.reshape(rows // NB, NB, d)
    return (y3 * (1.0 + scale)[None] + shift[None]).reshape(rows, d)


def _residual(x, gate, y):
    rows, d = x.shape
    y3 = y.reshape(rows // NB, NB, d) * gate[None]
    return x + y3.reshape(rows, d)


def _is_lat(i, nct):
    return jnp.where(i >= nct, 1, 0).astype(jnp.int32)


def _ada_kernel(s_ref, w_ref, b_ref, o_ref):
    s = s_ref[...]
    s = s * _sigmoid(s)
    o_ref[0] = _dot(s.astype(BF16), w_ref[0].astype(BF16)) + b_ref[0]


def _ada_call(cvec, ada_w, ada_b):
    depth, d, d6 = ada_w.shape
    tn = 1536
    return pl.pallas_call(
        _ada_kernel,
        grid=(depth, d6 // tn),
        in_specs=[
            pl.BlockSpec((2 * NB, d), lambda l, j: (0, 0)),
            pl.BlockSpec((1, d, tn), lambda l, j: (l, 0, j)),
            pl.BlockSpec((1, 1, tn), lambda l, j: (l, 0, j)),
        ],
        out_specs=pl.BlockSpec((1, 2 * NB, tn), lambda l, j: (l, 0, j)),
        out_shape=jax.ShapeDtypeStruct((depth, 2 * NB, d6), F32),
        compiler_params=_cparams(("arbitrary", "arbitrary")),
    )(cvec, ada_w, ada_b.reshape(depth, 1, d6))


def _ffn_kernel(x_ref, mod_ref, g_ref, router_ref, w1_ref, w3_ref, w2_ref, o_ref,
                hb_ref, acc_ref, gate_ref, *, n_e, n_k, moe):
    e = pl.program_id(1)
    k = pl.program_id(2)

    @pl.when((e == 0) & (k == 0))
    def _():
        h = _norm_mod(x_ref[...], g_ref[...], mod_ref[0, 3], mod_ref[0, 4])
        hb_ref[...] = h.astype(BF16)
        acc_ref[...] = jnp.zeros_like(acc_ref)
        if moe:
            logits = _dot_x3(h, router_ref[...])
            lane = lax.broadcasted_iota(jnp.int32, logits.shape, 1).astype(F32)
            neg = jnp.float32(-3.0e38)
            lg = jnp.where(lane < n_e, logits, neg)
            m1 = jnp.max(lg, axis=-1, keepdims=True)
            i1 = jnp.min(jnp.where(lg == m1, lane, 1.0e9), axis=-1, keepdims=True)
            lg2 = jnp.where(lane == i1, neg, lg)
            m2 = jnp.max(lg2, axis=-1, keepdims=True)
            i2 = jnp.min(jnp.where(lg2 == m2, lane, 1.0e9), axis=-1, keepdims=True)
            e2 = jnp.exp(m2 - m1)
            den = 1.0 + e2
            gate_ref[...] = (jnp.where(lane == i1, 1.0 / den, 0.0)
                             + jnp.where(lane == i2, e2 / den, 0.0))

    hb = hb_ref[...]
    a = _dot(hb, w1_ref[0])
    b = _dot(hb, w3_ref[0])
    act = a * _sigmoid(a) * b
    if moe:
        lane = lax.broadcasted_iota(jnp.int32, gate_ref.shape, 1)
        ge = jnp.sum(jnp.where(lane == e, gate_ref[...], 0.0), axis=-1, keepdims=True)
        act = act * ge
    acc_ref[...] += _dot(act.astype(BF16), w2_ref[0])

    @pl.when((e == n_e - 1) & (k == n_k - 1))
    def _():
        o_ref[...] = _residual(x_ref[...], mod_ref[0, 5], acc_ref[...])


def _ffn_call(x, mod, g, router, w1, w3, w2, *, nct, tile0, moe):
    n, d = x.shape
    n_e, _, f = w1.shape
    tf = 512
    n_k = f // tf
    nt = n // TM
    kern = functools.partial(_ffn_kernel, n_e=n_e, n_k=n_k, moe=moe)
    return pl.pallas_call(
        kern,
        grid=(nt - tile0, n_e, n_k),
        in_specs=[
            pl.BlockSpec((TM, d), lambda i, e, k: (i + tile0, 0)),
            pl.BlockSpec((1, 6, NB, d), lambda i, e, k: (_is_lat(i + tile0, nct), 0, 0, 0)),
            pl.BlockSpec((1, d), lambda i, e, k: (0, 0)),
            pl.BlockSpec((d, LANES), lambda i, e, k: (0, 0)),
            pl.BlockSpec((1, d, tf), lambda i, e, k: (e, 0, k)),
            pl.BlockSpec((1, d, tf), lambda i, e, k: (e, 0, k)),
            pl.BlockSpec((1, tf, d), lambda i, e, k: (e, k, 0)),
        ],
        out_specs=pl.BlockSpec((TM, d), lambda i, e, k: (i + tile0, 0)),
        out_shape=jax.ShapeDtypeStruct((n, d), F32),
        scratch_shapes=[
            pltpu.VMEM((TM, d), BF16),
            pltpu.VMEM((TM, d), F32),
            pltpu.VMEM((TM, LANES), F32),
        ],
        compiler_params=_cparams(("arbitrary", "arbitrary", "arbitrary")),
    )(x, mod, g, router, w1, w3, w2)


def _dir_tile(d, s, nct, nt):
    bwd = jnp.where(s < nct, nct - 1 - s, nt - 1 + nct - s)
    return jnp.where(d == 0, s, bwd)


def _s5_scan_kernel(x_ref, mod_ref, g_ref, bb_ref, cc_ref, ab_ref, y_ref,
                    hb_ref, bu_ref, st_ref):
    d = pl.program_id(0)
    s = pl.program_id(1)
    half = S5_JBLK * S5_STATE

    @pl.when(s == 0)
    def _():
        st_ref[...] = jnp.zeros_like(st_ref)

    h = _norm_mod(x_ref[...], g_ref[...], mod_ref[0, 0], mod_ref[0, 1])
    hb_ref[...] = h.astype(BF16)

    for j in range(S5_JBLK):
        bu_ref[...] = _dot(hb_ref[:, j * LANES:(j + 1) * LANES], bb_ref[0, j])
        ar = ab_ref[0, j, 0]
        ai = ab_ref[0, j, 1]

        def body(t, carry):
            re, im = carry
            tt = t + d * (TT - 1 - 2 * t)
            r0 = pl.multiple_of(tt * NB, NB)
            bre = bu_ref[pl.ds(r0, NB), 0:half]
            bim = bu_ref[pl.ds(r0, NB), half:2 * half]
            nre = ar * re - ai * im + bre
            nim = ar * im + ai * re + bim
            bu_ref[pl.ds(r0, NB), 0:half] = nre
            bu_ref[pl.ds(r0, NB), half:2 * half] = nim
            return nre, nim

        re, im = lax.fori_loop(0, TT, body, (st_ref[j, 0], st_ref[j, 1]), unroll=4)
        st_ref[j, 0] = re
        st_ref[j, 1] = im
        y_ref[0, j] = _dot(bu_ref[...].astype(BF16), cc_ref[0, j])


def _s5_scan_call(x, mod, g, bb, cc, ab, *, nct):
    n, d = x.shape
    nt = n // TM
    half = S5_JBLK * S5_STATE
    tile = lambda dd, s: _dir_tile(dd, s, nct, nt)
    return pl.pallas_call(
        _s5_scan_kernel,
        grid=(2, nt),
        in_specs=[
            pl.BlockSpec((TM, d), lambda dd, s: (tile(dd, s), 0)),
            pl.BlockSpec((1, 6, NB, d), lambda dd, s: (_is_lat(tile(dd, s), nct), 0, 0, 0)),
            pl.BlockSpec((1, d), lambda dd, s: (0, 0)),
            pl.BlockSpec((1, S5_JBLK, LANES, 2 * half), lambda dd, s: (dd, 0, 0, 0)),
            pl.BlockSpec((1, S5_JBLK, 2 * half, LANES), lambda dd, s: (dd, 0, 0, 0)),
            pl.BlockSpec((1, S5_JBLK, 2, NB, half), lambda dd, s: (dd, 0, 0, 0, 0)),
        ],
        out_specs=pl.BlockSpec((1, S5_JBLK, TM, LANES), lambda dd, s: (dd, 0, tile(dd, s), 0)),
        out_shape=jax.ShapeDtypeStruct((2, S5_JBLK, n, LANES), F32),
        scratch_shapes=[
            pltpu.VMEM((TM, d), BF16),
            pltpu.VMEM((TM, 2 * half), F32),
            pltpu.VMEM((S5_JBLK, 2, NB, half), F32),
        ],
        compiler_params=_cparams(("arbitrary", "arbitrary")),
    )(x, mod, g, bb, cc, ab)


def _gelu_tanh(x):
    c = math.sqrt(2.0 / math.pi)
    return 0.5 * x * (1.0 + jnp.tanh(c * (x + 0.044715 * (x * x * x))))


def _s5_glu_kernel(x_ref, mod_ref, g_ref, y_ref, dsk_ref, w_ref, b_ref, o_ref):
    x = x_ref[...]
    d = x.shape[1]
    h = _norm_mod(x, g_ref[...], mod_ref[0, 0], mod_ref[0, 1])
    y = jnp.concatenate([y_ref[0, j] + y_ref[1, j] for j in range(S5_JBLK)], axis=1)
    z = _gelu_tanh(y + dsk_ref[...] * h).astype(BF16)
    zz = _dot(z, w_ref[...]) + b_ref[...]
    out = zz[:, :d] * _sigmoid(zz[:, d:])
    o_ref[...] = _residual(x, mod_ref[0, 2], out)


def _s5_glu_call(x, mod, g, y, dsk, w, b, *, nct, tile0):
    n, d = x.shape
    nt = n // TM
    return pl.pallas_call(
        _s5_glu_kernel,
        grid=(nt - tile0,),
        in_specs=[
            pl.BlockSpec((TM, d), lambda i: (i + tile0, 0)),
            pl.BlockSpec((1, 6, NB, d), lambda i: (_is_lat(i + tile0, nct), 0, 0, 0)),
            pl.BlockSpec((1, d), lambda i: (0, 0)),
            pl.BlockSpec((2, S5_JBLK, TM, LANES), lambda i: (0, 0, i + tile0, 0)),
            pl.BlockSpec((1, d), lambda i: (0, 0)),
            pl.BlockSpec((d, 2 * d), lambda i: (0, 0)),
            pl.BlockSpec((1, 2 * d), lambda i: (0, 0)),
        ],
        out_specs=pl.BlockSpec((TM, d), lambda i: (i + tile0, 0)),
        out_shape=jax.ShapeDtypeStruct((n, d), F32),
        compiler_params=_cparams(("arbitrary",)),
    )(x, mod, g, y, dsk, w, b)


def _s5_params(lam_re, lam_im, log_dt, b_re, b_im, c_re, c_im):
    lam_re = jnp.minimum(lam_re.astype(F32), LAMBDA_RE_MAX)
    lam_im = lam_im.astype(F32)
    dt = jnp.exp(log_dt.astype(F32))[..., None]
    mag = jnp.exp(lam_re * dt)
    abar_re = mag * jnp.cos(lam_im * dt)
    abar_im = mag * jnp.sin(lam_im * dt)
    inv_den = 1.0 / (lam_re * lam_re + lam_im * lam_im)
    gam_re = ((abar_re - 1.0) * lam_re + abar_im * lam_im) * inv_den
    gam_im = (abar_im * lam_re - (abar_re - 1.0) * lam_im) * inv_den
    b_re = b_re.astype(F32)
    b_im = b_im.astype(F32)
    bb_re = gam_re[..., None] * b_re - gam_im[..., None] * b_im
    bb_im = gam_re[..., None] * b_im + gam_im[..., None] * b_re
    g_all = lam_re.shape[1]
    nj = g_all // S5_JBLK
    eye = jnp.eye(S5_JBLK, dtype=F32)

    def in_mat(m):
        m = m.reshape(2, nj, S5_JBLK, S5_STATE, S5_GROUP)
        m = jnp.einsum('djgnp,gh->djgphn', m, eye)
        return m.reshape(2, nj, S5_JBLK * S5_GROUP, S5_JBLK * S5_STATE)

    def out_mat(m):
        m = m.reshape(2, nj, S5_JBLK, S5_GROUP, S5_STATE)
        m = jnp.einsum('djgpn,gh->djgnhp', m, eye)
        return m.reshape(2, nj, S5_JBLK * S5_STATE, S5_JBLK * S5_GROUP)

    bb = jnp.concatenate([in_mat(bb_re), in_mat(bb_im)], axis=-1).astype(BF16)
    cc = jnp.concatenate([out_mat(c_re.astype(F32)), -out_mat(c_im.astype(F32))], axis=-2).astype(BF16)
    half = S5_JBLK * S5_STATE
    ab = jnp.stack([abar_re.reshape(2, nj, half), abar_im.reshape(2, nj, half)], axis=2)
    ab = jnp.broadcast_to(ab[:, :, :, None, :], (2, nj, 2, NB, half))
    return bb, cc, ab


def _cv_glu_kernel(x_ref, mod_ref, g_ref, w_ref, b_ref, o_ref):
    x = x_ref[...]
    d = x.shape[1]
    h = _norm_mod(x, g_ref[...], mod_ref[0, 0], mod_ref[0, 1]).astype(BF16)
    u = _dot(h, w_ref[...]) + b_ref[...]
    o_ref[...] = u[:, :d] * _sigmoid(u[:, d:])


def _cv_glu_call(x, mod, g, w, b, *, nct):
    n, d = x.shape
    nt = n // TM
    return pl.pallas_call(
        _cv_glu_kernel,
        grid=(nt,),
        in_specs=[
            pl.BlockSpec((TM, d), lambda i: (i, 0)),
            pl.BlockSpec((1, 6, NB, d), lambda i: (_is_lat(i, nct), 0, 0, 0)),
            pl.BlockSpec((1, d), lambda i: (0, 0)),
            pl.BlockSpec((d, 2 * d), lambda i: (0, 0)),
            pl.BlockSpec((1, 2 * d), lambda i: (0, 0)),
        ],
        out_specs=pl.BlockSpec((TM, d), lambda i: (i, 0)),
        out_shape=jax.ShapeDtypeStruct((n, d), F32),
        compiler_params=_cparams(("arbitrary",)),
    )(x, mod, g, w, b)


def _cv_conv_kernel(u_ref, up_ref, un_ref, x_ref, mod_ref, dw_ref, dwb_ref, lng_ref, lnb_ref,
                    w_ref, b_ref, o_ref, pad_ref, cv_ref, *, nct):
    i = pl.program_id(0)
    halo = CONV_PAD * NB
    prev_ok = (i > 0) & (i < nct)
    next_ok = i < nct - 1
    pad_ref[0:halo] = jnp.where(prev_ok, up_ref[HALO_ROWS - halo:HALO_ROWS], 0.0)
    pad_ref[halo:halo + TM] = u_ref[...]
    pad_ref[halo + TM:halo + TM + halo] = jnp.where(next_ok, un_ref[0:halo], 0.0)

    rb = 32

    def body(c, _):
        r0 = pl.multiple_of(c * rb, rb)
        acc = jnp.broadcast_to(dwb_ref[...], (rb, dwb_ref.shape[1]))
        for k in range(CONV_WIDTH):
            acc = acc + dw_ref[k] * pad_ref[pl.ds(r0 + k * NB, rb)]
        cv_ref[pl.ds(r0, rb)] = acc
        return 0

    lax.fori_loop(0, TM // rb, body, 0)

    c = cv_ref[...]
    mu = jnp.mean(c, axis=-1, keepdims=True)
    cc = c - mu
    var = jnp.mean(cc * cc, axis=-1, keepdims=True)
    y = cc * lax.rsqrt(var + LN_EPS) * lng_ref[...] + lnb_ref[...]
    y = (y * _sigmoid(y)).astype(BF16)
    out = _dot(y, w_ref[...]) + b_ref[...]
    o_ref[...] = _residual(x_ref[...], mod_ref[0, 2], out)


def _cv_conv_call(u, x, mod, dw_w, dw_b, ln_g, ln_b, w, b, *, nct):
    n, d = x.shape
    nt = n // TM
    hb = TM // HALO_ROWS
    nhb = n // HALO_ROWS
    kern = functools.partial(_cv_conv_kernel, nct=nct)
    row = lambda i: (0, 0)
    return pl.pallas_call(
        kern,
        grid=(nt,),
        in_specs=[
            pl.BlockSpec((TM, d), lambda i: (i, 0)),
            pl.BlockSpec((HALO_ROWS, d), lambda i: (jnp.maximum(i * hb - 1, 0), 0)),
            pl.BlockSpec((HALO_ROWS, d), lambda i: (jnp.minimum((i + 1) * hb, nhb - 1), 0)),
            pl.BlockSpec((TM, d), lambda i: (i, 0)),
            pl.BlockSpec((1, 6, NB, d), lambda i: (_is_lat(i, nct), 0, 0, 0)),
            pl.BlockSpec((CONV_WIDTH, 1, d), lambda i: (0, 0, 0)),
            pl.BlockSpec((1, d), row),
            pl.BlockSpec((1, d), row),
            pl.BlockSpec((1, d), row),
            pl.BlockSpec((d, d), row),
            pl.BlockSpec((1, d), row),
        ],
        out_specs=pl.BlockSpec((TM, d), lambda i: (i, 0)),
        out_shape=jax.ShapeDtypeStruct((n, d), F32),
        scratch_shapes=[
            pltpu.VMEM((TM + 2 * CONV_PAD * NB, d), F32),
            pltpu.VMEM((TM, d), F32),
        ],
        compiler_params=_cparams(("arbitrary",)),
    )(u, u, u, x, mod, dw_w.reshape(CONV_WIDTH, 1, d), dw_b, ln_g, ln_b, w, b)


def _seg_mats(d):
    ch = lax.broadcasted_iota(jnp.int32, (d, LANES), 0) // RWKV_HEAD
    hd = lax.broadcasted_iota(jnp.int32, (d, LANES), 1)
    seg = jnp.where(ch == hd, 1.0, 0.0).astype(BF16)
    hd_t = lax.broadcasted_iota(jnp.int32, (LANES, d), 0)
    ch_t = lax.broadcasted_iota(jnp.int32, (LANES, d), 1) // RWKV_HEAD
    seg_t = jnp.where(ch_t == hd_t, 1.0, 0.0).astype(BF16)
    return seg, seg_t


def _head_sum_bcast(x, seg, seg_t):
    return _dot_exact_rhs(_dot_exact_rhs(x, seg), seg_t)


def _rw_proj_kernel(x_ref, xp_ref, xn_ref, mod_ref, g_ref, mu_ref, wr_ref, wk_ref, wv_ref,
                    w1_ref, w2_ref, w0_ref, a1_ref, a2_ref, a0_ref, g1_ref, g2_ref, kkw_ref,
                    r_ref, k_ref, v_ref, kk_ref, ew_ref, as_ref, gg_ref, hp_ref, *, nct, nt):
    i = pl.program_id(0)
    tm, d = x_ref.shape
    g = g_ref[...]
    sh = mod_ref[0, 0]
    sc = mod_ref[0, 1]
    h = _norm_mod(x_ref[...], g, sh, sc)
    hp = _norm_mod(xp_ref[...], g, sh, sc)
    hn = _norm_mod(xn_ref[...], g, sh, sc)
    prev_ok = (i != 0) & (i != nct)
    next_ok = (i != nct - 1) & (i != nt - 1)
    hp_ref[0:NB] = jnp.where(prev_ok, hp, 0.0)
    hp_ref[NB:NB + tm] = h
    hp_ref[NB + tm:NB + tm + NB] = jnp.where(next_ok, hn, 0.0)
    xx = 0.5 * (hp_ref[0:tm] + hp_ref[2 * NB:2 * NB + tm]) - h

    def mix(j):
        return (h + xx * mu_ref[j]).astype(BF16)

    r = _dot(mix(0), wr_ref[...])
    k = _dot(mix(2), wk_ref[...])
    v = _dot(mix(3), wv_ref[...])
    zw = w0_ref[...] + _dot(jnp.tanh(_dot(mix(1), w1_ref[...])).astype(BF16), w2_ref[...])
    ew = _sigmoid(zw) * math.exp(-0.5)
    asg = _sigmoid(a0_ref[...] + _dot(_dot(mix(4), a1_ref[...]).astype(BF16), a2_ref[...]))
    gg_ref[...] = _dot(_sigmoid(_dot(mix(5), g1_ref[...])).astype(BF16), g2_ref[...])

    seg, seg_t = _seg_mats(d)
    kk = k * kkw_ref[...]
    kk = kk * lax.rsqrt(_head_sum_bcast(kk * kk, seg, seg_t) + KK_EPS)

    for p in range(d // LANES):
        sl = slice(p * LANES, (p + 1) * LANES)
        r_ref[p] = r[:, sl]
        k_ref[p] = k[:, sl]
        v_ref[p] = v[:, sl]
        kk_ref[p] = kk[:, sl]
        for dr in range(2):
            sl2 = slice(dr * d + p * LANES, dr * d + (p + 1) * LANES)
            ew_ref[dr, p] = ew[:, sl2]
            as_ref[dr, p] = asg[:, sl2]


def _rw_proj_call(x, mod, g, mu, wr, wk, wv, w1, w2, w0, a1, a2, a0, g1, g2, kkw, *, nct):
    n, d = x.shape
    tm = RW_TM
    nt = n // tm
    nct = nct * (TM // tm)
    npair = d // LANES
    tb = tm // NB
    nb8 = n // NB
    kern = functools.partial(_rw_proj_kernel, nct=nct, nt=nt)
    full2 = lambda a: pl.BlockSpec(a.shape, lambda i: (0, 0))
    slab = pl.BlockSpec((npair, tm, LANES), lambda i: (0, i, 0))
    slab2 = pl.BlockSpec((2, npair, tm, LANES), lambda i: (0, 0, i, 0))
    slab_shape = jax.ShapeDtypeStruct((npair, n, LANES), F32)
    slab2_shape = jax.ShapeDtypeStruct((2, npair, n, LANES), F32)
    mu3 = mu.reshape(6, 1, d)
    return pl.pallas_call(
        kern,
        grid=(nt,),
        in_specs=[
            pl.BlockSpec((tm, d), lambda i: (i, 0)),
            pl.BlockSpec((NB, d), lambda i: (jnp.maximum(i * tb - 1, 0), 0)),
            pl.BlockSpec((NB, d), lambda i: (jnp.minimum((i + 1) * tb, nb8 - 1), 0)),
            pl.BlockSpec((1, 6, NB, d), lambda i: (_is_lat(i, nct), 0, 0, 0)),
            pl.BlockSpec((1, d), lambda i: (0, 0)),
            pl.BlockSpec((6, 1, d), lambda i: (0, 0, 0)),
            full2(wr), full2(wk), full2(wv),
            full2(w1), full2(w2), full2(w0), full2(a1), full2(a2), full2(a0),
            full2(g1), full2(g2), full2(kkw),
        ],
        out_specs=[slab, slab, slab, slab, slab2, slab2, pl.BlockSpec((tm, d), lambda i: (i, 0))],
        out_shape=[slab_shape, slab_shape, slab_shape, slab_shape, slab2_shape, slab2_shape,
                   jax.ShapeDtypeStruct((n, d), F32)],
        scratch_shapes=[pltpu.VMEM((tm + 2 * NB, d), F32)],
        compiler_params=_cparams(("arbitrary",)),
    )(x, x, x, mod, g, mu3, wr, wk, wv, w1, w2, w0, a1, a2, a0, g1, g2, kkw)


def _wkv_kernel(r_ref, k_ref, v_ref, kk_ref, ew_ref, as_ref, ka_ref, y_ref, st_ref, *, L, npair):
    d = pl.program_id(0)
    c = pl.program_id(1)

    @pl.when(c == 0)
    def _():
        st_ref[...] = jnp.zeros_like(st_ref)

    sgn = 1 - 2 * d
    l2 = 2 * L
    rr = lax.broadcasted_iota(jnp.int32, (l2, l2), 0)
    cc = lax.broadcasted_iota(jnp.int32, (l2, l2), 1)
    same = (rr >= L) == (cc >= L)
    diff = (rr - cc) * sgn
    strict_f = jnp.where(same & (diff > 0), 1.0, 0.0)
    incl_f = jnp.where(same & (diff >= 0), 1.0, 0.0)
    r1 = lax.broadcasted_iota(jnp.int32, (L, L), 0)
    c1 = lax.broadcasted_iota(jnp.int32, (L, L), 1)
    mcum = jnp.where((r1 - c1) * sgn >= 0, 1.0, 0.0).astype(BF16)
    head0 = lax.broadcasted_iota(jnp.int32, (L, LANES), 1) < RWKV_HEAD
    sr_i = lax.broadcasted_iota(jnp.int32, (LANES, LANES), 0)
    sc_i = lax.broadcasted_iota(jnp.int32, (LANES, LANES), 1)
    blockdiag = (sr_i < RWKV_HEAD) == (sc_i < RWKV_HEAD)
    eye = jnp.where(sr_i == sc_i, 1.0, 0.0)
    n_dbl = int(round(math.log2(L)))

    def stack2(x):
        return jnp.concatenate([jnp.where(head0, x, 0.0), jnp.where(head0, 0.0, x)], axis=0)

    def unstack(xs):
        return xs[0:L] + xs[L:l2]

    def load(p, b):
        rows = pl.ds(b, L, stride=NB)
        return (r_ref[p, rows, :], k_ref[p, rows, :], v_ref[p, rows, :], kk_ref[p, rows, :],
                ew_ref[0, p, rows, :], as_ref[0, p, rows, :], st_ref[p * NB + b])

    def inst(ka, r, k, v, kk, ew, asg, s0):
        ldec = -ew
        h1, h2, h3 = _split3(ldec)
        cl3 = _dot(mcum, jnp.concatenate([h1, h2, h3], axis=1))
        yield
        cl = cl3[:, 0:LANES] + cl3[:, LANES:2 * LANES] + cl3[:, 2 * LANES:3 * LANES]
        ce = cl - ldec
        ecl = jnp.exp(cl)
        encl = jnp.exp(-cl)
        etot = jnp.exp(jnp.sum(ldec, axis=0, keepdims=True))
        at = -kk * jnp.exp(ce)
        rt = r * ecl
        bt = kk * asg * encl
        kt = k * (1.0 + (asg - 1.0) * ka) * encl

        ats = stack2(at)
        vsb = stack2(v).astype(BF16)
        lhs = jnp.concatenate([ats, stack2(rt)], axis=0).astype(BF16)
        rhs = jnp.concatenate([stack2(bt), stack2(kt)], axis=0).astype(BF16)
        pm = _dot_nt(lhs, rhs)
        yield
        nab = pm[0:l2, 0:l2] * strict_f
        nak = pm[0:l2, l2:2 * l2] * strict_f
        mrb = (pm[l2:2 * l2, 0:l2] * incl_f).astype(BF16)
        mrk = (pm[l2:2 * l2, l2:2 * l2] * incl_f).astype(BF16)

        xs = jnp.concatenate([ats, _dot(nak.astype(BF16), vsb)], axis=1)
        mkv = _dot(mrk, vsb)
        yield
        npow = nab
        for it in range(n_dbl):
            npb = npow.astype(BF16)
            xs = xs + _dot(npb, xs.astype(BF16))
            if it + 1 < n_dbl:
                npow = _dot(npb, npb)
            yield
        mx = _dot(mrb, xs.astype(BF16))
        at2 = unstack(xs[:, 0:LANES])
        ul = unstack(xs[:, LANES:2 * LANES])
        btb = bt.astype(BF16)
        gmat = (eye + jnp.where(blockdiag, _dot(at2.T.astype(BF16), btb), 0.0)) * etot
        uv = jnp.concatenate([ul, v], axis=0)
        bk = jnp.concatenate([btb, kt.astype(BF16)], axis=0)
        dmat = jnp.where(blockdiag, _dot(uv.T.astype(BF16), bk), 0.0) * etot
        yield
        rt2 = rt + unstack(mx[:, 0:LANES])
        yl = unstack(mx[:, LANES:2 * LANES] + mkv)

        return _dot_nt(rt2.astype(BF16), s0.astype(BF16)) + yl, _dot_x3(s0, gmat) + dmat

    def run_group(gens):
        outs = [None] * len(gens)
        live = list(enumerate(gens))
        while live:
            nxt = []
            for i, g in live:
                try:
                    next(g)
                    nxt.append((i, g))
                except StopIteration as stop:
                    outs[i] = stop.value
            live = nxt
        return outs

    def pair(p, carry):
        ka = ka_ref[p]
        for b0 in range(0, NB, WKV_GROUP):
            outs = run_group([inst(ka, *load(p, b0 + g)) for g in range(WKV_GROUP)])
            for g, (y, s1) in enumerate(outs):
                y_ref[0, p, pl.ds(b0 + g, L, stride=NB), :] = y
                st_ref[p * NB + b0 + g] = s1
        return carry

    lax.fori_loop(0, npair, pair, 0)


def _wkv_call(r, k, v, kk, ew, asg, ka, *, nct_rows):
    npair, n, _ = r.shape
    L = WKV_L
    rows = L * NB
    nc = n // rows
    ncc = nct_rows // rows
    kern = functools.partial(_wkv_kernel, L=L, npair=npair)

    def chunk(dd, c):
        bwd = jnp.where(c < ncc, ncc - 1 - c, nc - 1 + ncc - c)
        return jnp.where(dd == 0, c, bwd)

    slab = pl.BlockSpec((npair, rows, LANES), lambda dd, c: (0, chunk(dd, c), 0))
    slab2 = pl.BlockSpec((1, npair, rows, LANES), lambda dd, c: (dd, 0, chunk(dd, c), 0))
    return pl.pallas_call(
        kern,
        grid=(2, nc),
        in_specs=[slab, slab, slab, slab, slab2, slab2,
                  pl.BlockSpec((npair, 1, LANES), lambda dd, c: (0, 0, 0))],
        out_specs=slab2,
        out_shape=jax.ShapeDtypeStruct((2, npair, n, LANES), F32),
        scratch_shapes=[pltpu.VMEM((npair * NB, LANES, LANES), F32)],
        compiler_params=_cparams(("arbitrary", "arbitrary")),
    )(r, k, v, kk, ew, asg, ka)


def _rw_out_kernel(x_ref, mod_ref, y_ref, r_ref, k_ref, v_ref, as_ref, gg_ref, lng_ref, lnb_ref,
                   ka_ref, rk_ref, wo_ref, o_ref):
    x = x_ref[...]
    d = x.shape[1]
    npair = d // LANES
    cat = lambda f: jnp.concatenate([f(p) for p in range(npair)], axis=1)
    y = cat(lambda p: y_ref[0, p] + y_ref[1, p])
    r = cat(lambda p: r_ref[p])
    k = cat(lambda p: k_ref[p])
    v = cat(lambda p: v_ref[p])
    am = cat(lambda p: 0.5 * (as_ref[0, p] + as_ref[1, p]))
    seg, seg_t = _seg_mats(d)
    inv = 1.0 / RWKV_HEAD
    mean = _head_sum_bcast(y, seg, seg_t) * inv
    yc = y - mean
    var = _head_sum_bcast(yc * yc, seg, seg_t) * inv
    yn = yc * lax.rsqrt(var + GN_EPS) * lng_ref[...] + lnb_ref[...]
    k_mean = k * (1.0 + (am - 1.0) * ka_ref[...])
    bonus = _head_sum_bcast(r * k_mean * rk_ref[...], seg, seg_t) * v
    z = ((yn + bonus) * gg_ref[...]).astype(BF16)
    o_ref[...] = _residual(x, mod_ref[0, 2], _dot(z, wo_ref[...]))


def _rw_out_call(x, mod, y, r, k, v, asg, gg, lng, lnb, ka, rk, wo, *, nct):
    n, d = x.shape
    nt = n // TM
    npair = d // LANES
    slab = pl.BlockSpec((npair, TM, LANES), lambda i: (0, i, 0))
    slab2 = pl.BlockSpec((2, npair, TM, LANES), lambda i: (0, 0, i, 0))
    row = pl.BlockSpec((1, d), lambda i: (0, 0))
    return pl.pallas_call(
        _rw_out_kernel,
        grid=(nt,),
        in_specs=[
            pl.BlockSpec((TM, d), lambda i: (i, 0)),
            pl.BlockSpec((1, 6, NB, d), lambda i: (_is_lat(i, nct), 0, 0, 0)),
            slab2, slab, slab, slab, slab2,
            pl.BlockSpec((TM, d), lambda i: (i, 0)),
            row, row, row, row,
            pl.BlockSpec((d, d), lambda i: (0, 0)),
        ],
        out_specs=pl.BlockSpec((TM, d), lambda i: (i, 0)),
        out_shape=jax.ShapeDtypeStruct((n, d), F32),
        compiler_params=_cparams(("arbitrary",)),
    )(x, mod, y, r, k, v, asg, gg, lng, lnb, ka, rk, wo)


def _final_kernel(x_ref, g_ref, o_ref):
    x = x_ref[...]
    ms = jnp.mean(x * x, axis=-1, keepdims=True)
    o_ref[...] = x * lax.rsqrt(ms + RMS_EPS) * g_ref[...]


def _final_call(x, g, *, tile0):
    n, d = x.shape
    nt = n // TM
    return pl.pallas_call(
        _final_kernel,
        grid=(nt - tile0,),
        in_specs=[pl.BlockSpec((TM, d), lambda i: (i + tile0, 0)),
                  pl.BlockSpec((1, d), lambda i: (0, 0))],
        out_specs=pl.BlockSpec((TM, d), lambda i: (i, 0)),
        out_shape=jax.ShapeDtypeStruct((n - tile0 * TM, d), F32),
        compiler_params=_cparams(("arbitrary",)),
    )(x, g)


def _row(v):
    return v.reshape(1, -1).astype(F32)


def _s5_layer(x, mod, g, p, *, nct, tile0):
    bb, cc, ab = _s5_params(p['lam_re'], p['lam_im'], p['log_dt'], p['b_re'], p['b_im'],
                            p['c_re'], p['c_im'])
    y = _s5_scan_call(x, mod, g, bb, cc, ab, nct=nct)
    return _s5_glu_call(x, mod, g, y, _row(p['d']), p['glu_w'].astype(BF16), _row(p['glu_b']),
                        nct=nct, tile0=tile0)


def _conformer_layer(x, mod, g, p, *, nct):
    u = _cv_glu_call(x, mod, g, p['pw1_w'].astype(BF16), _row(p['pw1_b']), nct=nct)
    return _cv_conv_call(u, x, mod, p['dw_w'].astype(F32), _row(p['dw_b']), _row(p['ln_g']),
                         _row(p['ln_b']), p['pw2_w'].astype(BF16), _row(p['pw2_b']), nct=nct)


def _blockdiag2(m):
    z = jnp.zeros_like(m[0])
    return jnp.concatenate([jnp.concatenate([m[0], z], axis=1),
                            jnp.concatenate([z, m[1]], axis=1)], axis=0)


def _rwkv_layer(x, mod, g, p, *, nct):
    n, d = x.shape
    npair = d // LANES
    w1 = jnp.concatenate([p['w1'][0], p['w1'][1]], axis=1).astype(BF16)
    w2 = _blockdiag2(p['w2']).astype(BF16)
    a1 = jnp.concatenate([p['a1'][0], p['a1'][1]], axis=1).astype(BF16)
    a2 = _blockdiag2(p['a2']).astype(BF16)
    gl = p['g1'].shape[1]
    glp = -(-gl // LANES) * LANES
    g1 = jnp.pad(p['g1'], ((0, 0), (0, glp - gl))).astype(BF16)
    g2 = jnp.pad(p['g2'], ((0, glp - gl), (0, 0))).astype(BF16)
    r, k, v, kk, ew, asg, gg = _rw_proj_call(
        x, mod, g, p['mu'].astype(F32), p['w_r'].astype(BF16), p['w_k'].astype(BF16),
        p['w_v'].astype(BF16), w1, w2, _row(p['w0']), a1, a2, _row(p['a0']), g1, g2,
        _row(p['k_k']), nct=nct)
    ka_slab = p['k_a'].astype(F32).reshape(npair, 1, LANES)
    y = _wkv_call(r, k, v, kk, ew, asg, ka_slab, nct_rows=nct * TM)
    return _rw_out_call(x, mod, y, r, k, v, asg, gg, _row(p['lnx_g']), _row(p['lnx_b']),
                        _row(p['k_a']), _row(p['r_k']), p['w_o'].astype(BF16), nct=nct)


def _forward(x, c, ctx, c_ctx, ada_w, ada_b, norm_mix_g, norm_ffn_g, final_g, s5, cv, rw,
             ff_w1, ff_w3, ff_w2, moe_router, moe_w1, moe_w3, moe_w2):
    bsz, seq, d = x.shape
    lc = ctx.shape[1]
    depth = ada_w.shape[0]
    assert bsz == NB and lc % TT == 0 and seq % TT == 0
    assert TT == GRID_W, "one token tile must hold exactly one latent raster row per batch entry"
    nct = lc // TT
    t_all = lc + seq
    n = t_all * NB

    xs = jnp.concatenate([jnp.swapaxes(ctx, 0, 1), jnp.swapaxes(x, 0, 1)], axis=0)
    xs = xs.reshape(n, d).astype(F32)

    cvec = jnp.zeros((2 * NB, d), F32).at[:NB].set(c).at[NB].set(c_ctx)
    m_all = _ada_call(cvec, ada_w.astype(F32), ada_b.astype(F32))
    m_all = m_all.reshape(depth, 2 * NB, 6, d)
    m_lat = jnp.swapaxes(m_all[:, :NB], 1, 2)
    m_ctx = jnp.broadcast_to(m_all[:, NB][:, :, None, :], (depth, 6, NB, d))
    mods = jnp.stack([m_ctx, m_lat], axis=1)

    zero_router = jnp.zeros((d, LANES), F32)
    for i in range(depth):
        last = i == depth - 1
        tile0 = nct if last else 0
        mod = mods[i]
        g_mix = _row(norm_mix_g[i])
        j = i // 3
        if i % 3 == 0:
            xs = _s5_layer(xs, mod, g_mix, {k_: v_[j] for k_, v_ in s5.items()}, nct=nct, tile0=tile0)
        elif i % 3 == 1:
            xs = _conformer_layer(xs, mod, g_mix, {k_: v_[j] for k_, v_ in cv.items()}, nct=nct)
        else:
            xs = _rwkv_layer(xs, mod, g_mix, {k_: v_[j] for k_, v_ in rw.items()}, nct=nct)
        fi = i // 2
        g_ffn = _row(norm_ffn_g[i])
        if i % 2 == 0:
            xs = _ffn_call(xs, mod, g_ffn, zero_router, ff_w1[fi][None].astype(BF16),
                           ff_w3[fi][None].astype(BF16), ff_w2[fi][None].astype(BF16),
                           nct=nct, tile0=tile0, moe=False)
        else:
            n_e = moe_router.shape[-1]
            router = jnp.pad(moe_router[fi].astype(F32), ((0, 0), (0, LANES - n_e)))
            xs = _ffn_call(xs, mod, g_ffn, router, moe_w1[fi].astype(BF16), moe_w3[fi].astype(BF16),
                           moe_w2[fi].astype(BF16), nct=nct, tile0=tile0, moe=True)

    out = _final_call(xs, _row(final_g), tile0=nct)
    return jnp.swapaxes(out.reshape(seq, NB, d), 0, 1)


def kernel(x, c, ctx, c_ctx, ada_w, ada_b, norm_mix_g, norm_ffn_g, final_g, s5_lambda_re, s5_lambda_im, s5_log_dt, s5_b_re, s5_b_im, s5_c_re, s5_c_im, s5_d, s5_glu_w, s5_glu_b, cv_pw1_w, cv_pw1_b, cv_dw_w, cv_dw_b, cv_ln_g, cv_ln_b, cv_pw2_w, cv_pw2_b, rw_mu, rw_w_r, rw_w_k, rw_w_v, rw_w_o, rw_w0, rw_w1, rw_w2, rw_a0, rw_a1, rw_a2, rw_g1, rw_g2, rw_k_k, rw_k_a, rw_r_k, rw_lnx_g, rw_lnx_b, ff_w1, ff_w3, ff_w2, moe_router, moe_w1, moe_w3, moe_w2):
    s5 = dict(lam_re=s5_lambda_re, lam_im=s5_lambda_im, log_dt=s5_log_dt, b_re=s5_b_re, b_im=s5_b_im,
              c_re=s5_c_re, c_im=s5_c_im, d=s5_d, glu_w=s5_glu_w, glu_b=s5_glu_b)
    cv = dict(pw1_w=cv_pw1_w, pw1_b=cv_pw1_b, dw_w=cv_dw_w, dw_b=cv_dw_b, ln_g=cv_ln_g, ln_b=cv_ln_b,
              pw2_w=cv_pw2_w, pw2_b=cv_pw2_b)
    rw = dict(mu=rw_mu, w_r=rw_w_r, w_k=rw_w_k, w_v=rw_w_v, w_o=rw_w_o, w0=rw_w0, w1=rw_w1, w2=rw_w2,
              a0=rw_a0, a1=rw_a1, a2=rw_a2, g1=rw_g1, g2=rw_g2, k_k=rw_k_k, k_a=rw_k_a, r_k=rw_r_k,
              lnx_g=rw_lnx_g, lnx_b=rw_lnx_b)
    return _forward(x, c, ctx, c_ctx, ada_w, ada_b, norm_mix_g, norm_ffn_g, final_g, s5, cv, rw,
                    ff_w1, ff_w3, ff_w2, moe_router, moe_w1, moe_w3, moe_w2)
```

```python
import functools
import math

import jax
import jax.numpy as jnp
from jax import lax
from jax.experimental import pallas as pl
from jax.experimental.pallas import tpu as pltpu

F32 = jnp.float32
BF16 = jnp.bfloat16

NB = 8
LANES = 128
TM = 512
TT = TM // NB
GRID_W = 64
CONV_WIDTH = 31
CONV_PAD = CONV_WIDTH // 2
HALO_ROWS = 128
S5_GROUP = 16
S5_STATE = 64
S5_JBLK = 8
RWKV_HEAD = 64
WKV_L = 64
WKV_GROUP = 8
RW_TM = 256
FFN_TM = 1024
MOE_T = 1024
MOE_BLK = 64
MOE_SLOTS = 16
RMS_EPS = 1e-6
LN_EPS = 1e-5
GN_EPS = 64e-5
KK_EPS = 1e-12
LAMBDA_RE_MAX = -1e-4
VMEM_LIMIT = 56 * 1024 * 1024


def _cparams(sem):
    return pltpu.CompilerParams(dimension_semantics=sem, vmem_limit_bytes=VMEM_LIMIT)


def _dot(a, b):
    return jnp.dot(a, b, preferred_element_type=F32)


def _dot_nt(a, b):
    return lax.dot_general(a, b, (((1,), (1,)), ((), ())), preferred_element_type=F32)


def _split3(x):
    h1 = x.astype(BF16)
    r1 = x - h1.astype(F32)
    h2 = r1.astype(BF16)
    h3 = (r1 - h2.astype(F32)).astype(BF16)
    return h1, h2, h3


def _dot_exact_rhs(x, m_bf16):
    h1, h2, h3 = _split3(x)
    return _dot(h1, m_bf16) + _dot(h2, m_bf16) + _dot(h3, m_bf16)


def _dot_exact_lhs(m_bf16, x):
    h1, h2, h3 = _split3(x)
    return _dot(m_bf16, h1) + _dot(m_bf16, h2) + _dot(m_bf16, h3)


def _dot_x3(a, b):
    a1 = a.astype(BF16)
    a2 = (a - a1.astype(F32)).astype(BF16)
    b1 = b.astype(BF16)
    b2 = (b - b1.astype(F32)).astype(BF16)
    return _dot(a1, b1) + _dot(a1, b2) + _dot(a2, b1)


def _sigmoid(x):
    return jax.nn.sigmoid(x)


def _norm_mod(x, g, shift, scale):
    rows, d = x.shape
    ms = jnp.mean(x * x, axis=-1, keepdims=True)
    y = x * lax.rsqrt(ms + RMS_EPS) * g
    y3 = y.reshape(rows // NB, NB, d)
    return (y3 * (1.0 + scale)[None] + shift[None]).reshape(rows, d)


def _residual(x, gate, y):
    rows, d = x.shape
    y3 = y.reshape(rows // NB, NB, d) * gate[None]
    return x + y3.reshape(rows, d)


def _is_lat(i, nct):
    return jnp.where(i >= nct, 1, 0).astype(jnp.int32)


def _ada_kernel(s_ref, w_ref, b_ref, o_ref):
    s = s_ref[...]
    s = s * _sigmoid(s)
    o_ref[0] = _dot(s.astype(BF16), w_ref[0].astype(BF16)) + b_ref[0]


def _ada_call(cvec, ada_w, ada_b):
    depth, d, d6 = ada_w.shape
    tn = 1536
    return pl.pallas_call(
        _ada_kernel,
        grid=(depth, d6 // tn),
        in_specs=[
            pl.BlockSpec((2 * NB, d), lambda l, j: (0, 0)),
            pl.BlockSpec((1, d, tn), lambda l, j: (l, 0, j)),
            pl.BlockSpec((1, 1, tn), lambda l, j: (l, 0, j)),
        ],
        out_specs=pl.BlockSpec((1, 2 * NB, tn), lambda l, j: (l, 0, j)),
        out_shape=jax.ShapeDtypeStruct((depth, 2 * NB, d6), F32),
        compiler_params=_cparams(("arbitrary", "arbitrary")),
    )(cvec, ada_w, ada_b.reshape(depth, 1, d6))


def _ffn_kernel(x_ref, mod_ref, g_ref, w1_ref, w3_ref, w2_ref, o_ref, hb_ref, acc_ref, *, n_k):
    k = pl.program_id(1)

    @pl.when(k == 0)
    def _():
        h = _norm_mod(x_ref[...], g_ref[...], mod_ref[0, 3], mod_ref[0, 4])
        hb_ref[...] = h.astype(BF16)
        acc_ref[...] = jnp.zeros_like(acc_ref)

    hb = hb_ref[...]
    a = _dot(hb, w1_ref[...])
    b = _dot(hb, w3_ref[...])
    acc_ref[...] += _dot((a * _sigmoid(a) * b).astype(BF16), w2_ref[...])

    @pl.when(k == n_k - 1)
    def _():
        o_ref[...] = _residual(x_ref[...], mod_ref[0, 5], acc_ref[...])


def _ffn_call(x, mod, g, w1, w3, w2, *, nct_rows):
    n, d = x.shape
    f = w1.shape[1]
    tm = FFN_TM
    tf = 512
    n_k = f // tf
    nct = nct_rows // tm
    return pl.pallas_call(
        functools.partial(_ffn_kernel, n_k=n_k),
        grid=(n // tm, n_k),
        in_specs=[
            pl.BlockSpec((tm, d), lambda i, k: (i, 0)),
            pl.BlockSpec((1, 6, NB, d), lambda i, k: (_is_lat(i, nct), 0, 0, 0)),
            pl.BlockSpec((1, d), lambda i, k: (0, 0)),
            pl.BlockSpec((d, tf), lambda i, k: (0, k)),
            pl.BlockSpec((d, tf), lambda i, k: (0, k)),
            pl.BlockSpec((tf, d), lambda i, k: (k, 0)),
        ],
        out_specs=pl.BlockSpec((tm, d), lambda i, k: (i, 0)),
        out_shape=jax.ShapeDtypeStruct((n, d), F32),
        scratch_shapes=[pltpu.VMEM((tm, d), BF16), pltpu.VMEM((tm, d), F32)],
        compiler_params=_cparams(("arbitrary", "arbitrary")),
    )(x, mod, g, w1, w3, w2)


def _top2(logits, n_e):
    lane = lax.broadcasted_iota(jnp.int32, logits.shape, 1).astype(F32)
    neg = jnp.float32(-3.0e38)
    lg = jnp.where(lane < n_e, logits, neg)
    m1 = jnp.max(lg, axis=-1, keepdims=True)
    i1 = jnp.min(jnp.where(lg == m1, lane, 1.0e9), axis=-1, keepdims=True)
    lg2 = jnp.where(lane == i1, neg, lg)
    m2 = jnp.max(lg2, axis=-1, keepdims=True)
    i2 = jnp.min(jnp.where(lg2 == m2, lane, 1.0e9), axis=-1, keepdims=True)
    e2 = jnp.exp(m2 - m1)
    den = 1.0 + e2
    return lane, i1, i2, 1.0 / den, e2 / den


def _moe_route_kernel(x_ref, mod_ref, g_ref, router_ref, ltri_ref, z_ref, dest_ref, cnt_ref,
                      hb_ref, *, n_e, cap):
    t, d = x_ref.shape
    h = _norm_mod(x_ref[...], g_ref[...], mod_ref[0, 3], mod_ref[0, 4])
    hb_ref[...] = h.astype(BF16)
    lane, i1, i2, p1, p2 = _top2(_dot_x3(h, router_ref[...]), n_e)
    sel1 = lane == i1
    sel2 = lane == i2
    c = jnp.where(sel1, 1.0, 0.0) + jnp.where(sel2, 1.0, 0.0)
    rank = _dot(ltri_ref[...], c.astype(BF16))
    cnt = jnp.sum(c, axis=0, keepdims=True)
    seg = jnp.floor((cnt + (MOE_BLK - 1)) * (1.0 / MOE_BLK)) * MOE_BLK
    ur = lax.broadcasted_iota(jnp.int32, (LANES, LANES), 0)
    uc = lax.broadcasted_iota(jnp.int32, (LANES, LANES), 1)
    before = jnp.where(ur < uc, 1.0, 0.0).astype(BF16)
    start = _dot(jnp.broadcast_to(seg, (NB, LANES)).astype(BF16), before)[0:1]
    pos = start + rank
    d1 = jnp.sum(jnp.where(sel1, pos, 0.0), axis=-1, keepdims=True)
    d2 = jnp.sum(jnp.where(sel2, pos, 0.0), axis=-1, keepdims=True)
    packed = (jnp.where(lane == 0.0, d1, 0.0) + jnp.where(lane == 1.0, d2, 0.0)
              + jnp.where(lane == 2.0, p1, 0.0) + jnp.where(lane == 3.0, p2, 0.0))
    dest_ref[...] = packed
    cnt_ref[0] = jnp.broadcast_to(cnt, (NB, LANES))
    packed_t = packed.T
    d1r = packed_t[0:1]
    d2r = packed_t[1:2]
    p1r = packed_t[2:3]
    p2r = packed_t[3:4]
    rc = 256

    def chunk(ci, carry):
        r0 = pl.multiple_of(ci * rc, rc)
        row = (lax.broadcasted_iota(jnp.int32, (rc, t), 0) + r0).astype(F32)
        m1 = row == d1r
        m2 = row == d2r
        perm = (jnp.where(m1, 1.0, 0.0) + jnp.where(m2, 1.0, 0.0)).astype(BF16)
        zc = _dot(perm, hb_ref[...])
        gate = jnp.sum(jnp.where(m1, p1r, 0.0) + jnp.where(m2, p2r, 0.0), axis=-1, keepdims=True)
        ghi = gate.astype(BF16).astype(F32)
        glo = gate - ghi
        le = lax.broadcasted_iota(jnp.int32, (rc, LANES), 1)
        ext = jnp.where(le == 0, ghi, jnp.where(le == 1, glo, 0.0))
        z_ref[pl.ds(r0, rc), 0:d] = zc.astype(BF16)
        z_ref[pl.ds(r0, rc), d:d + LANES] = ext.astype(BF16)
        return carry

    lax.fori_loop(0, cap // rc, chunk, 0)


def _moe_experts_kernel(tin_ref, se_ref, nu_ref, *refs, n_k, d):
    z_refs = refs[:MOE_SLOTS]
    w1_ref, w3_ref, w2_ref, o_ref, xs_ref, acc_ref = refs[MOE_SLOTS:]
    s = pl.program_id(0)
    k = pl.program_id(1)
    used = s < nu_ref[0]

    @pl.when(used & (k == 0))
    def _():
        for j in range(MOE_SLOTS):
            xs_ref[j * MOE_BLK:(j + 1) * MOE_BLK] = z_refs[j][...]
        acc_ref[...] = jnp.zeros_like(acc_ref)

    @pl.when(used)
    def _():
        hb = xs_ref[:, 0:d]
        ext = xs_ref[:, d:d + LANES].astype(F32)
        le = lax.broadcasted_iota(jnp.int32, ext.shape, 1)
        gate = jnp.sum(jnp.where(le < 2, ext, 0.0), axis=-1, keepdims=True)
        a = _dot(hb, w1_ref[0])
        b = _dot(hb, w3_ref[0])
        act = a * _sigmoid(a) * b * gate
        acc_ref[...] += _dot(act.astype(BF16), w2_ref[0])

    @pl.when(k == n_k - 1)
    def _():
        o_ref[...] = jnp.where(used, acc_ref[...], 0.0).astype(BF16)


def _moe_combine_kernel(tc_ref, *refs, nblk, cap):
    zb_refs = refs[:nblk]
    dest_ref, x_ref, mod_ref, o_ref, zbuf_ref = refs[nblk:]
    t, d = x_ref.shape
    for j in range(nblk):
        zbuf_ref[j * MOE_BLK:(j + 1) * MOE_BLK] = zb_refs[j][...]
    packed = dest_ref[...]
    lane = lax.broadcasted_iota(jnp.int32, packed.shape, 1)
    d1 = jnp.sum(jnp.where(lane == 0, packed, 0.0), axis=-1, keepdims=True)
    d2 = jnp.sum(jnp.where(lane == 1, packed, 0.0), axis=-1, keepdims=True)
    kc = 512
    acc = jnp.zeros((t, d), F32)
    for c0 in range(0, cap, kc):
        col = (lax.broadcasted_iota(jnp.int32, (t, kc), 1) + c0).astype(F32)
        sel = (jnp.where(col == d1, 1.0, 0.0) + jnp.where(col == d2, 1.0, 0.0)).astype(BF16)
        acc = acc + _dot(sel, zbuf_ref[c0:c0 + kc])
    o_ref[...] = _residual(x_ref[...], mod_ref[0, 5], acc)


def _moe_tables(cnt, *, n_tiles, n_e, nblk, n_steps):
    pc = (cnt + (MOE_BLK - 1)) // MOE_BLK
    lb = jnp.cumsum(pc, axis=1) - pc
    jj = jnp.arange(nblk, dtype=jnp.int32)
    valid = jnp.transpose(jj[None, None, :] < pc[:, :, None], (1, 0, 2)).reshape(n_e, -1)
    tile = jnp.broadcast_to(jnp.arange(n_tiles, dtype=jnp.int32)[None, :, None], (n_e, n_tiles, nblk))
    local = jnp.transpose(lb, (1, 0))[:, :, None] + jj[None, None, :]
    src = (tile * nblk + local).reshape(n_e, -1)
    pos = jnp.cumsum(valid.astype(jnp.int32), axis=1) - 1
    nb_e = jnp.sum(valid.astype(jnp.int32), axis=1)
    steps_e = (nb_e + (MOE_SLOTS - 1)) // MOE_SLOTS
    ends = jnp.cumsum(steps_e)
    step0 = ends - steps_e
    flat = (step0[:, None] + pos // MOE_SLOTS) * MOE_SLOTS + pos % MOE_SLOTS
    total = n_steps * MOE_SLOTS
    tin = jnp.zeros((total + 1,), jnp.int32).at[jnp.where(valid, flat, total).ravel()].set(src.ravel())[:total]
    nsrc = n_tiles * nblk
    tcomb = jnp.zeros((nsrc + 1,), jnp.int32).at[jnp.where(valid, src, nsrc).ravel()].set(flat.ravel())[:nsrc]
    sidx = jnp.arange(n_steps, dtype=jnp.int32)
    step_e = jnp.minimum(jnp.sum((sidx[:, None] >= ends[None, :]).astype(jnp.int32), axis=1), n_e - 1)
    return tin, tcomb, step_e, ends[-1:].astype(jnp.int32)


def _moe_call(x, mod, g, router, w1, w3, w2, *, nct_rows):
    n, d = x.shape
    n_e, _, f = w1.shape
    t = MOE_T
    assert n % t == 0 and nct_rows % t == 0
    n_tiles = n // t
    nct = nct_rows // t
    cap = 2 * t + n_e * MOE_BLK
    nblk = cap // MOE_BLK
    rows = MOE_SLOTS * MOE_BLK
    n_steps = -(-(n_tiles * nblk) // MOE_SLOTS) + n_e
    tf = 512
    n_k = f // tf
    router_p = jnp.pad(router.astype(F32), ((0, 0), (0, LANES - n_e)))
    ltri = jnp.tril(jnp.ones((t, t), F32), k=-1).astype(BF16)
    mod_spec = lambda off: pl.BlockSpec((1, 6, NB, d), lambda i, *_: (_is_lat(i, nct), 0, 0, 0))

    z, dest, cnt = pl.pallas_call(
        functools.partial(_moe_route_kernel, n_e=n_e, cap=cap),
        grid=(n_tiles,),
        in_specs=[
            pl.BlockSpec((t, d), lambda i: (i, 0)),
            mod_spec(0),
            pl.BlockSpec((1, d), lambda i: (0, 0)),
            pl.BlockSpec((d, LANES), lambda i: (0, 0)),
            pl.BlockSpec((t, t), lambda i: (0, 0)),
        ],
        out_specs=[
            pl.BlockSpec((cap, d + LANES), lambda i: (i, 0)),
            pl.BlockSpec((t, LANES), lambda i: (i, 0)),
            pl.BlockSpec((1, NB, LANES), lambda i: (i, 0, 0)),
        ],
        out_shape=[
            jax.ShapeDtypeStruct((n_tiles * cap, d + LANES), BF16),
            jax.ShapeDtypeStruct((n, LANES), F32),
            jax.ShapeDtypeStruct((n_tiles, NB, LANES), F32),
        ],
        scratch_shapes=[pltpu.VMEM((t, d), BF16)],
        compiler_params=_cparams(("arbitrary",)),
    )(x, mod, g, router_p, ltri)

    tin, tcomb, step_e, nused = _moe_tables(cnt[:, 0, :n_e].astype(jnp.int32), n_tiles=n_tiles,
                                            n_e=n_e, nblk=nblk, n_steps=n_steps)

    def z_spec(j):
        return pl.BlockSpec((MOE_BLK, d + LANES), lambda s, k, tin_r, se_r, nu_r: (tin_r[s * MOE_SLOTS + j], 0))

    zo = pl.pallas_call(
        functools.partial(_moe_experts_kernel, n_k=n_k, d=d),
        grid_spec=pltpu.PrefetchScalarGridSpec(
            num_scalar_prefetch=3,
            grid=(n_steps, n_k),
            in_specs=[z_spec(j) for j in range(MOE_SLOTS)] + [
                pl.BlockSpec((1, d, tf), lambda s, k, tin_r, se_r, nu_r: (se_r[s], 0, k)),
                pl.BlockSpec((1, d, tf), lambda s, k, tin_r, se_r, nu_r: (se_r[s], 0, k)),
                pl.BlockSpec((1, tf, d), lambda s, k, tin_r, se_r, nu_r: (se_r[s], k, 0)),
            ],
            out_specs=pl.BlockSpec((rows, d), lambda s, k, tin_r, se_r, nu_r: (s, 0)),
            scratch_shapes=[pltpu.VMEM((rows, d + LANES), BF16), pltpu.VMEM((rows, d), F32)],
        ),
        out_shape=jax.ShapeDtypeStruct((n_steps * rows, d), BF16),
        compiler_params=_cparams(("arbitrary", "arbitrary")),
    )(tin, step_e, nused, *([z] * MOE_SLOTS), w1, w3, w2)

    def zo_spec(j):
        return pl.BlockSpec((MOE_BLK, d), lambda i, tc_r: (tc_r[i * nblk + j], 0))

    return pl.pallas_call(
        functools.partial(_moe_combine_kernel, nblk=nblk, cap=cap),
        grid_spec=pltpu.PrefetchScalarGridSpec(
            num_scalar_prefetch=1,
            grid=(n_tiles,),
            in_specs=[zo_spec(j) for j in range(nblk)] + [
                pl.BlockSpec((t, LANES), lambda i, tc_r: (i, 0)),
                pl.BlockSpec((t, d), lambda i, tc_r: (i, 0)),
                mod_spec(0),
            ],
            out_specs=pl.BlockSpec((t, d), lambda i, tc_r: (i, 0)),
            scratch_shapes=[pltpu.VMEM((cap, d), BF16)],
        ),
        out_shape=jax.ShapeDtypeStruct((n, d), F32),
        compiler_params=_cparams(("arbitrary",)),
    )(tcomb, *([zo] * nblk), dest, x, mod)


def _dir_tile(d, s, nct, nt):
    bwd = jnp.where(s < nct, nct - 1 - s, nt - 1 + nct - s)
    return jnp.where(d == 0, s, bwd)


def _s5_scan_kernel(x_ref, mod_ref, g_ref, bb_ref, cc_ref, ab_ref, y_ref,
                    hb_ref, bu_ref, st_ref):
    d = pl.program_id(0)
    s = pl.program_id(1)
    half = S5_JBLK * S5_STATE

    @pl.when(s == 0)
    def _():
        st_ref[...] = jnp.zeros_like(st_ref)

    h = _norm_mod(x_ref[...], g_ref[...], mod_ref[0, 0], mod_ref[0, 1])
    hb_ref[...] = h.astype(BF16)

    for j in range(S5_JBLK):
        bu_ref[...] = _dot(hb_ref[:, j * LANES:(j + 1) * LANES], bb_ref[0, j])
        ar = ab_ref[0, j, 0]
        ai = ab_ref[0, j, 1]

        def body(t, carry):
            re, im = carry
            tt = t + d * (TT - 1 - 2 * t)
            r0 = pl.multiple_of(tt * NB, NB)
            bre = bu_ref[pl.ds(r0, NB), 0:half]
            bim = bu_ref[pl.ds(r0, NB), half:2 * half]
            nre = ar * re - ai * im + bre
            nim = ar * im + ai * re + bim
            bu_ref[pl.ds(r0, NB), 0:half] = nre
            bu_ref[pl.ds(r0, NB), half:2 * half] = nim
            return nre, nim

        re, im = lax.fori_loop(0, TT, body, (st_ref[j, 0], st_ref[j, 1]), unroll=4)
        st_ref[j, 0] = re
        st_ref[j, 1] = im
        y_ref[0, j] = _dot(bu_ref[...].astype(BF16), cc_ref[0, j])


def _s5_scan_call(x, mod, g, bb, cc, ab, *, nct):
    n, d = x.shape
    nt = n // TM
    half = S5_JBLK * S5_STATE
    tile = lambda dd, s: _dir_tile(dd, s, nct, nt)
    return pl.pallas_call(
        _s5_scan_kernel,
        grid=(2, nt),
        in_specs=[
            pl.BlockSpec((TM, d), lambda dd, s: (tile(dd, s), 0)),
            pl.BlockSpec((1, 6, NB, d), lambda dd, s: (_is_lat(tile(dd, s), nct), 0, 0, 0)),
            pl.BlockSpec((1, d), lambda dd, s: (0, 0)),
            pl.BlockSpec((1, S5_JBLK, LANES, 2 * half), lambda dd, s: (dd, 0, 0, 0)),
            pl.BlockSpec((1, S5_JBLK, 2 * half, LANES), lambda dd, s: (dd, 0, 0, 0)),
            pl.BlockSpec((1, S5_JBLK, 2, NB, half), lambda dd, s: (dd, 0, 0, 0, 0)),
        ],
        out_specs=pl.BlockSpec((1, S5_JBLK, TM, LANES), lambda dd, s: (dd, 0, tile(dd, s), 0)),
        out_shape=jax.ShapeDtypeStruct((2, S5_JBLK, n, LANES), F32),
        scratch_shapes=[
            pltpu.VMEM((TM, d), BF16),
            pltpu.VMEM((TM, 2 * half), F32),
            pltpu.VMEM((S5_JBLK, 2, NB, half), F32),
        ],
        compiler_params=_cparams(("arbitrary", "arbitrary")),
    )(x, mod, g, bb, cc, ab)


def _gelu_tanh(x):
    c = math.sqrt(2.0 / math.pi)
    return 0.5 * x * (1.0 + jnp.tanh(c * (x + 0.044715 * (x * x * x))))


def _s5_glu_kernel(x_ref, mod_ref, g_ref, y_ref, dsk_ref, w_ref, b_ref, o_ref):
    x = x_ref[...]
    d = x.shape[1]
    h = _norm_mod(x, g_ref[...], mod_ref[0, 0], mod_ref[0, 1])
    y = jnp.concatenate([y_ref[0, j] + y_ref[1, j] for j in range(S5_JBLK)], axis=1)
    z = _gelu_tanh(y + dsk_ref[...] * h).astype(BF16)
    zz = _dot(z, w_ref[...]) + b_ref[...]
    out = zz[:, :d] * _sigmoid(zz[:, d:])
    o_ref[...] = _residual(x, mod_ref[0, 2], out)


def _s5_glu_call(x, mod, g, y, dsk, w, b, *, nct, tile0):
    n, d = x.shape
    nt = n // TM
    return pl.pallas_call(
        _s5_glu_kernel,
        grid=(nt - tile0,),
        in_specs=[
            pl.BlockSpec((TM, d), lambda i: (i + tile0, 0)),
            pl.BlockSpec((1, 6, NB, d), lambda i: (_is_lat(i + tile0, nct), 0, 0, 0)),
            pl.BlockSpec((1, d), lambda i: (0, 0)),
            pl.BlockSpec((2, S5_JBLK, TM, LANES), lambda i: (0, 0, i + tile0, 0)),
            pl.BlockSpec((1, d), lambda i: (0, 0)),
            pl.BlockSpec((d, 2 * d), lambda i: (0, 0)),
            pl.BlockSpec((1, 2 * d), lambda i: (0, 0)),
        ],
        out_specs=pl.BlockSpec((TM, d), lambda i: (i, 0)),
        out_shape=jax.ShapeDtypeStruct((n - tile0 * TM, d), F32),
        compiler_params=_cparams(("arbitrary",)),
    )(x, mod, g, y, dsk, w, b)


def _s5_params(lam_re, lam_im, log_dt, b_re, b_im, c_re, c_im):
    lam_re = jnp.minimum(lam_re.astype(F32), LAMBDA_RE_MAX)
    lam_im = lam_im.astype(F32)
    dt = jnp.exp(log_dt.astype(F32))[..., None]
    mag = jnp.exp(lam_re * dt)
    abar_re = mag * jnp.cos(lam_im * dt)
    abar_im = mag * jnp.sin(lam_im * dt)
    inv_den = 1.0 / (lam_re * lam_re + lam_im * lam_im)
    gam_re = ((abar_re - 1.0) * lam_re + abar_im * lam_im) * inv_den
    gam_im = (abar_im * lam_re - (abar_re - 1.0) * lam_im) * inv_den
    b_re = b_re.astype(F32)
    b_im = b_im.astype(F32)
    bb_re = gam_re[..., None] * b_re - gam_im[..., None] * b_im
    bb_im = gam_re[..., None] * b_im + gam_im[..., None] * b_re
    g_all = lam_re.shape[1]
    nj = g_all // S5_JBLK
    eye = jnp.eye(S5_JBLK, dtype=F32)

    def in_mat(m):
        m = m.reshape(2, nj, S5_JBLK, S5_STATE, S5_GROUP)
        m = jnp.einsum('djgnp,gh->djgphn', m, eye)
        return m.reshape(2, nj, S5_JBLK * S5_GROUP, S5_JBLK * S5_STATE)

    def out_mat(m):
        m = m.reshape(2, nj, S5_JBLK, S5_GROUP, S5_STATE)
        m = jnp.einsum('djgpn,gh->djgnhp', m, eye)
        return m.reshape(2, nj, S5_JBLK * S5_STATE, S5_JBLK * S5_GROUP)

    bb = jnp.concatenate([in_mat(bb_re), in_mat(bb_im)], axis=-1).astype(BF16)
    cc = jnp.concatenate([out_mat(c_re.astype(F32)), -out_mat(c_im.astype(F32))], axis=-2).astype(BF16)
    half = S5_JBLK * S5_STATE
    ab = jnp.stack([abar_re.reshape(2, nj, half), abar_im.reshape(2, nj, half)], axis=2)
    ab = jnp.broadcast_to(ab[:, :, :, None, :], (2, nj, 2, NB, half))
    return bb, cc, ab


def _cv_glu_kernel(x_ref, mod_ref, g_ref, w_ref, b_ref, o_ref):
    x = x_ref[...]
    d = x.shape[1]
    h = _norm_mod(x, g_ref[...], mod_ref[0, 0], mod_ref[0, 1]).astype(BF16)
    u = _dot(h, w_ref[...]) + b_ref[...]
    o_ref[...] = u[:, :d] * _sigmoid(u[:, d:])


def _cv_glu_call(x, mod, g, w, b, *, nct):
    n, d = x.shape
    nt = n // TM
    return pl.pallas_call(
        _cv_glu_kernel,
        grid=(nt,),
        in_specs=[
            pl.BlockSpec((TM, d), lambda i: (i, 0)),
            pl.BlockSpec((1, 6, NB, d), lambda i: (_is_lat(i, nct), 0, 0, 0)),
            pl.BlockSpec((1, d), lambda i: (0, 0)),
            pl.BlockSpec((d, 2 * d), lambda i: (0, 0)),
            pl.BlockSpec((1, 2 * d), lambda i: (0, 0)),
        ],
        out_specs=pl.BlockSpec((TM, d), lambda i: (i, 0)),
        out_shape=jax.ShapeDtypeStruct((n, d), F32),
        compiler_params=_cparams(("arbitrary",)),
    )(x, mod, g, w, b)


def _cv_conv_kernel(u_ref, up_ref, un_ref, x_ref, mod_ref, dw_ref, dwb_ref, lng_ref, lnb_ref,
                    w_ref, b_ref, o_ref, pad_ref, cv_ref, *, nct):
    i = pl.program_id(0)
    halo = CONV_PAD * NB
    prev_ok = (i > 0) & (i < nct)
    next_ok = i < nct - 1
    pad_ref[0:halo] = jnp.where(prev_ok, up_ref[HALO_ROWS - halo:HALO_ROWS], 0.0)
    pad_ref[halo:halo + TM] = u_ref[...]
    pad_ref[halo + TM:halo + TM + halo] = jnp.where(next_ok, un_ref[0:halo], 0.0)

    rb = 32

    def body(c, _):
        r0 = pl.multiple_of(c * rb, rb)
        acc = jnp.broadcast_to(dwb_ref[...], (rb, dwb_ref.shape[1]))
        for k in range(CONV_WIDTH):
            acc = acc + dw_ref[k] * pad_ref[pl.ds(r0 + k * NB, rb)]
        cv_ref[pl.ds(r0, rb)] = acc
        return 0

    lax.fori_loop(0, TM // rb, body, 0)

    c = cv_ref[...]
    mu = jnp.mean(c, axis=-1, keepdims=True)
    cc = c - mu
    var = jnp.mean(cc * cc, axis=-1, keepdims=True)
    y = cc * lax.rsqrt(var + LN_EPS) * lng_ref[...] + lnb_ref[...]
    y = (y * _sigmoid(y)).astype(BF16)
    out = _dot(y, w_ref[...]) + b_ref[...]
    o_ref[...] = _residual(x_ref[...], mod_ref[0, 2], out)


def _cv_conv_call(u, x, mod, dw_w, dw_b, ln_g, ln_b, w, b, *, nct):
    n, d = x.shape
    nt = n // TM
    hb = TM // HALO_ROWS
    nhb = n // HALO_ROWS
    kern = functools.partial(_cv_conv_kernel, nct=nct)
    row = lambda i: (0, 0)
    return pl.pallas_call(
        kern,
        grid=(nt,),
        in_specs=[
            pl.BlockSpec((TM, d), lambda i: (i, 0)),
            pl.BlockSpec((HALO_ROWS, d), lambda i: (jnp.maximum(i * hb - 1, 0), 0)),
            pl.BlockSpec((HALO_ROWS, d), lambda i: (jnp.minimum((i + 1) * hb, nhb - 1), 0)),
            pl.BlockSpec((TM, d), lambda i: (i, 0)),
            pl.BlockSpec((1, 6, NB, d), lambda i: (_is_lat(i, nct), 0, 0, 0)),
            pl.BlockSpec((CONV_WIDTH, 1, d), lambda i: (0, 0, 0)),
            pl.BlockSpec((1, d), row),
            pl.BlockSpec((1, d), row),
            pl.BlockSpec((1, d), row),
            pl.BlockSpec((d, d), row),
            pl.BlockSpec((1, d), row),
        ],
        out_specs=pl.BlockSpec((TM, d), lambda i: (i, 0)),
        out_shape=jax.ShapeDtypeStruct((n, d), F32),
        scratch_shapes=[
            pltpu.VMEM((TM + 2 * CONV_PAD * NB, d), F32),
            pltpu.VMEM((TM, d), F32),
        ],
        compiler_params=_cparams(("arbitrary",)),
    )(u, u, u, x, mod, dw_w.reshape(CONV_WIDTH, 1, d), dw_b, ln_g, ln_b, w, b)


def _seg_mats(d):
    ch = lax.broadcasted_iota(jnp.int32, (d, LANES), 0) // RWKV_HEAD
    hd = lax.broadcasted_iota(jnp.int32, (d, LANES), 1)
    seg = jnp.where(ch == hd, 1.0, 0.0).astype(BF16)
    hd_t = lax.broadcasted_iota(jnp.int32, (LANES, d), 0)
    ch_t = lax.broadcasted_iota(jnp.int32, (LANES, d), 1) // RWKV_HEAD
    seg_t = jnp.where(ch_t == hd_t, 1.0, 0.0).astype(BF16)
    return seg, seg_t


def _head_sum_bcast(x, seg, seg_t):
    return _dot_exact_rhs(_dot_exact_rhs(x, seg), seg_t)


def _rw_proj_kernel(x_ref, xp_ref, xn_ref, mod_ref, g_ref, mu_ref, wr_ref, wk_ref, wv_ref,
                    w1_ref, w2_ref, w0_ref, a1_ref, a2_ref, a0_ref, g1_ref, g2_ref, kkw_ref,
                    r_ref, k_ref, v_ref, kk_ref, ew_ref, as_ref, gg_ref, hp_ref, *, nct, nt):
    i = pl.program_id(0)
    tm, d = x_ref.shape
    g = g_ref[...]
    sh = mod_ref[0, 0]
    sc = mod_ref[0, 1]
    h = _norm_mod(x_ref[...], g, sh, sc)
    hp = _norm_mod(xp_ref[...], g, sh, sc)
    hn = _norm_mod(xn_ref[...], g, sh, sc)
    prev_ok = (i != 0) & (i != nct)
    next_ok = (i != nct - 1) & (i != nt - 1)
    hp_ref[0:NB] = jnp.where(prev_ok, hp, 0.0)
    hp_ref[NB:NB + tm] = h
    hp_ref[NB + tm:NB + tm + NB] = jnp.where(next_ok, hn, 0.0)
    xx = 0.5 * (hp_ref[0:tm] + hp_ref[2 * NB:2 * NB + tm]) - h

    def mix(j):
        return (h + xx * mu_ref[j]).astype(BF16)

    r = _dot(mix(0), wr_ref[...])
    k = _dot(mix(2), wk_ref[...])
    v = _dot(mix(3), wv_ref[...])
    zw = w0_ref[...] + _dot(jnp.tanh(_dot(mix(1), w1_ref[...])).astype(BF16), w2_ref[...])
    ew = _sigmoid(zw) * math.exp(-0.5)
    asg = _sigmoid(a0_ref[...] + _dot(_dot(mix(4), a1_ref[...]).astype(BF16), a2_ref[...]))
    gg_ref[...] = _dot(_sigmoid(_dot(mix(5), g1_ref[...])).astype(BF16), g2_ref[...])

    seg, seg_t = _seg_mats(d)
    kk = k * kkw_ref[...]
    kk = kk * lax.rsqrt(_head_sum_bcast(kk * kk, seg, seg_t) + KK_EPS)

    for p in range(d // LANES):
        sl = slice(p * LANES, (p + 1) * LANES)
        r_ref[p] = r[:, sl]
        k_ref[p] = k[:, sl]
        v_ref[p] = v[:, sl]
        kk_ref[p] = kk[:, sl]
        for dr in range(2):
            sl2 = slice(dr * d + p * LANES, dr * d + (p + 1) * LANES)
            ew_ref[dr, p] = ew[:, sl2]
            as_ref[dr, p] = asg[:, sl2]


def _rw_proj_call(x, mod, g, mu, wr, wk, wv, w1, w2, w0, a1, a2, a0, g1, g2, kkw, *, nct):
    n, d = x.shape
    tm = RW_TM
    nt = n // tm
    nct = nct * (TM // tm)
    npair = d // LANES
    tb = tm // NB
    nb8 = n // NB
    kern = functools.partial(_rw_proj_kernel, nct=nct, nt=nt)
    full2 = lambda a: pl.BlockSpec(a.shape, lambda i: (0, 0))
    slab = pl.BlockSpec((npair, tm, LANES), lambda i: (0, i, 0))
    slab2 = pl.BlockSpec((2, npair, tm, LANES), lambda i: (0, 0, i, 0))
    slab_shape = jax.ShapeDtypeStruct((npair, n, LANES), F32)
    slab2_shape = jax.ShapeDtypeStruct((2, npair, n, LANES), F32)
    mu3 = mu.reshape(6, 1, d)
    return pl.pallas_call(
        kern,
        grid=(nt,),
        in_specs=[
            pl.BlockSpec((tm, d), lambda i: (i, 0)),
            pl.BlockSpec((NB, d), lambda i: (jnp.maximum(i * tb - 1, 0), 0)),
            pl.BlockSpec((NB, d), lambda i: (jnp.minimum((i + 1) * tb, nb8 - 1), 0)),
            pl.BlockSpec((1, 6, NB, d), lambda i: (_is_lat(i, nct), 0, 0, 0)),
            pl.BlockSpec((1, d), lambda i: (0, 0)),
            pl.BlockSpec((6, 1, d), lambda i: (0, 0, 0)),
            full2(wr), full2(wk), full2(wv),
            full2(w1), full2(w2), full2(w0), full2(a1), full2(a2), full2(a0),
            full2(g1), full2(g2), full2(kkw),
        ],
        out_specs=[slab, slab, slab, slab, slab2, slab2, pl.BlockSpec((tm, d), lambda i: (i, 0))],
        out_shape=[slab_shape, slab_shape, slab_shape, slab_shape, slab2_shape, slab2_shape,
                   jax.ShapeDtypeStruct((n, d), F32)],
        scratch_shapes=[pltpu.VMEM((tm + 2 * NB, d), F32)],
        compiler_params=_cparams(("arbitrary",)),
    )(x, x, x, mod, g, mu3, wr, wk, wv, w1, w2, w0, a1, a2, a0, g1, g2, kkw)


def _wkv_kernel(r_ref, k_ref, v_ref, kk_ref, ew_ref, as_ref, ka_ref, y_ref, st_ref, *, L, npair):
    d = pl.program_id(0)
    c = pl.program_id(1)

    @pl.when(c == 0)
    def _():
        st_ref[...] = jnp.zeros_like(st_ref)

    sgn = 1 - 2 * d
    l2 = 2 * L
    rr = lax.broadcasted_iota(jnp.int32, (l2, l2), 0)
    cc = lax.broadcasted_iota(jnp.int32, (l2, l2), 1)
    same = (rr >= L) == (cc >= L)
    diff = (rr - cc) * sgn
    strict_f = jnp.where(same & (diff > 0), 1.0, 0.0)
    incl_f = jnp.where(same & (diff >= 0), 1.0, 0.0)
    r1 = lax.broadcasted_iota(jnp.int32, (L, L), 0)
    c1 = lax.broadcasted_iota(jnp.int32, (L, L), 1)
    mcum = jnp.where((r1 - c1) * sgn >= 0, 1.0, 0.0).astype(BF16)
    head0 = lax.broadcasted_iota(jnp.int32, (L, LANES), 1) < RWKV_HEAD
    sr_i = lax.broadcasted_iota(jnp.int32, (LANES, LANES), 0)
    sc_i = lax.broadcasted_iota(jnp.int32, (LANES, LANES), 1)
    blockdiag = (sr_i < RWKV_HEAD) == (sc_i < RWKV_HEAD)
    eye = jnp.where(sr_i == sc_i, 1.0, 0.0)
    n_dbl = int(round(math.log2(L)))

    def stack2(x):
        return jnp.concatenate([jnp.where(head0, x, 0.0), jnp.where(head0, 0.0, x)], axis=0)

    def unstack(xs):
        return xs[0:L] + xs[L:l2]

    def load(p, b):
        rows = pl.ds(b, L, stride=NB)
        return (r_ref[p, rows, :], k_ref[p, rows, :], v_ref[p, rows, :], kk_ref[p, rows, :],
                ew_ref[0, p, rows, :], as_ref[0, p, rows, :], st_ref[p * NB + b])

    def inst(ka, r, k, v, kk, ew, asg, s0):
        ldec = -ew
        h1, h2, h3 = _split3(ldec)
        cl3 = _dot(mcum, jnp.concatenate([h1, h2, h3], axis=1))
        yield
        cl = cl3[:, 0:LANES] + cl3[:, LANES:2 * LANES] + cl3[:, 2 * LANES:3 * LANES]
        ce = cl - ldec
        ecl = jnp.exp(cl)
        encl = jnp.exp(-cl)
        etot = jnp.exp(jnp.sum(ldec, axis=0, keepdims=True))
        at = -kk * jnp.exp(ce)
        rt = r * ecl
        bt = kk * asg * encl
        kt = k * (1.0 + (asg - 1.0) * ka) * encl

        ats = stack2(at)
        vsb = stack2(v).astype(BF16)
        lhs = jnp.concatenate([ats, stack2(rt)], axis=0).astype(BF16)
        rhs = jnp.concatenate([stack2(bt), stack2(kt)], axis=0).astype(BF16)
        pm = _dot_nt(lhs, rhs)
        yield
        nab = pm[0:l2, 0:l2] * strict_f
        nak = pm[0:l2, l2:2 * l2] * strict_f
        mrb = (pm[l2:2 * l2, 0:l2] * incl_f).astype(BF16)
        mrk = (pm[l2:2 * l2, l2:2 * l2] * incl_f).astype(BF16)

        xs = jnp.concatenate([ats, _dot(nak.astype(BF16), vsb)], axis=1)
        mkv = _dot(mrk, vsb)
        yield
        npow = nab
        for it in range(n_dbl):
            npb = npow.astype(BF16)
            xs = xs + _dot(npb, xs.astype(BF16))
            if it + 1 < n_dbl:
                npow = _dot(npb, npb)
            yield
        mx = _dot(mrb, xs.astype(BF16))
        at2 = unstack(xs[:, 0:LANES])
        ul = unstack(xs[:, LANES:2 * LANES])
        btb = bt.astype(BF16)
        gmat = (eye + jnp.where(blockdiag, _dot(at2.T.astype(BF16), btb), 0.0)) * etot
        uv = jnp.concatenate([ul, v], axis=0)
        bk = jnp.concatenate([btb, kt.astype(BF16)], axis=0)
        dmat = jnp.where(blockdiag, _dot(uv.T.astype(BF16), bk), 0.0) * etot
        yield
        rt2 = rt + unstack(mx[:, 0:LANES])
        yl = unstack(mx[:, LANES:2 * LANES] + mkv)

        return _dot_nt(rt2.astype(BF16), s0.astype(BF16)) + yl, _dot_x3(s0, gmat) + dmat

    def run_group(gens):
        outs = [None] * len(gens)
        live = list(enumerate(gens))
        while live:
            nxt = []
            for i, g in live:
                try:
                    next(g)
                    nxt.append((i, g))
                except StopIteration as stop:
                    outs[i] = stop.value
            live = nxt
        return outs

    def pair(p, carry):
        ka = ka_ref[p]
        for b0 in range(0, NB, WKV_GROUP):
            outs = run_group([inst(ka, *load(p, b0 + g)) for g in range(WKV_GROUP)])
            for g, (y, s1) in enumerate(outs):
                y_ref[0, p, pl.ds(b0 + g, L, stride=NB), :] = y
                st_ref[p * NB + b0 + g] = s1
        return carry

    lax.fori_loop(0, npair, pair, 0)


def _wkv_call(r, k, v, kk, ew, asg, ka, *, nct_rows):
    npair, n, _ = r.shape
    L = WKV_L
    rows = L * NB
    nc = n // rows
    ncc = nct_rows // rows
    kern = functools.partial(_wkv_kernel, L=L, npair=npair)

    def chunk(dd, c):
        bwd = jnp.where(c < ncc, ncc - 1 - c, nc - 1 + ncc - c)
        return jnp.where(dd == 0, c, bwd)

    slab = pl.BlockSpec((npair, rows, LANES), lambda dd, c: (0, chunk(dd, c), 0))
    slab2 = pl.BlockSpec((1, npair, rows, LANES), lambda dd, c: (dd, 0, chunk(dd, c), 0))
    return pl.pallas_call(
        kern,
        grid=(2, nc),
        in_specs=[slab, slab, slab, slab, slab2, slab2,
                  pl.BlockSpec((npair, 1, LANES), lambda dd, c: (0, 0, 0))],
        out_specs=slab2,
        out_shape=jax.ShapeDtypeStruct((2, npair, n, LANES), F32),
        scratch_shapes=[pltpu.VMEM((npair * NB, LANES, LANES), F32)],
        compiler_params=_cparams(("arbitrary", "arbitrary")),
    )(r, k, v, kk, ew, asg, ka)


def _rw_out_kernel(x_ref, mod_ref, y_ref, r_ref, k_ref, v_ref, as_ref, gg_ref, lng_ref, lnb_ref,
                   ka_ref, rk_ref, wo_ref, o_ref):
    x = x_ref[...]
    d = x.shape[1]
    npair = d // LANES
    cat = lambda f: jnp.concatenate([f(p) for p in range(npair)], axis=1)
    y = cat(lambda p: y_ref[0, p] + y_ref[1, p])
    r = cat(lambda p: r_ref[p])
    k = cat(lambda p: k_ref[p])
    v = cat(lambda p: v_ref[p])
    am = cat(lambda p: 0.5 * (as_ref[0, p] + as_ref[1, p]))
    seg, seg_t = _seg_mats(d)
    inv = 1.0 / RWKV_HEAD
    mean = _head_sum_bcast(y, seg, seg_t) * inv
    yc = y - mean
    var = _head_sum_bcast(yc * yc, seg, seg_t) * inv
    yn = yc * lax.rsqrt(var + GN_EPS) * lng_ref[...] + lnb_ref[...]
    k_mean = k * (1.0 + (am - 1.0) * ka_ref[...])
    bonus = _head_sum_bcast(r * k_mean * rk_ref[...], seg, seg_t) * v
    z = ((yn + bonus) * gg_ref[...]).astype(BF16)
    o_ref[...] = _residual(x, mod_ref[0, 2], _dot(z, wo_ref[...]))


def _rw_out_call(x, mod, y, r, k, v, asg, gg, lng, lnb, ka, rk, wo, *, nct):
    n, d = x.shape
    nt = n // TM
    npair = d // LANES
    slab = pl.BlockSpec((npair, TM, LANES), lambda i: (0, i, 0))
    slab2 = pl.BlockSpec((2, npair, TM, LANES), lambda i: (0, 0, i, 0))
    row = pl.BlockSpec((1, d), lambda i: (0, 0))
    return pl.pallas_call(
        _rw_out_kernel,
        grid=(nt,),
        in_specs=[
            pl.BlockSpec((TM, d), lambda i: (i, 0)),
            pl.BlockSpec((1, 6, NB, d), lambda i: (_is_lat(i, nct), 0, 0, 0)),
            slab2, slab, slab, slab, slab2,
            pl.BlockSpec((TM, d), lambda i: (i, 0)),
            row, row, row, row,
            pl.BlockSpec((d, d), lambda i: (0, 0)),
        ],
        out_specs=pl.BlockSpec((TM, d), lambda i: (i, 0)),
        out_shape=jax.ShapeDtypeStruct((n, d), F32),
        compiler_params=_cparams(("arbitrary",)),
    )(x, mod, y, r, k, v, asg, gg, lng, lnb, ka, rk, wo)


def _final_kernel(x_ref, g_ref, o_ref):
    x = x_ref[...]
    ms = jnp.mean(x * x, axis=-1, keepdims=True)
    o_ref[...] = x * lax.rsqrt(ms + RMS_EPS) * g_ref[...]


def _final_call(x, g, *, tile0):
    n, d = x.shape
    nt = n // TM
    return pl.pallas_call(
        _final_kernel,
        grid=(nt - tile0,),
        in_specs=[pl.BlockSpec((TM, d), lambda i: (i + tile0, 0)),
                  pl.BlockSpec((1, d), lambda i: (0, 0))],
        out_specs=pl.BlockSpec((TM, d), lambda i: (i, 0)),
        out_shape=jax.ShapeDtypeStruct((n - tile0 * TM, d), F32),
        compiler_params=_cparams(("arbitrary",)),
    )(x, g)


def _row(v):
    return v.reshape(1, -1).astype(F32)


def _s5_layer(x, mod, g, p, *, nct, tile0):
    bb, cc, ab = _s5_params(p['lam_re'], p['lam_im'], p['log_dt'], p['b_re'], p['b_im'],
                            p['c_re'], p['c_im'])
    y = _s5_scan_call(x, mod, g, bb, cc, ab, nct=nct)
    return _s5_glu_call(x, mod, g, y, _row(p['d']), p['glu_w'].astype(BF16), _row(p['glu_b']),
                        nct=nct, tile0=tile0)


def _conformer_layer(x, mod, g, p, *, nct):
    u = _cv_glu_call(x, mod, g, p['pw1_w'].astype(BF16), _row(p['pw1_b']), nct=nct)
    return _cv_conv_call(u, x, mod, p['dw_w'].astype(F32), _row(p['dw_b']), _row(p['ln_g']),
                         _row(p['ln_b']), p['pw2_w'].astype(BF16), _row(p['pw2_b']), nct=nct)


def _blockdiag2(m):
    z = jnp.zeros_like(m[0])
    return jnp.concatenate([jnp.concatenate([m[0], z], axis=1),
                            jnp.concatenate([z, m[1]], axis=1)], axis=0)


def _rwkv_layer(x, mod, g, p, *, nct):
    n, d = x.shape
    npair = d // LANES
    w1 = jnp.concatenate([p['w1'][0], p['w1'][1]], axis=1).astype(BF16)
    w2 = _blockdiag2(p['w2']).astype(BF16)
    a1 = jnp.concatenate([p['a1'][0], p['a1'][1]], axis=1).astype(BF16)
    a2 = _blockdiag2(p['a2']).astype(BF16)
    gl = p['g1'].shape[1]
    glp = -(-gl // LANES) * LANES
    g1 = jnp.pad(p['g1'], ((0, 0), (0, glp - gl))).astype(BF16)
    g2 = jnp.pad(p['g2'], ((0, glp - gl), (0, 0))).astype(BF16)
    r, k, v, kk, ew, asg, gg = _rw_proj_call(
        x, mod, g, p['mu'].astype(F32), p['w_r'].astype(BF16), p['w_k'].astype(BF16),
        p['w_v'].astype(BF16), w1, w2, _row(p['w0']), a1, a2, _row(p['a0']), g1, g2,
        _row(p['k_k']), nct=nct)
    ka_slab = p['k_a'].astype(F32).reshape(npair, 1, LANES)
    y = _wkv_call(r, k, v, kk, ew, asg, ka_slab, nct_rows=nct * TM)
    return _rw_out_call(x, mod, y, r, k, v, asg, gg, _row(p['lnx_g']), _row(p['lnx_b']),
                        _row(p['k_a']), _row(p['r_k']), p['w_o'].astype(BF16), nct=nct)


def _forward(x, c, ctx, c_ctx, ada_w, ada_b, norm_mix_g, norm_ffn_g, final_g, s5, cv, rw,
             ff_w1, ff_w3, ff_w2, moe_router, moe_w1, moe_w3, moe_w2):
    bsz, seq, d = x.shape
    lc = ctx.shape[1]
    depth = ada_w.shape[0]
    assert bsz == NB and lc % TT == 0 and seq % TT == 0
    assert TT == GRID_W, "one token tile must hold exactly one latent raster row per batch entry"
    nct = lc // TT
    t_all = lc + seq
    n = t_all * NB

    xs = jnp.concatenate([jnp.swapaxes(ctx, 0, 1), jnp.swapaxes(x, 0, 1)], axis=0)
    xs = xs.reshape(n, d).astype(F32)

    cvec = jnp.zeros((2 * NB, d), F32).at[:NB].set(c).at[NB].set(c_ctx)
    m_all = _ada_call(cvec, ada_w.astype(F32), ada_b.astype(F32))
    m_all = m_all.reshape(depth, 2 * NB, 6, d)
    m_lat = jnp.swapaxes(m_all[:, :NB], 1, 2)
    m_ctx = jnp.broadcast_to(m_all[:, NB][:, :, None, :], (depth, 6, NB, d))
    mods = jnp.stack([m_ctx, m_lat], axis=1)

    ctx_rows = nct * TM
    for i in range(depth):
        last = i == depth - 1
        mod = mods[i]
        g_mix = _row(norm_mix_g[i])
        j = i // 3
        if i % 3 == 0:
            xs = _s5_layer(xs, mod, g_mix, {k_: v_[j] for k_, v_ in s5.items()}, nct=nct,
                           tile0=nct if last else 0)
            ctx_rows = 0 if last else ctx_rows
        elif i % 3 == 1:
            xs = _conformer_layer(xs, mod, g_mix, {k_: v_[j] for k_, v_ in cv.items()}, nct=nct)
        else:
            xs = _rwkv_layer(xs, mod, g_mix, {k_: v_[j] for k_, v_ in rw.items()}, nct=nct)
        fi = i // 2
        g_ffn = _row(norm_ffn_g[i])
        if i % 2 == 0:
            xs = _ffn_call(xs, mod, g_ffn, ff_w1[fi].astype(BF16), ff_w3[fi].astype(BF16),
                           ff_w2[fi].astype(BF16), nct_rows=ctx_rows)
        else:
            xs = _moe_call(xs, mod, g_ffn, moe_router[fi], moe_w1[fi].astype(BF16),
                           moe_w3[fi].astype(BF16), moe_w2[fi].astype(BF16), nct_rows=ctx_rows)

    out = _final_call(xs, _row(final_g), tile0=ctx_rows // TM)
    return jnp.swapaxes(out.reshape(seq, NB, d), 0, 1)


def kernel(x, c, ctx, c_ctx, ada_w, ada_b, norm_mix_g, norm_ffn_g, final_g, s5_lambda_re, s5_lambda_im, s5_log_dt, s5_b_re, s5_b_im, s5_c_re, s5_c_im, s5_d, s5_glu_w, s5_glu_b, cv_pw1_w, cv_pw1_b, cv_dw_w, cv_dw_b, cv_ln_g, cv_ln_b, cv_pw2_w, cv_pw2_b, rw_mu, rw_w_r, rw_w_k, rw_w_v, rw_w_o, rw_w0, rw_w1, rw_w2, rw_a0, rw_a1, rw_a2, rw_g1, rw_g2, rw_k_k, rw_k_a, rw_r_k, rw_lnx_g, rw_lnx_b, ff_w1, ff_w3, ff_w2, moe_router, moe_w1, moe_w3, moe_w2):
    s5 = dict(lam_re=s5_lambda_re, lam_im=s5_lambda_im, log_dt=s5_log_dt, b_re=s5_b_re, b_im=s5_b_im,
              c_re=s5_c_re, c_im=s5_c_im, d=s5_d, glu_w=s5_glu_w, glu_b=s5_glu_b)
    cv = dict(pw1_w=cv_pw1_w, pw1_b=cv_pw1_b, dw_w=cv_dw_w, dw_b=cv_dw_b, ln_g=cv_ln_g, ln_b=cv_ln_b,
              pw2_w=cv_pw2_w, pw2_b=cv_pw2_b)
    rw = dict(mu=rw_mu, w_r=rw_w_r, w_k=rw_w_k, w_v=rw_w_v, w_o=rw_w_o, w0=rw_w0, w1=rw_w1, w2=rw_w2,
              a0=rw_a0, a1=rw_a1, a2=rw_a2, g1=rw_g1, g2=rw_g2, k_k=rw_k_k, k_a=rw_k_a, r_k=rw_r_k,
              lnx_g=rw_lnx_g, lnx_b=rw_lnx_b)
    return _forward(x, c, ctx, c_ctx, ada_w, ada_b, norm_mix_g, norm_ffn_g, final_g, s5, cv, rw,
                    ff_w1, ff_w3, ff_w2, moe_router, moe_w1, moe_w3, moe_w2)
```

```python
import functools
import math

import jax
import jax.numpy as jnp
from jax import lax
from jax.experimental import pallas as pl
from jax.experimental.pallas import tpu as pltpu

F32 = jnp.float32
BF16 = jnp.bfloat16

NB = 8
LANES = 128
TM = 512
TT = TM // NB
GRID_W = 64
CONV_WIDTH = 31
CONV_PAD = CONV_WIDTH // 2
HALO_ROWS = 128
S5_GROUP = 16
S5_STATE = 64
S5_JBLK = 8
RWKV_HEAD = 64
WKV_L = 64
WKV_GROUP = 8
RW_TM = 256
FFN_TM = 1024
MOE_T = 1024
MOE_BLK = 64
MOE_SLOTS = 16
RMS_EPS = 1e-6
LN_EPS = 1e-5
GN_EPS = 64e-5
KK_EPS = 1e-12
LAMBDA_RE_MAX = -1e-4
VMEM_LIMIT = 56 * 1024 * 1024


def _cparams(sem):
    return pltpu.CompilerParams(dimension_semantics=sem, vmem_limit_bytes=VMEM_LIMIT)


def _dot(a, b):
    return jnp.dot(a, b, preferred_element_type=F32)


def _dot_nt(a, b):
    return lax.dot_general(a, b, (((1,), (1,)), ((), ())), preferred_element_type=F32)


def _split3(x):
    h1 = x.astype(BF16)
    r1 = x - h1.astype(F32)
    h2 = r1.astype(BF16)
    h3 = (r1 - h2.astype(F32)).astype(BF16)
    return h1, h2, h3


def _dot_exact_rhs(x, m_bf16):
    h1, h2, h3 = _split3(x)
    return _dot(h1, m_bf16) + _dot(h2, m_bf16) + _dot(h3, m_bf16)


def _dot_exact_lhs(m_bf16, x):
    h1, h2, h3 = _split3(x)
    return _dot(m_bf16, h1) + _dot(m_bf16, h2) + _dot(m_bf16, h3)


def _dot_x3(a, b):
    a1 = a.astype(BF16)
    a2 = (a - a1.astype(F32)).astype(BF16)
    b1 = b.astype(BF16)
    b2 = (b - b1.astype(F32)).astype(BF16)
    return _dot(a1, b1) + _dot(a1, b2) + _dot(a2, b1)


def _sigmoid(x):
    return jax.nn.sigmoid(x)


def _norm_mod(x, g, shift, scale):
    rows, d = x.shape
    ms = jnp.mean(x * x, axis=-1, keepdims=True)
    y = x * lax.rsqrt(ms + RMS_EPS) * g
    y3 = y.reshape(rows // NB, NB, d)
    return (y3 * (1.0 + scale)[None] + shift[None]).reshape(rows, d)


def _residual(x, gate, y):
    rows, d = x.shape
    y3 = y.reshape(rows // NB, NB, d) * gate[None]
    return x + y3.reshape(rows, d)


def _is_lat(i, nct):
    return jnp.where(i >= nct, 1, 0).astype(jnp.int32)


def _ada_kernel(s_ref, w_ref, b_ref, o_ref):
    s = s_ref[...]
    s = s * _sigmoid(s)
    o_ref[0] = _dot(s.astype(BF16), w_ref[0].astype(BF16)) + b_ref[0]


def _ada_call(cvec, ada_w, ada_b):
    depth, d, d6 = ada_w.shape
    tn = 1536
    return pl.pallas_call(
        _ada_kernel,
        grid=(depth, d6 // tn),
        in_specs=[
            pl.BlockSpec((2 * NB, d), lambda l, j: (0, 0)),
            pl.BlockSpec((1, d, tn), lambda l, j: (l, 0, j)),
            pl.BlockSpec((1, 1, tn), lambda l, j: (l, 0, j)),
        ],
        out_specs=pl.BlockSpec((1, 2 * NB, tn), lambda l, j: (l, 0, j)),
        out_shape=jax.ShapeDtypeStruct((depth, 2 * NB, d6), F32),
        compiler_params=_cparams(("arbitrary", "arbitrary")),
    )(cvec, ada_w, ada_b.reshape(depth, 1, d6))


def _ffn_kernel(x_ref, mod_ref, g_ref, w1_ref, w3_ref, w2_ref, o_ref, hb_ref, acc_ref, *, n_k):
    k = pl.program_id(1)

    @pl.when(k == 0)
    def _():
        h = _norm_mod(x_ref[...], g_ref[...], mod_ref[0, 3], mod_ref[0, 4])
        hb_ref[...] = h.astype(BF16)
        acc_ref[...] = jnp.zeros_like(acc_ref)

    hb = hb_ref[...]
    a = _dot(hb, w1_ref[0].astype(BF16))
    b = _dot(hb, w3_ref[0].astype(BF16))
    acc_ref[...] += _dot((a * _sigmoid(a) * b).astype(BF16), w2_ref[0].astype(BF16))

    @pl.when(k == n_k - 1)
    def _():
        o_ref[...] = _residual(x_ref[...], mod_ref[0, 5], acc_ref[...])


def _ffn_call(x, mod, g, w1, w3, w2, *, layer, nct_rows):
    n, d = x.shape
    f = w1.shape[2]
    tm = FFN_TM
    tf = 512
    n_k = f // tf
    nct = nct_rows // tm
    return pl.pallas_call(
        functools.partial(_ffn_kernel, n_k=n_k),
        grid=(n // tm, n_k),
        in_specs=[
            pl.BlockSpec((tm, d), lambda i, k: (i, 0)),
            pl.BlockSpec((1, 6, NB, d), lambda i, k: (_is_lat(i, nct), 0, 0, 0)),
            pl.BlockSpec((1, d), lambda i, k: (0, 0)),
            pl.BlockSpec((1, d, tf), lambda i, k: (layer, 0, k)),
            pl.BlockSpec((1, d, tf), lambda i, k: (layer, 0, k)),
            pl.BlockSpec((1, tf, d), lambda i, k: (layer, k, 0)),
        ],
        out_specs=pl.BlockSpec((tm, d), lambda i, k: (i, 0)),
        out_shape=jax.ShapeDtypeStruct((n, d), F32),
        scratch_shapes=[pltpu.VMEM((tm, d), BF16), pltpu.VMEM((tm, d), F32)],
        compiler_params=_cparams(("arbitrary", "arbitrary")),
    )(x, mod, g, w1, w3, w2)


def _top2(logits, n_e):
    lane = lax.broadcasted_iota(jnp.int32, logits.shape, 1).astype(F32)
    neg = jnp.float32(-3.0e38)
    lg = jnp.where(lane < n_e, logits, neg)
    m1 = jnp.max(lg, axis=-1, keepdims=True)
    i1 = jnp.min(jnp.where(lg == m1, lane, 1.0e9), axis=-1, keepdims=True)
    lg2 = jnp.where(lane == i1, neg, lg)
    m2 = jnp.max(lg2, axis=-1, keepdims=True)
    i2 = jnp.min(jnp.where(lg2 == m2, lane, 1.0e9), axis=-1, keepdims=True)
    e2 = jnp.exp(m2 - m1)
    den = 1.0 + e2
    return lane, i1, i2, 1.0 / den, e2 / den


def _moe_route_kernel(x_ref, mod_ref, g_ref, router_ref, ltri_ref, z_ref, dest_ref, cnt_ref,
                      hb_ref, *, n_e, cap):
    t, d = x_ref.shape
    h = _norm_mod(x_ref[...], g_ref[...], mod_ref[0, 3], mod_ref[0, 4])
    hb_ref[...] = h.astype(BF16)
    lane, i1, i2, p1, p2 = _top2(_dot_x3(h, router_ref[...]), n_e)
    sel1 = lane == i1
    sel2 = lane == i2
    c = jnp.where(sel1, 1.0, 0.0) + jnp.where(sel2, 1.0, 0.0)
    rank = _dot(ltri_ref[...], c.astype(BF16))
    cnt = jnp.sum(c, axis=0, keepdims=True)
    seg = jnp.floor((cnt + (MOE_BLK - 1)) * (1.0 / MOE_BLK)) * MOE_BLK
    ur = lax.broadcasted_iota(jnp.int32, (LANES, LANES), 0)
    uc = lax.broadcasted_iota(jnp.int32, (LANES, LANES), 1)
    before = jnp.where(ur < uc, 1.0, 0.0).astype(BF16)
    start = _dot(jnp.broadcast_to(seg, (NB, LANES)).astype(BF16), before)[0:1]
    pos = start + rank
    d1 = jnp.sum(jnp.where(sel1, pos, 0.0), axis=-1, keepdims=True)
    d2 = jnp.sum(jnp.where(sel2, pos, 0.0), axis=-1, keepdims=True)
    packed = (jnp.where(lane == 0.0, d1, 0.0) + jnp.where(lane == 1.0, d2, 0.0)
              + jnp.where(lane == 2.0, p1, 0.0) + jnp.where(lane == 3.0, p2, 0.0))
    dest_ref[...] = packed
    cnt_ref[0] = jnp.broadcast_to(cnt, (NB, LANES))
    packed_t = packed.T
    d1r = packed_t[0:1]
    d2r = packed_t[1:2]
    p1r = packed_t[2:3]
    p2r = packed_t[3:4]
    rc = 256

    def chunk(ci, carry):
        r0 = pl.multiple_of(ci * rc, rc)
        row = (lax.broadcasted_iota(jnp.int32, (rc, t), 0) + r0).astype(F32)
        m1 = row == d1r
        m2 = row == d2r
        perm = (jnp.where(m1, 1.0, 0.0) + jnp.where(m2, 1.0, 0.0)).astype(BF16)
        zc = _dot(perm, hb_ref[...])
        gate = jnp.sum(jnp.where(m1, p1r, 0.0) + jnp.where(m2, p2r, 0.0), axis=-1, keepdims=True)
        ghi = gate.astype(BF16).astype(F32)
        glo = gate - ghi
        le = lax.broadcasted_iota(jnp.int32, (rc, LANES), 1)
        ext = jnp.where(le == 0, ghi, jnp.where(le == 1, glo, 0.0))
        z_ref[pl.ds(r0, rc), 0:d] = zc.astype(BF16)
        z_ref[pl.ds(r0, rc), d:d + LANES] = ext.astype(BF16)
        return carry

    lax.fori_loop(0, cap // rc, chunk, 0)


def _moe_experts_kernel(tin_ref, se_ref, nu_ref, *refs, n_k, d):
    z_refs = refs[:MOE_SLOTS]
    w1_ref, w3_ref, w2_ref, o_ref, xs_ref, acc_ref = refs[MOE_SLOTS:]
    s = pl.program_id(0)
    k = pl.program_id(1)
    used = s < nu_ref[0]

    @pl.when(used & (k == 0))
    def _():
        for j in range(MOE_SLOTS):
            xs_ref[j * MOE_BLK:(j + 1) * MOE_BLK] = z_refs[j][...]
        acc_ref[...] = jnp.zeros_like(acc_ref)

    @pl.when(used)
    def _():
        hb = xs_ref[:, 0:d]
        ext = xs_ref[:, d:d + LANES].astype(F32)
        le = lax.broadcasted_iota(jnp.int32, ext.shape, 1)
        gate = jnp.sum(jnp.where(le < 2, ext, 0.0), axis=-1, keepdims=True)
        a = _dot(hb, w1_ref[0, 0].astype(BF16))
        b = _dot(hb, w3_ref[0, 0].astype(BF16))
        act = a * _sigmoid(a) * b * gate
        acc_ref[...] += _dot(act.astype(BF16), w2_ref[0, 0].astype(BF16))

    @pl.when(k == n_k - 1)
    def _():
        o_ref[...] = jnp.where(used, acc_ref[...], 0.0).astype(BF16)


def _moe_combine_kernel(tc_ref, *refs, nblk, cap):
    zb_refs = refs[:nblk]
    dest_ref, x_ref, mod_ref, o_ref, zbuf_ref = refs[nblk:]
    t, d = x_ref.shape
    for j in range(nblk):
        zbuf_ref[j * MOE_BLK:(j + 1) * MOE_BLK] = zb_refs[j][...]
    packed = dest_ref[...]
    lane = lax.broadcasted_iota(jnp.int32, packed.shape, 1)
    d1 = jnp.sum(jnp.where(lane == 0, packed, 0.0), axis=-1, keepdims=True)
    d2 = jnp.sum(jnp.where(lane == 1, packed, 0.0), axis=-1, keepdims=True)
    kc = 512
    acc = jnp.zeros((t, d), F32)
    for c0 in range(0, cap, kc):
        col = (lax.broadcasted_iota(jnp.int32, (t, kc), 1) + c0).astype(F32)
        sel = (jnp.where(col == d1, 1.0, 0.0) + jnp.where(col == d2, 1.0, 0.0)).astype(BF16)
        acc = acc + _dot(sel, zbuf_ref[c0:c0 + kc])
    o_ref[...] = _residual(x_ref[...], mod_ref[0, 5], acc)


def _moe_tables(cnt, *, n_tiles, n_e, nblk, n_steps):
    i32 = jnp.int32
    pc = (cnt + (MOE_BLK - 1)) // MOE_BLK
    lb = jnp.cumsum(pc, axis=1) - pc
    cum = jnp.cumsum(pc, axis=0)
    cum_ex = cum - pc
    nb_e = cum[-1]
    steps_e = (nb_e + (MOE_SLOTS - 1)) // MOE_SLOTS
    ends = jnp.cumsum(steps_e)
    step0 = ends - steps_e
    sidx = jnp.arange(n_steps, dtype=i32)
    step_e = jnp.minimum(jnp.sum((sidx[:, None] >= ends[None, :]).astype(i32), axis=1), n_e - 1)
    rank = ((sidx - step0[step_e])[:, None] * MOE_SLOTS
            + jnp.arange(MOE_SLOTS, dtype=i32)[None, :])
    cum_s = cum.T[step_e]
    tile = jnp.sum((cum_s[:, None, :] <= rank[:, :, None]).astype(i32), axis=2)
    ok = (sidx < ends[-1])[:, None] & (rank < nb_e[step_e][:, None])
    tile = jnp.minimum(tile, n_tiles - 1)
    e_b = jnp.broadcast_to(step_e[:, None], tile.shape)
    src = tile * nblk + lb[tile, e_b] + rank - cum_ex[tile, e_b]
    tin = jnp.where(ok, src, 0).reshape(-1).astype(i32)
    lblk = jnp.arange(nblk, dtype=i32)[None, :]
    seg_end = lb + pc
    e_l = jnp.minimum(jnp.sum((seg_end[:, None, :] <= lblk[:, :, None]).astype(i32), axis=2), n_e - 1)
    t_l = jnp.broadcast_to(jnp.arange(n_tiles, dtype=i32)[:, None], e_l.shape)
    r_l = cum_ex[t_l, e_l] + lblk - lb[t_l, e_l]
    flat = (step0[e_l] + r_l // MOE_SLOTS) * MOE_SLOTS + r_l % MOE_SLOTS
    tcomb = jnp.where(lblk < seg_end[:, -1:], flat, 0).reshape(-1).astype(i32)
    return tin, tcomb, step_e.astype(i32), ends[-1:].astype(i32)


def _moe_call(x, mod, g, router, w1, w3, w2, *, layer, nct_rows):
    n, d = x.shape
    _, n_e, _, f = w1.shape
    t = MOE_T
    assert n % t == 0 and nct_rows % t == 0
    n_tiles = n // t
    nct = nct_rows // t
    cap = 2 * t + n_e * MOE_BLK
    nblk = cap // MOE_BLK
    rows = MOE_SLOTS * MOE_BLK
    n_steps = -(-(n_tiles * nblk) // MOE_SLOTS) + n_e
    tf = 512
    n_k = f // tf
    router_p = jnp.pad(router.astype(F32), ((0, 0), (0, LANES - n_e)))
    ltri = jnp.tril(jnp.ones((t, t), F32), k=-1).astype(BF16)
    mod_spec = lambda off: pl.BlockSpec((1, 6, NB, d), lambda i, *_: (_is_lat(i, nct), 0, 0, 0))

    z, dest, cnt = pl.pallas_call(
        functools.partial(_moe_route_kernel, n_e=n_e, cap=cap),
        grid=(n_tiles,),
        in_specs=[
            pl.BlockSpec((t, d), lambda i: (i, 0)),
            mod_spec(0),
            pl.BlockSpec((1, d), lambda i: (0, 0)),
            pl.BlockSpec((d, LANES), lambda i: (0, 0)),
            pl.BlockSpec((t, t), lambda i: (0, 0)),
        ],
        out_specs=[
            pl.BlockSpec((cap, d + LANES), lambda i: (i, 0)),
            pl.BlockSpec((t, LANES), lambda i: (i, 0)),
            pl.BlockSpec((1, NB, LANES), lambda i: (i, 0, 0)),
        ],
        out_shape=[
            jax.ShapeDtypeStruct((n_tiles * cap, d + LANES), BF16),
            jax.ShapeDtypeStruct((n, LANES), F32),
            jax.ShapeDtypeStruct((n_tiles, NB, LANES), F32),
        ],
        scratch_shapes=[pltpu.VMEM((t, d), BF16)],
        compiler_params=_cparams(("arbitrary",)),
    )(x, mod, g, router_p, ltri)

    tin, tcomb, step_e, nused = _moe_tables(cnt[:, 0, :n_e].astype(jnp.int32), n_tiles=n_tiles,
                                            n_e=n_e, nblk=nblk, n_steps=n_steps)

    def z_spec(j):
        return pl.BlockSpec((MOE_BLK, d + LANES), lambda s, k, tin_r, se_r, nu_r: (tin_r[s * MOE_SLOTS + j], 0))

    zo = pl.pallas_call(
        functools.partial(_moe_experts_kernel, n_k=n_k, d=d),
        grid_spec=pltpu.PrefetchScalarGridSpec(
            num_scalar_prefetch=3,
            grid=(n_steps, n_k),
            in_specs=[z_spec(j) for j in range(MOE_SLOTS)] + [
                pl.BlockSpec((1, 1, d, tf), lambda s, k, tin_r, se_r, nu_r: (layer, se_r[s], 0, k)),
                pl.BlockSpec((1, 1, d, tf), lambda s, k, tin_r, se_r, nu_r: (layer, se_r[s], 0, k)),
                pl.BlockSpec((1, 1, tf, d), lambda s, k, tin_r, se_r, nu_r: (layer, se_r[s], k, 0)),
            ],
            out_specs=pl.BlockSpec((rows, d), lambda s, k, tin_r, se_r, nu_r: (s, 0)),
            scratch_shapes=[pltpu.VMEM((rows, d + LANES), BF16), pltpu.VMEM((rows, d), F32)],
        ),
        out_shape=jax.ShapeDtypeStruct((n_steps * rows, d), BF16),
        compiler_params=_cparams(("arbitrary", "arbitrary")),
    )(tin, step_e, nused, *([z] * MOE_SLOTS), w1, w3, w2)

    def zo_spec(j):
        return pl.BlockSpec((MOE_BLK, d), lambda i, tc_r: (tc_r[i * nblk + j], 0))

    return pl.pallas_call(
        functools.partial(_moe_combine_kernel, nblk=nblk, cap=cap),
        grid_spec=pltpu.PrefetchScalarGridSpec(
            num_scalar_prefetch=1,
            grid=(n_tiles,),
            in_specs=[zo_spec(j) for j in range(nblk)] + [
                pl.BlockSpec((t, LANES), lambda i, tc_r: (i, 0)),
                pl.BlockSpec((t, d), lambda i, tc_r: (i, 0)),
                mod_spec(0),
            ],
            out_specs=pl.BlockSpec((t, d), lambda i, tc_r: (i, 0)),
            scratch_shapes=[pltpu.VMEM((cap, d), BF16)],
        ),
        out_shape=jax.ShapeDtypeStruct((n, d), F32),
        compiler_params=_cparams(("arbitrary",)),
    )(tcomb, *([zo] * nblk), dest, x, mod)


def _dir_tile(d, s, nct, nt):
    bwd = jnp.where(s < nct, nct - 1 - s, nt - 1 + nct - s)
    return jnp.where(d == 0, s, bwd)


def _s5_scan_kernel(x_ref, mod_ref, g_ref, bb_ref, cc_ref, ab_ref, y_ref,
                    hb_ref, bu0_ref, bu1_ref, bu2_ref, st_ref):
    d = pl.program_id(0)
    s = pl.program_id(1)
    half = S5_JBLK * S5_STATE
    bufs = (bu0_ref, bu1_ref, bu2_ref)

    @pl.when(s == 0)
    def _():
        st_ref[...] = jnp.zeros_like(st_ref)

    h = _norm_mod(x_ref[...], g_ref[...], mod_ref[0, 0], mod_ref[0, 1])
    hb_ref[...] = h.astype(BF16)

    def project(j):
        bufs[j % 3][...] = _dot(hb_ref[:, j * LANES:(j + 1) * LANES], bb_ref[0, j])

    def scan(j):
        buf = bufs[j % 3]
        ar = ab_ref[0, j, 0]
        ai = ab_ref[0, j, 1]
        re = st_ref[j, 0]
        im = st_ref[j, 1]
        for t in range(TT):
            tt = t + d * (TT - 1 - 2 * t)
            r0 = pl.multiple_of(tt * NB, NB)
            bre = buf[pl.ds(r0, NB), 0:half]
            bim = buf[pl.ds(r0, NB), half:2 * half]
            re, im = ar * re - ai * im + bre, ar * im + ai * re + bim
            buf[pl.ds(r0, NB), 0:half] = re
            buf[pl.ds(r0, NB), half:2 * half] = im
        st_ref[j, 0] = re
        st_ref[j, 1] = im

    def readout(j):
        y_ref[0, j] = _dot(bufs[j % 3][...].astype(BF16), cc_ref[0, j])

    for i in range(S5_JBLK + 2):
        if i < S5_JBLK:
            project(i)
        if 0 <= i - 1 < S5_JBLK:
            scan(i - 1)
        if 0 <= i - 2 < S5_JBLK:
            readout(i - 2)


def _s5_scan_call(x, mod, g, bb, cc, ab, *, nct):
    n, d = x.shape
    nt = n // TM
    half = S5_JBLK * S5_STATE
    tile = lambda dd, s: _dir_tile(dd, s, nct, nt)
    return pl.pallas_call(
        _s5_scan_kernel,
        grid=(2, nt),
        in_specs=[
            pl.BlockSpec((TM, d), lambda dd, s: (tile(dd, s), 0)),
            pl.BlockSpec((1, 6, NB, d), lambda dd, s: (_is_lat(tile(dd, s), nct), 0, 0, 0)),
            pl.BlockSpec((1, d), lambda dd, s: (0, 0)),
            pl.BlockSpec((1, S5_JBLK, LANES, 2 * half), lambda dd, s: (dd, 0, 0, 0)),
            pl.BlockSpec((1, S5_JBLK, 2 * half, LANES), lambda dd, s: (dd, 0, 0, 0)),
            pl.BlockSpec((1, S5_JBLK, 2, NB, half), lambda dd, s: (dd, 0, 0, 0, 0)),
        ],
        out_specs=pl.BlockSpec((1, S5_JBLK, TM, LANES), lambda dd, s: (dd, 0, tile(dd, s), 0)),
        out_shape=jax.ShapeDtypeStruct((2, S5_JBLK, n, LANES), F32),
        scratch_shapes=[
            pltpu.VMEM((TM, d), BF16),
            pltpu.VMEM((TM, 2 * half), F32),
            pltpu.VMEM((TM, 2 * half), F32),
            pltpu.VMEM((TM, 2 * half), F32),
            pltpu.VMEM((S5_JBLK, 2, NB, half), F32),
        ],
        compiler_params=_cparams(("arbitrary", "arbitrary")),
    )(x, mod, g, bb, cc, ab)


def _gelu_tanh(x):
    c = math.sqrt(2.0 / math.pi)
    return 0.5 * x * (1.0 + jnp.tanh(c * (x + 0.044715 * (x * x * x))))


def _s5_glu_kernel(x_ref, mod_ref, g_ref, y_ref, dsk_ref, w_ref, b_ref, o_ref):
    x = x_ref[...]
    d = x.shape[1]
    h = _norm_mod(x, g_ref[...], mod_ref[0, 0], mod_ref[0, 1])
    y = jnp.concatenate([y_ref[0, j] + y_ref[1, j] for j in range(S5_JBLK)], axis=1)
    z = _gelu_tanh(y + dsk_ref[...] * h).astype(BF16)
    zz = _dot(z, w_ref[...]) + b_ref[...]
    out = zz[:, :d] * _sigmoid(zz[:, d:])
    o_ref[...] = _residual(x, mod_ref[0, 2], out)


def _s5_glu_call(x, mod, g, y, dsk, w, b, *, nct, tile0):
    n, d = x.shape
    nt = n // TM
    return pl.pallas_call(
        _s5_glu_kernel,
        grid=(nt - tile0,),
        in_specs=[
            pl.BlockSpec((TM, d), lambda i: (i + tile0, 0)),
            pl.BlockSpec((1, 6, NB, d), lambda i: (_is_lat(i + tile0, nct), 0, 0, 0)),
            pl.BlockSpec((1, d), lambda i: (0, 0)),
            pl.BlockSpec((2, S5_JBLK, TM, LANES), lambda i: (0, 0, i + tile0, 0)),
            pl.BlockSpec((1, d), lambda i: (0, 0)),
            pl.BlockSpec((d, 2 * d), lambda i: (0, 0)),
            pl.BlockSpec((1, 2 * d), lambda i: (0, 0)),
        ],
        out_specs=pl.BlockSpec((TM, d), lambda i: (i, 0)),
        out_shape=jax.ShapeDtypeStruct((n - tile0 * TM, d), F32),
        compiler_params=_cparams(("arbitrary",)),
    )(x, mod, g, y, dsk, w, b)


def _s5_params(lam_re, lam_im, log_dt, b_re, b_im, c_re, c_im):
    lam_re = jnp.minimum(lam_re.astype(F32), LAMBDA_RE_MAX)
    lam_im = lam_im.astype(F32)
    dt = jnp.exp(log_dt.astype(F32))[..., None]
    mag = jnp.exp(lam_re * dt)
    abar_re = mag * jnp.cos(lam_im * dt)
    abar_im = mag * jnp.sin(lam_im * dt)
    inv_den = 1.0 / (lam_re * lam_re + lam_im * lam_im)
    gam_re = ((abar_re - 1.0) * lam_re + abar_im * lam_im) * inv_den
    gam_im = (abar_im * lam_re - (abar_re - 1.0) * lam_im) * inv_den
    b_re = b_re.astype(F32)
    b_im = b_im.astype(F32)
    bb_re = gam_re[..., None] * b_re - gam_im[..., None] * b_im
    bb_im = gam_re[..., None] * b_im + gam_im[..., None] * b_re
    g_all = lam_re.shape[1]
    nj = g_all // S5_JBLK
    eye = jnp.eye(S5_JBLK, dtype=F32)

    def in_mat(m):
        m = m.reshape(2, nj, S5_JBLK, S5_STATE, S5_GROUP)
        m = jnp.einsum('djgnp,gh->djgphn', m, eye)
        return m.reshape(2, nj, S5_JBLK * S5_GROUP, S5_JBLK * S5_STATE)

    def out_mat(m):
        m = m.reshape(2, nj, S5_JBLK, S5_GROUP, S5_STATE)
        m = jnp.einsum('djgpn,gh->djgnhp', m, eye)
        return m.reshape(2, nj, S5_JBLK * S5_STATE, S5_JBLK * S5_GROUP)

    bb = jnp.concatenate([in_mat(bb_re), in_mat(bb_im)], axis=-1).astype(BF16)
    cc = jnp.concatenate([out_mat(c_re.astype(F32)), -out_mat(c_im.astype(F32))], axis=-2).astype(BF16)
    half = S5_JBLK * S5_STATE
    ab = jnp.stack([abar_re.reshape(2, nj, half), abar_im.reshape(2, nj, half)], axis=2)
    ab = jnp.broadcast_to(ab[:, :, :, None, :], (2, nj, 2, NB, half))
    return bb, cc, ab


def _cv_glu_kernel(x_ref, mod_ref, g_ref, w_ref, b_ref, o_ref):
    x = x_ref[...]
    d = x.shape[1]
    h = _norm_mod(x, g_ref[...], mod_ref[0, 0], mod_ref[0, 1]).astype(BF16)
    u = _dot(h, w_ref[...]) + b_ref[...]
    o_ref[...] = u[:, :d] * _sigmoid(u[:, d:])


def _cv_glu_call(x, mod, g, w, b, *, nct):
    n, d = x.shape
    nt = n // TM
    return pl.pallas_call(
        _cv_glu_kernel,
        grid=(nt,),
        in_specs=[
            pl.BlockSpec((TM, d), lambda i: (i, 0)),
            pl.BlockSpec((1, 6, NB, d), lambda i: (_is_lat(i, nct), 0, 0, 0)),
            pl.BlockSpec((1, d), lambda i: (0, 0)),
            pl.BlockSpec((d, 2 * d), lambda i: (0, 0)),
            pl.BlockSpec((1, 2 * d), lambda i: (0, 0)),
        ],
        out_specs=pl.BlockSpec((TM, d), lambda i: (i, 0)),
        out_shape=jax.ShapeDtypeStruct((n, d), F32),
        compiler_params=_cparams(("arbitrary",)),
    )(x, mod, g, w, b)


def _cv_conv_kernel(u_ref, up_ref, un_ref, x_ref, mod_ref, dw_ref, dwb_ref, lng_ref, lnb_ref,
                    w_ref, b_ref, o_ref, pad_ref, cv_ref, *, nct):
    i = pl.program_id(0)
    halo = CONV_PAD * NB
    prev_ok = (i > 0) & (i < nct)
    next_ok = i < nct - 1
    pad_ref[0:halo] = jnp.where(prev_ok, up_ref[HALO_ROWS - halo:HALO_ROWS], 0.0)
    pad_ref[halo:halo + TM] = u_ref[...]
    pad_ref[halo + TM:halo + TM + halo] = jnp.where(next_ok, un_ref[0:halo], 0.0)

    rb = 32

    def body(c, _):
        r0 = pl.multiple_of(c * rb, rb)
        acc = jnp.broadcast_to(dwb_ref[...], (rb, dwb_ref.shape[1]))
        for k in range(CONV_WIDTH):
            acc = acc + dw_ref[k] * pad_ref[pl.ds(r0 + k * NB, rb)]
        cv_ref[pl.ds(r0, rb)] = acc
        return 0

    lax.fori_loop(0, TM // rb, body, 0)

    c = cv_ref[...]
    mu = jnp.mean(c, axis=-1, keepdims=True)
    cc = c - mu
    var = jnp.mean(cc * cc, axis=-1, keepdims=True)
    y = cc * lax.rsqrt(var + LN_EPS) * lng_ref[...] + lnb_ref[...]
    y = (y * _sigmoid(y)).astype(BF16)
    out = _dot(y, w_ref[...]) + b_ref[...]
    o_ref[...] = _residual(x_ref[...], mod_ref[0, 2], out)


def _cv_conv_call(u, x, mod, dw_w, dw_b, ln_g, ln_b, w, b, *, nct):
    n, d = x.shape
    nt = n // TM
    hb = TM // HALO_ROWS
    nhb = n // HALO_ROWS
    kern = functools.partial(_cv_conv_kernel, nct=nct)
    row = lambda i: (0, 0)
    return pl.pallas_call(
        kern,
        grid=(nt,),
        in_specs=[
            pl.BlockSpec((TM, d), lambda i: (i, 0)),
            pl.BlockSpec((HALO_ROWS, d), lambda i: (jnp.maximum(i * hb - 1, 0), 0)),
            pl.BlockSpec((HALO_ROWS, d), lambda i: (jnp.minimum((i + 1) * hb, nhb - 1), 0)),
            pl.BlockSpec((TM, d), lambda i: (i, 0)),
            pl.BlockSpec((1, 6, NB, d), lambda i: (_is_lat(i, nct), 0, 0, 0)),
            pl.BlockSpec((CONV_WIDTH, 1, d), lambda i: (0, 0, 0)),
            pl.BlockSpec((1, d), row),
            pl.BlockSpec((1, d), row),
            pl.BlockSpec((1, d), row),
            pl.BlockSpec((d, d), row),
            pl.BlockSpec((1, d), row),
        ],
        out_specs=pl.BlockSpec((TM, d), lambda i: (i, 0)),
        out_shape=jax.ShapeDtypeStruct((n, d), F32),
        scratch_shapes=[
            pltpu.VMEM((TM + 2 * CONV_PAD * NB, d), F32),
            pltpu.VMEM((TM, d), F32),
        ],
        compiler_params=_cparams(("arbitrary",)),
    )(u, u, u, x, mod, dw_w.reshape(CONV_WIDTH, 1, d), dw_b, ln_g, ln_b, w, b)


def _seg_mats(d):
    ch = lax.broadcasted_iota(jnp.int32, (d, LANES), 0) // RWKV_HEAD
    hd = lax.broadcasted_iota(jnp.int32, (d, LANES), 1)
    seg = jnp.where(ch == hd, 1.0, 0.0).astype(BF16)
    hd_t = lax.broadcasted_iota(jnp.int32, (LANES, d), 0)
    ch_t = lax.broadcasted_iota(jnp.int32, (LANES, d), 1) // RWKV_HEAD
    seg_t = jnp.where(ch_t == hd_t, 1.0, 0.0).astype(BF16)
    return seg, seg_t


def _head_sum_bcast(x, seg, seg_t):
    return _dot_exact_rhs(_dot_exact_rhs(x, seg), seg_t)


def _rw_proj_kernel(x_ref, xp_ref, xn_ref, mod_ref, g_ref, mu_ref, wr_ref, wk_ref, wv_ref,
                    w1_ref, w2_ref, w0_ref, a1_ref, a2_ref, a0_ref, g1_ref, g2_ref, kkw_ref,
                    r_ref, k_ref, v_ref, kk_ref, ew_ref, as_ref, gg_ref, hp_ref, *, nct, nt):
    i = pl.program_id(0)
    tm, d = x_ref.shape
    g = g_ref[...]
    sh = mod_ref[0, 0]
    sc = mod_ref[0, 1]
    h = _norm_mod(x_ref[...], g, sh, sc)
    hp = _norm_mod(xp_ref[...], g, sh, sc)
    hn = _norm_mod(xn_ref[...], g, sh, sc)
    prev_ok = (i != 0) & (i != nct)
    next_ok = (i != nct - 1) & (i != nt - 1)
    hp_ref[0:NB] = jnp.where(prev_ok, hp, 0.0)
    hp_ref[NB:NB + tm] = h
    hp_ref[NB + tm:NB + tm + NB] = jnp.where(next_ok, hn, 0.0)
    xx = 0.5 * (hp_ref[0:tm] + hp_ref[2 * NB:2 * NB + tm]) - h

    def mix(j):
        return (h + xx * mu_ref[j]).astype(BF16)

    r = _dot(mix(0), wr_ref[...])
    k = _dot(mix(2), wk_ref[...])
    v = _dot(mix(3), wv_ref[...])
    zw = w0_ref[...] + _dot(jnp.tanh(_dot(mix(1), w1_ref[...])).astype(BF16), w2_ref[...])
    ew = _sigmoid(zw) * math.exp(-0.5)
    asg = _sigmoid(a0_ref[...] + _dot(_dot(mix(4), a1_ref[...]).astype(BF16), a2_ref[...]))
    gg_ref[...] = _dot(_sigmoid(_dot(mix(5), g1_ref[...])).astype(BF16), g2_ref[...])

    seg, seg_t = _seg_mats(d)
    kk = k * kkw_ref[...]
    kk = kk * lax.rsqrt(_head_sum_bcast(kk * kk, seg, seg_t) + KK_EPS)

    for p in range(d // LANES):
        sl = slice(p * LANES, (p + 1) * LANES)
        r_ref[p] = r[:, sl]
        k_ref[p] = k[:, sl]
        v_ref[p] = v[:, sl]
        kk_ref[p] = kk[:, sl]
        for dr in range(2):
            sl2 = slice(dr * d + p * LANES, dr * d + (p + 1) * LANES)
            ew_ref[dr, p] = ew[:, sl2]
            as_ref[dr, p] = asg[:, sl2]


def _rw_proj_call(x, mod, g, mu, wr, wk, wv, w1, w2, w0, a1, a2, a0, g1, g2, kkw, *, nct):
    n, d = x.shape
    tm = RW_TM
    nt = n // tm
    nct = nct * (TM // tm)
    npair = d // LANES
    tb = tm // NB
    nb8 = n // NB
    kern = functools.partial(_rw_proj_kernel, nct=nct, nt=nt)
    full2 = lambda a: pl.BlockSpec(a.shape, lambda i: (0, 0))
    slab = pl.BlockSpec((npair, tm, LANES), lambda i: (0, i, 0))
    slab2 = pl.BlockSpec((2, npair, tm, LANES), lambda i: (0, 0, i, 0))
    slab_shape = jax.ShapeDtypeStruct((npair, n, LANES), F32)
    slab2_shape = jax.ShapeDtypeStruct((2, npair, n, LANES), F32)
    mu3 = mu.reshape(6, 1, d)
    return pl.pallas_call(
        kern,
        grid=(nt,),
        in_specs=[
            pl.BlockSpec((tm, d), lambda i: (i, 0)),
            pl.BlockSpec((NB, d), lambda i: (jnp.maximum(i * tb - 1, 0), 0)),
            pl.BlockSpec((NB, d), lambda i: (jnp.minimum((i + 1) * tb, nb8 - 1), 0)),
            pl.BlockSpec((1, 6, NB, d), lambda i: (_is_lat(i, nct), 0, 0, 0)),
            pl.BlockSpec((1, d), lambda i: (0, 0)),
            pl.BlockSpec((6, 1, d), lambda i: (0, 0, 0)),
            full2(wr), full2(wk), full2(wv),
            full2(w1), full2(w2), full2(w0), full2(a1), full2(a2), full2(a0),
            full2(g1), full2(g2), full2(kkw),
        ],
        out_specs=[slab, slab, slab, slab, slab2, slab2, pl.BlockSpec((tm, d), lambda i: (i, 0))],
        out_shape=[slab_shape, slab_shape, slab_shape, slab_shape, slab2_shape, slab2_shape,
                   jax.ShapeDtypeStruct((n, d), F32)],
        scratch_shapes=[pltpu.VMEM((tm + 2 * NB, d), F32)],
        compiler_params=_cparams(("arbitrary",)),
    )(x, x, x, mod, g, mu3, wr, wk, wv, w1, w2, w0, a1, a2, a0, g1, g2, kkw)


def _wkv_kernel(r_ref, k_ref, v_ref, kk_ref, ew_ref, as_ref, ka_ref, y_ref, st_ref, *, L, npair):
    d = pl.program_id(0)
    c = pl.program_id(1)

    @pl.when(c == 0)
    def _():
        st_ref[...] = jnp.zeros_like(st_ref)

    sgn = 1 - 2 * d
    l2 = 2 * L
    rr = lax.broadcasted_iota(jnp.int32, (l2, l2), 0)
    cc = lax.broadcasted_iota(jnp.int32, (l2, l2), 1)
    same = (rr >= L) == (cc >= L)
    diff = (rr - cc) * sgn
    strict_f = jnp.where(same & (diff > 0), 1.0, 0.0)
    incl_f = jnp.where(same & (diff >= 0), 1.0, 0.0)
    r1 = lax.broadcasted_iota(jnp.int32, (L, L), 0)
    c1 = lax.broadcasted_iota(jnp.int32, (L, L), 1)
    mcum = jnp.where((r1 - c1) * sgn >= 0, 1.0, 0.0).astype(BF16)
    head0 = lax.broadcasted_iota(jnp.int32, (L, LANES), 1) < RWKV_HEAD
    sr_i = lax.broadcasted_iota(jnp.int32, (LANES, LANES), 0)
    sc_i = lax.broadcasted_iota(jnp.int32, (LANES, LANES), 1)
    blockdiag = (sr_i < RWKV_HEAD) == (sc_i < RWKV_HEAD)
    n_dbl = int(round(math.log2(L)))

    def stack2(x):
        return jnp.concatenate([jnp.where(head0, x, 0.0), jnp.where(head0, 0.0, x)], axis=0)

    def unstack(xs):
        return xs[0:L] + xs[L:l2]

    def load(p, b):
        rows = pl.ds(b, L, stride=NB)
        return (r_ref[p, rows, :], k_ref[p, rows, :], v_ref[p, rows, :], kk_ref[p, rows, :],
                ew_ref[0, p, rows, :], as_ref[0, p, rows, :], st_ref[p * NB + b])

    def inst(ka, r, k, v, kk, ew, asg, s0):
        ldec = -ew
        h1, h2, h3 = _split3(ldec)
        cl3 = _dot(mcum, jnp.concatenate([h1, h2, h3], axis=1))
        yield
        cl = cl3[:, 0:LANES] + cl3[:, LANES:2 * LANES] + cl3[:, 2 * LANES:3 * LANES]
        ce = cl - ldec
        ecl = jnp.exp(cl)
        encl = jnp.exp(-cl)
        etot = jnp.exp(jnp.sum(ldec, axis=0, keepdims=True))
        at = -kk * jnp.exp(ce)
        rt = r * ecl
        bt = kk * asg * encl
        kt = k * (1.0 + (asg - 1.0) * ka) * encl

        ats = stack2(at)
        vsb = stack2(v).astype(BF16)
        lhs = jnp.concatenate([ats, stack2(rt)], axis=0).astype(BF16)
        rhs = jnp.concatenate([stack2(bt), stack2(kt)], axis=0).astype(BF16)
        pm = _dot_nt(lhs, rhs)
        yield
        nab = pm[0:l2, 0:l2] * strict_f
        nak = pm[0:l2, l2:2 * l2] * strict_f
        mrb = (pm[l2:2 * l2, 0:l2] * incl_f).astype(BF16)
        mrk = (pm[l2:2 * l2, l2:2 * l2] * incl_f).astype(BF16)

        xs = jnp.concatenate([ats, _dot(nak.astype(BF16), vsb)], axis=1)
        mkv = _dot(mrk, vsb)
        yield
        npow = nab
        for it in range(n_dbl):
            npb = npow.astype(BF16)
            xs = xs + _dot(npb, xs.astype(BF16))
            if it + 1 < n_dbl:
                npow = _dot(npb, npb)
            yield
        mx = _dot(mrb, xs.astype(BF16))
        at2 = unstack(xs[:, 0:LANES])
        ul = unstack(xs[:, LANES:2 * LANES])
        btb = bt.astype(BF16)
        emat = (jnp.where(blockdiag, _dot(at2.T.astype(BF16), btb), 0.0) * etot).astype(BF16)
        uv = jnp.concatenate([ul, v], axis=0)
        bk = jnp.concatenate([btb, kt.astype(BF16)], axis=0)
        dmat = jnp.where(blockdiag, _dot(uv.T.astype(BF16), bk), 0.0) * etot
        yield
        rt2 = rt + unstack(mx[:, 0:LANES])
        yl = unstack(mx[:, LANES:2 * LANES] + mkv)

        s0b = s0.astype(BF16)
        return _dot_nt(rt2.astype(BF16), s0b) + yl, s0 * etot + _dot(s0b, emat) + dmat

    def run_group(gens):
        outs = [None] * len(gens)
        live = list(enumerate(gens))
        while live:
            nxt = []
            for i, g in live:
                try:
                    next(g)
                    nxt.append((i, g))
                except StopIteration as stop:
                    outs[i] = stop.value
            live = nxt
        return outs

    def pair(p, carry):
        ka = ka_ref[p]
        for b0 in range(0, NB, WKV_GROUP):
            outs = run_group([inst(ka, *load(p, b0 + g)) for g in range(WKV_GROUP)])
            for g, (y, s1) in enumerate(outs):
                y_ref[0, p, pl.ds(b0 + g, L, stride=NB), :] = y
                st_ref[p * NB + b0 + g] = s1
        return carry

    lax.fori_loop(0, npair, pair, 0)


def _wkv_call(r, k, v, kk, ew, asg, ka, *, nct_rows):
    npair, n, _ = r.shape
    L = WKV_L
    rows = L * NB
    nc = n // rows
    ncc = nct_rows // rows
    kern = functools.partial(_wkv_kernel, L=L, npair=npair)

    def chunk(dd, c):
        bwd = jnp.where(c < ncc, ncc - 1 - c, nc - 1 + ncc - c)
        return jnp.where(dd == 0, c, bwd)

    slab = pl.BlockSpec((npair, rows, LANES), lambda dd, c: (0, chunk(dd, c), 0))
    slab2 = pl.BlockSpec((1, npair, rows, LANES), lambda dd, c: (dd, 0, chunk(dd, c), 0))
    return pl.pallas_call(
        kern,
        grid=(2, nc),
        in_specs=[slab, slab, slab, slab, slab2, slab2,
                  pl.BlockSpec((npair, 1, LANES), lambda dd, c: (0, 0, 0))],
        out_specs=slab2,
        out_shape=jax.ShapeDtypeStruct((2, npair, n, LANES), F32),
        scratch_shapes=[pltpu.VMEM((npair * NB, LANES, LANES), F32)],
        compiler_params=_cparams(("arbitrary", "arbitrary")),
    )(r, k, v, kk, ew, asg, ka)


def _rw_out_kernel(x_ref, mod_ref, y_ref, r_ref, k_ref, v_ref, as_ref, gg_ref, lng_ref, lnb_ref,
                   ka_ref, rk_ref, wo_ref, o_ref):
    x = x_ref[...]
    d = x.shape[1]
    npair = d // LANES
    cat = lambda f: jnp.concatenate([f(p) for p in range(npair)], axis=1)
    y = cat(lambda p: y_ref[0, p] + y_ref[1, p])
    r = cat(lambda p: r_ref[p])
    k = cat(lambda p: k_ref[p])
    v = cat(lambda p: v_ref[p])
    am = cat(lambda p: 0.5 * (as_ref[0, p] + as_ref[1, p]))
    seg, seg_t = _seg_mats(d)
    inv = 1.0 / RWKV_HEAD
    mean = _head_sum_bcast(y, seg, seg_t) * inv
    yc = y - mean
    var = _head_sum_bcast(yc * yc, seg, seg_t) * inv
    yn = yc * lax.rsqrt(var + GN_EPS) * lng_ref[...] + lnb_ref[...]
    k_mean = k * (1.0 + (am - 1.0) * ka_ref[...])
    bonus = _head_sum_bcast(r * k_mean * rk_ref[...], seg, seg_t) * v
    z = ((yn + bonus) * gg_ref[...]).astype(BF16)
    o_ref[...] = _residual(x, mod_ref[0, 2], _dot(z, wo_ref[...]))


def _rw_out_call(x, mod, y, r, k, v, asg, gg, lng, lnb, ka, rk, wo, *, nct):
    n, d = x.shape
    nt = n // TM
    npair = d // LANES
    slab = pl.BlockSpec((npair, TM, LANES), lambda i: (0, i, 0))
    slab2 = pl.BlockSpec((2, npair, TM, LANES), lambda i: (0, 0, i, 0))
    row = pl.BlockSpec((1, d), lambda i: (0, 0))
    return pl.pallas_call(
        _rw_out_kernel,
        grid=(nt,),
        in_specs=[
            pl.BlockSpec((TM, d), lambda i: (i, 0)),
            pl.BlockSpec((1, 6, NB, d), lambda i: (_is_lat(i, nct), 0, 0, 0)),
            slab2, slab, slab, slab, slab2,
            pl.BlockSpec((TM, d), lambda i: (i, 0)),
            row, row, row, row,
            pl.BlockSpec((d, d), lambda i: (0, 0)),
        ],
        out_specs=pl.BlockSpec((TM, d), lambda i: (i, 0)),
        out_shape=jax.ShapeDtypeStruct((n, d), F32),
        compiler_params=_cparams(("arbitrary",)),
    )(x, mod, y, r, k, v, asg, gg, lng, lnb, ka, rk, wo)


def _final_kernel(x_ref, g_ref, o_ref):
    x = x_ref[...]
    ms = jnp.mean(x * x, axis=-1, keepdims=True)
    o_ref[...] = x * lax.rsqrt(ms + RMS_EPS) * g_ref[...]


def _final_call(x, g, *, tile0):
    n, d = x.shape
    nt = n // TM
    return pl.pallas_call(
        _final_kernel,
        grid=(nt - tile0,),
        in_specs=[pl.BlockSpec((TM, d), lambda i: (i + tile0, 0)),
                  pl.BlockSpec((1, d), lambda i: (0, 0))],
        out_specs=pl.BlockSpec((TM, d), lambda i: (i, 0)),
        out_shape=jax.ShapeDtypeStruct((n - tile0 * TM, d), F32),
        compiler_params=_cparams(("arbitrary",)),
    )(x, g)


def _row(v):
    return v.reshape(1, -1).astype(F32)


def _s5_layer(x, mod, g, p, *, nct, tile0):
    bb, cc, ab = _s5_params(p['lam_re'], p['lam_im'], p['log_dt'], p['b_re'], p['b_im'],
                            p['c_re'], p['c_im'])
    y = _s5_scan_call(x, mod, g, bb, cc, ab, nct=nct)
    return _s5_glu_call(x, mod, g, y, _row(p['d']), p['glu_w'].astype(BF16), _row(p['glu_b']),
                        nct=nct, tile0=tile0)


def _conformer_layer(x, mod, g, p, *, nct):
    u = _cv_glu_call(x, mod, g, p['pw1_w'].astype(BF16), _row(p['pw1_b']), nct=nct)
    return _cv_conv_call(u, x, mod, p['dw_w'].astype(F32), _row(p['dw_b']), _row(p['ln_g']),
                         _row(p['ln_b']), p['pw2_w'].astype(BF16), _row(p['pw2_b']), nct=nct)


def _blockdiag2(m):
    z = jnp.zeros_like(m[0])
    return jnp.concatenate([jnp.concatenate([m[0], z], axis=1),
                            jnp.concatenate([z, m[1]], axis=1)], axis=0)


def _rwkv_layer(x, mod, g, p, *, nct):
    n, d = x.shape
    npair = d // LANES
    w1 = jnp.concatenate([p['w1'][0], p['w1'][1]], axis=1).astype(BF16)
    w2 = _blockdiag2(p['w2']).astype(BF16)
    a1 = jnp.concatenate([p['a1'][0], p['a1'][1]], axis=1).astype(BF16)
    a2 = _blockdiag2(p['a2']).astype(BF16)
    gl = p['g1'].shape[1]
    glp = -(-gl // LANES) * LANES
    g1 = jnp.pad(p['g1'], ((0, 0), (0, glp - gl))).astype(BF16)
    g2 = jnp.pad(p['g2'], ((0, glp - gl), (0, 0))).astype(BF16)
    r, k, v, kk, ew, asg, gg = _rw_proj_call(
        x, mod, g, p['mu'].astype(F32), p['w_r'].astype(BF16), p['w_k'].astype(BF16),
        p['w_v'].astype(BF16), w1, w2, _row(p['w0']), a1, a2, _row(p['a0']), g1, g2,
        _row(p['k_k']), nct=nct)
    ka_slab = p['k_a'].astype(F32).reshape(npair, 1, LANES)
    y = _wkv_call(r, k, v, kk, ew, asg, ka_slab, nct_rows=nct * TM)
    return _rw_out_call(x, mod, y, r, k, v, asg, gg, _row(p['lnx_g']), _row(p['lnx_b']),
                        _row(p['k_a']), _row(p['r_k']), p['w_o'].astype(BF16), nct=nct)


def _forward(x, c, ctx, c_ctx, ada_w, ada_b, norm_mix_g, norm_ffn_g, final_g, s5, cv, rw,
             ff_w1, ff_w3, ff_w2, moe_router, moe_w1, moe_w3, moe_w2):
    bsz, seq, d = x.shape
    lc = ctx.shape[1]
    depth = ada_w.shape[0]
    assert bsz == NB and lc % TT == 0 and seq % TT == 0
    assert TT == GRID_W, "one token tile must hold exactly one latent raster row per batch entry"
    nct = lc // TT
    t_all = lc + seq
    n = t_all * NB

    xs = jnp.concatenate([jnp.swapaxes(ctx, 0, 1), jnp.swapaxes(x, 0, 1)], axis=0)
    xs = xs.reshape(n, d).astype(F32)

    cvec = jnp.zeros((2 * NB, d), F32).at[:NB].set(c).at[NB].set(c_ctx)
    m_all = _ada_call(cvec, ada_w.astype(F32), ada_b.astype(F32))
    m_all = m_all.reshape(depth, 2 * NB, 6, d)
    m_lat = jnp.swapaxes(m_all[:, :NB], 1, 2)
    m_ctx = jnp.broadcast_to(m_all[:, NB][:, :, None, :], (depth, 6, NB, d))
    mods = jnp.stack([m_ctx, m_lat], axis=1)

    ctx_rows = nct * TM
    for i in range(depth):
        last = i == depth - 1
        mod = mods[i]
        g_mix = _row(norm_mix_g[i])
        j = i // 3
        if i % 3 == 0:
            xs = _s5_layer(xs, mod, g_mix, {k_: v_[j] for k_, v_ in s5.items()}, nct=nct,
                           tile0=nct if last else 0)
            ctx_rows = 0 if last else ctx_rows
        elif i % 3 == 1:
            xs = _conformer_layer(xs, mod, g_mix, {k_: v_[j] for k_, v_ in cv.items()}, nct=nct)
        else:
            xs = _rwkv_layer(xs, mod, g_mix, {k_: v_[j] for k_, v_ in rw.items()}, nct=nct)
        fi = i // 2
        g_ffn = _row(norm_ffn_g[i])
        if i % 2 == 0:
            xs = _ffn_call(xs, mod, g_ffn, ff_w1, ff_w3, ff_w2, layer=fi, nct_rows=ctx_rows)
        else:
            xs = _moe_call(xs, mod, g_ffn, moe_router[fi], moe_w1, moe_w3, moe_w2, layer=fi,
                           nct_rows=ctx_rows)

    out = _final_call(xs, _row(final_g), tile0=ctx_rows // TM)
    return jnp.swapaxes(out.reshape(seq, NB, d), 0, 1)


def kernel(x, c, ctx, c_ctx, ada_w, ada_b, norm_mix_g, norm_ffn_g, final_g, s5_lambda_re, s5_lambda_im, s5_log_dt, s5_b_re, s5_b_im, s5_c_re, s5_c_im, s5_d, s5_glu_w, s5_glu_b, cv_pw1_w, cv_pw1_b, cv_dw_w, cv_dw_b, cv_ln_g, cv_ln_b, cv_pw2_w, cv_pw2_b, rw_mu, rw_w_r, rw_w_k, rw_w_v, rw_w_o, rw_w0, rw_w1, rw_w2, rw_a0, rw_a1, rw_a2, rw_g1, rw_g2, rw_k_k, rw_k_a, rw_r_k, rw_lnx_g, rw_lnx_b, ff_w1, ff_w3, ff_w2, moe_router, moe_w1, moe_w3, moe_w2):
    s5 = dict(lam_re=s5_lambda_re, lam_im=s5_lambda_im, log_dt=s5_log_dt, b_re=s5_b_re, b_im=s5_b_im,
              c_re=s5_c_re, c_im=s5_c_im, d=s5_d, glu_w=s5_glu_w, glu_b=s5_glu_b)
    cv = dict(pw1_w=cv_pw1_w, pw1_b=cv_pw1_b, dw_w=cv_dw_w, dw_b=cv_dw_b, ln_g=cv_ln_g, ln_b=cv_ln_b,
              pw2_w=cv_pw2_w, pw2_b=cv_pw2_b)
    rw = dict(mu=rw_mu, w_r=rw_w_r, w_k=rw_w_k, w_v=rw_w_v, w_o=rw_w_o, w0=rw_w0, w1=rw_w1, w2=rw_w2,
              a0=rw_a0, a1=rw_a1, a2=rw_a2, g1=rw_g1, g2=rw_g2, k_k=rw_k_k, k_a=rw_k_a, r_k=rw_r_k,
              lnx_g=rw_lnx_g, lnx_b=rw_lnx_b)
    return _forward(x, c, ctx, c_ctx, ada_w, ada_b, norm_mix_g, norm_ffn_g, final_g, s5, cv, rw,
                    ff_w1, ff_w3, ff_w2, moe_router, moe_w1, moe_w3, moe_w2)
```

```python
import functools
import math

import jax
import jax.numpy as jnp
from jax import lax
from jax.experimental import pallas as pl
from jax.experimental.pallas import tpu as pltpu

F32 = jnp.float32
BF16 = jnp.bfloat16

NB = 8
LANES = 128
TM = 512
TT = TM // NB
GRID_W = 64
CONV_WIDTH = 31
CONV_PAD = CONV_WIDTH // 2
HALO_ROWS = 128
S5_GROUP = 16
S5_STATE = 64
S5_JBLK = 8
RWKV_HEAD = 64
WKV_L = 64
WKV_GROUP = 8
RW_TM = 256
FFN_TM = 1024
MOE_T = 1024
MOE_BLK = 32
MOE_SLOTS = 32
RMS_EPS = 1e-6
LN_EPS = 1e-5
GN_EPS = 64e-5
KK_EPS = 1e-12
LAMBDA_RE_MAX = -1e-4
VMEM_LIMIT = 56 * 1024 * 1024


def _cparams(sem):
    return pltpu.CompilerParams(dimension_semantics=sem, vmem_limit_bytes=VMEM_LIMIT)


def _dot(a, b):
    return jnp.dot(a, b, preferred_element_type=F32)


def _dot_nt(a, b):
    return lax.dot_general(a, b, (((1,), (1,)), ((), ())), preferred_element_type=F32)


def _split3(x):
    h1 = x.astype(BF16)
    r1 = x - h1.astype(F32)
    h2 = r1.astype(BF16)
    h3 = (r1 - h2.astype(F32)).astype(BF16)
    return h1, h2, h3


def _dot_x3(a, b):
    a1 = a.astype(BF16)
    a2 = (a - a1.astype(F32)).astype(BF16)
    b1 = b.astype(BF16)
    b2 = (b - b1.astype(F32)).astype(BF16)
    return _dot(a1, b1) + _dot(a1, b2) + _dot(a2, b1)


def _sigmoid(x):
    return jax.nn.sigmoid(x)


def _norm_mod(x, g, shift, scale):
    rows, d = x.shape
    ms = jnp.mean(x * x, axis=-1, keepdims=True)
    y = x * lax.rsqrt(ms + RMS_EPS) * g
    y3 = y.reshape(rows // NB, NB, d)
    return (y3 * (1.0 + scale)[None] + shift[None]).reshape(rows, d)


def _residual(x, gate, y):
    rows, d = x.shape
    y3 = y.reshape(rows // NB, NB, d) * gate[None]
    return x + y3.reshape(rows, d)


def _is_lat(i, nct):
    return jnp.where(i >= nct, 1, 0).astype(jnp.int32)


def _ada_kernel(s_ref, w_ref, b_ref, o_ref):
    s = s_ref[...]
    s = s * _sigmoid(s)
    o_ref[0] = _dot(s.astype(BF16), w_ref[0].astype(BF16)) + b_ref[0]


def _ada_call(cvec, ada_w, ada_b):
    depth, d, d6 = ada_w.shape
    tn = 1536
    return pl.pallas_call(
        _ada_kernel,
        grid=(depth, d6 // tn),
        in_specs=[
            pl.BlockSpec((2 * NB, d), lambda l, j: (0, 0)),
            pl.BlockSpec((1, d, tn), lambda l, j: (l, 0, j)),
            pl.BlockSpec((1, 1, tn), lambda l, j: (l, 0, j)),
        ],
        out_specs=pl.BlockSpec((1, 2 * NB, tn), lambda l, j: (l, 0, j)),
        out_shape=jax.ShapeDtypeStruct((depth, 2 * NB, d6), F32),
        compiler_params=_cparams(("arbitrary", "arbitrary")),
    )(cvec, ada_w, ada_b.reshape(depth, 1, d6))


def _ffn_kernel(x_ref, mod_ref, g_ref, w1_ref, w3_ref, w2_ref, o_ref, hb_ref, acc_ref, *, n_k):
    k = pl.program_id(1)

    @pl.when(k == 0)
    def _():
        h = _norm_mod(x_ref[...], g_ref[...], mod_ref[0, 3], mod_ref[0, 4])
        hb_ref[...] = h.astype(BF16)
        acc_ref[...] = jnp.zeros_like(acc_ref)

    hb = hb_ref[...]
    a = _dot(hb, w1_ref[0].astype(BF16))
    b = _dot(hb, w3_ref[0].astype(BF16))
    acc_ref[...] += _dot((a * _sigmoid(a) * b).astype(BF16), w2_ref[0].astype(BF16))

    @pl.when(k == n_k - 1)
    def _():
        o_ref[...] = _residual(x_ref[...], mod_ref[0, 5], acc_ref[...])


def _ffn_call(x, mod, g, w1, w3, w2, *, layer, nct_rows):
    n, d = x.shape
    f = w1.shape[2]
    tm = FFN_TM
    tf = 512
    n_k = f // tf
    nct = nct_rows // tm
    return pl.pallas_call(
        functools.partial(_ffn_kernel, n_k=n_k),
        grid=(n // tm, n_k),
        in_specs=[
            pl.BlockSpec((tm, d), lambda i, k: (i, 0)),
            pl.BlockSpec((1, 6, NB, d), lambda i, k: (_is_lat(i, nct), 0, 0, 0)),
            pl.BlockSpec((1, d), lambda i, k: (0, 0)),
            pl.BlockSpec((1, d, tf), lambda i, k: (layer, 0, k)),
            pl.BlockSpec((1, d, tf), lambda i, k: (layer, 0, k)),
            pl.BlockSpec((1, tf, d), lambda i, k: (layer, k, 0)),
        ],
        out_specs=pl.BlockSpec((tm, d), lambda i, k: (i, 0)),
        out_shape=jax.ShapeDtypeStruct((n, d), F32),
        scratch_shapes=[pltpu.VMEM((tm, d), BF16), pltpu.VMEM((tm, d), F32)],
        compiler_params=_cparams(("arbitrary", "arbitrary")),
    )(x, mod, g, w1, w3, w2)


def _top2(logits, n_e):
    lane = lax.broadcasted_iota(jnp.int32, logits.shape, 1).astype(F32)
    neg = jnp.float32(-3.0e38)
    lg = jnp.where(lane < n_e, logits, neg)
    m1 = jnp.max(lg, axis=-1, keepdims=True)
    i1 = jnp.min(jnp.where(lg == m1, lane, 1.0e9), axis=-1, keepdims=True)
    lg2 = jnp.where(lane == i1, neg, lg)
    m2 = jnp.max(lg2, axis=-1, keepdims=True)
    i2 = jnp.min(jnp.where(lg2 == m2, lane, 1.0e9), axis=-1, keepdims=True)
    e2 = jnp.exp(m2 - m1)
    den = 1.0 + e2
    return lane, i1, i2, 1.0 / den, e2 / den


def _moe_route_kernel(x_ref, mod_ref, g_ref, router_ref, ltri_ref, z_ref, dest_ref, cnt_ref,
                      hb_ref, *, n_e, cap):
    t, d = x_ref.shape
    h = _norm_mod(x_ref[...], g_ref[...], mod_ref[0, 3], mod_ref[0, 4])
    hb_ref[...] = h.astype(BF16)
    lane, i1, i2, p1, p2 = _top2(_dot_x3(h, router_ref[...]), n_e)
    sel1 = lane == i1
    sel2 = lane == i2
    c = jnp.where(sel1, 1.0, 0.0) + jnp.where(sel2, 1.0, 0.0)
    rank = _dot(ltri_ref[...], c.astype(BF16))
    cnt = jnp.sum(c, axis=0, keepdims=True)
    seg = jnp.floor((cnt + (MOE_BLK - 1)) * (1.0 / MOE_BLK)) * MOE_BLK
    ur = lax.broadcasted_iota(jnp.int32, (LANES, LANES), 0)
    uc = lax.broadcasted_iota(jnp.int32, (LANES, LANES), 1)
    before = jnp.where(ur < uc, 1.0, 0.0).astype(BF16)
    start = _dot(jnp.broadcast_to(seg, (NB, LANES)).astype(BF16), before)[0:1]
    pos = start + rank
    d1 = jnp.sum(jnp.where(sel1, pos, 0.0), axis=-1, keepdims=True)
    d2 = jnp.sum(jnp.where(sel2, pos, 0.0), axis=-1, keepdims=True)
    packed = (jnp.where(lane == 0.0, d1, 0.0) + jnp.where(lane == 1.0, d2, 0.0)
              + jnp.where(lane == 2.0, p1, 0.0) + jnp.where(lane == 3.0, p2, 0.0))
    dest_ref[...] = packed
    cnt_ref[0] = jnp.broadcast_to(cnt, (NB, LANES))
    packed_t = packed.T
    d1r = packed_t[0:1]
    d2r = packed_t[1:2]
    p1r = packed_t[2:3]
    p2r = packed_t[3:4]
    rc = 256

    def chunk(ci, carry):
        r0 = pl.multiple_of(ci * rc, rc)
        row = (lax.broadcasted_iota(jnp.int32, (rc, t), 0) + r0).astype(F32)
        m1 = row == d1r
        m2 = row == d2r
        perm = (jnp.where(m1, 1.0, 0.0) + jnp.where(m2, 1.0, 0.0)).astype(BF16)
        zc = _dot(perm, hb_ref[...])
        gate = jnp.sum(jnp.where(m1, p1r, 0.0) + jnp.where(m2, p2r, 0.0), axis=-1, keepdims=True)
        ghi = gate.astype(BF16).astype(F32)
        glo = gate - ghi
        le = lax.broadcasted_iota(jnp.int32, (rc, LANES), 1)
        ext = jnp.where(le == 0, ghi, jnp.where(le == 1, glo, 0.0))
        z_ref[pl.ds(r0, rc), 0:d] = zc.astype(BF16)
        z_ref[pl.ds(r0, rc), d:d + LANES] = ext.astype(BF16)
        return carry

    lax.fori_loop(0, cap // rc, chunk, 0)


def _moe_experts_kernel(tin_ref, se_ref, nu_ref, *refs, n_k, d):
    z_refs = refs[:MOE_SLOTS]
    w1_ref, w3_ref, w2_ref, o_ref, xs_ref, acc_ref = refs[MOE_SLOTS:]
    s = pl.program_id(0)
    k = pl.program_id(1)
    used = s < nu_ref[0]

    @pl.when(used & (k == 0))
    def _():
        for j in range(MOE_SLOTS):
            xs_ref[j * MOE_BLK:(j + 1) * MOE_BLK] = z_refs[j][...]
        acc_ref[...] = jnp.zeros_like(acc_ref)

    @pl.when(used)
    def _():
        hb = xs_ref[:, 0:d]
        ext = xs_ref[:, d:d + LANES].astype(F32)
        le = lax.broadcasted_iota(jnp.int32, ext.shape, 1)
        gate = jnp.sum(jnp.where(le < 2, ext, 0.0), axis=-1, keepdims=True)
        a = _dot(hb, w1_ref[0, 0].astype(BF16))
        b = _dot(hb, w3_ref[0, 0].astype(BF16))
        act = a * _sigmoid(a) * b * gate
        acc_ref[...] += _dot(act.astype(BF16), w2_ref[0, 0].astype(BF16))

    @pl.when(k == n_k - 1)
    def _():
        o_ref[...] = jnp.where(used, acc_ref[...], 0.0).astype(BF16)


def _moe_combine_kernel(tc_ref, *refs, nblk, cap):
    zb_refs = refs[:nblk]
    dest_ref, x_ref, mod_ref, o_ref, zbuf_ref = refs[nblk:]
    t, d = x_ref.shape
    for j in range(nblk):
        zbuf_ref[j * MOE_BLK:(j + 1) * MOE_BLK] = zb_refs[j][...]
    packed = dest_ref[...]
    lane = lax.broadcasted_iota(jnp.int32, packed.shape, 1)
    d1 = jnp.sum(jnp.where(lane == 0, packed, 0.0), axis=-1, keepdims=True)
    d2 = jnp.sum(jnp.where(lane == 1, packed, 0.0), axis=-1, keepdims=True)
    kc = 256
    acc = jnp.zeros((t, d), F32)
    for c0 in range(0, cap, kc):
        col = (lax.broadcasted_iota(jnp.int32, (t, kc), 1) + c0).astype(F32)
        sel = (jnp.where(col == d1, 1.0, 0.0) + jnp.where(col == d2, 1.0, 0.0)).astype(BF16)
        acc = acc + _dot(sel, zbuf_ref[c0:c0 + kc])
    o_ref[...] = _residual(x_ref[...], mod_ref[0, 5], acc)


def _moe_tables(cnt, *, n_tiles, n_e, nblk, n_steps):
    i32 = jnp.int32
    pc = (cnt + (MOE_BLK - 1)) // MOE_BLK
    lb = jnp.cumsum(pc, axis=1) - pc
    cum = jnp.cumsum(pc, axis=0)
    cum_ex = cum - pc
    nb_e = cum[-1]
    steps_e = (nb_e + (MOE_SLOTS - 1)) // MOE_SLOTS
    ends = jnp.cumsum(steps_e)
    step0 = ends - steps_e
    sidx = jnp.arange(n_steps, dtype=i32)
    step_e = jnp.minimum(jnp.sum((sidx[:, None] >= ends[None, :]).astype(i32), axis=1), n_e - 1)
    rank = ((sidx - step0[step_e])[:, None] * MOE_SLOTS
            + jnp.arange(MOE_SLOTS, dtype=i32)[None, :])
    cum_s = cum.T[step_e]
    tile = jnp.sum((cum_s[:, None, :] <= rank[:, :, None]).astype(i32), axis=2)
    ok = (sidx < ends[-1])[:, None] & (rank < nb_e[step_e][:, None])
    tile = jnp.minimum(tile, n_tiles - 1)
    e_b = jnp.broadcast_to(step_e[:, None], tile.shape)
    src = tile * nblk + lb[tile, e_b] + rank - cum_ex[tile, e_b]
    tin = jnp.where(ok, src, 0).reshape(-1).astype(i32)
    lblk = jnp.arange(nblk, dtype=i32)[None, :]
    seg_end = lb + pc
    e_l = jnp.minimum(jnp.sum((seg_end[:, None, :] <= lblk[:, :, None]).astype(i32), axis=2), n_e - 1)
    t_l = jnp.broadcast_to(jnp.arange(n_tiles, dtype=i32)[:, None], e_l.shape)
    r_l = cum_ex[t_l, e_l] + lblk - lb[t_l, e_l]
    flat = (step0[e_l] + r_l // MOE_SLOTS) * MOE_SLOTS + r_l % MOE_SLOTS
    tcomb = jnp.where(lblk < seg_end[:, -1:], flat, 0).reshape(-1).astype(i32)
    return tin, tcomb, step_e.astype(i32), ends[-1:].astype(i32)


def _moe_call(x, mod, g, router, w1, w3, w2, *, layer, nct_rows):
    n, d = x.shape
    _, n_e, _, f = w1.shape
    t = MOE_T
    assert n % t == 0 and nct_rows % t == 0
    n_tiles = n // t
    nct = nct_rows // t
    cap = 2 * t + n_e * MOE_BLK
    nblk = cap // MOE_BLK
    rows = MOE_SLOTS * MOE_BLK
    n_steps = -(-(n_tiles * nblk) // MOE_SLOTS) + n_e
    tf = 512
    n_k = f // tf
    router_p = jnp.pad(router.astype(F32), ((0, 0), (0, LANES - n_e)))
    ltri = jnp.tril(jnp.ones((t, t), F32), k=-1).astype(BF16)
    mod_spec = lambda off: pl.BlockSpec((1, 6, NB, d), lambda i, *_: (_is_lat(i, nct), 0, 0, 0))

    z, dest, cnt = pl.pallas_call(
        functools.partial(_moe_route_kernel, n_e=n_e, cap=cap),
        grid=(n_tiles,),
        in_specs=[
            pl.BlockSpec((t, d), lambda i: (i, 0)),
            mod_spec(0),
            pl.BlockSpec((1, d), lambda i: (0, 0)),
            pl.BlockSpec((d, LANES), lambda i: (0, 0)),
            pl.BlockSpec((t, t), lambda i: (0, 0)),
        ],
        out_specs=[
            pl.BlockSpec((cap, d + LANES), lambda i: (i, 0)),
            pl.BlockSpec((t, LANES), lambda i: (i, 0)),
            pl.BlockSpec((1, NB, LANES), lambda i: (i, 0, 0)),
        ],
        out_shape=[
            jax.ShapeDtypeStruct((n_tiles * cap, d + LANES), BF16),
            jax.ShapeDtypeStruct((n, LANES), F32),
            jax.ShapeDtypeStruct((n_tiles, NB, LANES), F32),
        ],
        scratch_shapes=[pltpu.VMEM((t, d), BF16)],
        compiler_params=_cparams(("arbitrary",)),
    )(x, mod, g, router_p, ltri)

    tin, tcomb, step_e, nused = _moe_tables(cnt[:, 0, :n_e].astype(jnp.int32), n_tiles=n_tiles,
                                            n_e=n_e, nblk=nblk, n_steps=n_steps)

    def kk(s, k, nu_r):
        return jnp.where(s < nu_r[0], k, n_k - 1)

    def z_spec(j):
        return pl.BlockSpec((MOE_BLK, d + LANES), lambda s, k, tin_r, se_r, nu_r: (tin_r[s * MOE_SLOTS + j], 0))

    zo = pl.pallas_call(
        functools.partial(_moe_experts_kernel, n_k=n_k, d=d),
        grid_spec=pltpu.PrefetchScalarGridSpec(
            num_scalar_prefetch=3,
            grid=(n_steps, n_k),
            in_specs=[z_spec(j) for j in range(MOE_SLOTS)] + [
                pl.BlockSpec((1, 1, d, tf), lambda s, k, tin_r, se_r, nu_r: (layer, se_r[s], 0, kk(s, k, nu_r))),
                pl.BlockSpec((1, 1, d, tf), lambda s, k, tin_r, se_r, nu_r: (layer, se_r[s], 0, kk(s, k, nu_r))),
                pl.BlockSpec((1, 1, tf, d), lambda s, k, tin_r, se_r, nu_r: (layer, se_r[s], kk(s, k, nu_r), 0)),
            ],
            out_specs=pl.BlockSpec((rows, d), lambda s, k, tin_r, se_r, nu_r: (s, 0)),
            scratch_shapes=[pltpu.VMEM((rows, d + LANES), BF16), pltpu.VMEM((rows, d), F32)],
        ),
        out_shape=jax.ShapeDtypeStruct((n_steps * rows, d), BF16),
        compiler_params=_cparams(("arbitrary", "arbitrary")),
    )(tin, step_e, nused, *([z] * MOE_SLOTS), w1, w3, w2)

    def zo_spec(j):
        return pl.BlockSpec((MOE_BLK, d), lambda i, tc_r: (tc_r[i * nblk + j], 0))

    return pl.pallas_call(
        functools.partial(_moe_combine_kernel, nblk=nblk, cap=cap),
        grid_spec=pltpu.PrefetchScalarGridSpec(
            num_scalar_prefetch=1,
            grid=(n_tiles,),
            in_specs=[zo_spec(j) for j in range(nblk)] + [
                pl.BlockSpec((t, LANES), lambda i, tc_r: (i, 0)),
                pl.BlockSpec((t, d), lambda i, tc_r: (i, 0)),
                mod_spec(0),
            ],
            out_specs=pl.BlockSpec((t, d), lambda i, tc_r: (i, 0)),
            scratch_shapes=[pltpu.VMEM((cap, d), BF16)],
        ),
        out_shape=jax.ShapeDtypeStruct((n, d), F32),
        compiler_params=_cparams(("arbitrary",)),
    )(tcomb, *([zo] * nblk), dest, x, mod)


def _dir_tile(d, s, nct, nt):
    bwd = jnp.where(s < nct, nct - 1 - s, nt - 1 + nct - s)
    return jnp.where(d == 0, s, bwd)


def _s5_scan_kernel(x_ref, mod_ref, g_ref, bb_ref, cc_ref, ab_ref, y_ref,
                    hb_ref, bu0_ref, bu1_ref, bu2_ref, st_ref):
    d = pl.program_id(0)
    s = pl.program_id(1)
    half = S5_JBLK * S5_STATE
    bufs = (bu0_ref, bu1_ref, bu2_ref)

    @pl.when(s == 0)
    def _():
        st_ref[...] = jnp.zeros_like(st_ref)

    h = _norm_mod(x_ref[...], g_ref[...], mod_ref[0, 0], mod_ref[0, 1])
    hb_ref[...] = h.astype(BF16)

    def project(j):
        bufs[j % 3][...] = _dot(hb_ref[:, j * LANES:(j + 1) * LANES], bb_ref[0, j])

    def scan(j):
        buf = bufs[j % 3]
        ar = ab_ref[0, j, 0]
        ai = ab_ref[0, j, 1]
        re = st_ref[j, 0]
        im = st_ref[j, 1]
        for t in range(TT):
            tt = t + d * (TT - 1 - 2 * t)
            r0 = pl.multiple_of(tt * NB, NB)
            bre = buf[pl.ds(r0, NB), 0:half]
            bim = buf[pl.ds(r0, NB), half:2 * half]
            re, im = ar * re - ai * im + bre, ar * im + ai * re + bim
            buf[pl.ds(r0, NB), 0:half] = re
            buf[pl.ds(r0, NB), half:2 * half] = im
        st_ref[j, 0] = re
        st_ref[j, 1] = im

    def readout(j):
        y_ref[0, j] = _dot(bufs[j % 3][...].astype(BF16), cc_ref[0, j])

    for i in range(S5_JBLK + 2):
        if i < S5_JBLK:
            project(i)
        if 0 <= i - 1 < S5_JBLK:
            scan(i - 1)
        if 0 <= i - 2 < S5_JBLK:
            readout(i - 2)


def _s5_scan_call(x, mod, g, bb, cc, ab, *, nct):
    n, d = x.shape
    nt = n // TM
    half = S5_JBLK * S5_STATE
    tile = lambda dd, s: _dir_tile(dd, s, nct, nt)
    return pl.pallas_call(
        _s5_scan_kernel,
        grid=(2, nt),
        in_specs=[
            pl.BlockSpec((TM, d), lambda dd, s: (tile(dd, s), 0)),
            pl.BlockSpec((1, 6, NB, d), lambda dd, s: (_is_lat(tile(dd, s), nct), 0, 0, 0)),
            pl.BlockSpec((1, d), lambda dd, s: (0, 0)),
            pl.BlockSpec((1, S5_JBLK, LANES, 2 * half), lambda dd, s: (dd, 0, 0, 0)),
            pl.BlockSpec((1, S5_JBLK, 2 * half, LANES), lambda dd, s: (dd, 0, 0, 0)),
            pl.BlockSpec((1, S5_JBLK, 2, NB, half), lambda dd, s: (dd, 0, 0, 0, 0)),
        ],
        out_specs=pl.BlockSpec((1, S5_JBLK, TM, LANES), lambda dd, s: (dd, 0, tile(dd, s), 0)),
        out_shape=jax.ShapeDtypeStruct((2, S5_JBLK, n, LANES), F32),
        scratch_shapes=[
            pltpu.VMEM((TM, d), BF16),
            pltpu.VMEM((TM, 2 * half), F32),
            pltpu.VMEM((TM, 2 * half), F32),
            pltpu.VMEM((TM, 2 * half), F32),
            pltpu.VMEM((S5_JBLK, 2, NB, half), F32),
        ],
        compiler_params=_cparams(("arbitrary", "arbitrary")),
    )(x, mod, g, bb, cc, ab)


def _gelu_tanh(x):
    c = math.sqrt(2.0 / math.pi)
    return 0.5 * x * (1.0 + jnp.tanh(c * (x + 0.044715 * (x * x * x))))


def _s5_glu_kernel(x_ref, mod_ref, g_ref, y_ref, dsk_ref, w_ref, b_ref, o_ref):
    x = x_ref[...]
    d = x.shape[1]
    h = _norm_mod(x, g_ref[...], mod_ref[0, 0], mod_ref[0, 1])
    y = jnp.concatenate([y_ref[0, j] + y_ref[1, j] for j in range(S5_JBLK)], axis=1)
    z = _gelu_tanh(y + dsk_ref[...] * h).astype(BF16)
    zz = _dot(z, w_ref[...]) + b_ref[...]
    out = zz[:, :d] * _sigmoid(zz[:, d:])
    o_ref[...] = _residual(x, mod_ref[0, 2], out)


def _s5_glu_call(x, mod, g, y, dsk, w, b, *, nct, tile0):
    n, d = x.shape
    nt = n // TM
    return pl.pallas_call(
        _s5_glu_kernel,
        grid=(nt - tile0,),
        in_specs=[
            pl.BlockSpec((TM, d), lambda i: (i + tile0, 0)),
            pl.BlockSpec((1, 6, NB, d), lambda i: (_is_lat(i + tile0, nct), 0, 0, 0)),
            pl.BlockSpec((1, d), lambda i: (0, 0)),
            pl.BlockSpec((2, S5_JBLK, TM, LANES), lambda i: (0, 0, i + tile0, 0)),
            pl.BlockSpec((1, d), lambda i: (0, 0)),
            pl.BlockSpec((d, 2 * d), lambda i: (0, 0)),
            pl.BlockSpec((1, 2 * d), lambda i: (0, 0)),
        ],
        out_specs=pl.BlockSpec((TM, d), lambda i: (i, 0)),
        out_shape=jax.ShapeDtypeStruct((n - tile0 * TM, d), F32),
        compiler_params=_cparams(("arbitrary",)),
    )(x, mod, g, y, dsk, w, b)


def _s5_params(lam_re, lam_im, log_dt, b_re, b_im, c_re, c_im):
    lam_re = jnp.minimum(lam_re.astype(F32), LAMBDA_RE_MAX)
    lam_im = lam_im.astype(F32)
    dt = jnp.exp(log_dt.astype(F32))[..., None]
    mag = jnp.exp(lam_re * dt)
    abar_re = mag * jnp.cos(lam_im * dt)
    abar_im = mag * jnp.sin(lam_im * dt)
    inv_den = 1.0 / (lam_re * lam_re + lam_im * lam_im)
    gam_re = ((abar_re - 1.0) * lam_re + abar_im * lam_im) * inv_den
    gam_im = (abar_im * lam_re - (abar_re - 1.0) * lam_im) * inv_den
    b_re = b_re.astype(F32)
    b_im = b_im.astype(F32)
    bb_re = gam_re[..., None] * b_re - gam_im[..., None] * b_im
    bb_im = gam_re[..., None] * b_im + gam_im[..., None] * b_re
    g_all = lam_re.shape[1]
    nj = g_all // S5_JBLK
    eye = jnp.eye(S5_JBLK, dtype=F32)

    def in_mat(m):
        m = m.reshape(2, nj, S5_JBLK, S5_STATE, S5_GROUP)
        m = jnp.einsum('djgnp,gh->djgphn', m, eye)
        return m.reshape(2, nj, S5_JBLK * S5_GROUP, S5_JBLK * S5_STATE)

    def out_mat(m):
        m = m.reshape(2, nj, S5_JBLK, S5_GROUP, S5_STATE)
        m = jnp.einsum('djgpn,gh->djgnhp', m, eye)
        return m.reshape(2, nj, S5_JBLK * S5_STATE, S5_JBLK * S5_GROUP)

    bb = jnp.concatenate([in_mat(bb_re), in_mat(bb_im)], axis=-1).astype(BF16)
    cc = jnp.concatenate([out_mat(c_re.astype(F32)), -out_mat(c_im.astype(F32))], axis=-2).astype(BF16)
    half = S5_JBLK * S5_STATE
    ab = jnp.stack([abar_re.reshape(2, nj, half), abar_im.reshape(2, nj, half)], axis=2)
    ab = jnp.broadcast_to(ab[:, :, :, None, :], (2, nj, 2, NB, half))
    return bb, cc, ab


def _cv_glu_kernel(x_ref, mod_ref, g_ref, w_ref, b_ref, o_ref):
    x = x_ref[...]
    d = x.shape[1]
    h = _norm_mod(x, g_ref[...], mod_ref[0, 0], mod_ref[0, 1]).astype(BF16)
    u = _dot(h, w_ref[...]) + b_ref[...]
    o_ref[...] = u[:, :d] * _sigmoid(u[:, d:])


def _cv_glu_call(x, mod, g, w, b, *, nct):
    n, d = x.shape
    nt = n // TM
    return pl.pallas_call(
        _cv_glu_kernel,
        grid=(nt,),
        in_specs=[
            pl.BlockSpec((TM, d), lambda i: (i, 0)),
            pl.BlockSpec((1, 6, NB, d), lambda i: (_is_lat(i, nct), 0, 0, 0)),
            pl.BlockSpec((1, d), lambda i: (0, 0)),
            pl.BlockSpec((d, 2 * d), lambda i: (0, 0)),
            pl.BlockSpec((1, 2 * d), lambda i: (0, 0)),
        ],
        out_specs=pl.BlockSpec((TM, d), lambda i: (i, 0)),
        out_shape=jax.ShapeDtypeStruct((n, d), F32),
        compiler_params=_cparams(("arbitrary",)),
    )(x, mod, g, w, b)


def _cv_conv_kernel(u_ref, up_ref, un_ref, x_ref, mod_ref, dw_ref, dwb_ref, lng_ref, lnb_ref,
                    w_ref, b_ref, o_ref, pad_ref, cv_ref, *, nct):
    i = pl.program_id(0)
    halo = CONV_PAD * NB
    prev_ok = (i > 0) & (i < nct)
    next_ok = i < nct - 1
    pad_ref[0:halo] = jnp.where(prev_ok, up_ref[HALO_ROWS - halo:HALO_ROWS], 0.0)
    pad_ref[halo:halo + TM] = u_ref[...]
    pad_ref[halo + TM:halo + TM + halo] = jnp.where(next_ok, un_ref[0:halo], 0.0)

    rb = 32

    def body(c, _):
        r0 = pl.multiple_of(c * rb, rb)
        acc = jnp.broadcast_to(dwb_ref[...], (rb, dwb_ref.shape[1]))
        for k in range(CONV_WIDTH):
            acc = acc + dw_ref[k] * pad_ref[pl.ds(r0 + k * NB, rb)]
        cv_ref[pl.ds(r0, rb)] = acc
        return 0

    lax.fori_loop(0, TM // rb, body, 0)

    c = cv_ref[...]
    mu = jnp.mean(c, axis=-1, keepdims=True)
    cc = c - mu
    var = jnp.mean(cc * cc, axis=-1, keepdims=True)
    y = cc * lax.rsqrt(var + LN_EPS) * lng_ref[...] + lnb_ref[...]
    y = (y * _sigmoid(y)).astype(BF16)
    out = _dot(y, w_ref[...]) + b_ref[...]
    o_ref[...] = _residual(x_ref[...], mod_ref[0, 2], out)


def _cv_conv_call(u, x, mod, dw_w, dw_b, ln_g, ln_b, w, b, *, nct):
    n, d = x.shape
    nt = n // TM
    hb = TM // HALO_ROWS
    nhb = n // HALO_ROWS
    kern = functools.partial(_cv_conv_kernel, nct=nct)
    row = lambda i: (0, 0)
    return pl.pallas_call(
        kern,
        grid=(nt,),
        in_specs=[
            pl.BlockSpec((TM, d), lambda i: (i, 0)),
            pl.BlockSpec((HALO_ROWS, d), lambda i: (jnp.maximum(i * hb - 1, 0), 0)),
            pl.BlockSpec((HALO_ROWS, d), lambda i: (jnp.minimum((i + 1) * hb, nhb - 1), 0)),
            pl.BlockSpec((TM, d), lambda i: (i, 0)),
            pl.BlockSpec((1, 6, NB, d), lambda i: (_is_lat(i, nct), 0, 0, 0)),
            pl.BlockSpec((CONV_WIDTH, 1, d), lambda i: (0, 0, 0)),
            pl.BlockSpec((1, d), row),
            pl.BlockSpec((1, d), row),
            pl.BlockSpec((1, d), row),
            pl.BlockSpec((d, d), row),
            pl.BlockSpec((1, d), row),
        ],
        out_specs=pl.BlockSpec((TM, d), lambda i: (i, 0)),
        out_shape=jax.ShapeDtypeStruct((n, d), F32),
        scratch_shapes=[
            pltpu.VMEM((TM + 2 * CONV_PAD * NB, d), F32),
            pltpu.VMEM((TM, d), F32),
        ],
        compiler_params=_cparams(("arbitrary",)),
    )(u, u, u, x, mod, dw_w.reshape(CONV_WIDTH, 1, d), dw_b, ln_g, ln_b, w, b)


def _head_ones():
    w = 2 * LANES
    r = lax.broadcasted_iota(jnp.int32, (w, w), 0) // RWKV_HEAD
    c = lax.broadcasted_iota(jnp.int32, (w, w), 1) // RWKV_HEAD
    return jnp.where(r == c, 1.0, 0.0).astype(BF16)


def _head_sum_bcast(x, ones_bd):
    hi = x.astype(BF16)
    lo = (x - hi.astype(F32)).astype(BF16)
    w = ones_bd.shape[0]
    cols = [_dot(hi[:, c0:c0 + w], ones_bd) + _dot(lo[:, c0:c0 + w], ones_bd)
            for c0 in range(0, x.shape[1], w)]
    return jnp.concatenate(cols, axis=1)


def _rw_proj_kernel(x_ref, xp_ref, xn_ref, mod_ref, g_ref, mu_ref, wr_ref, wk_ref, wv_ref,
                    w1_ref, w2_ref, w0_ref, a1_ref, a2_ref, a0_ref, g1_ref, g2_ref, kkw_ref,
                    r_ref, k_ref, v_ref, kk_ref, ew_ref, as_ref, gg_ref, hp_ref, *, nct, nt):
    i = pl.program_id(0)
    tm, d = x_ref.shape
    g = g_ref[...]
    sh = mod_ref[0, 0]
    sc = mod_ref[0, 1]
    h = _norm_mod(x_ref[...], g, sh, sc)
    hp = _norm_mod(xp_ref[...], g, sh, sc)
    hn = _norm_mod(xn_ref[...], g, sh, sc)
    prev_ok = (i != 0) & (i != nct)
    next_ok = (i != nct - 1) & (i != nt - 1)
    hp_ref[0:NB] = jnp.where(prev_ok, hp, 0.0)
    hp_ref[NB:NB + tm] = h
    hp_ref[NB + tm:NB + tm + NB] = jnp.where(next_ok, hn, 0.0)
    xx = 0.5 * (hp_ref[0:tm] + hp_ref[2 * NB:2 * NB + tm]) - h

    def mix(j):
        return (h + xx * mu_ref[j]).astype(BF16)

    r = _dot(mix(0), wr_ref[...])
    k = _dot(mix(2), wk_ref[...])
    v = _dot(mix(3), wv_ref[...])
    zw = w0_ref[...] + _dot(jnp.tanh(_dot(mix(1), w1_ref[...])).astype(BF16), w2_ref[...])
    ew = _sigmoid(zw) * math.exp(-0.5)
    asg = _sigmoid(a0_ref[...] + _dot(_dot(mix(4), a1_ref[...]).astype(BF16), a2_ref[...]))
    gg_ref[...] = _dot(_sigmoid(_dot(mix(5), g1_ref[...])).astype(BF16), g2_ref[...])

    kk = k * kkw_ref[...]
    kk = kk * lax.rsqrt(_head_sum_bcast(kk * kk, _head_ones()) + KK_EPS)

    for p in range(d // LANES):
        sl = slice(p * LANES, (p + 1) * LANES)
        r_ref[p] = r[:, sl]
        k_ref[p] = k[:, sl]
        v_ref[p] = v[:, sl]
        kk_ref[p] = kk[:, sl]
        for dr in range(2):
            sl2 = slice(dr * d + p * LANES, dr * d + (p + 1) * LANES)
            ew_ref[dr, p] = ew[:, sl2]
            as_ref[dr, p] = asg[:, sl2]


def _rw_proj_call(x, mod, g, mu, wr, wk, wv, w1, w2, w0, a1, a2, a0, g1, g2, kkw, *, nct):
    n, d = x.shape
    tm = RW_TM
    nt = n // tm
    nct = nct * (TM // tm)
    npair = d // LANES
    tb = tm // NB
    nb8 = n // NB
    kern = functools.partial(_rw_proj_kernel, nct=nct, nt=nt)
    full2 = lambda a: pl.BlockSpec(a.shape, lambda i: (0, 0))
    slab = pl.BlockSpec((npair, tm, LANES), lambda i: (0, i, 0))
    slab2 = pl.BlockSpec((2, npair, tm, LANES), lambda i: (0, 0, i, 0))
    slab_shape = jax.ShapeDtypeStruct((npair, n, LANES), F32)
    slab2_shape = jax.ShapeDtypeStruct((2, npair, n, LANES), F32)
    mu3 = mu.reshape(6, 1, d)
    return pl.pallas_call(
        kern,
        grid=(nt,),
        in_specs=[
            pl.BlockSpec((tm, d), lambda i: (i, 0)),
            pl.BlockSpec((NB, d), lambda i: (jnp.maximum(i * tb - 1, 0), 0)),
            pl.BlockSpec((NB, d), lambda i: (jnp.minimum((i + 1) * tb, nb8 - 1), 0)),
            pl.BlockSpec((1, 6, NB, d), lambda i: (_is_lat(i, nct), 0, 0, 0)),
            pl.BlockSpec((1, d), lambda i: (0, 0)),
            pl.BlockSpec((6, 1, d), lambda i: (0, 0, 0)),
            full2(wr), full2(wk), full2(wv),
            full2(w1), full2(w2), full2(w0), full2(a1), full2(a2), full2(a0),
            full2(g1), full2(g2), full2(kkw),
        ],
        out_specs=[slab, slab, slab, slab, slab2, slab2, pl.BlockSpec((tm, d), lambda i: (i, 0))],
        out_shape=[slab_shape, slab_shape, slab_shape, slab_shape, slab2_shape, slab2_shape,
                   jax.ShapeDtypeStruct((n, d), F32)],
        scratch_shapes=[pltpu.VMEM((tm + 2 * NB, d), F32)],
        compiler_params=_cparams(("arbitrary",)),
    )(x, x, x, mod, g, mu3, wr, wk, wv, w1, w2, w0, a1, a2, a0, g1, g2, kkw)


def _wkv_kernel(r_ref, k_ref, v_ref, kk_ref, ew_ref, as_ref, ka_ref, y_ref, st_ref, *, L, npair):
    d = pl.program_id(0)
    c = pl.program_id(1)

    @pl.when(c == 0)
    def _():
        st_ref[...] = jnp.zeros_like(st_ref)

    sgn = 1 - 2 * d
    l2 = 2 * L
    rr = lax.broadcasted_iota(jnp.int32, (l2, l2), 0)
    cc = lax.broadcasted_iota(jnp.int32, (l2, l2), 1)
    same = (rr >= L) == (cc >= L)
    diff = (rr - cc) * sgn
    strict_f = jnp.where(same & (diff > 0), 1.0, 0.0)
    incl_f = jnp.where(same & (diff >= 0), 1.0, 0.0)
    r1 = lax.broadcasted_iota(jnp.int32, (L, L), 0)
    c1 = lax.broadcasted_iota(jnp.int32, (L, L), 1)
    mcum = jnp.where((r1 - c1) * sgn >= 0, 1.0, 0.0).astype(BF16)
    head0 = lax.broadcasted_iota(jnp.int32, (L, LANES), 1) < RWKV_HEAD
    sr_i = lax.broadcasted_iota(jnp.int32, (LANES, LANES), 0)
    sc_i = lax.broadcasted_iota(jnp.int32, (LANES, LANES), 1)
    blockdiag = (sr_i < RWKV_HEAD) == (sc_i < RWKV_HEAD)
    n_dbl = int(round(math.log2(L)))

    def stack2(x):
        return jnp.concatenate([jnp.where(head0, x, 0.0), jnp.where(head0, 0.0, x)], axis=0)

    def unstack(xs):
        return xs[0:L] + xs[L:l2]

    def load(p, b):
        rows = pl.ds(b, L, stride=NB)
        return (r_ref[p, rows, :], k_ref[p, rows, :], v_ref[p, rows, :], kk_ref[p, rows, :],
                ew_ref[0, p, rows, :], as_ref[0, p, rows, :], st_ref[p * NB + b])

    def inst(ka, r, k, v, kk, ew, asg, s0):
        ldec = -ew
        h1, h2, h3 = _split3(ldec)
        cl3 = _dot(mcum, jnp.concatenate([h1, h2, h3], axis=1))
        yield
        cl = cl3[:, 0:LANES] + cl3[:, LANES:2 * LANES] + cl3[:, 2 * LANES:3 * LANES]
        ce = cl - ldec
        ecl = jnp.exp(cl)
        encl = jnp.exp(-cl)
        etot = jnp.exp(jnp.sum(ldec, axis=0, keepdims=True))
        at = -kk * jnp.exp(ce)
        rt = r * ecl
        bt = kk * asg * encl
        kt = k * (1.0 + (asg - 1.0) * ka) * encl

        ats = stack2(at)
        vsb = stack2(v).astype(BF16)
        lhs = jnp.concatenate([ats, stack2(rt)], axis=0).astype(BF16)
        rhs = jnp.concatenate([stack2(bt), stack2(kt)], axis=0).astype(BF16)
        pm = _dot_nt(lhs, rhs)
        yield
        nab = pm[0:l2, 0:l2] * strict_f
        nak = pm[0:l2, l2:2 * l2] * strict_f
        mrb = (pm[l2:2 * l2, 0:l2] * incl_f).astype(BF16)
        mrk = (pm[l2:2 * l2, l2:2 * l2] * incl_f).astype(BF16)

        xs = jnp.concatenate([ats, _dot(nak.astype(BF16), vsb)], axis=1)
        mkv = _dot(mrk, vsb)
        yield
        npow = nab
        for it in range(n_dbl):
            npb = npow.astype(BF16)
            xs = xs + _dot(npb, xs.astype(BF16))
            if it + 1 < n_dbl:
                npow = _dot(npb, npb)
            yield
        mx = _dot(mrb, xs.astype(BF16))
        at2 = unstack(xs[:, 0:LANES])
        ul = unstack(xs[:, LANES:2 * LANES])
        btb = bt.astype(BF16)
        emat = (jnp.where(blockdiag, _dot(at2.T.astype(BF16), btb), 0.0) * etot).astype(BF16)
        uv = jnp.concatenate([ul, v], axis=0)
        bk = jnp.concatenate([btb, kt.astype(BF16)], axis=0)
        dmat = jnp.where(blockdiag, _dot(uv.T.astype(BF16), bk), 0.0) * etot
        yield
        rt2 = rt + unstack(mx[:, 0:LANES])
        yl = unstack(mx[:, LANES:2 * LANES] + mkv)

        s0b = s0.astype(BF16)
        return _dot_nt(rt2.astype(BF16), s0b) + yl, s0 * etot + _dot(s0b, emat) + dmat

    def run_group(gens):
        outs = [None] * len(gens)
        live = list(enumerate(gens))
        while live:
            nxt = []
            for i, g in live:
                try:
                    next(g)
                    nxt.append((i, g))
                except StopIteration as stop:
                    outs[i] = stop.value
            live = nxt
        return outs

    def pair(p, carry):
        ka = ka_ref[p]
        for b0 in range(0, NB, WKV_GROUP):
            outs = run_group([inst(ka, *load(p, b0 + g)) for g in range(WKV_GROUP)])
            for g, (y, s1) in enumerate(outs):
                y_ref[0, p, pl.ds(b0 + g, L, stride=NB), :] = y
                st_ref[p * NB + b0 + g] = s1
        return carry

    lax.fori_loop(0, npair, pair, 0)


def _wkv_call(r, k, v, kk, ew, asg, ka, *, nct_rows):
    npair, n, _ = r.shape
    L = WKV_L
    rows = L * NB
    nc = n // rows
    ncc = nct_rows // rows
    kern = functools.partial(_wkv_kernel, L=L, npair=npair)

    def chunk(dd, c):
        bwd = jnp.where(c < ncc, ncc - 1 - c, nc - 1 + ncc - c)
        return jnp.where(dd == 0, c, bwd)

    slab = pl.BlockSpec((npair, rows, LANES), lambda dd, c: (0, chunk(dd, c), 0))
    slab2 = pl.BlockSpec((1, npair, rows, LANES), lambda dd, c: (dd, 0, chunk(dd, c), 0))
    return pl.pallas_call(
        kern,
        grid=(2, nc),
        in_specs=[slab, slab, slab, slab, slab2, slab2,
                  pl.BlockSpec((npair, 1, LANES), lambda dd, c: (0, 0, 0))],
        out_specs=slab2,
        out_shape=jax.ShapeDtypeStruct((2, npair, n, LANES), F32),
        scratch_shapes=[pltpu.VMEM((npair * NB, LANES, LANES), F32)],
        compiler_params=_cparams(("arbitrary", "arbitrary")),
    )(r, k, v, kk, ew, asg, ka)


def _rw_out_kernel(x_ref, mod_ref, y_ref, r_ref, k_ref, v_ref, as_ref, gg_ref, lng_ref, lnb_ref,
                   ka_ref, rk_ref, wo_ref, o_ref):
    x = x_ref[...]
    d = x.shape[1]
    npair = d // LANES
    cat = lambda f: jnp.concatenate([f(p) for p in range(npair)], axis=1)
    y = cat(lambda p: y_ref[0, p] + y_ref[1, p])
    r = cat(lambda p: r_ref[p])
    k = cat(lambda p: k_ref[p])
    v = cat(lambda p: v_ref[p])
    am = cat(lambda p: 0.5 * (as_ref[0, p] + as_ref[1, p]))
    ones_bd = _head_ones()
    inv = 1.0 / RWKV_HEAD
    mean = _head_sum_bcast(y, ones_bd) * inv
    yc = y - mean
    var = _head_sum_bcast(yc * yc, ones_bd) * inv
    yn = yc * lax.rsqrt(var + GN_EPS) * lng_ref[...] + lnb_ref[...]
    k_mean = k * (1.0 + (am - 1.0) * ka_ref[...])
    bonus = _head_sum_bcast(r * k_mean * rk_ref[...], ones_bd) * v
    z = ((yn + bonus) * gg_ref[...]).astype(BF16)
    o_ref[...] = _residual(x, mod_ref[0, 2], _dot(z, wo_ref[...]))


def _rw_out_call(x, mod, y, r, k, v, asg, gg, lng, lnb, ka, rk, wo, *, nct):
    n, d = x.shape
    nt = n // TM
    npair = d // LANES
    slab = pl.BlockSpec((npair, TM, LANES), lambda i: (0, i, 0))
    slab2 = pl.BlockSpec((2, npair, TM, LANES), lambda i: (0, 0, i, 0))
    row = pl.BlockSpec((1, d), lambda i: (0, 0))
    return pl.pallas_call(
        _rw_out_kernel,
        grid=(nt,),
        in_specs=[
            pl.BlockSpec((TM, d), lambda i: (i, 0)),
            pl.BlockSpec((1, 6, NB, d), lambda i: (_is_lat(i, nct), 0, 0, 0)),
            slab2, slab, slab, slab, slab2,
            pl.BlockSpec((TM, d), lambda i: (i, 0)),
            row, row, row, row,
            pl.BlockSpec((d, d), lambda i: (0, 0)),
        ],
        out_specs=pl.BlockSpec((TM, d), lambda i: (i, 0)),
        out_shape=jax.ShapeDtypeStruct((n, d), F32),
        compiler_params=_cparams(("arbitrary",)),
    )(x, mod, y, r, k, v, asg, gg, lng, lnb, ka, rk, wo)


def _final_kernel(x_ref, g_ref, o_ref):
    x = x_ref[...]
    ms = jnp.mean(x * x, axis=-1, keepdims=True)
    o_ref[...] = x * lax.rsqrt(ms + RMS_EPS) * g_ref[...]


def _final_call(x, g, *, tile0):
    n, d = x.shape
    nt = n // TM
    return pl.pallas_call(
        _final_kernel,
        grid=(nt - tile0,),
        in_specs=[pl.BlockSpec((TM, d), lambda i: (i + tile0, 0)),
                  pl.BlockSpec((1, d), lambda i: (0, 0))],
        out_specs=pl.BlockSpec((TM, d), lambda i: (i, 0)),
        out_shape=jax.ShapeDtypeStruct((n - tile0 * TM, d), F32),
        compiler_params=_cparams(("arbitrary",)),
    )(x, g)


def _row(v):
    return v.reshape(1, -1).astype(F32)


def _s5_layer(x, mod, g, p, *, nct, tile0):
    bb, cc, ab = _s5_params(p['lam_re'], p['lam_im'], p['log_dt'], p['b_re'], p['b_im'],
                            p['c_re'], p['c_im'])
    y = _s5_scan_call(x, mod, g, bb, cc, ab, nct=nct)
    return _s5_glu_call(x, mod, g, y, _row(p['d']), p['glu_w'].astype(BF16), _row(p['glu_b']),
                        nct=nct, tile0=tile0)


def _conformer_layer(x, mod, g, p, *, nct):
    u = _cv_glu_call(x, mod, g, p['pw1_w'].astype(BF16), _row(p['pw1_b']), nct=nct)
    return _cv_conv_call(u, x, mod, p['dw_w'].astype(F32), _row(p['dw_b']), _row(p['ln_g']),
                         _row(p['ln_b']), p['pw2_w'].astype(BF16), _row(p['pw2_b']), nct=nct)


def _blockdiag2(m):
    z = jnp.zeros_like(m[0])
    return jnp.concatenate([jnp.concatenate([m[0], z], axis=1),
                            jnp.concatenate([z, m[1]], axis=1)], axis=0)


def _rwkv_layer(x, mod, g, p, *, nct):
    n, d = x.shape
    npair = d // LANES
    w1 = jnp.concatenate([p['w1'][0], p['w1'][1]], axis=1).astype(BF16)
    w2 = _blockdiag2(p['w2']).astype(BF16)
    a1 = jnp.concatenate([p['a1'][0], p['a1'][1]], axis=1).astype(BF16)
    a2 = _blockdiag2(p['a2']).astype(BF16)
    gl = p['g1'].shape[1]
    glp = -(-gl // LANES) * LANES
    g1 = jnp.pad(p['g1'], ((0, 0), (0, glp - gl))).astype(BF16)
    g2 = jnp.pad(p['g2'], ((0, glp - gl), (0, 0))).astype(BF16)
    r, k, v, kk, ew, asg, gg = _rw_proj_call(
        x, mod, g, p['mu'].astype(F32), p['w_r'].astype(BF16), p['w_k'].astype(BF16),
        p['w_v'].astype(BF16), w1, w2, _row(p['w0']), a1, a2, _row(p['a0']), g1, g2,
        _row(p['k_k']), nct=nct)
    ka_slab = p['k_a'].astype(F32).reshape(npair, 1, LANES)
    y = _wkv_call(r, k, v, kk, ew, asg, ka_slab, nct_rows=nct * TM)
    return _rw_out_call(x, mod, y, r, k, v, asg, gg, _row(p['lnx_g']), _row(p['lnx_b']),
                        _row(p['k_a']), _row(p['r_k']), p['w_o'].astype(BF16), nct=nct)


def _forward(x, c, ctx, c_ctx, ada_w, ada_b, norm_mix_g, norm_ffn_g, final_g, s5, cv, rw,
             ff_w1, ff_w3, ff_w2, moe_router, moe_w1, moe_w3, moe_w2):
    bsz, seq, d = x.shape
    lc = ctx.shape[1]
    depth = ada_w.shape[0]
    assert bsz == NB and lc % TT == 0 and seq % TT == 0
    assert TT == GRID_W, "one token tile must hold exactly one latent raster row per batch entry"
    nct = lc // TT
    t_all = lc + seq
    n = t_all * NB

    xs = jnp.concatenate([jnp.swapaxes(ctx, 0, 1), jnp.swapaxes(x, 0, 1)], axis=0)
    xs = xs.reshape(n, d).astype(F32)

    cvec = jnp.zeros((2 * NB, d), F32).at[:NB].set(c).at[NB].set(c_ctx)
    m_all = _ada_call(cvec, ada_w.astype(F32), ada_b.astype(F32))
    m_all = m_all.reshape(depth, 2 * NB, 6, d)
    m_lat = jnp.swapaxes(m_all[:, :NB], 1, 2)
    m_ctx = jnp.broadcast_to(m_all[:, NB][:, :, None, :], (depth, 6, NB, d))
    mods = jnp.stack([m_ctx, m_lat], axis=1)

    ctx_rows = nct * TM
    for i in range(depth):
        last = i == depth - 1
        mod = mods[i]
        g_mix = _row(norm_mix_g[i])
        j = i // 3
        if i % 3 == 0:
            xs = _s5_layer(xs, mod, g_mix, {k_: v_[j] for k_, v_ in s5.items()}, nct=nct,
                           tile0=nct if last else 0)
            ctx_rows = 0 if last else ctx_rows
        elif i % 3 == 1:
            xs = _conformer_layer(xs, mod, g_mix, {k_: v_[j] for k_, v_ in cv.items()}, nct=nct)
        else:
            xs = _rwkv_layer(xs, mod, g_mix, {k_: v_[j] for k_, v_ in rw.items()}, nct=nct)
        fi = i // 2
        g_ffn = _row(norm_ffn_g[i])
        if i % 2 == 0:
            xs = _ffn_call(xs, mod, g_ffn, ff_w1, ff_w3, ff_w2, layer=fi, nct_rows=ctx_rows)
        else:
            xs = _moe_call(xs, mod, g_ffn, moe_router[fi], moe_w1, moe_w3, moe_w2, layer=fi,
                           nct_rows=ctx_rows)

    out = _final_call(xs, _row(final_g), tile0=ctx_rows // TM)
    return jnp.swapaxes(out.reshape(seq, NB, d), 0, 1)


def kernel(x, c, ctx, c_ctx, ada_w, ada_b, norm_mix_g, norm_ffn_g, final_g, s5_lambda_re, s5_lambda_im, s5_log_dt, s5_b_re, s5_b_im, s5_c_re, s5_c_im, s5_d, s5_glu_w, s5_glu_b, cv_pw1_w, cv_pw1_b, cv_dw_w, cv_dw_b, cv_ln_g, cv_ln_b, cv_pw2_w, cv_pw2_b, rw_mu, rw_w_r, rw_w_k, rw_w_v, rw_w_o, rw_w0, rw_w1, rw_w2, rw_a0, rw_a1, rw_a2, rw_g1, rw_g2, rw_k_k, rw_k_a, rw_r_k, rw_lnx_g, rw_lnx_b, ff_w1, ff_w3, ff_w2, moe_router, moe_w1, moe_w3, moe_w2):
    s5 = dict(lam_re=s5_lambda_re, lam_im=s5_lambda_im, log_dt=s5_log_dt, b_re=s5_b_re, b_im=s5_b_im,
              c_re=s5_c_re, c_im=s5_c_im, d=s5_d, glu_w=s5_glu_w, glu_b=s5_glu_b)
    cv = dict(pw1_w=cv_pw1_w, pw1_b=cv_pw1_b, dw_w=cv_dw_w, dw_b=cv_dw_b, ln_g=cv_ln_g, ln_b=cv_ln_b,
              pw2_w=cv_pw2_w, pw2_b=cv_pw2_b)
    rw = dict(mu=rw_mu, w_r=rw_w_r, w_k=rw_w_k, w_v=rw_w_v, w_o=rw_w_o, w0=rw_w0, w1=rw_w1, w2=rw_w2,
              a0=rw_a0, a1=rw_a1, a2=rw_a2, g1=rw_g1, g2=rw_g2, k_k=rw_k_k, k_a=rw_k_a, r_k=rw_r_k,
              lnx_g=rw_lnx_g, lnx_b=rw_lnx_b)
    return _forward(x, c, ctx, c_ctx, ada_w, ada_b, norm_mix_g, norm_ffn_g, final_g, s5, cv, rw,
                    ff_w1, ff_w3, ff_w2, moe_router, moe_w1, moe_w3, moe_w2)
```

```python
import functools
import math

import jax
import jax.numpy as jnp
from jax import lax
from jax.experimental import pallas as pl
from jax.experimental.pallas import tpu as pltpu

F32 = jnp.float32
BF16 = jnp.bfloat16

NB = 8
LANES = 128
TM = 512
TT = TM // NB
GRID_W = 64
CONV_WIDTH = 31
CONV_PAD = CONV_WIDTH // 2
HALO_ROWS = 128
S5_GROUP = 16
S5_STATE = 64
S5_JBLK = 8
RWKV_HEAD = 64
WKV_L = 64
WKV_GROUP = 8
RW_TM = 256
FFN_TM = 1024
MOE_T = 1024
MOE_BLK = 32
MOE_SLOTS = 32
RMS_EPS = 1e-6
LN_EPS = 1e-5
GN_EPS = 64e-5
KK_EPS = 1e-12
LAMBDA_RE_MAX = -1e-4
VMEM_LIMIT = 56 * 1024 * 1024


def _cparams(sem):
    return pltpu.CompilerParams(dimension_semantics=sem, vmem_limit_bytes=VMEM_LIMIT)


def _dot(a, b):
    return jnp.dot(a, b, preferred_element_type=F32)


def _dot_nt(a, b):
    return lax.dot_general(a, b, (((1,), (1,)), ((), ())), preferred_element_type=F32)


def _split3(x):
    h1 = x.astype(BF16)
    r1 = x - h1.astype(F32)
    h2 = r1.astype(BF16)
    h3 = (r1 - h2.astype(F32)).astype(BF16)
    return h1, h2, h3


def _dot_x3(a, b):
    a1 = a.astype(BF16)
    a2 = (a - a1.astype(F32)).astype(BF16)
    b1 = b.astype(BF16)
    b2 = (b - b1.astype(F32)).astype(BF16)
    return _dot(a1, b1) + _dot(a1, b2) + _dot(a2, b1)


def _sigmoid(x):
    return jax.nn.sigmoid(x)


def _norm_mod(x, g, shift, scale):
    rows, d = x.shape
    ms = jnp.mean(x * x, axis=-1, keepdims=True)
    y = x * lax.rsqrt(ms + RMS_EPS) * g
    y3 = y.reshape(rows // NB, NB, d)
    return (y3 * (1.0 + scale)[None] + shift[None]).reshape(rows, d)


def _residual(x, gate, y):
    rows, d = x.shape
    y3 = y.reshape(rows // NB, NB, d) * gate[None]
    return x + y3.reshape(rows, d)


def _is_lat(i, nct):
    return jnp.where(i >= nct, 1, 0).astype(jnp.int32)


def _ada_kernel(s_ref, w_ref, b_ref, o_ref):
    s = s_ref[...]
    s = s * _sigmoid(s)
    o_ref[0] = _dot(s.astype(BF16), w_ref[0].astype(BF16)) + b_ref[0]


def _ada_call(cvec, ada_w, ada_b):
    depth, d, d6 = ada_w.shape
    tn = 1536
    return pl.pallas_call(
        _ada_kernel,
        grid=(depth, d6 // tn),
        in_specs=[
            pl.BlockSpec((2 * NB, d), lambda l, j: (0, 0)),
            pl.BlockSpec((1, d, tn), lambda l, j: (l, 0, j)),
            pl.BlockSpec((1, 1, tn), lambda l, j: (l, 0, j)),
        ],
        out_specs=pl.BlockSpec((1, 2 * NB, tn), lambda l, j: (l, 0, j)),
        out_shape=jax.ShapeDtypeStruct((depth, 2 * NB, d6), F32),
        compiler_params=_cparams(("arbitrary", "arbitrary")),
    )(cvec, ada_w, ada_b.reshape(depth, 1, d6))


def _ffn_kernel(x_ref, mod_ref, g_ref, w1_ref, w3_ref, w2_ref, o_ref, hb_ref, acc_ref, *, n_k):
    k = pl.program_id(1)

    @pl.when(k == 0)
    def _():
        h = _norm_mod(x_ref[...], g_ref[...], mod_ref[0, 3], mod_ref[0, 4])
        hb_ref[...] = h.astype(BF16)
        acc_ref[...] = jnp.zeros_like(acc_ref)

    hb = hb_ref[...]
    a = _dot(hb, w1_ref[0].astype(BF16))
    b = _dot(hb, w3_ref[0].astype(BF16))
    acc_ref[...] += _dot((a * _sigmoid(a) * b).astype(BF16), w2_ref[0].astype(BF16))

    @pl.when(k == n_k - 1)
    def _():
        o_ref[...] = _residual(x_ref[...], mod_ref[0, 5], acc_ref[...])


def _ffn_call(x, mod, g, w1, w3, w2, *, layer, nct_rows):
    n, d = x.shape
    f = w1.shape[2]
    tm = FFN_TM
    tf = 512
    n_k = f // tf
    nct = nct_rows // tm
    return pl.pallas_call(
        functools.partial(_ffn_kernel, n_k=n_k),
        grid=(n // tm, n_k),
        in_specs=[
            pl.BlockSpec((tm, d), lambda i, k: (i, 0)),
            pl.BlockSpec((1, 6, NB, d), lambda i, k: (_is_lat(i, nct), 0, 0, 0)),
            pl.BlockSpec((1, d), lambda i, k: (0, 0)),
            pl.BlockSpec((1, d, tf), lambda i, k: (layer, 0, k)),
            pl.BlockSpec((1, d, tf), lambda i, k: (layer, 0, k)),
            pl.BlockSpec((1, tf, d), lambda i, k: (layer, k, 0)),
        ],
        out_specs=pl.BlockSpec((tm, d), lambda i, k: (i, 0)),
        out_shape=jax.ShapeDtypeStruct((n, d), F32),
        scratch_shapes=[pltpu.VMEM((tm, d), BF16), pltpu.VMEM((tm, d), F32)],
        compiler_params=_cparams(("arbitrary", "arbitrary")),
    )(x, mod, g, w1, w3, w2)


def _top2(logits, n_e):
    lane = lax.broadcasted_iota(jnp.int32, logits.shape, 1).astype(F32)
    neg = jnp.float32(-3.0e38)
    lg = jnp.where(lane < n_e, logits, neg)
    m1 = jnp.max(lg, axis=-1, keepdims=True)
    i1 = jnp.min(jnp.where(lg == m1, lane, 1.0e9), axis=-1, keepdims=True)
    lg2 = jnp.where(lane == i1, neg, lg)
    m2 = jnp.max(lg2, axis=-1, keepdims=True)
    i2 = jnp.min(jnp.where(lg2 == m2, lane, 1.0e9), axis=-1, keepdims=True)
    e2 = jnp.exp(m2 - m1)
    den = 1.0 + e2
    return lane, i1, i2, 1.0 / den, e2 / den


def _moe_route_kernel(x_ref, mod_ref, g_ref, router_ref, ltri_ref, z_ref, dest_ref, cnt_ref,
                      hb_ref, *, n_e, cap):
    t, d = x_ref.shape
    h = _norm_mod(x_ref[...], g_ref[...], mod_ref[0, 3], mod_ref[0, 4])
    hb_ref[...] = h.astype(BF16)
    lane, i1, i2, p1, p2 = _top2(_dot_x3(h, router_ref[...]), n_e)
    sel1 = lane == i1
    sel2 = lane == i2
    c = jnp.where(sel1, 1.0, 0.0) + jnp.where(sel2, 1.0, 0.0)
    rank = _dot(ltri_ref[...], c.astype(BF16))
    cnt = jnp.sum(c, axis=0, keepdims=True)
    seg = jnp.floor((cnt + (MOE_BLK - 1)) * (1.0 / MOE_BLK)) * MOE_BLK
    ur = lax.broadcasted_iota(jnp.int32, (LANES, LANES), 0)
    uc = lax.broadcasted_iota(jnp.int32, (LANES, LANES), 1)
    before = jnp.where(ur < uc, 1.0, 0.0).astype(BF16)
    start = _dot(jnp.broadcast_to(seg, (NB, LANES)).astype(BF16), before)[0:1]
    pos = start + rank
    d1 = jnp.sum(jnp.where(sel1, pos, 0.0), axis=-1, keepdims=True)
    d2 = jnp.sum(jnp.where(sel2, pos, 0.0), axis=-1, keepdims=True)
    packed = (jnp.where(lane == 0.0, d1, 0.0) + jnp.where(lane == 1.0, d2, 0.0)
              + jnp.where(lane == 2.0, p1, 0.0) + jnp.where(lane == 3.0, p2, 0.0))
    dest_ref[...] = packed
    cnt_ref[0] = jnp.broadcast_to(cnt, (NB, LANES))
    packed_t = packed.T
    d1r = packed_t[0:1]
    d2r = packed_t[1:2]
    p1r = packed_t[2:3]
    p2r = packed_t[3:4]
    rc = 256

    def chunk(ci, carry):
        r0 = pl.multiple_of(ci * rc, rc)
        row = (lax.broadcasted_iota(jnp.int32, (rc, t), 0) + r0).astype(F32)
        m1 = row == d1r
        m2 = row == d2r
        perm = (jnp.where(m1, 1.0, 0.0) + jnp.where(m2, 1.0, 0.0)).astype(BF16)
        zc = _dot(perm, hb_ref[...])
        gate = jnp.sum(jnp.where(m1, p1r, 0.0) + jnp.where(m2, p2r, 0.0), axis=-1, keepdims=True)
        ghi = gate.astype(BF16).astype(F32)
        glo = gate - ghi
        le = lax.broadcasted_iota(jnp.int32, (rc, LANES), 1)
        ext = jnp.where(le == 0, ghi, jnp.where(le == 1, glo, 0.0))
        z_ref[pl.ds(r0, rc), 0:d] = zc.astype(BF16)
        z_ref[pl.ds(r0, rc), d:d + LANES] = ext.astype(BF16)
        return carry

    lax.fori_loop(0, cap // rc, chunk, 0)


def _moe_experts_kernel(tin_ref, se_ref, nu_ref, *refs, n_k, d):
    z_refs = refs[:MOE_SLOTS]
    w1_ref, w3_ref, w2_ref, o_ref, xs_ref, acc_ref = refs[MOE_SLOTS:]
    s = pl.program_id(0)
    k = pl.program_id(1)
    used = s < nu_ref[0]

    @pl.when(used & (k == 0))
    def _():
        for j in range(MOE_SLOTS):
            xs_ref[j * MOE_BLK:(j + 1) * MOE_BLK] = z_refs[j][...]
        acc_ref[...] = jnp.zeros_like(acc_ref)

    @pl.when(used)
    def _():
        hb = xs_ref[:, 0:d]
        ext = xs_ref[:, d:d + LANES].astype(F32)
        le = lax.broadcasted_iota(jnp.int32, ext.shape, 1)
        gate = jnp.sum(jnp.where(le < 2, ext, 0.0), axis=-1, keepdims=True)
        a = _dot(hb, w1_ref[0, 0].astype(BF16))
        b = _dot(hb, w3_ref[0, 0].astype(BF16))
        act = a * _sigmoid(a) * b * gate
        acc_ref[...] += _dot(act.astype(BF16), w2_ref[0, 0].astype(BF16))

    @pl.when(k == n_k - 1)
    def _():
        o_ref[...] = jnp.where(used, acc_ref[...], 0.0).astype(BF16)


def _moe_combine_kernel(tc_ref, *refs, nblk, cap):
    zb_refs = refs[:nblk]
    dest_ref, x_ref, mod_ref, o_ref, zbuf_ref = refs[nblk:]
    t, d = x_ref.shape
    for j in range(nblk):
        zbuf_ref[j * MOE_BLK:(j + 1) * MOE_BLK] = zb_refs[j][...]
    packed = dest_ref[...]
    lane = lax.broadcasted_iota(jnp.int32, packed.shape, 1)
    d1 = jnp.sum(jnp.where(lane == 0, packed, 0.0), axis=-1, keepdims=True)
    d2 = jnp.sum(jnp.where(lane == 1, packed, 0.0), axis=-1, keepdims=True)
    kc = 256
    acc = jnp.zeros((t, d), F32)
    for c0 in range(0, cap, kc):
        col = (lax.broadcasted_iota(jnp.int32, (t, kc), 1) + c0).astype(F32)
        sel = (jnp.where(col == d1, 1.0, 0.0) + jnp.where(col == d2, 1.0, 0.0)).astype(BF16)
        acc = acc + _dot(sel, zbuf_ref[c0:c0 + kc])
    o_ref[...] = _residual(x_ref[...], mod_ref[0, 5], acc)


def _moe_tables(cnt, *, n_tiles, n_e, nblk, n_steps):
    i32 = jnp.int32
    e_ids = jnp.arange(n_e, dtype=i32)
    t_ids = jnp.arange(n_tiles, dtype=i32)
    pc = (cnt + (MOE_BLK - 1)) // MOE_BLK
    lb = jnp.cumsum(pc, axis=1) - pc
    cum = jnp.cumsum(pc, axis=0)
    cum_ex = cum - pc
    nb_e = cum[-1]
    steps_e = (nb_e + (MOE_SLOTS - 1)) // MOE_SLOTS
    ends = jnp.cumsum(steps_e)
    step0 = ends - steps_e
    sidx = jnp.arange(n_steps, dtype=i32)
    step_e = jnp.minimum(jnp.sum((sidx[:, None] >= ends[None, :]).astype(i32), axis=1), n_e - 1)
    oh_e = (step_e[:, None] == e_ids[None, :]).astype(i32)
    by_step = lambda v: jnp.sum(oh_e * v[None, :], axis=1)
    by_step_tile = lambda m: jnp.sum(oh_e[:, None, :] * m[None, :, :], axis=2)
    rank = ((sidx - by_step(step0))[:, None] * MOE_SLOTS
            + jnp.arange(MOE_SLOTS, dtype=i32)[None, :])
    cum_s = by_step_tile(cum)
    tile = jnp.sum((cum_s[:, None, :] <= rank[:, :, None]).astype(i32), axis=2)
    ok = (sidx < ends[-1])[:, None] & (rank < by_step(nb_e)[:, None])
    oh_t = (jnp.minimum(tile, n_tiles - 1)[:, :, None] == t_ids[None, None, :]).astype(i32)
    at_tile = lambda m: jnp.sum(oh_t * by_step_tile(m)[:, None, :], axis=2)
    src = jnp.sum(oh_t * t_ids, axis=2) * nblk + at_tile(lb) + rank - at_tile(cum_ex)
    tin = jnp.where(ok, src, 0).reshape(-1).astype(i32)
    lblk = jnp.arange(nblk, dtype=i32)[None, :]
    seg_end = lb + pc
    e_l = jnp.minimum(jnp.sum((seg_end[:, None, :] <= lblk[:, :, None]).astype(i32), axis=2), n_e - 1)
    oh_l = (e_l[:, :, None] == e_ids[None, None, :]).astype(i32)
    r_l = jnp.sum(oh_l * (cum_ex - lb)[:, None, :], axis=2) + lblk
    flat = (jnp.sum(oh_l * step0, axis=2) + r_l // MOE_SLOTS) * MOE_SLOTS + r_l % MOE_SLOTS
    tcomb = jnp.where(lblk < seg_end[:, -1:], flat, 0).reshape(-1).astype(i32)
    return tin, tcomb, step_e.astype(i32), ends[-1:].astype(i32)


def _moe_call(x, mod, g, router, w1, w3, w2, *, layer, nct_rows):
    n, d = x.shape
    _, n_e, _, f = w1.shape
    t = MOE_T
    assert n % t == 0 and nct_rows % t == 0
    n_tiles = n // t
    nct = nct_rows // t
    cap = 2 * t + n_e * MOE_BLK
    nblk = cap // MOE_BLK
    rows = MOE_SLOTS * MOE_BLK
    n_steps = -(-(n_tiles * nblk) // MOE_SLOTS) + n_e
    tf = 512
    n_k = f // tf
    router_p = jnp.pad(router.astype(F32), ((0, 0), (0, LANES - n_e)))
    ltri = jnp.tril(jnp.ones((t, t), F32), k=-1).astype(BF16)
    mod_spec = lambda off: pl.BlockSpec((1, 6, NB, d), lambda i, *_: (_is_lat(i, nct), 0, 0, 0))

    z, dest, cnt = pl.pallas_call(
        functools.partial(_moe_route_kernel, n_e=n_e, cap=cap),
        grid=(n_tiles,),
        in_specs=[
            pl.BlockSpec((t, d), lambda i: (i, 0)),
            mod_spec(0),
            pl.BlockSpec((1, d), lambda i: (0, 0)),
            pl.BlockSpec((d, LANES), lambda i: (0, 0)),
            pl.BlockSpec((t, t), lambda i: (0, 0)),
        ],
        out_specs=[
            pl.BlockSpec((cap, d + LANES), lambda i: (i, 0)),
            pl.BlockSpec((t, LANES), lambda i: (i, 0)),
            pl.BlockSpec((1, NB, LANES), lambda i: (i, 0, 0)),
        ],
        out_shape=[
            jax.ShapeDtypeStruct((n_tiles * cap, d + LANES), BF16),
            jax.ShapeDtypeStruct((n, LANES), F32),
            jax.ShapeDtypeStruct((n_tiles, NB, LANES), F32),
        ],
        scratch_shapes=[pltpu.VMEM((t, d), BF16)],
        compiler_params=_cparams(("arbitrary",)),
    )(x, mod, g, router_p, ltri)

    tin, tcomb, step_e, nused = _moe_tables(cnt[:, 0, :n_e].astype(jnp.int32), n_tiles=n_tiles,
                                            n_e=n_e, nblk=nblk, n_steps=n_steps)

    def kk(s, k, nu_r):
        return jnp.where(s < nu_r[0], k, n_k - 1)

    def z_spec(j):
        return pl.BlockSpec((MOE_BLK, d + LANES), lambda s, k, tin_r, se_r, nu_r: (tin_r[s * MOE_SLOTS + j], 0))

    zo = pl.pallas_call(
        functools.partial(_moe_experts_kernel, n_k=n_k, d=d),
        grid_spec=pltpu.PrefetchScalarGridSpec(
            num_scalar_prefetch=3,
            grid=(n_steps, n_k),
            in_specs=[z_spec(j) for j in range(MOE_SLOTS)] + [
                pl.BlockSpec((1, 1, d, tf), lambda s, k, tin_r, se_r, nu_r: (layer, se_r[s], 0, kk(s, k, nu_r))),
                pl.BlockSpec((1, 1, d, tf), lambda s, k, tin_r, se_r, nu_r: (layer, se_r[s], 0, kk(s, k, nu_r))),
                pl.BlockSpec((1, 1, tf, d), lambda s, k, tin_r, se_r, nu_r: (layer, se_r[s], kk(s, k, nu_r), 0)),
            ],
            out_specs=pl.BlockSpec((rows, d), lambda s, k, tin_r, se_r, nu_r: (s, 0)),
            scratch_shapes=[pltpu.VMEM((rows, d + LANES), BF16), pltpu.VMEM((rows, d), F32)],
        ),
        out_shape=jax.ShapeDtypeStruct((n_steps * rows, d), BF16),
        compiler_params=_cparams(("arbitrary", "arbitrary")),
    )(tin, step_e, nused, *([z] * MOE_SLOTS), w1, w3, w2)

    def zo_spec(j):
        return pl.BlockSpec((MOE_BLK, d), lambda i, tc_r: (tc_r[i * nblk + j], 0))

    return pl.pallas_call(
        functools.partial(_moe_combine_kernel, nblk=nblk, cap=cap),
        grid_spec=pltpu.PrefetchScalarGridSpec(
            num_scalar_prefetch=1,
            grid=(n_tiles,),
            in_specs=[zo_spec(j) for j in range(nblk)] + [
                pl.BlockSpec((t, LANES), lambda i, tc_r: (i, 0)),
                pl.BlockSpec((t, d), lambda i, tc_r: (i, 0)),
                mod_spec(0),
            ],
            out_specs=pl.BlockSpec((t, d), lambda i, tc_r: (i, 0)),
            scratch_shapes=[pltpu.VMEM((cap, d), BF16)],
        ),
        out_shape=jax.ShapeDtypeStruct((n, d), F32),
        compiler_params=_cparams(("arbitrary",)),
    )(tcomb, *([zo] * nblk), dest, x, mod)


def _dir_tile(d, s, nct, nt):
    bwd = jnp.where(s < nct, nct - 1 - s, nt - 1 + nct - s)
    return jnp.where(d == 0, s, bwd)


def _s5_scan_kernel(x_ref, mod_ref, g_ref, bb_ref, cc_ref, ab_ref, y_ref,
                    hb_ref, bu0_ref, bu1_ref, bu2_ref, st_ref):
    d = pl.program_id(0)
    s = pl.program_id(1)
    half = S5_JBLK * S5_STATE
    bufs = (bu0_ref, bu1_ref, bu2_ref)

    @pl.when(s == 0)
    def _():
        st_ref[...] = jnp.zeros_like(st_ref)

    h = _norm_mod(x_ref[...], g_ref[...], mod_ref[0, 0], mod_ref[0, 1])
    hb_ref[...] = h.astype(BF16)

    def project(j):
        bufs[j % 3][...] = _dot(hb_ref[:, j * LANES:(j + 1) * LANES], bb_ref[0, j])

    def scan(j):
        buf = bufs[j % 3]
        ar = ab_ref[0, j, 0]
        ai = ab_ref[0, j, 1]
        re = st_ref[j, 0]
        im = st_ref[j, 1]
        for t in range(TT):
            tt = t + d * (TT - 1 - 2 * t)
            r0 = pl.multiple_of(tt * NB, NB)
            bre = buf[pl.ds(r0, NB), 0:half]
            bim = buf[pl.ds(r0, NB), half:2 * half]
            re, im = ar * re - ai * im + bre, ar * im + ai * re + bim
            buf[pl.ds(r0, NB), 0:half] = re
            buf[pl.ds(r0, NB), half:2 * half] = im
        st_ref[j, 0] = re
        st_ref[j, 1] = im

    def readout(j):
        y_ref[0, j] = _dot(bufs[j % 3][...].astype(BF16), cc_ref[0, j])

    for i in range(S5_JBLK + 2):
        if i < S5_JBLK:
            project(i)
        if 0 <= i - 1 < S5_JBLK:
            scan(i - 1)
        if 0 <= i - 2 < S5_JBLK:
            readout(i - 2)


def _s5_scan_call(x, mod, g, bb, cc, ab, *, nct):
    n, d = x.shape
    nt = n // TM
    half = S5_JBLK * S5_STATE
    tile = lambda dd, s: _dir_tile(dd, s, nct, nt)
    return pl.pallas_call(
        _s5_scan_kernel,
        grid=(2, nt),
        in_specs=[
            pl.BlockSpec((TM, d), lambda dd, s: (tile(dd, s), 0)),
            pl.BlockSpec((1, 6, NB, d), lambda dd, s: (_is_lat(tile(dd, s), nct), 0, 0, 0)),
            pl.BlockSpec((1, d), lambda dd, s: (0, 0)),
            pl.BlockSpec((1, S5_JBLK, LANES, 2 * half), lambda dd, s: (dd, 0, 0, 0)),
            pl.BlockSpec((1, S5_JBLK, 2 * half, LANES), lambda dd, s: (dd, 0, 0, 0)),
            pl.BlockSpec((1, S5_JBLK, 2, NB, half), lambda dd, s: (dd, 0, 0, 0, 0)),
        ],
        out_specs=pl.BlockSpec((1, S5_JBLK, TM, LANES), lambda dd, s: (dd, 0, tile(dd, s), 0)),
        out_shape=jax.ShapeDtypeStruct((2, S5_JBLK, n, LANES), F32),
        scratch_shapes=[
            pltpu.VMEM((TM, d), BF16),
            pltpu.VMEM((TM, 2 * half), F32),
            pltpu.VMEM((TM, 2 * half), F32),
            pltpu.VMEM((TM, 2 * half), F32),
            pltpu.VMEM((S5_JBLK, 2, NB, half), F32),
        ],
        compiler_params=_cparams(("arbitrary", "arbitrary")),
    )(x, mod, g, bb, cc, ab)


def _gelu_tanh(x):
    c = math.sqrt(2.0 / math.pi)
    return 0.5 * x * (1.0 + jnp.tanh(c * (x + 0.044715 * (x * x * x))))


def _s5_glu_kernel(x_ref, mod_ref, g_ref, y_ref, dsk_ref, w_ref, b_ref, o_ref):
    x = x_ref[...]
    d = x.shape[1]
    h = _norm_mod(x, g_ref[...], mod_ref[0, 0], mod_ref[0, 1])
    y = jnp.concatenate([y_ref[0, j] + y_ref[1, j] for j in range(S5_JBLK)], axis=1)
    z = _gelu_tanh(y + dsk_ref[...] * h).astype(BF16)
    zz = _dot(z, w_ref[...]) + b_ref[...]
    out = zz[:, :d] * _sigmoid(zz[:, d:])
    o_ref[...] = _residual(x, mod_ref[0, 2], out)


def _s5_glu_call(x, mod, g, y, dsk, w, b, *, nct, tile0):
    n, d = x.shape
    nt = n // TM
    return pl.pallas_call(
        _s5_glu_kernel,
        grid=(nt - tile0,),
        in_specs=[
            pl.BlockSpec((TM, d), lambda i: (i + tile0, 0)),
            pl.BlockSpec((1, 6, NB, d), lambda i: (_is_lat(i + tile0, nct), 0, 0, 0)),
            pl.BlockSpec((1, d), lambda i: (0, 0)),
            pl.BlockSpec((2, S5_JBLK, TM, LANES), lambda i: (0, 0, i + tile0, 0)),
            pl.BlockSpec((1, d), lambda i: (0, 0)),
            pl.BlockSpec((d, 2 * d), lambda i: (0, 0)),
            pl.BlockSpec((1, 2 * d), lambda i: (0, 0)),
        ],
        out_specs=pl.BlockSpec((TM, d), lambda i: (i, 0)),
        out_shape=jax.ShapeDtypeStruct((n - tile0 * TM, d), F32),
        compiler_params=_cparams(("arbitrary",)),
    )(x, mod, g, y, dsk, w, b)


def _s5_params(lam_re, lam_im, log_dt, b_re, b_im, c_re, c_im):
    lam_re = jnp.minimum(lam_re.astype(F32), LAMBDA_RE_MAX)
    lam_im = lam_im.astype(F32)
    dt = jnp.exp(log_dt.astype(F32))[..., None]
    mag = jnp.exp(lam_re * dt)
    abar_re = mag * jnp.cos(lam_im * dt)
    abar_im = mag * jnp.sin(lam_im * dt)
    inv_den = 1.0 / (lam_re * lam_re + lam_im * lam_im)
    gam_re = ((abar_re - 1.0) * lam_re + abar_im * lam_im) * inv_den
    gam_im = (abar_im * lam_re - (abar_re - 1.0) * lam_im) * inv_den
    b_re = b_re.astype(F32)
    b_im = b_im.astype(F32)
    bb_re = gam_re[..., None] * b_re - gam_im[..., None] * b_im
    bb_im = gam_re[..., None] * b_im + gam_im[..., None] * b_re
    g_all = lam_re.shape[1]
    nj = g_all // S5_JBLK
    eye = jnp.eye(S5_JBLK, dtype=F32)

    def in_mat(m):
        m = m.reshape(2, nj, S5_JBLK, S5_STATE, S5_GROUP)
        m = jnp.einsum('djgnp,gh->djgphn', m, eye)
        return m.reshape(2, nj, S5_JBLK * S5_GROUP, S5_JBLK * S5_STATE)

    def out_mat(m):
        m = m.reshape(2, nj, S5_JBLK, S5_GROUP, S5_STATE)
        m = jnp.einsum('djgpn,gh->djgnhp', m, eye)
        return m.reshape(2, nj, S5_JBLK * S5_STATE, S5_JBLK * S5_GROUP)

    bb = jnp.concatenate([in_mat(bb_re), in_mat(bb_im)], axis=-1).astype(BF16)
    cc = jnp.concatenate([out_mat(c_re.astype(F32)), -out_mat(c_im.astype(F32))], axis=-2).astype(BF16)
    half = S5_JBLK * S5_STATE
    ab = jnp.stack([abar_re.reshape(2, nj, half), abar_im.reshape(2, nj, half)], axis=2)
    ab = jnp.broadcast_to(ab[:, :, :, None, :], (2, nj, 2, NB, half))
    return bb, cc, ab


def _cv_glu_kernel(x_ref, mod_ref, g_ref, w_ref, b_ref, o_ref):
    x = x_ref[...]
    d = x.shape[1]
    h = _norm_mod(x, g_ref[...], mod_ref[0, 0], mod_ref[0, 1]).astype(BF16)
    u = _dot(h, w_ref[...]) + b_ref[...]
    o_ref[...] = u[:, :d] * _sigmoid(u[:, d:])


def _cv_glu_call(x, mod, g, w, b, *, nct):
    n, d = x.shape
    nt = n // TM
    return pl.pallas_call(
        _cv_glu_kernel,
        grid=(nt,),
        in_specs=[
            pl.BlockSpec((TM, d), lambda i: (i, 0)),
            pl.BlockSpec((1, 6, NB, d), lambda i: (_is_lat(i, nct), 0, 0, 0)),
            pl.BlockSpec((1, d), lambda i: (0, 0)),
            pl.BlockSpec((d, 2 * d), lambda i: (0, 0)),
            pl.BlockSpec((1, 2 * d), lambda i: (0, 0)),
        ],
        out_specs=pl.BlockSpec((TM, d), lambda i: (i, 0)),
        out_shape=jax.ShapeDtypeStruct((n, d), F32),
        compiler_params=_cparams(("arbitrary",)),
    )(x, mod, g, w, b)


def _cv_conv_kernel(u_ref, up_ref, un_ref, x_ref, mod_ref, dw_ref, dwb_ref, lng_ref, lnb_ref,
                    w_ref, b_ref, o_ref, pad_ref, cv_ref, *, nct):
    i = pl.program_id(0)
    halo = CONV_PAD * NB
    prev_ok = (i > 0) & (i < nct)
    next_ok = i < nct - 1
    pad_ref[0:halo] = jnp.where(prev_ok, up_ref[HALO_ROWS - halo:HALO_ROWS], 0.0)
    pad_ref[halo:halo + TM] = u_ref[...]
    pad_ref[halo + TM:halo + TM + halo] = jnp.where(next_ok, un_ref[0:halo], 0.0)

    rb = 32
    lw = 512
    d = dwb_ref.shape[1]

    def body(c, _):
        r0 = pl.multiple_of(c * rb, rb)
        for l0 in range(0, d, lw):
            bias = jnp.broadcast_to(dwb_ref[:, l0:l0 + lw], (NB, lw))
            acc = [bias] * (rb // NB)
            for k in range(CONV_WIDTH):
                w = dw_ref[k, :, l0:l0 + lw]
                for g in range(rb // NB):
                    acc[g] = acc[g] + w * pad_ref[pl.ds(r0 + (k + g) * NB, NB), l0:l0 + lw]
            for g in range(rb // NB):
                cv_ref[pl.ds(r0 + g * NB, NB), l0:l0 + lw] = acc[g]
        return 0

    lax.fori_loop(0, TM // rb, body, 0)

    c = cv_ref[...]
    mu = jnp.mean(c, axis=-1, keepdims=True)
    cc = c - mu
    var = jnp.mean(cc * cc, axis=-1, keepdims=True)
    y = cc * lax.rsqrt(var + LN_EPS) * lng_ref[...] + lnb_ref[...]
    y = (y * _sigmoid(y)).astype(BF16)
    out = _dot(y, w_ref[...]) + b_ref[...]
    o_ref[...] = _residual(x_ref[...], mod_ref[0, 2], out)


def _cv_conv_call(u, x, mod, dw_w, dw_b, ln_g, ln_b, w, b, *, nct):
    n, d = x.shape
    nt = n // TM
    hb = TM // HALO_ROWS
    nhb = n // HALO_ROWS
    kern = functools.partial(_cv_conv_kernel, nct=nct)
    row = lambda i: (0, 0)
    return pl.pallas_call(
        kern,
        grid=(nt,),
        in_specs=[
            pl.BlockSpec((TM, d), lambda i: (i, 0)),
            pl.BlockSpec((HALO_ROWS, d), lambda i: (jnp.maximum(i * hb - 1, 0), 0)),
            pl.BlockSpec((HALO_ROWS, d), lambda i: (jnp.minimum((i + 1) * hb, nhb - 1), 0)),
            pl.BlockSpec((TM, d), lambda i: (i, 0)),
            pl.BlockSpec((1, 6, NB, d), lambda i: (_is_lat(i, nct), 0, 0, 0)),
            pl.BlockSpec((CONV_WIDTH, NB, d), lambda i: (0, 0, 0)),
            pl.BlockSpec((1, d), row),
            pl.BlockSpec((1, d), row),
            pl.BlockSpec((1, d), row),
            pl.BlockSpec((d, d), row),
            pl.BlockSpec((1, d), row),
        ],
        out_specs=pl.BlockSpec((TM, d), lambda i: (i, 0)),
        out_shape=jax.ShapeDtypeStruct((n, d), F32),
        scratch_shapes=[
            pltpu.VMEM((TM + 2 * CONV_PAD * NB, d), F32),
            pltpu.VMEM((TM, d), F32),
        ],
        compiler_params=_cparams(("arbitrary",)),
    )(u, u, u, x, mod, jnp.broadcast_to(dw_w[:, None, :], (CONV_WIDTH, NB, d)), dw_b, ln_g, ln_b, w, b)


def _head_ones():
    w = 2 * LANES
    r = lax.broadcasted_iota(jnp.int32, (w, w), 0) // RWKV_HEAD
    c = lax.broadcasted_iota(jnp.int32, (w, w), 1) // RWKV_HEAD
    return jnp.where(r == c, 1.0, 0.0).astype(BF16)


def _head_sum_bcast(x, ones_bd):
    hi = x.astype(BF16)
    lo = (x - hi.astype(F32)).astype(BF16)
    w = ones_bd.shape[0]
    cols = [_dot(hi[:, c0:c0 + w], ones_bd) + _dot(lo[:, c0:c0 + w], ones_bd)
            for c0 in range(0, x.shape[1], w)]
    return jnp.concatenate(cols, axis=1)


def _rw_proj_kernel(x_ref, xp_ref, xn_ref, mod_ref, g_ref, mu_ref, wr_ref, wk_ref, wv_ref,
                    w1_ref, w2_ref, w0_ref, a1_ref, a2_ref, a0_ref, g1_ref, g2_ref, kkw_ref,
                    r_ref, k_ref, v_ref, kk_ref, ew_ref, as_ref, gg_ref, hp_ref, *, nct, nt):
    i = pl.program_id(0)
    tm, d = x_ref.shape
    g = g_ref[...]
    sh = mod_ref[0, 0]
    sc = mod_ref[0, 1]
    h = _norm_mod(x_ref[...], g, sh, sc)
    hp = _norm_mod(xp_ref[...], g, sh, sc)
    hn = _norm_mod(xn_ref[...], g, sh, sc)
    prev_ok = (i != 0) & (i != nct)
    next_ok = (i != nct - 1) & (i != nt - 1)
    hp_ref[0:NB] = jnp.where(prev_ok, hp, 0.0)
    hp_ref[NB:NB + tm] = h
    hp_ref[NB + tm:NB + tm + NB] = jnp.where(next_ok, hn, 0.0)
    xx = 0.5 * (hp_ref[0:tm] + hp_ref[2 * NB:2 * NB + tm]) - h

    def mix(j):
        return (h + xx * mu_ref[j]).astype(BF16)

    r = _dot(mix(0), wr_ref[...])
    k = _dot(mix(2), wk_ref[...])
    v = _dot(mix(3), wv_ref[...])
    zw = w0_ref[...] + _dot(jnp.tanh(_dot(mix(1), w1_ref[...])).astype(BF16), w2_ref[...])
    ew = _sigmoid(zw) * math.exp(-0.5)
    asg = _sigmoid(a0_ref[...] + _dot(_dot(mix(4), a1_ref[...]).astype(BF16), a2_ref[...]))
    gg_ref[...] = _dot(_sigmoid(_dot(mix(5), g1_ref[...])).astype(BF16), g2_ref[...])

    kk = k * kkw_ref[...]
    kk = kk * lax.rsqrt(_head_sum_bcast(kk * kk, _head_ones()) + KK_EPS)

    for p in range(d // LANES):
        sl = slice(p * LANES, (p + 1) * LANES)
        r_ref[p] = r[:, sl]
        k_ref[p] = k[:, sl]
        v_ref[p] = v[:, sl]
        kk_ref[p] = kk[:, sl]
        for dr in range(2):
            sl2 = slice(dr * d + p * LANES, dr * d + (p + 1) * LANES)
            ew_ref[dr, p] = ew[:, sl2]
            as_ref[dr, p] = asg[:, sl2]


def _rw_proj_call(x, mod, g, mu, wr, wk, wv, w1, w2, w0, a1, a2, a0, g1, g2, kkw, *, nct):
    n, d = x.shape
    tm = RW_TM
    nt = n // tm
    nct = nct * (TM // tm)
    npair = d // LANES
    tb = tm // NB
    nb8 = n // NB
    kern = functools.partial(_rw_proj_kernel, nct=nct, nt=nt)
    full2 = lambda a: pl.BlockSpec(a.shape, lambda i: (0, 0))
    slab = pl.BlockSpec((npair, tm, LANES), lambda i: (0, i, 0))
    slab2 = pl.BlockSpec((2, npair, tm, LANES), lambda i: (0, 0, i, 0))
    slab_shape = jax.ShapeDtypeStruct((npair, n, LANES), F32)
    slab2_shape = jax.ShapeDtypeStruct((2, npair, n, LANES), F32)
    mu3 = mu.reshape(6, 1, d)
    return pl.pallas_call(
        kern,
        grid=(nt,),
        in_specs=[
            pl.BlockSpec((tm, d), lambda i: (i, 0)),
            pl.BlockSpec((NB, d), lambda i: (jnp.maximum(i * tb - 1, 0), 0)),
            pl.BlockSpec((NB, d), lambda i: (jnp.minimum((i + 1) * tb, nb8 - 1), 0)),
            pl.BlockSpec((1, 6, NB, d), lambda i: (_is_lat(i, nct), 0, 0, 0)),
            pl.BlockSpec((1, d), lambda i: (0, 0)),
            pl.BlockSpec((6, 1, d), lambda i: (0, 0, 0)),
            full2(wr), full2(wk), full2(wv),
            full2(w1), full2(w2), full2(w0), full2(a1), full2(a2), full2(a0),
            full2(g1), full2(g2), full2(kkw),
        ],
        out_specs=[slab, slab, slab, slab, slab2, slab2, pl.BlockSpec((tm, d), lambda i: (i, 0))],
        out_shape=[slab_shape, slab_shape, slab_shape, slab_shape, slab2_shape, slab2_shape,
                   jax.ShapeDtypeStruct((n, d), F32)],
        scratch_shapes=[pltpu.VMEM((tm + 2 * NB, d), F32)],
        compiler_params=_cparams(("arbitrary",)),
    )(x, x, x, mod, g, mu3, wr, wk, wv, w1, w2, w0, a1, a2, a0, g1, g2, kkw)


def _wkv_kernel(r_ref, k_ref, v_ref, kk_ref, ew_ref, as_ref, ka_ref, y_ref, st_ref, *, L, npair):
    d = pl.program_id(0)
    c = pl.program_id(1)

    @pl.when(c == 0)
    def _():
        st_ref[...] = jnp.zeros_like(st_ref)

    sgn = 1 - 2 * d
    l2 = 2 * L
    rr = lax.broadcasted_iota(jnp.int32, (l2, l2), 0)
    cc = lax.broadcasted_iota(jnp.int32, (l2, l2), 1)
    same = (rr >= L) == (cc >= L)
    diff = (rr - cc) * sgn
    strict_f = jnp.where(same & (diff > 0), 1.0, 0.0)
    incl_f = jnp.where(same & (diff >= 0), 1.0, 0.0)
    r1 = lax.broadcasted_iota(jnp.int32, (L, L), 0)
    c1 = lax.broadcasted_iota(jnp.int32, (L, L), 1)
    mcum = jnp.where((r1 - c1) * sgn >= 0, 1.0, 0.0).astype(BF16)
    head0 = lax.broadcasted_iota(jnp.int32, (L, LANES), 1) < RWKV_HEAD
    sr_i = lax.broadcasted_iota(jnp.int32, (LANES, LANES), 0)
    sc_i = lax.broadcasted_iota(jnp.int32, (LANES, LANES), 1)
    blockdiag = (sr_i < RWKV_HEAD) == (sc_i < RWKV_HEAD)
    n_dbl = int(round(math.log2(L)))

    def stack2(x):
        return jnp.concatenate([jnp.where(head0, x, 0.0), jnp.where(head0, 0.0, x)], axis=0)

    def unstack(xs):
        return xs[0:L] + xs[L:l2]

    def load(p, b):
        rows = pl.ds(b, L, stride=NB)
        return (r_ref[p, rows, :], k_ref[p, rows, :], v_ref[p, rows, :], kk_ref[p, rows, :],
                ew_ref[0, p, rows, :], as_ref[0, p, rows, :], st_ref[p * NB + b])

    def inst(ka, r, k, v, kk, ew, asg, s0):
        ldec = -ew
        h1, h2, h3 = _split3(ldec)
        cl3 = _dot(mcum, jnp.concatenate([h1, h2, h3], axis=1))
        yield
        cl = cl3[:, 0:LANES] + cl3[:, LANES:2 * LANES] + cl3[:, 2 * LANES:3 * LANES]
        ce = cl - ldec
        ecl = jnp.exp(cl)
        encl = jnp.exp(-cl)
        etot = jnp.exp(jnp.sum(ldec, axis=0, keepdims=True))
        at = -kk * jnp.exp(ce)
        rt = r * ecl
        bt = kk * asg * encl
        kt = k * (1.0 + (asg - 1.0) * ka) * encl

        ats = stack2(at)
        vsb = stack2(v).astype(BF16)
        lhs = jnp.concatenate([ats, stack2(rt)], axis=0).astype(BF16)
        rhs = jnp.concatenate([stack2(bt), stack2(kt)], axis=0).astype(BF16)
        pm = _dot_nt(lhs, rhs)
        yield
        nab = pm[0:l2, 0:l2] * strict_f
        nak = pm[0:l2, l2:2 * l2] * strict_f
        mrb = (pm[l2:2 * l2, 0:l2] * incl_f).astype(BF16)
        mrk = (pm[l2:2 * l2, l2:2 * l2] * incl_f).astype(BF16)

        xs = jnp.concatenate([ats, _dot(nak.astype(BF16), vsb)], axis=1)
        mkv = _dot(mrk, vsb)
        yield
        npow = nab
        for it in range(n_dbl):
            npb = npow.astype(BF16)
            xs = xs + _dot(npb, xs.astype(BF16))
            if it + 1 < n_dbl:
                npow = _dot(npb, npb)
            yield
        mx = _dot(mrb, xs.astype(BF16))
        at2 = unstack(xs[:, 0:LANES])
        ul = unstack(xs[:, LANES:2 * LANES])
        btb = bt.astype(BF16)
        emat = (jnp.where(blockdiag, _dot(at2.T.astype(BF16), btb), 0.0) * etot).astype(BF16)
        uv = jnp.concatenate([ul, v], axis=0)
        bk = jnp.concatenate([btb, kt.astype(BF16)], axis=0)
        dmat = jnp.where(blockdiag, _dot(uv.T.astype(BF16), bk), 0.0) * etot
        yield
        rt2 = rt + unstack(mx[:, 0:LANES])
        yl = unstack(mx[:, LANES:2 * LANES] + mkv)

        s0b = s0.astype(BF16)
        return _dot_nt(rt2.astype(BF16), s0b) + yl, s0 * etot + _dot(s0b, emat) + dmat

    def run_group(gens):
        outs = [None] * len(gens)
        live = list(enumerate(gens))
        while live:
            nxt = []
            for i, g in live:
                try:
                    next(g)
                    nxt.append((i, g))
                except StopIteration as stop:
                    outs[i] = stop.value
            live = nxt
        return outs

    def pair(p, carry):
        ka = ka_ref[p]
        for b0 in range(0, NB, WKV_GROUP):
            outs = run_group([inst(ka, *load(p, b0 + g)) for g in range(WKV_GROUP)])
            for g, (y, s1) in enumerate(outs):
                y_ref[0, p, pl.ds(b0 + g, L, stride=NB), :] = y
                st_ref[p * NB + b0 + g] = s1
        return carry

    lax.fori_loop(0, npair, pair, 0)


def _wkv_call(r, k, v, kk, ew, asg, ka, *, nct_rows):
    npair, n, _ = r.shape
    L = WKV_L
    rows = L * NB
    nc = n // rows
    ncc = nct_rows // rows
    kern = functools.partial(_wkv_kernel, L=L, npair=npair)

    def chunk(dd, c):
        bwd = jnp.where(c < ncc, ncc - 1 - c, nc - 1 + ncc - c)
        return jnp.where(dd == 0, c, bwd)

    slab = pl.BlockSpec((npair, rows, LANES), lambda dd, c: (0, chunk(dd, c), 0))
    slab2 = pl.BlockSpec((1, npair, rows, LANES), lambda dd, c: (dd, 0, chunk(dd, c), 0))
    return pl.pallas_call(
        kern,
        grid=(2, nc),
        in_specs=[slab, slab, slab, slab, slab2, slab2,
                  pl.BlockSpec((npair, 1, LANES), lambda dd, c: (0, 0, 0))],
        out_specs=slab2,
        out_shape=jax.ShapeDtypeStruct((2, npair, n, LANES), F32),
        scratch_shapes=[pltpu.VMEM((npair * NB, LANES, LANES), F32)],
        compiler_params=_cparams(("arbitrary", "arbitrary")),
    )(r, k, v, kk, ew, asg, ka)


def _rw_out_kernel(x_ref, mod_ref, y_ref, r_ref, k_ref, v_ref, as_ref, gg_ref, lng_ref, lnb_ref,
                   ka_ref, rk_ref, wo_ref, o_ref):
    x = x_ref[...]
    d = x.shape[1]
    npair = d // LANES
    cat = lambda f: jnp.concatenate([f(p) for p in range(npair)], axis=1)
    y = cat(lambda p: y_ref[0, p] + y_ref[1, p])
    r = cat(lambda p: r_ref[p])
    k = cat(lambda p: k_ref[p])
    v = cat(lambda p: v_ref[p])
    am = cat(lambda p: 0.5 * (as_ref[0, p] + as_ref[1, p]))
    ones_bd = _head_ones()
    inv = 1.0 / RWKV_HEAD
    mean = _head_sum_bcast(y, ones_bd) * inv
    yc = y - mean
    var = _head_sum_bcast(yc * yc, ones_bd) * inv
    yn = yc * lax.rsqrt(var + GN_EPS) * lng_ref[...] + lnb_ref[...]
    k_mean = k * (1.0 + (am - 1.0) * ka_ref[...])
    bonus = _head_sum_bcast(r * k_mean * rk_ref[...], ones_bd) * v
    z = ((yn + bonus) * gg_ref[...]).astype(BF16)
    o_ref[...] = _residual(x, mod_ref[0, 2], _dot(z, wo_ref[...]))


def _rw_out_call(x, mod, y, r, k, v, asg, gg, lng, lnb, ka, rk, wo, *, nct):
    n, d = x.shape
    nt = n // TM
    npair = d // LANES
    slab = pl.BlockSpec((npair, TM, LANES), lambda i: (0, i, 0))
    slab2 = pl.BlockSpec((2, npair, TM, LANES), lambda i: (0, 0, i, 0))
    row = pl.BlockSpec((1, d), lambda i: (0, 0))
    return pl.pallas_call(
        _rw_out_kernel,
        grid=(nt,),
        in_specs=[
            pl.BlockSpec((TM, d), lambda i: (i, 0)),
            pl.BlockSpec((1, 6, NB, d), lambda i: (_is_lat(i, nct), 0, 0, 0)),
            slab2, slab, slab, slab, slab2,
            pl.BlockSpec((TM, d), lambda i: (i, 0)),
            row, row, row, row,
            pl.BlockSpec((d, d), lambda i: (0, 0)),
        ],
        out_specs=pl.BlockSpec((TM, d), lambda i: (i, 0)),
        out_shape=jax.ShapeDtypeStruct((n, d), F32),
        compiler_params=_cparams(("arbitrary",)),
    )(x, mod, y, r, k, v, asg, gg, lng, lnb, ka, rk, wo)


def _final_kernel(x_ref, g_ref, o_ref, slab_ref):
    x = x_ref[...]
    d = x.shape[1]
    ms = jnp.mean(x * x, axis=-1, keepdims=True)
    y = x * lax.rsqrt(ms + RMS_EPS) * g_ref[...]
    for p in range(d // LANES):
        slab_ref[p] = y[:, p * LANES:(p + 1) * LANES]
    for b in range(NB):
        for p in range(d // LANES):
            o_ref[b, :, p * LANES:(p + 1) * LANES] = slab_ref[p, pl.ds(b, TT, stride=NB), :]


def _final_call(x, g, *, tile0):
    n, d = x.shape
    nt = n // TM
    return pl.pallas_call(
        _final_kernel,
        grid=(nt - tile0,),
        in_specs=[pl.BlockSpec((TM, d), lambda i: (i + tile0, 0)),
                  pl.BlockSpec((1, d), lambda i: (0, 0))],
        out_specs=pl.BlockSpec((NB, TT, d), lambda i: (0, i, 0)),
        out_shape=jax.ShapeDtypeStruct((NB, (nt - tile0) * TT, d), F32),
        scratch_shapes=[pltpu.VMEM((d // LANES, TM, LANES), F32)],
        compiler_params=_cparams(("arbitrary",)),
    )(x, g)


def _time_major_kernel(c_ref, x_ref, o_ref, slab_ref, *, nct):
    d = o_ref.shape[1]

    def emit(src_ref):
        for b in range(NB):
            for p in range(d // LANES):
                slab_ref[p, pl.ds(b, TT, stride=NB), :] = src_ref[b, :, p * LANES:(p + 1) * LANES]
        for p in range(d // LANES):
            o_ref[:, p * LANES:(p + 1) * LANES] = slab_ref[p]

    @pl.when(pl.program_id(0) < nct)
    def _():
        emit(c_ref)

    @pl.when(pl.program_id(0) >= nct)
    def _():
        emit(x_ref)


def _time_major_call(ctx, x):
    _, lc, d = ctx.shape
    seq = x.shape[1]
    nct = lc // TT
    nt = (lc + seq) // TT
    return pl.pallas_call(
        functools.partial(_time_major_kernel, nct=nct),
        grid=(nt,),
        in_specs=[pl.BlockSpec((NB, TT, d), lambda i: (0, jnp.minimum(i, nct - 1), 0)),
                  pl.BlockSpec((NB, TT, d), lambda i: (0, jnp.maximum(i - nct, 0), 0))],
        out_specs=pl.BlockSpec((TM, d), lambda i: (i, 0)),
        out_shape=jax.ShapeDtypeStruct((nt * TM, d), F32),
        scratch_shapes=[pltpu.VMEM((d // LANES, TM, LANES), F32)],
        compiler_params=_cparams(("arbitrary",)),
    )(ctx, x)


def _row(v):
    return v.reshape(1, -1).astype(F32)


def _s5_layer(x, mod, g, p, *, nct, tile0):
    bb, cc, ab = _s5_params(p['lam_re'], p['lam_im'], p['log_dt'], p['b_re'], p['b_im'],
                            p['c_re'], p['c_im'])
    y = _s5_scan_call(x, mod, g, bb, cc, ab, nct=nct)
    return _s5_glu_call(x, mod, g, y, _row(p['d']), p['glu_w'].astype(BF16), _row(p['glu_b']),
                        nct=nct, tile0=tile0)


def _conformer_layer(x, mod, g, p, *, nct):
    u = _cv_glu_call(x, mod, g, p['pw1_w'].astype(BF16), _row(p['pw1_b']), nct=nct)
    return _cv_conv_call(u, x, mod, p['dw_w'].astype(F32), _row(p['dw_b']), _row(p['ln_g']),
                         _row(p['ln_b']), p['pw2_w'].astype(BF16), _row(p['pw2_b']), nct=nct)


def _blockdiag2(m):
    z = jnp.zeros_like(m[0])
    return jnp.concatenate([jnp.concatenate([m[0], z], axis=1),
                            jnp.concatenate([z, m[1]], axis=1)], axis=0)


def _rwkv_layer(x, mod, g, p, *, nct):
    n, d = x.shape
    npair = d // LANES
    w1 = jnp.concatenate([p['w1'][0], p['w1'][1]], axis=1).astype(BF16)
    w2 = _blockdiag2(p['w2']).astype(BF16)
    a1 = jnp.concatenate([p['a1'][0], p['a1'][1]], axis=1).astype(BF16)
    a2 = _blockdiag2(p['a2']).astype(BF16)
    gl = p['g1'].shape[1]
    glp = -(-gl // LANES) * LANES
    g1 = jnp.pad(p['g1'], ((0, 0), (0, glp - gl))).astype(BF16)
    g2 = jnp.pad(p['g2'], ((0, glp - gl), (0, 0))).astype(BF16)
    r, k, v, kk, ew, asg, gg = _rw_proj_call(
        x, mod, g, p['mu'].astype(F32), p['w_r'].astype(BF16), p['w_k'].astype(BF16),
        p['w_v'].astype(BF16), w1, w2, _row(p['w0']), a1, a2, _row(p['a0']), g1, g2,
        _row(p['k_k']), nct=nct)
    ka_slab = p['k_a'].astype(F32).reshape(npair, 1, LANES)
    y = _wkv_call(r, k, v, kk, ew, asg, ka_slab, nct_rows=nct * TM)
    return _rw_out_call(x, mod, y, r, k, v, asg, gg, _row(p['lnx_g']), _row(p['lnx_b']),
                        _row(p['k_a']), _row(p['r_k']), p['w_o'].astype(BF16), nct=nct)


def _forward(x, c, ctx, c_ctx, ada_w, ada_b, norm_mix_g, norm_ffn_g, final_g, s5, cv, rw,
             ff_w1, ff_w3, ff_w2, moe_router, moe_w1, moe_w3, moe_w2):
    bsz, seq, d = x.shape
    lc = ctx.shape[1]
    depth = ada_w.shape[0]
    assert bsz == NB and lc % TT == 0 and seq % TT == 0
    assert TT == GRID_W, "one token tile must hold exactly one latent raster row per batch entry"
    nct = lc // TT
    t_all = lc + seq
    n = t_all * NB

    xs = _time_major_call(ctx.astype(F32), x.astype(F32))

    cvec = jnp.zeros((2 * NB, d), F32).at[:NB].set(c).at[NB].set(c_ctx)
    m_all = _ada_call(cvec, ada_w.astype(F32), ada_b.astype(F32))
    m_all = m_all.reshape(depth, 2 * NB, 6, d)
    m_lat = jnp.swapaxes(m_all[:, :NB], 1, 2)
    m_ctx = jnp.broadcast_to(m_all[:, NB][:, :, None, :], (depth, 6, NB, d))
    mods = jnp.stack([m_ctx, m_lat], axis=1)

    ctx_rows = nct * TM
    for i in range(depth):
        last = i == depth - 1
        mod = mods[i]
        g_mix = _row(norm_mix_g[i])
        j = i // 3
        if i % 3 == 0:
            xs = _s5_layer(xs, mod, g_mix, {k_: v_[j] for k_, v_ in s5.items()}, nct=nct,
                           tile0=nct if last else 0)
            ctx_rows = 0 if last else ctx_rows
        elif i % 3 == 1:
            xs = _conformer_layer(xs, mod, g_mix, {k_: v_[j] for k_, v_ in cv.items()}, nct=nct)
        else:
            xs = _rwkv_layer(xs, mod, g_mix, {k_: v_[j] for k_, v_ in rw.items()}, nct=nct)
        fi = i // 2
        g_ffn = _row(norm_ffn_g[i])
        if i % 2 == 0:
            xs = _ffn_call(xs, mod, g_ffn, ff_w1, ff_w3, ff_w2, layer=fi, nct_rows=ctx_rows)
        else:
            xs = _moe_call(xs, mod, g_ffn, moe_router[fi], moe_w1, moe_w3, moe_w2, layer=fi,
                           nct_rows=ctx_rows)

    return _final_call(xs, _row(final_g), tile0=ctx_rows // TM)


def kernel(x, c, ctx, c_ctx, ada_w, ada_b, norm_mix_g, norm_ffn_g, final_g, s5_lambda_re, s5_lambda_im, s5_log_dt, s5_b_re, s5_b_im, s5_c_re, s5_c_im, s5_d, s5_glu_w, s5_glu_b, cv_pw1_w, cv_pw1_b, cv_dw_w, cv_dw_b, cv_ln_g, cv_ln_b, cv_pw2_w, cv_pw2_b, rw_mu, rw_w_r, rw_w_k, rw_w_v, rw_w_o, rw_w0, rw_w1, rw_w2, rw_a0, rw_a1, rw_a2, rw_g1, rw_g2, rw_k_k, rw_k_a, rw_r_k, rw_lnx_g, rw_lnx_b, ff_w1, ff_w3, ff_w2, moe_router, moe_w1, moe_w3, moe_w2):
    s5 = dict(lam_re=s5_lambda_re, lam_im=s5_lambda_im, log_dt=s5_log_dt, b_re=s5_b_re, b_im=s5_b_im,
              c_re=s5_c_re, c_im=s5_c_im, d=s5_d, glu_w=s5_glu_w, glu_b=s5_glu_b)
    cv = dict(pw1_w=cv_pw1_w, pw1_b=cv_pw1_b, dw_w=cv_dw_w, dw_b=cv_dw_b, ln_g=cv_ln_g, ln_b=cv_ln_b,
              pw2_w=cv_pw2_w, pw2_b=cv_pw2_b)
    rw = dict(mu=rw_mu, w_r=rw_w_r, w_k=rw_w_k, w_v=rw_w_v, w_o=rw_w_o, w0=rw_w0, w1=rw_w1, w2=rw_w2,
              a0=rw_a0, a1=rw_a1, a2=rw_a2, g1=rw_g1, g2=rw_g2, k_k=rw_k_k, k_a=rw_k_a, r_k=rw_r_k,
              lnx_g=rw_lnx_g, lnx_b=rw_lnx_b)
    return _forward(x, c, ctx, c_ctx, ada_w, ada_b, norm_mix_g, norm_ffn_g, final_g, s5, cv, rw,
                    ff_w1, ff_w3, ff_w2, moe_router, moe_w1, moe_w3, moe_w2)
```

```python
import functools
import math

import jax
import jax.numpy as jnp
from jax import lax
from jax.experimental import pallas as pl
from jax.experimental.pallas import tpu as pltpu

F32 = jnp.float32
BF16 = jnp.bfloat16

NB = 8
LANES = 128
TM = 512
TT = TM // NB
GRID_W = 64
CONV_WIDTH = 31
CONV_PAD = CONV_WIDTH // 2
HALO_ROWS = 128
S5_GROUP = 16
S5_STATE = 64
S5_JBLK = 8
RWKV_HEAD = 64
WKV_L = 64
WKV_GROUP = 8
RW_TM = 256
FFN_TM = 1024
MOE_T = 1024
MOE_BLK = 32
MOE_SLOTS = 64
RMS_EPS = 1e-6
LN_EPS = 1e-5
GN_EPS = 64e-5
KK_EPS = 1e-12
LAMBDA_RE_MAX = -1e-4
VMEM_LIMIT = 56 * 1024 * 1024


def _cparams(sem):
    return pltpu.CompilerParams(dimension_semantics=sem, vmem_limit_bytes=VMEM_LIMIT)


def _dot(a, b):
    return jnp.dot(a, b, preferred_element_type=F32)


def _dot_nt(a, b):
    return lax.dot_general(a, b, (((1,), (1,)), ((), ())), preferred_element_type=F32)


def _split3(x):
    h1 = x.astype(BF16)
    r1 = x - h1.astype(F32)
    h2 = r1.astype(BF16)
    h3 = (r1 - h2.astype(F32)).astype(BF16)
    return h1, h2, h3


def _dot_x3(a, b):
    a1 = a.astype(BF16)
    a2 = (a - a1.astype(F32)).astype(BF16)
    b1 = b.astype(BF16)
    b2 = (b - b1.astype(F32)).astype(BF16)
    return _dot(a1, b1) + _dot(a1, b2) + _dot(a2, b1)


def _sigmoid(x):
    return jax.nn.sigmoid(x)


def _norm_mod(x, g, shift, scale):
    rows, d = x.shape
    ms = jnp.mean(x * x, axis=-1, keepdims=True)
    y = x * lax.rsqrt(ms + RMS_EPS) * g
    y3 = y.reshape(rows // NB, NB, d)
    return (y3 * (1.0 + scale)[None] + shift[None]).reshape(rows, d)


def _residual(x, gate, y):
    rows, d = x.shape
    y3 = y.reshape(rows // NB, NB, d) * gate[None]
    return x + y3.reshape(rows, d)


def _is_lat(i, nct):
    return jnp.where(i >= nct, 1, 0).astype(jnp.int32)


def _ada_kernel(s_ref, w_ref, b_ref, o_ref):
    s = s_ref[...]
    s = s * _sigmoid(s)
    o_ref[0] = _dot(s.astype(BF16), w_ref[0].astype(BF16)) + b_ref[0]


def _ada_call(cvec, ada_w, ada_b):
    depth, d, d6 = ada_w.shape
    tn = 1536
    return pl.pallas_call(
        _ada_kernel,
        grid=(depth, d6 // tn),
        in_specs=[
            pl.BlockSpec((2 * NB, d), lambda l, j: (0, 0)),
            pl.BlockSpec((1, d, tn), lambda l, j: (l, 0, j)),
            pl.BlockSpec((1, 1, tn), lambda l, j: (l, 0, j)),
        ],
        out_specs=pl.BlockSpec((1, 2 * NB, tn), lambda l, j: (l, 0, j)),
        out_shape=jax.ShapeDtypeStruct((depth, 2 * NB, d6), F32),
        compiler_params=_cparams(("arbitrary", "arbitrary")),
    )(cvec, ada_w, ada_b.reshape(depth, 1, d6))


def _ffn_kernel(x_ref, mod_ref, g_ref, w1_ref, w3_ref, w2_ref, o_ref, hb_ref, acc_ref, *, n_k):
    k = pl.program_id(1)

    @pl.when(k == 0)
    def _():
        h = _norm_mod(x_ref[...], g_ref[...], mod_ref[0, 3], mod_ref[0, 4])
        hb_ref[...] = h.astype(BF16)
        acc_ref[...] = jnp.zeros_like(acc_ref)

    hb = hb_ref[...]
    a = _dot(hb, w1_ref[0].astype(BF16))
    b = _dot(hb, w3_ref[0].astype(BF16))
    acc_ref[...] += _dot((a * _sigmoid(a) * b).astype(BF16), w2_ref[0].astype(BF16))

    @pl.when(k == n_k - 1)
    def _():
        o_ref[...] = _residual(x_ref[...], mod_ref[0, 5], acc_ref[...])


def _ffn_call(x, mod, g, w1, w3, w2, *, layer, nct_rows):
    n, d = x.shape
    f = w1.shape[2]
    tm = FFN_TM
    tf = 512
    n_k = f // tf
    nct = nct_rows // tm
    return pl.pallas_call(
        functools.partial(_ffn_kernel, n_k=n_k),
        grid=(n // tm, n_k),
        in_specs=[
            pl.BlockSpec((tm, d), lambda i, k: (i, 0)),
            pl.BlockSpec((1, 6, NB, d), lambda i, k: (_is_lat(i, nct), 0, 0, 0)),
            pl.BlockSpec((1, d), lambda i, k: (0, 0)),
            pl.BlockSpec((1, d, tf), lambda i, k: (layer, 0, k)),
            pl.BlockSpec((1, d, tf), lambda i, k: (layer, 0, k)),
            pl.BlockSpec((1, tf, d), lambda i, k: (layer, k, 0)),
        ],
        out_specs=pl.BlockSpec((tm, d), lambda i, k: (i, 0)),
        out_shape=jax.ShapeDtypeStruct((n, d), F32),
        scratch_shapes=[pltpu.VMEM((tm, d), BF16), pltpu.VMEM((tm, d), F32)],
        compiler_params=_cparams(("arbitrary", "arbitrary")),
    )(x, mod, g, w1, w3, w2)


def _top2(logits, n_e):
    lane = lax.broadcasted_iota(jnp.int32, logits.shape, 1).astype(F32)
    neg = jnp.float32(-3.0e38)
    lg = jnp.where(lane < n_e, logits, neg)
    m1 = jnp.max(lg, axis=-1, keepdims=True)
    i1 = jnp.min(jnp.where(lg == m1, lane, 1.0e9), axis=-1, keepdims=True)
    lg2 = jnp.where(lane == i1, neg, lg)
    m2 = jnp.max(lg2, axis=-1, keepdims=True)
    i2 = jnp.min(jnp.where(lg2 == m2, lane, 1.0e9), axis=-1, keepdims=True)
    e2 = jnp.exp(m2 - m1)
    den = 1.0 + e2
    return lane, i1, i2, 1.0 / den, e2 / den


def _moe_route_kernel(x_ref, mod_ref, g_ref, router_ref, ltri_ref, z_ref, dest_ref, cnt_ref,
                      hb_ref, *, n_e, cap):
    t, d = x_ref.shape
    h = _norm_mod(x_ref[...], g_ref[...], mod_ref[0, 3], mod_ref[0, 4])
    hb_ref[...] = h.astype(BF16)
    lane, i1, i2, p1, p2 = _top2(_dot_x3(h, router_ref[...]), n_e)
    sel1 = lane == i1
    sel2 = lane == i2
    c = jnp.where(sel1, 1.0, 0.0) + jnp.where(sel2, 1.0, 0.0)
    rank = _dot(ltri_ref[...], c.astype(BF16))
    cnt = jnp.sum(c, axis=0, keepdims=True)
    seg = jnp.floor((cnt + (MOE_BLK - 1)) * (1.0 / MOE_BLK)) * MOE_BLK
    ur = lax.broadcasted_iota(jnp.int32, (LANES, LANES), 0)
    uc = lax.broadcasted_iota(jnp.int32, (LANES, LANES), 1)
    before = jnp.where(ur < uc, 1.0, 0.0).astype(BF16)
    start = _dot(jnp.broadcast_to(seg, (NB, LANES)).astype(BF16), before)[0:1]
    pos = start + rank
    d1 = jnp.sum(jnp.where(sel1, pos, 0.0), axis=-1, keepdims=True)
    d2 = jnp.sum(jnp.where(sel2, pos, 0.0), axis=-1, keepdims=True)
    packed = (jnp.where(lane == 0.0, d1, 0.0) + jnp.where(lane == 1.0, d2, 0.0)
              + jnp.where(lane == 2.0, p1, 0.0) + jnp.where(lane == 3.0, p2, 0.0))
    dest_ref[...] = packed
    cnt_ref[0] = jnp.broadcast_to(cnt, (NB, LANES))
    packed_t = packed.T
    d1r = packed_t[0:1]
    d2r = packed_t[1:2]
    p1r = packed_t[2:3]
    p2r = packed_t[3:4]
    rc = 256

    def chunk(ci, carry):
        r0 = pl.multiple_of(ci * rc, rc)
        row = (lax.broadcasted_iota(jnp.int32, (rc, t), 0) + r0).astype(F32)
        m1 = row == d1r
        m2 = row == d2r
        perm = (jnp.where(m1, 1.0, 0.0) + jnp.where(m2, 1.0, 0.0)).astype(BF16)
        zc = _dot(perm, hb_ref[...])
        gate = jnp.sum(jnp.where(m1, p1r, 0.0) + jnp.where(m2, p2r, 0.0), axis=-1, keepdims=True)
        ghi = gate.astype(BF16).astype(F32)
        glo = gate - ghi
        le = lax.broadcasted_iota(jnp.int32, (rc, LANES), 1)
        ext = jnp.where(le == 0, ghi, jnp.where(le == 1, glo, 0.0))
        z_ref[pl.ds(r0, rc), 0:d] = zc.astype(BF16)
        z_ref[pl.ds(r0, rc), d:d + LANES] = ext.astype(BF16)
        return carry

    lax.fori_loop(0, cap // rc, chunk, 0)


def _moe_experts_kernel(tin_ref, se_ref, nv_ref, *refs, n_k, d):
    z_refs = refs[:MOE_SLOTS]
    w1_ref, w3_ref, w2_ref, o_ref, xs_ref, acc_ref = refs[MOE_SLOTS:]
    s = pl.program_id(0)
    k = pl.program_id(1)
    nv = nv_ref[s]
    half = MOE_SLOTS // 2
    hrows = half * MOE_BLK

    @pl.when((nv > 0) & (k == 0))
    def _():
        for j in range(MOE_SLOTS):
            xs_ref[j * MOE_BLK:(j + 1) * MOE_BLK] = z_refs[j][...]
        acc_ref[...] = jnp.zeros_like(acc_ref)

    for h in range(2):
        @pl.when(nv > h * half)
        def _():
            rows = slice(h * hrows, (h + 1) * hrows)
            hb = xs_ref[rows, 0:d]
            ext = xs_ref[rows, d:d + LANES].astype(F32)
            le = lax.broadcasted_iota(jnp.int32, ext.shape, 1)
            gate = jnp.sum(jnp.where(le < 2, ext, 0.0), axis=-1, keepdims=True)
            a = _dot(hb, w1_ref[0, 0].astype(BF16))
            b = _dot(hb, w3_ref[0, 0].astype(BF16))
            act = a * _sigmoid(a) * b * gate
            acc_ref[rows] += _dot(act.astype(BF16), w2_ref[0, 0].astype(BF16))

    @pl.when(k == n_k - 1)
    def _():
        o_ref[...] = jnp.where(nv > 0, acc_ref[...], 0.0).astype(BF16)


def _moe_combine_kernel(tc_ref, *refs, nblk, cap):
    zb_refs = refs[:nblk]
    dest_ref, x_ref, mod_ref, o_ref, zbuf_ref = refs[nblk:]
    t, d = x_ref.shape
    for j in range(nblk):
        zbuf_ref[j * MOE_BLK:(j + 1) * MOE_BLK] = zb_refs[j][...]
    packed = dest_ref[...]
    lane = lax.broadcasted_iota(jnp.int32, packed.shape, 1)
    d1 = jnp.sum(jnp.where(lane == 0, packed, 0.0), axis=-1, keepdims=True)
    d2 = jnp.sum(jnp.where(lane == 1, packed, 0.0), axis=-1, keepdims=True)
    kc = 256
    acc = jnp.zeros((t, d), F32)
    for c0 in range(0, cap, kc):
        col = (lax.broadcasted_iota(jnp.int32, (t, kc), 1) + c0).astype(F32)
        sel = (jnp.where(col == d1, 1.0, 0.0) + jnp.where(col == d2, 1.0, 0.0)).astype(BF16)
        acc = acc + _dot(sel, zbuf_ref[c0:c0 + kc])
    o_ref[...] = _residual(x_ref[...], mod_ref[0, 5], acc)


def _moe_tables(cnt, *, n_tiles, n_e, nblk, n_steps):
    i32 = jnp.int32
    e_ids = jnp.arange(n_e, dtype=i32)
    t_ids = jnp.arange(n_tiles, dtype=i32)
    pc = (cnt + (MOE_BLK - 1)) // MOE_BLK
    lb = jnp.cumsum(pc, axis=1) - pc
    cum = jnp.cumsum(pc, axis=0)
    cum_ex = cum - pc
    nb_e = cum[-1]
    steps_e = (nb_e + (MOE_SLOTS - 1)) // MOE_SLOTS
    ends = jnp.cumsum(steps_e)
    step0 = ends - steps_e
    sidx = jnp.arange(n_steps, dtype=i32)
    step_e = jnp.minimum(jnp.sum((sidx[:, None] >= ends[None, :]).astype(i32), axis=1), n_e - 1)
    oh_e = (step_e[:, None] == e_ids[None, :]).astype(i32)
    by_step = lambda v: jnp.sum(oh_e * v[None, :], axis=1)
    by_step_tile = lambda m: jnp.sum(oh_e[:, None, :] * m[None, :, :], axis=2)
    rank = ((sidx - by_step(step0))[:, None] * MOE_SLOTS
            + jnp.arange(MOE_SLOTS, dtype=i32)[None, :])
    cum_s = by_step_tile(cum)
    tile = jnp.sum((cum_s[:, None, :] <= rank[:, :, None]).astype(i32), axis=2)
    nvalid = jnp.where(sidx < ends[-1], jnp.clip(by_step(nb_e) - rank[:, 0], 0, MOE_SLOTS), 0)
    ok = rank < (rank[:, :1] + nvalid[:, None])
    oh_t = (jnp.minimum(tile, n_tiles - 1)[:, :, None] == t_ids[None, None, :]).astype(i32)
    at_tile = lambda m: jnp.sum(oh_t * by_step_tile(m)[:, None, :], axis=2)
    src = jnp.sum(oh_t * t_ids, axis=2) * nblk + at_tile(lb) + rank - at_tile(cum_ex)
    tin = jnp.where(ok, src, 0).reshape(-1).astype(i32)
    lblk = jnp.arange(nblk, dtype=i32)[None, :]
    seg_end = lb + pc
    e_l = jnp.minimum(jnp.sum((seg_end[:, None, :] <= lblk[:, :, None]).astype(i32), axis=2), n_e - 1)
    oh_l = (e_l[:, :, None] == e_ids[None, None, :]).astype(i32)
    r_l = jnp.sum(oh_l * (cum_ex - lb)[:, None, :], axis=2) + lblk
    flat = (jnp.sum(oh_l * step0, axis=2) + r_l // MOE_SLOTS) * MOE_SLOTS + r_l % MOE_SLOTS
    tcomb = jnp.where(lblk < seg_end[:, -1:], flat, 0).reshape(-1).astype(i32)
    return tin, tcomb, step_e.astype(i32), nvalid.astype(i32)


def _moe_call(x, mod, g, router, w1, w3, w2, *, layer, nct_rows):
    n, d = x.shape
    _, n_e, _, f = w1.shape
    t = MOE_T
    assert n % t == 0 and nct_rows % t == 0
    n_tiles = n // t
    nct = nct_rows // t
    cap = 2 * t + n_e * MOE_BLK
    nblk = cap // MOE_BLK
    rows = MOE_SLOTS * MOE_BLK
    n_steps = -(-(n_tiles * nblk) // MOE_SLOTS) + n_e
    tf = 512
    n_k = f // tf
    router_p = jnp.pad(router.astype(F32), ((0, 0), (0, LANES - n_e)))
    ltri = jnp.tril(jnp.ones((t, t), F32), k=-1).astype(BF16)
    mod_spec = lambda off: pl.BlockSpec((1, 6, NB, d), lambda i, *_: (_is_lat(i, nct), 0, 0, 0))

    z, dest, cnt = pl.pallas_call(
        functools.partial(_moe_route_kernel, n_e=n_e, cap=cap),
        grid=(n_tiles,),
        in_specs=[
            pl.BlockSpec((t, d), lambda i: (i, 0)),
            mod_spec(0),
            pl.BlockSpec((1, d), lambda i: (0, 0)),
            pl.BlockSpec((d, LANES), lambda i: (0, 0)),
            pl.BlockSpec((t, t), lambda i: (0, 0)),
        ],
        out_specs=[
            pl.BlockSpec((cap, d + LANES), lambda i: (i, 0)),
            pl.BlockSpec((t, LANES), lambda i: (i, 0)),
            pl.BlockSpec((1, NB, LANES), lambda i: (i, 0, 0)),
        ],
        out_shape=[
            jax.ShapeDtypeStruct((n_tiles * cap, d + LANES), BF16),
            jax.ShapeDtypeStruct((n, LANES), F32),
            jax.ShapeDtypeStruct((n_tiles, NB, LANES), F32),
        ],
        scratch_shapes=[pltpu.VMEM((t, d), BF16)],
        compiler_params=_cparams(("arbitrary",)),
    )(x, mod, g, router_p, ltri)

    tin, tcomb, step_e, nvalid = _moe_tables(cnt[:, 0, :n_e].astype(jnp.int32), n_tiles=n_tiles,
                                            n_e=n_e, nblk=nblk, n_steps=n_steps)

    def kk(s, k, nv_r):
        return jnp.where(nv_r[s] > 0, k, n_k - 1)

    def z_spec(j):
        return pl.BlockSpec((MOE_BLK, d + LANES), lambda s, k, tin_r, se_r, nu_r: (tin_r[s * MOE_SLOTS + j], 0))

    zo = pl.pallas_call(
        functools.partial(_moe_experts_kernel, n_k=n_k, d=d),
        grid_spec=pltpu.PrefetchScalarGridSpec(
            num_scalar_prefetch=3,
            grid=(n_steps, n_k),
            in_specs=[z_spec(j) for j in range(MOE_SLOTS)] + [
                pl.BlockSpec((1, 1, d, tf), lambda s, k, tin_r, se_r, nu_r: (layer, se_r[s], 0, kk(s, k, nu_r))),
                pl.BlockSpec((1, 1, d, tf), lambda s, k, tin_r, se_r, nu_r: (layer, se_r[s], 0, kk(s, k, nu_r))),
                pl.BlockSpec((1, 1, tf, d), lambda s, k, tin_r, se_r, nu_r: (layer, se_r[s], kk(s, k, nu_r), 0)),
            ],
            out_specs=pl.BlockSpec((rows, d), lambda s, k, tin_r, se_r, nu_r: (s, 0)),
            scratch_shapes=[pltpu.VMEM((rows, d + LANES), BF16), pltpu.VMEM((rows, d), F32)],
        ),
        out_shape=jax.ShapeDtypeStruct((n_steps * rows, d), BF16),
        compiler_params=_cparams(("arbitrary", "arbitrary")),
    )(tin, step_e, nvalid, *([z] * MOE_SLOTS), w1, w3, w2)

    def zo_spec(j):
        return pl.BlockSpec((MOE_BLK, d), lambda i, tc_r: (tc_r[i * nblk + j], 0))

    return pl.pallas_call(
        functools.partial(_moe_combine_kernel, nblk=nblk, cap=cap),
        grid_spec=pltpu.PrefetchScalarGridSpec(
            num_scalar_prefetch=1,
            grid=(n_tiles,),
            in_specs=[zo_spec(j) for j in range(nblk)] + [
                pl.BlockSpec((t, LANES), lambda i, tc_r: (i, 0)),
                pl.BlockSpec((t, d), lambda i, tc_r: (i, 0)),
                mod_spec(0),
            ],
            out_specs=pl.BlockSpec((t, d), lambda i, tc_r: (i, 0)),
            scratch_shapes=[pltpu.VMEM((cap, d), BF16)],
        ),
        out_shape=jax.ShapeDtypeStruct((n, d), F32),
        compiler_params=_cparams(("arbitrary",)),
    )(tcomb, *([zo] * nblk), dest, x, mod)


def _dir_tile(d, s, nct, nt):
    bwd = jnp.where(s < nct, nct - 1 - s, nt - 1 + nct - s)
    return jnp.where(d == 0, s, bwd)


def _s5_scan_kernel(x_ref, mod_ref, g_ref, bb_ref, cc_ref, ab_ref, y_ref,
                    hb_ref, bu0_ref, bu1_ref, bu2_ref, st_ref):
    d = pl.program_id(0)
    s = pl.program_id(1)
    half = S5_JBLK * S5_STATE
    bufs = (bu0_ref, bu1_ref, bu2_ref)

    @pl.when(s == 0)
    def _():
        st_ref[...] = jnp.zeros_like(st_ref)

    h = _norm_mod(x_ref[...], g_ref[...], mod_ref[0, 0], mod_ref[0, 1])
    hb_ref[...] = h.astype(BF16)

    def project(j):
        bufs[j % 3][...] = _dot(hb_ref[:, j * LANES:(j + 1) * LANES], bb_ref[0, j])

    def scan(j):
        buf = bufs[j % 3]
        ar = ab_ref[0, j, 0]
        ai = ab_ref[0, j, 1]
        re = st_ref[j, 0]
        im = st_ref[j, 1]
        for t in range(TT):
            tt = t + d * (TT - 1 - 2 * t)
            r0 = pl.multiple_of(tt * NB, NB)
            bre = buf[pl.ds(r0, NB), 0:half]
            bim = buf[pl.ds(r0, NB), half:2 * half]
            re, im = ar * re - ai * im + bre, ar * im + ai * re + bim
            buf[pl.ds(r0, NB), 0:half] = re
            buf[pl.ds(r0, NB), half:2 * half] = im
        st_ref[j, 0] = re
        st_ref[j, 1] = im

    def readout(j):
        y_ref[0, j] = _dot(bufs[j % 3][...].astype(BF16), cc_ref[0, j])

    for i in range(S5_JBLK + 2):
        if i < S5_JBLK:
            project(i)
        if 0 <= i - 1 < S5_JBLK:
            scan(i - 1)
        if 0 <= i - 2 < S5_JBLK:
            readout(i - 2)


def _s5_scan_call(x, mod, g, bb, cc, ab, *, nct):
    n, d = x.shape
    nt = n // TM
    half = S5_JBLK * S5_STATE
    tile = lambda dd, s: _dir_tile(dd, s, nct, nt)
    return pl.pallas_call(
        _s5_scan_kernel,
        grid=(2, nt),
        in_specs=[
            pl.BlockSpec((TM, d), lambda dd, s: (tile(dd, s), 0)),
            pl.BlockSpec((1, 6, NB, d), lambda dd, s: (_is_lat(tile(dd, s), nct), 0, 0, 0)),
            pl.BlockSpec((1, d), lambda dd, s: (0, 0)),
            pl.BlockSpec((1, S5_JBLK, LANES, 2 * half), lambda dd, s: (dd, 0, 0, 0)),
            pl.BlockSpec((1, S5_JBLK, 2 * half, LANES), lambda dd, s: (dd, 0, 0, 0)),
            pl.BlockSpec((1, S5_JBLK, 2, NB, half), lambda dd, s: (dd, 0, 0, 0, 0)),
        ],
        out_specs=pl.BlockSpec((1, S5_JBLK, TM, LANES), lambda dd, s: (dd, 0, tile(dd, s), 0)),
        out_shape=jax.ShapeDtypeStruct((2, S5_JBLK, n, LANES), F32),
        scratch_shapes=[
            pltpu.VMEM((TM, d), BF16),
            pltpu.VMEM((TM, 2 * half), F32),
            pltpu.VMEM((TM, 2 * half), F32),
            pltpu.VMEM((TM, 2 * half), F32),
            pltpu.VMEM((S5_JBLK, 2, NB, half), F32),
        ],
        compiler_params=_cparams(("arbitrary", "arbitrary")),
    )(x, mod, g, bb, cc, ab)


def _gelu_tanh(x):
    c = math.sqrt(2.0 / math.pi)
    return 0.5 * x * (1.0 + jnp.tanh(c * (x + 0.044715 * (x * x * x))))


def _s5_glu_kernel(x_ref, mod_ref, g_ref, y_ref, dsk_ref, w_ref, b_ref, o_ref):
    x = x_ref[...]
    d = x.shape[1]
    h = _norm_mod(x, g_ref[...], mod_ref[0, 0], mod_ref[0, 1])
    y = jnp.concatenate([y_ref[0, j] + y_ref[1, j] for j in range(S5_JBLK)], axis=1)
    z = _gelu_tanh(y + dsk_ref[...] * h).astype(BF16)
    zz = _dot(z, w_ref[...]) + b_ref[...]
    out = zz[:, :d] * _sigmoid(zz[:, d:])
    o_ref[...] = _residual(x, mod_ref[0, 2], out)


def _s5_glu_call(x, mod, g, y, dsk, w, b, *, nct, tile0):
    n, d = x.shape
    nt = n // TM
    return pl.pallas_call(
        _s5_glu_kernel,
        grid=(nt - tile0,),
        in_specs=[
            pl.BlockSpec((TM, d), lambda i: (i + tile0, 0)),
            pl.BlockSpec((1, 6, NB, d), lambda i: (_is_lat(i + tile0, nct), 0, 0, 0)),
            pl.BlockSpec((1, d), lambda i: (0, 0)),
            pl.BlockSpec((2, S5_JBLK, TM, LANES), lambda i: (0, 0, i + tile0, 0)),
            pl.BlockSpec((1, d), lambda i: (0, 0)),
            pl.BlockSpec((d, 2 * d), lambda i: (0, 0)),
            pl.BlockSpec((1, 2 * d), lambda i: (0, 0)),
        ],
        out_specs=pl.BlockSpec((TM, d), lambda i: (i, 0)),
        out_shape=jax.ShapeDtypeStruct((n - tile0 * TM, d), F32),
        compiler_params=_cparams(("arbitrary",)),
    )(x, mod, g, y, dsk, w, b)


def _s5_params(lam_re, lam_im, log_dt, b_re, b_im, c_re, c_im):
    lam_re = jnp.minimum(lam_re.astype(F32), LAMBDA_RE_MAX)
    lam_im = lam_im.astype(F32)
    dt = jnp.exp(log_dt.astype(F32))[..., None]
    mag = jnp.exp(lam_re * dt)
    abar_re = mag * jnp.cos(lam_im * dt)
    abar_im = mag * jnp.sin(lam_im * dt)
    inv_den = 1.0 / (lam_re * lam_re + lam_im * lam_im)
    gam_re = ((abar_re - 1.0) * lam_re + abar_im * lam_im) * inv_den
    gam_im = (abar_im * lam_re - (abar_re - 1.0) * lam_im) * inv_den
    b_re = b_re.astype(F32)
    b_im = b_im.astype(F32)
    bb_re = gam_re[..., None] * b_re - gam_im[..., None] * b_im
    bb_im = gam_re[..., None] * b_im + gam_im[..., None] * b_re
    g_all = lam_re.shape[1]
    nj = g_all // S5_JBLK
    eye = jnp.eye(S5_JBLK, dtype=F32)

    def in_mat(m):
        m = m.reshape(2, nj, S5_JBLK, S5_STATE, S5_GROUP)
        m = jnp.einsum('djgnp,gh->djgphn', m, eye)
        return m.reshape(2, nj, S5_JBLK * S5_GROUP, S5_JBLK * S5_STATE)

    def out_mat(m):
        m = m.reshape(2, nj, S5_JBLK, S5_GROUP, S5_STATE)
        m = jnp.einsum('djgpn,gh->djgnhp', m, eye)
        return m.reshape(2, nj, S5_JBLK * S5_STATE, S5_JBLK * S5_GROUP)

    bb = jnp.concatenate([in_mat(bb_re), in_mat(bb_im)], axis=-1).astype(BF16)
    cc = jnp.concatenate([out_mat(c_re.astype(F32)), -out_mat(c_im.astype(F32))], axis=-2).astype(BF16)
    half = S5_JBLK * S5_STATE
    ab = jnp.stack([abar_re.reshape(2, nj, half), abar_im.reshape(2, nj, half)], axis=2)
    ab = jnp.broadcast_to(ab[:, :, :, None, :], (2, nj, 2, NB, half))
    return bb, cc, ab


def _cv_glu_kernel(x_ref, mod_ref, g_ref, w_ref, b_ref, o_ref):
    x = x_ref[...]
    d = x.shape[1]
    h = _norm_mod(x, g_ref[...], mod_ref[0, 0], mod_ref[0, 1]).astype(BF16)
    u = _dot(h, w_ref[...]) + b_ref[...]
    o_ref[...] = u[:, :d] * _sigmoid(u[:, d:])


def _cv_glu_call(x, mod, g, w, b, *, nct):
    n, d = x.shape
    nt = n // TM
    return pl.pallas_call(
        _cv_glu_kernel,
        grid=(nt,),
        in_specs=[
            pl.BlockSpec((TM, d), lambda i: (i, 0)),
            pl.BlockSpec((1, 6, NB, d), lambda i: (_is_lat(i, nct), 0, 0, 0)),
            pl.BlockSpec((1, d), lambda i: (0, 0)),
            pl.BlockSpec((d, 2 * d), lambda i: (0, 0)),
            pl.BlockSpec((1, 2 * d), lambda i: (0, 0)),
        ],
        out_specs=pl.BlockSpec((TM, d), lambda i: (i, 0)),
        out_shape=jax.ShapeDtypeStruct((n, d), F32),
        compiler_params=_cparams(("arbitrary",)),
    )(x, mod, g, w, b)


def _cv_conv_kernel(u_ref, up_ref, un_ref, x_ref, mod_ref, dw_ref, dwb_ref, lng_ref, lnb_ref,
                    w_ref, b_ref, o_ref, pad_ref, cv_ref, *, nct):
    i = pl.program_id(0)
    halo = CONV_PAD * NB
    prev_ok = (i > 0) & (i < nct)
    next_ok = i < nct - 1
    pad_ref[0:halo] = jnp.where(prev_ok, up_ref[HALO_ROWS - halo:HALO_ROWS], 0.0)
    pad_ref[halo:halo + TM] = u_ref[...]
    pad_ref[halo + TM:halo + TM + halo] = jnp.where(next_ok, un_ref[0:halo], 0.0)

    rb = 32
    lw = 512
    d = dwb_ref.shape[1]

    def body(c, _):
        r0 = pl.multiple_of(c * rb, rb)
        for l0 in range(0, d, lw):
            bias = jnp.broadcast_to(dwb_ref[:, l0:l0 + lw], (NB, lw))
            acc = [bias] * (rb // NB)
            for k in range(CONV_WIDTH):
                w = dw_ref[k, :, l0:l0 + lw]
                for g in range(rb // NB):
                    acc[g] = acc[g] + w * pad_ref[pl.ds(r0 + (k + g) * NB, NB), l0:l0 + lw]
            for g in range(rb // NB):
                cv_ref[pl.ds(r0 + g * NB, NB), l0:l0 + lw] = acc[g]
        return 0

    lax.fori_loop(0, TM // rb, body, 0)

    c = cv_ref[...]
    mu = jnp.mean(c, axis=-1, keepdims=True)
    cc = c - mu
    var = jnp.mean(cc * cc, axis=-1, keepdims=True)
    y = cc * lax.rsqrt(var + LN_EPS) * lng_ref[...] + lnb_ref[...]
    y = (y * _sigmoid(y)).astype(BF16)
    out = _dot(y, w_ref[...]) + b_ref[...]
    o_ref[...] = _residual(x_ref[...], mod_ref[0, 2], out)


def _cv_conv_call(u, x, mod, dw_w, dw_b, ln_g, ln_b, w, b, *, nct):
    n, d = x.shape
    nt = n // TM
    hb = TM // HALO_ROWS
    nhb = n // HALO_ROWS
    kern = functools.partial(_cv_conv_kernel, nct=nct)
    row = lambda i: (0, 0)
    return pl.pallas_call(
        kern,
        grid=(nt,),
        in_specs=[
            pl.BlockSpec((TM, d), lambda i: (i, 0)),
            pl.BlockSpec((HALO_ROWS, d), lambda i: (jnp.maximum(i * hb - 1, 0), 0)),
            pl.BlockSpec((HALO_ROWS, d), lambda i: (jnp.minimum((i + 1) * hb, nhb - 1), 0)),
            pl.BlockSpec((TM, d), lambda i: (i, 0)),
            pl.BlockSpec((1, 6, NB, d), lambda i: (_is_lat(i, nct), 0, 0, 0)),
            pl.BlockSpec((CONV_WIDTH, NB, d), lambda i: (0, 0, 0)),
            pl.BlockSpec((1, d), row),
            pl.BlockSpec((1, d), row),
            pl.BlockSpec((1, d), row),
            pl.BlockSpec((d, d), row),
            pl.BlockSpec((1, d), row),
        ],
        out_specs=pl.BlockSpec((TM, d), lambda i: (i, 0)),
        out_shape=jax.ShapeDtypeStruct((n, d), F32),
        scratch_shapes=[
            pltpu.VMEM((TM + 2 * CONV_PAD * NB, d), F32),
            pltpu.VMEM((TM, d), F32),
        ],
        compiler_params=_cparams(("arbitrary",)),
    )(u, u, u, x, mod, jnp.broadcast_to(dw_w[:, None, :], (CONV_WIDTH, NB, d)), dw_b, ln_g, ln_b, w, b)


def _head_ones():
    w = 2 * LANES
    r = lax.broadcasted_iota(jnp.int32, (w, w), 0) // RWKV_HEAD
    c = lax.broadcasted_iota(jnp.int32, (w, w), 1) // RWKV_HEAD
    return jnp.where(r == c, 1.0, 0.0).astype(BF16)


def _head_sum_bcast(x, ones_bd):
    hi = x.astype(BF16)
    lo = (x - hi.astype(F32)).astype(BF16)
    w = ones_bd.shape[0]
    cols = [_dot(hi[:, c0:c0 + w], ones_bd) + _dot(lo[:, c0:c0 + w], ones_bd)
            for c0 in range(0, x.shape[1], w)]
    return jnp.concatenate(cols, axis=1)


def _rw_proj_kernel(x_ref, xp_ref, xn_ref, mod_ref, g_ref, mu_ref, wr_ref, wk_ref, wv_ref,
                    w1_ref, w2_ref, w0_ref, a1_ref, a2_ref, a0_ref, g1_ref, g2_ref, kkw_ref,
                    r_ref, k_ref, v_ref, kk_ref, ew_ref, as_ref, gg_ref, hp_ref, *, nct, nt):
    i = pl.program_id(0)
    tm, d = x_ref.shape
    g = g_ref[...]
    sh = mod_ref[0, 0]
    sc = mod_ref[0, 1]
    h = _norm_mod(x_ref[...], g, sh, sc)
    hp = _norm_mod(xp_ref[...], g, sh, sc)
    hn = _norm_mod(xn_ref[...], g, sh, sc)
    prev_ok = (i != 0) & (i != nct)
    next_ok = (i != nct - 1) & (i != nt - 1)
    hp_ref[0:NB] = jnp.where(prev_ok, hp, 0.0)
    hp_ref[NB:NB + tm] = h
    hp_ref[NB + tm:NB + tm + NB] = jnp.where(next_ok, hn, 0.0)
    xx = 0.5 * (hp_ref[0:tm] + hp_ref[2 * NB:2 * NB + tm]) - h

    def mix(j):
        return (h + xx * mu_ref[j]).astype(BF16)

    r = _dot(mix(0), wr_ref[...])
    k = _dot(mix(2), wk_ref[...])
    v = _dot(mix(3), wv_ref[...])
    zw = w0_ref[...] + _dot(jnp.tanh(_dot(mix(1), w1_ref[...])).astype(BF16), w2_ref[...])
    ew = _sigmoid(zw) * math.exp(-0.5)
    asg = _sigmoid(a0_ref[...] + _dot(_dot(mix(4), a1_ref[...]).astype(BF16), a2_ref[...]))
    gg_ref[...] = _dot(_sigmoid(_dot(mix(5), g1_ref[...])).astype(BF16), g2_ref[...])

    kk = k * kkw_ref[...]
    kk = kk * lax.rsqrt(_head_sum_bcast(kk * kk, _head_ones()) + KK_EPS)

    for p in range(d // LANES):
        sl = slice(p * LANES, (p + 1) * LANES)
        r_ref[p] = r[:, sl]
        k_ref[p] = k[:, sl]
        v_ref[p] = v[:, sl]
        kk_ref[p] = kk[:, sl]
        for dr in range(2):
            sl2 = slice(dr * d + p * LANES, dr * d + (p + 1) * LANES)
            ew_ref[dr, p] = ew[:, sl2]
            as_ref[dr, p] = asg[:, sl2]


def _rw_proj_call(x, mod, g, mu, wr, wk, wv, w1, w2, w0, a1, a2, a0, g1, g2, kkw, *, nct):
    n, d = x.shape
    tm = RW_TM
    nt = n // tm
    nct = nct * (TM // tm)
    npair = d // LANES
    tb = tm // NB
    nb8 = n // NB
    kern = functools.partial(_rw_proj_kernel, nct=nct, nt=nt)
    full2 = lambda a: pl.BlockSpec(a.shape, lambda i: (0, 0))
    slab = pl.BlockSpec((npair, tm, LANES), lambda i: (0, i, 0))
    slab2 = pl.BlockSpec((2, npair, tm, LANES), lambda i: (0, 0, i, 0))
    slab_shape = jax.ShapeDtypeStruct((npair, n, LANES), F32)
    slab2_shape = jax.ShapeDtypeStruct((2, npair, n, LANES), F32)
    mu3 = mu.reshape(6, 1, d)
    return pl.pallas_call(
        kern,
        grid=(nt,),
        in_specs=[
            pl.BlockSpec((tm, d), lambda i: (i, 0)),
            pl.BlockSpec((NB, d), lambda i: (jnp.maximum(i * tb - 1, 0), 0)),
            pl.BlockSpec((NB, d), lambda i: (jnp.minimum((i + 1) * tb, nb8 - 1), 0)),
            pl.BlockSpec((1, 6, NB, d), lambda i: (_is_lat(i, nct), 0, 0, 0)),
            pl.BlockSpec((1, d), lambda i: (0, 0)),
            pl.BlockSpec((6, 1, d), lambda i: (0, 0, 0)),
            full2(wr), full2(wk), full2(wv),
            full2(w1), full2(w2), full2(w0), full2(a1), full2(a2), full2(a0),
            full2(g1), full2(g2), full2(kkw),
        ],
        out_specs=[slab, slab, slab, slab, slab2, slab2, pl.BlockSpec((tm, d), lambda i: (i, 0))],
        out_shape=[slab_shape, slab_shape, slab_shape, slab_shape, slab2_shape, slab2_shape,
                   jax.ShapeDtypeStruct((n, d), F32)],
        scratch_shapes=[pltpu.VMEM((tm + 2 * NB, d), F32)],
        compiler_params=_cparams(("arbitrary",)),
    )(x, x, x, mod, g, mu3, wr, wk, wv, w1, w2, w0, a1, a2, a0, g1, g2, kkw)


def _wkv_kernel(r_ref, k_ref, v_ref, kk_ref, ew_ref, as_ref, ka_ref, y_ref, st_ref, *, L, npair):
    d = pl.program_id(0)
    c = pl.program_id(1)

    @pl.when(c == 0)
    def _():
        st_ref[...] = jnp.zeros_like(st_ref)

    sgn = 1 - 2 * d
    l2 = 2 * L
    rr = lax.broadcasted_iota(jnp.int32, (l2, l2), 0)
    cc = lax.broadcasted_iota(jnp.int32, (l2, l2), 1)
    same = (rr >= L) == (cc >= L)
    diff = (rr - cc) * sgn
    strict_f = jnp.where(same & (diff > 0), 1.0, 0.0)
    incl_f = jnp.where(same & (diff >= 0), 1.0, 0.0)
    r1 = lax.broadcasted_iota(jnp.int32, (L, L), 0)
    c1 = lax.broadcasted_iota(jnp.int32, (L, L), 1)
    mcum = jnp.where((r1 - c1) * sgn >= 0, 1.0, 0.0).astype(BF16)
    head0 = lax.broadcasted_iota(jnp.int32, (L, LANES), 1) < RWKV_HEAD
    sr_i = lax.broadcasted_iota(jnp.int32, (LANES, LANES), 0)
    sc_i = lax.broadcasted_iota(jnp.int32, (LANES, LANES), 1)
    blockdiag = (sr_i < RWKV_HEAD) == (sc_i < RWKV_HEAD)
    n_dbl = int(round(math.log2(L)))

    def stack2(x):
        return jnp.concatenate([jnp.where(head0, x, 0.0), jnp.where(head0, 0.0, x)], axis=0)

    def unstack(xs):
        return xs[0:L] + xs[L:l2]

    def load(p, b):
        rows = pl.ds(b, L, stride=NB)
        return (r_ref[p, rows, :], k_ref[p, rows, :], v_ref[p, rows, :], kk_ref[p, rows, :],
                ew_ref[0, p, rows, :], as_ref[0, p, rows, :], st_ref[p * NB + b])

    def inst(ka, r, k, v, kk, ew, asg, s0):
        ldec = -ew
        h1, h2, h3 = _split3(ldec)
        cl3 = _dot(mcum, jnp.concatenate([h1, h2, h3], axis=1))
        yield
        cl = cl3[:, 0:LANES] + cl3[:, LANES:2 * LANES] + cl3[:, 2 * LANES:3 * LANES]
        ce = cl - ldec
        ecl = jnp.exp(cl)
        encl = jnp.exp(-cl)
        etot = jnp.exp(jnp.sum(ldec, axis=0, keepdims=True))
        at = -kk * jnp.exp(ce)
        rt = r * ecl
        bt = kk * asg * encl
        kt = k * (1.0 + (asg - 1.0) * ka) * encl

        ats = stack2(at)
        vsb = stack2(v).astype(BF16)
        lhs = jnp.concatenate([ats, stack2(rt)], axis=0).astype(BF16)
        rhs = jnp.concatenate([stack2(bt), stack2(kt)], axis=0).astype(BF16)
        pm = _dot_nt(lhs, rhs)
        yield
        nab = pm[0:l2, 0:l2] * strict_f
        nak = pm[0:l2, l2:2 * l2] * strict_f
        mrb = (pm[l2:2 * l2, 0:l2] * incl_f).astype(BF16)
        mrk = (pm[l2:2 * l2, l2:2 * l2] * incl_f).astype(BF16)

        xs = jnp.concatenate([ats, _dot(nak.astype(BF16), vsb)], axis=1)
        mkv = _dot(mrk, vsb)
        yield
        npow = nab
        for it in range(n_dbl):
            npb = npow.astype(BF16)
            xs = xs + _dot(npb, xs.astype(BF16))
            if it + 1 < n_dbl:
                npow = _dot(npb, npb)
            yield
        mx = _dot(mrb, xs.astype(BF16))
        at2 = unstack(xs[:, 0:LANES])
        ul = unstack(xs[:, LANES:2 * LANES])
        btb = bt.astype(BF16)
        emat = (jnp.where(blockdiag, _dot(at2.T.astype(BF16), btb), 0.0) * etot).astype(BF16)
        uv = jnp.concatenate([ul, v], axis=0)
        bk = jnp.concatenate([btb, kt.astype(BF16)], axis=0)
        dmat = jnp.where(blockdiag, _dot(uv.T.astype(BF16), bk), 0.0) * etot
        yield
        rt2 = rt + unstack(mx[:, 0:LANES])
        yl = unstack(mx[:, LANES:2 * LANES] + mkv)

        s0b = s0.astype(BF16)
        return _dot_nt(rt2.astype(BF16), s0b) + yl, s0 * etot + _dot(s0b, emat) + dmat

    def run_group(gens):
        outs = [None] * len(gens)
        live = list(enumerate(gens))
        while live:
            nxt = []
            for i, g in live:
                try:
                    next(g)
                    nxt.append((i, g))
                except StopIteration as stop:
                    outs[i] = stop.value
            live = nxt
        return outs

    def pair(p, carry):
        ka = ka_ref[p]
        for b0 in range(0, NB, WKV_GROUP):
            outs = run_group([inst(ka, *load(p, b0 + g)) for g in range(WKV_GROUP)])
            for g, (y, s1) in enumerate(outs):
                y_ref[0, p, pl.ds(b0 + g, L, stride=NB), :] = y
                st_ref[p * NB + b0 + g] = s1
        return carry

    lax.fori_loop(0, npair, pair, 0)


def _wkv_call(r, k, v, kk, ew, asg, ka, *, nct_rows):
    npair, n, _ = r.shape
    L = WKV_L
    rows = L * NB
    nc = n // rows
    ncc = nct_rows // rows
    kern = functools.partial(_wkv_kernel, L=L, npair=npair)

    def chunk(dd, c):
        bwd = jnp.where(c < ncc, ncc - 1 - c, nc - 1 + ncc - c)
        return jnp.where(dd == 0, c, bwd)

    slab = pl.BlockSpec((npair, rows, LANES), lambda dd, c: (0, chunk(dd, c), 0))
    slab2 = pl.BlockSpec((1, npair, rows, LANES), lambda dd, c: (dd, 0, chunk(dd, c), 0))
    return pl.pallas_call(
        kern,
        grid=(2, nc),
        in_specs=[slab, slab, slab, slab, slab2, slab2,
                  pl.BlockSpec((npair, 1, LANES), lambda dd, c: (0, 0, 0))],
        out_specs=slab2,
        out_shape=jax.ShapeDtypeStruct((2, npair, n, LANES), F32),
        scratch_shapes=[pltpu.VMEM((npair * NB, LANES, LANES), F32)],
        compiler_params=_cparams(("arbitrary", "arbitrary")),
    )(r, k, v, kk, ew, asg, ka)


def _rw_out_kernel(x_ref, mod_ref, y_ref, r_ref, k_ref, v_ref, as_ref, gg_ref, lng_ref, lnb_ref,
                   ka_ref, rk_ref, wo_ref, o_ref):
    x = x_ref[...]
    d = x.shape[1]
    npair = d // LANES
    cat = lambda f: jnp.concatenate([f(p) for p in range(npair)], axis=1)
    y = cat(lambda p: y_ref[0, p] + y_ref[1, p])
    r = cat(lambda p: r_ref[p])
    k = cat(lambda p: k_ref[p])
    v = cat(lambda p: v_ref[p])
    am = cat(lambda p: 0.5 * (as_ref[0, p] + as_ref[1, p]))
    ones_bd = _head_ones()
    inv = 1.0 / RWKV_HEAD
    mean = _head_sum_bcast(y, ones_bd) * inv
    yc = y - mean
    var = _head_sum_bcast(yc * yc, ones_bd) * inv
    yn = yc * lax.rsqrt(var + GN_EPS) * lng_ref[...] + lnb_ref[...]
    k_mean = k * (1.0 + (am - 1.0) * ka_ref[...])
    bonus = _head_sum_bcast(r * k_mean * rk_ref[...], ones_bd) * v
    z = ((yn + bonus) * gg_ref[...]).astype(BF16)
    o_ref[...] = _residual(x, mod_ref[0, 2], _dot(z, wo_ref[...]))


def _rw_out_call(x, mod, y, r, k, v, asg, gg, lng, lnb, ka, rk, wo, *, nct):
    n, d = x.shape
    nt = n // TM
    npair = d // LANES
    slab = pl.BlockSpec((npair, TM, LANES), lambda i: (0, i, 0))
    slab2 = pl.BlockSpec((2, npair, TM, LANES), lambda i: (0, 0, i, 0))
    row = pl.BlockSpec((1, d), lambda i: (0, 0))
    return pl.pallas_call(
        _rw_out_kernel,
        grid=(nt,),
        in_specs=[
            pl.BlockSpec((TM, d), lambda i: (i, 0)),
            pl.BlockSpec((1, 6, NB, d), lambda i: (_is_lat(i, nct), 0, 0, 0)),
            slab2, slab, slab, slab, slab2,
            pl.BlockSpec((TM, d), lambda i: (i, 0)),
            row, row, row, row,
            pl.BlockSpec((d, d), lambda i: (0, 0)),
        ],
        out_specs=pl.BlockSpec((TM, d), lambda i: (i, 0)),
        out_shape=jax.ShapeDtypeStruct((n, d), F32),
        compiler_params=_cparams(("arbitrary",)),
    )(x, mod, y, r, k, v, asg, gg, lng, lnb, ka, rk, wo)


def _final_kernel(x_ref, g_ref, o_ref, slab_ref):
    x = x_ref[...]
    d = x.shape[1]
    ms = jnp.mean(x * x, axis=-1, keepdims=True)
    y = x * lax.rsqrt(ms + RMS_EPS) * g_ref[...]
    for p in range(d // LANES):
        slab_ref[p] = y[:, p * LANES:(p + 1) * LANES]
    for b in range(NB):
        for p in range(d // LANES):
            o_ref[b, :, p * LANES:(p + 1) * LANES] = slab_ref[p, pl.ds(b, TT, stride=NB), :]


def _final_call(x, g, *, tile0):
    n, d = x.shape
    nt = n // TM
    return pl.pallas_call(
        _final_kernel,
        grid=(nt - tile0,),
        in_specs=[pl.BlockSpec((TM, d), lambda i: (i + tile0, 0)),
                  pl.BlockSpec((1, d), lambda i: (0, 0))],
        out_specs=pl.BlockSpec((NB, TT, d), lambda i: (0, i, 0)),
        out_shape=jax.ShapeDtypeStruct((NB, (nt - tile0) * TT, d), F32),
        scratch_shapes=[pltpu.VMEM((d // LANES, TM, LANES), F32)],
        compiler_params=_cparams(("arbitrary",)),
    )(x, g)


def _time_major_kernel(c_ref, x_ref, o_ref, slab_ref, *, nct):
    d = o_ref.shape[1]

    def emit(src_ref):
        for b in range(NB):
            for p in range(d // LANES):
                slab_ref[p, pl.ds(b, TT, stride=NB), :] = src_ref[b, :, p * LANES:(p + 1) * LANES]
        for p in range(d // LANES):
            o_ref[:, p * LANES:(p + 1) * LANES] = slab_ref[p]

    @pl.when(pl.program_id(0) < nct)
    def _():
        emit(c_ref)

    @pl.when(pl.program_id(0) >= nct)
    def _():
        emit(x_ref)


def _time_major_call(ctx, x):
    _, lc, d = ctx.shape
    seq = x.shape[1]
    nct = lc // TT
    nt = (lc + seq) // TT
    return pl.pallas_call(
        functools.partial(_time_major_kernel, nct=nct),
        grid=(nt,),
        in_specs=[pl.BlockSpec((NB, TT, d), lambda i: (0, jnp.minimum(i, nct - 1), 0)),
                  pl.BlockSpec((NB, TT, d), lambda i: (0, jnp.maximum(i - nct, 0), 0))],
        out_specs=pl.BlockSpec((TM, d), lambda i: (i, 0)),
        out_shape=jax.ShapeDtypeStruct((nt * TM, d), F32),
        scratch_shapes=[pltpu.VMEM((d // LANES, TM, LANES), F32)],
        compiler_params=_cparams(("arbitrary",)),
    )(ctx, x)


def _row(v):
    return v.reshape(1, -1).astype(F32)


def _s5_layer(x, mod, g, p, *, nct, tile0):
    bb, cc, ab = _s5_params(p['lam_re'], p['lam_im'], p['log_dt'], p['b_re'], p['b_im'],
                            p['c_re'], p['c_im'])
    y = _s5_scan_call(x, mod, g, bb, cc, ab, nct=nct)
    return _s5_glu_call(x, mod, g, y, _row(p['d']), p['glu_w'].astype(BF16), _row(p['glu_b']),
                        nct=nct, tile0=tile0)


def _conformer_layer(x, mod, g, p, *, nct):
    u = _cv_glu_call(x, mod, g, p['pw1_w'].astype(BF16), _row(p['pw1_b']), nct=nct)
    return _cv_conv_call(u, x, mod, p['dw_w'].astype(F32), _row(p['dw_b']), _row(p['ln_g']),
                         _row(p['ln_b']), p['pw2_w'].astype(BF16), _row(p['pw2_b']), nct=nct)


def _blockdiag2(m):
    z = jnp.zeros_like(m[0])
    return jnp.concatenate([jnp.concatenate([m[0], z], axis=1),
                            jnp.concatenate([z, m[1]], axis=1)], axis=0)


def _rwkv_layer(x, mod, g, p, *, nct):
    n, d = x.shape
    npair = d // LANES
    w1 = jnp.concatenate([p['w1'][0], p['w1'][1]], axis=1).astype(BF16)
    w2 = _blockdiag2(p['w2']).astype(BF16)
    a1 = jnp.concatenate([p['a1'][0], p['a1'][1]], axis=1).astype(BF16)
    a2 = _blockdiag2(p['a2']).astype(BF16)
    gl = p['g1'].shape[1]
    glp = -(-gl // LANES) * LANES
    g1 = jnp.pad(p['g1'], ((0, 0), (0, glp - gl))).astype(BF16)
    g2 = jnp.pad(p['g2'], ((0, glp - gl), (0, 0))).astype(BF16)
    r, k, v, kk, ew, asg, gg = _rw_proj_call(
        x, mod, g, p['mu'].astype(F32), p['w_r'].astype(BF16), p['w_k'].astype(BF16),
        p['w_v'].astype(BF16), w1, w2, _row(p['w0']), a1, a2, _row(p['a0']), g1, g2,
        _row(p['k_k']), nct=nct)
    ka_slab = p['k_a'].astype(F32).reshape(npair, 1, LANES)
    y = _wkv_call(r, k, v, kk, ew, asg, ka_slab, nct_rows=nct * TM)
    return _rw_out_call(x, mod, y, r, k, v, asg, gg, _row(p['lnx_g']), _row(p['lnx_b']),
                        _row(p['k_a']), _row(p['r_k']), p['w_o'].astype(BF16), nct=nct)


def _forward(x, c, ctx, c_ctx, ada_w, ada_b, norm_mix_g, norm_ffn_g, final_g, s5, cv, rw,
             ff_w1, ff_w3, ff_w2, moe_router, moe_w1, moe_w3, moe_w2):
    bsz, seq, d = x.shape
    lc = ctx.shape[1]
    depth = ada_w.shape[0]
    assert bsz == NB and lc % TT == 0 and seq % TT == 0
    assert TT == GRID_W, "one token tile must hold exactly one latent raster row per batch entry"
    nct = lc // TT
    t_all = lc + seq
    n = t_all * NB

    xs = _time_major_call(ctx.astype(F32), x.astype(F32))

    cvec = jnp.zeros((2 * NB, d), F32).at[:NB].set(c).at[NB].set(c_ctx)
    m_all = _ada_call(cvec, ada_w.astype(F32), ada_b.astype(F32))
    m_all = m_all.reshape(depth, 2 * NB, 6, d)
    m_lat = jnp.swapaxes(m_all[:, :NB], 1, 2)
    m_ctx = jnp.broadcast_to(m_all[:, NB][:, :, None, :], (depth, 6, NB, d))
    mods = jnp.stack([m_ctx, m_lat], axis=1)

    ctx_rows = nct * TM
    for i in range(depth):
        last = i == depth - 1
        mod = mods[i]
        g_mix = _row(norm_mix_g[i])
        j = i // 3
        if i % 3 == 0:
            xs = _s5_layer(xs, mod, g_mix, {k_: v_[j] for k_, v_ in s5.items()}, nct=nct,
                           tile0=nct if last else 0)
            ctx_rows = 0 if last else ctx_rows
        elif i % 3 == 1:
            xs = _conformer_layer(xs, mod, g_mix, {k_: v_[j] for k_, v_ in cv.items()}, nct=nct)
        else:
            xs = _rwkv_layer(xs, mod, g_mix, {k_: v_[j] for k_, v_ in rw.items()}, nct=nct)
        fi = i // 2
        g_ffn = _row(norm_ffn_g[i])
        if i % 2 == 0:
            xs = _ffn_call(xs, mod, g_ffn, ff_w1, ff_w3, ff_w2, layer=fi, nct_rows=ctx_rows)
        else:
            xs = _moe_call(xs, mod, g_ffn, moe_router[fi], moe_w1, moe_w3, moe_w2, layer=fi,
                           nct_rows=ctx_rows)

    return _final_call(xs, _row(final_g), tile0=ctx_rows // TM)


def kernel(x, c, ctx, c_ctx, ada_w, ada_b, norm_mix_g, norm_ffn_g, final_g, s5_lambda_re, s5_lambda_im, s5_log_dt, s5_b_re, s5_b_im, s5_c_re, s5_c_im, s5_d, s5_glu_w, s5_glu_b, cv_pw1_w, cv_pw1_b, cv_dw_w, cv_dw_b, cv_ln_g, cv_ln_b, cv_pw2_w, cv_pw2_b, rw_mu, rw_w_r, rw_w_k, rw_w_v, rw_w_o, rw_w0, rw_w1, rw_w2, rw_a0, rw_a1, rw_a2, rw_g1, rw_g2, rw_k_k, rw_k_a, rw_r_k, rw_lnx_g, rw_lnx_b, ff_w1, ff_w3, ff_w2, moe_router, moe_w1, moe_w3, moe_w2):
    s5 = dict(lam_re=s5_lambda_re, lam_im=s5_lambda_im, log_dt=s5_log_dt, b_re=s5_b_re, b_im=s5_b_im,
              c_re=s5_c_re, c_im=s5_c_im, d=s5_d, glu_w=s5_glu_w, glu_b=s5_glu_b)
    cv = dict(pw1_w=cv_pw1_w, pw1_b=cv_pw1_b, dw_w=cv_dw_w, dw_b=cv_dw_b, ln_g=cv_ln_g, ln_b=cv_ln_b,
              pw2_w=cv_pw2_w, pw2_b=cv_pw2_b)
    rw = dict(mu=rw_mu, w_r=rw_w_r, w_k=rw_w_k, w_v=rw_w_v, w_o=rw_w_o, w0=rw_w0, w1=rw_w1, w2=rw_w2,
              a0=rw_a0, a1=rw_a1, a2=rw_a2, g1=rw_g1, g2=rw_g2, k_k=rw_k_k, k_a=rw_k_a, r_k=rw_r_k,
              lnx_g=rw_lnx_g, lnx_b=rw_lnx_b)
    return _forward(x, c, ctx, c_ctx, ada_w, ada_b, norm_mix_g, norm_ffn_g, final_g, s5, cv, rw,
                    ff_w1, ff_w3, ff_w2, moe_router, moe_w1, moe_w3, moe_w2)
```

```python
import functools
import math

import jax
import jax.numpy as jnp
from jax import lax
from jax.experimental import pallas as pl
from jax.experimental.pallas import tpu as pltpu

F32 = jnp.float32
BF16 = jnp.bfloat16

NB = 8
LANES = 128
TM = 512
TT = TM // NB
GRID_W = 64
CONV_WIDTH = 31
CONV_PAD = CONV_WIDTH // 2
HALO_ROWS = 128
S5_GROUP = 16
S5_STATE = 64
S5_JBLK = 8
RWKV_HEAD = 64
WKV_L = 64
WKV_PAIRS = 2
RW_TM = 256
FFN_TM = 1024
MOE_T = 1024
MOE_BLK = 32
MOE_SLOTS = 32
RMS_EPS = 1e-6
LN_EPS = 1e-5
GN_EPS = 64e-5
KK_EPS = 1e-12
LAMBDA_RE_MAX = -1e-4
VMEM_LIMIT = 56 * 1024 * 1024


def _cparams(sem):
    return pltpu.CompilerParams(dimension_semantics=sem, vmem_limit_bytes=VMEM_LIMIT)


def _dot(a, b):
    return jnp.dot(a, b, preferred_element_type=F32)


def _dot_nt(a, b):
    return lax.dot_general(a, b, (((1,), (1,)), ((), ())), preferred_element_type=F32)


def _split3(x):
    h1 = x.astype(BF16)
    r1 = x - h1.astype(F32)
    h2 = r1.astype(BF16)
    h3 = (r1 - h2.astype(F32)).astype(BF16)
    return h1, h2, h3


def _dot_x3(a, b):
    a1 = a.astype(BF16)
    a2 = (a - a1.astype(F32)).astype(BF16)
    b1 = b.astype(BF16)
    b2 = (b - b1.astype(F32)).astype(BF16)
    return _dot(a1, b1) + _dot(a1, b2) + _dot(a2, b1)


def _sigmoid(x):
    return jax.nn.sigmoid(x)


def _norm_mod(x, g, shift, scale):
    rows, d = x.shape
    ms = jnp.mean(x * x, axis=-1, keepdims=True)
    y = x * lax.rsqrt(ms + RMS_EPS) * g
    y3 = y.reshape(rows // NB, NB, d)
    return (y3 * (1.0 + scale)[None] + shift[None]).reshape(rows, d)


def _residual(x, gate, y):
    rows, d = x.shape
    y3 = y.reshape(rows // NB, NB, d) * gate[None]
    return x + y3.reshape(rows, d)


def _is_lat(i, nct):
    return jnp.where(i >= nct, 1, 0).astype(jnp.int32)


def _ada_kernel(s_ref, w_ref, b_ref, o_ref):
    s = s_ref[...]
    s = s * _sigmoid(s)
    o_ref[0] = _dot(s.astype(BF16), w_ref[0].astype(BF16)) + b_ref[0]


def _ada_call(cvec, ada_w, ada_b):
    depth, d, d6 = ada_w.shape
    tn = 1536
    return pl.pallas_call(
        _ada_kernel,
        grid=(depth, d6 // tn),
        in_specs=[
            pl.BlockSpec((2 * NB, d), lambda l, j: (0, 0)),
            pl.BlockSpec((1, d, tn), lambda l, j: (l, 0, j)),
            pl.BlockSpec((1, 1, tn), lambda l, j: (l, 0, j)),
        ],
        out_specs=pl.BlockSpec((1, 2 * NB, tn), lambda l, j: (l, 0, j)),
        out_shape=jax.ShapeDtypeStruct((depth, 2 * NB, d6), F32),
        compiler_params=_cparams(("arbitrary", "arbitrary")),
    )(cvec, ada_w, ada_b.reshape(depth, 1, d6))


def _ffn_kernel(x_ref, mod_ref, g_ref, w1_ref, w3_ref, w2_ref, o_ref, hb_ref, acc_ref, *, n_k):
    k = pl.program_id(1)

    @pl.when(k == 0)
    def _():
        h = _norm_mod(x_ref[...], g_ref[...], mod_ref[0, 3], mod_ref[0, 4])
        hb_ref[...] = h.astype(BF16)
        acc_ref[...] = jnp.zeros_like(acc_ref)

    hb = hb_ref[...]
    a = _dot(hb, w1_ref[0].astype(BF16))
    b = _dot(hb, w3_ref[0].astype(BF16))
    acc_ref[...] += _dot((a * _sigmoid(a) * b).astype(BF16), w2_ref[0].astype(BF16))

    @pl.when(k == n_k - 1)
    def _():
        o_ref[...] = _residual(x_ref[...], mod_ref[0, 5], acc_ref[...])


def _ffn_call(x, mod, g, w1, w3, w2, *, layer, nct_rows):
    n, d = x.shape
    f = w1.shape[2]
    tm = FFN_TM
    tf = 512
    n_k = f // tf
    nct = nct_rows // tm
    return pl.pallas_call(
        functools.partial(_ffn_kernel, n_k=n_k),
        grid=(n // tm, n_k),
        in_specs=[
            pl.BlockSpec((tm, d), lambda i, k: (i, 0)),
            pl.BlockSpec((1, 6, NB, d), lambda i, k: (_is_lat(i, nct), 0, 0, 0)),
            pl.BlockSpec((1, d), lambda i, k: (0, 0)),
            pl.BlockSpec((1, d, tf), lambda i, k: (layer, 0, k)),
            pl.BlockSpec((1, d, tf), lambda i, k: (layer, 0, k)),
            pl.BlockSpec((1, tf, d), lambda i, k: (layer, k, 0)),
        ],
        out_specs=pl.BlockSpec((tm, d), lambda i, k: (i, 0)),
        out_shape=jax.ShapeDtypeStruct((n, d), F32),
        scratch_shapes=[pltpu.VMEM((tm, d), BF16), pltpu.VMEM((tm, d), F32)],
        compiler_params=_cparams(("arbitrary", "arbitrary")),
    )(x, mod, g, w1, w3, w2)


def _top2(logits, n_e):
    lane = lax.broadcasted_iota(jnp.int32, logits.shape, 1).astype(F32)
    neg = jnp.float32(-3.0e38)
    lg = jnp.where(lane < n_e, logits, neg)
    m1 = jnp.max(lg, axis=-1, keepdims=True)
    i1 = jnp.min(jnp.where(lg == m1, lane, 1.0e9), axis=-1, keepdims=True)
    lg2 = jnp.where(lane == i1, neg, lg)
    m2 = jnp.max(lg2, axis=-1, keepdims=True)
    i2 = jnp.min(jnp.where(lg2 == m2, lane, 1.0e9), axis=-1, keepdims=True)
    e2 = jnp.exp(m2 - m1)
    den = 1.0 + e2
    return lane, i1, i2, 1.0 / den, e2 / den


def _moe_route_kernel(x_ref, mod_ref, g_ref, router_ref, ltri_ref, z_ref, dest_ref, cnt_ref,
                      hb_ref, *, n_e, cap):
    t, d = x_ref.shape
    h = _norm_mod(x_ref[...], g_ref[...], mod_ref[0, 3], mod_ref[0, 4])
    hb_ref[...] = h.astype(BF16)
    lane, i1, i2, p1, p2 = _top2(_dot_x3(h, router_ref[...]), n_e)
    sel1 = lane == i1
    sel2 = lane == i2
    c = jnp.where(sel1, 1.0, 0.0) + jnp.where(sel2, 1.0, 0.0)
    rank = _dot(ltri_ref[...], c.astype(BF16))
    cnt = jnp.sum(c, axis=0, keepdims=True)
    seg = jnp.floor((cnt + (MOE_BLK - 1)) * (1.0 / MOE_BLK)) * MOE_BLK
    ur = lax.broadcasted_iota(jnp.int32, (LANES, LANES), 0)
    uc = lax.broadcasted_iota(jnp.int32, (LANES, LANES), 1)
    before = jnp.where(ur < uc, 1.0, 0.0).astype(BF16)
    start = _dot(jnp.broadcast_to(seg, (NB, LANES)).astype(BF16), before)[0:1]
    pos = start + rank
    d1 = jnp.sum(jnp.where(sel1, pos, 0.0), axis=-1, keepdims=True)
    d2 = jnp.sum(jnp.where(sel2, pos, 0.0), axis=-1, keepdims=True)
    packed = (jnp.where(lane == 0.0, d1, 0.0) + jnp.where(lane == 1.0, d2, 0.0)
              + jnp.where(lane == 2.0, p1, 0.0) + jnp.where(lane == 3.0, p2, 0.0))
    dest_ref[...] = packed
    cnt_ref[0] = jnp.broadcast_to(cnt, (NB, LANES))
    packed_t = packed.T
    d1r = packed_t[0:1]
    d2r = packed_t[1:2]
    p1r = packed_t[2:3]
    p2r = packed_t[3:4]
    rc = 256

    def chunk(ci, carry):
        r0 = pl.multiple_of(ci * rc, rc)
        row = (lax.broadcasted_iota(jnp.int32, (rc, t), 0) + r0).astype(F32)
        m1 = row == d1r
        m2 = row == d2r
        perm = (jnp.where(m1, 1.0, 0.0) + jnp.where(m2, 1.0, 0.0)).astype(BF16)
        zc = _dot(perm, hb_ref[...])
        gate = jnp.sum(jnp.where(m1, p1r, 0.0) + jnp.where(m2, p2r, 0.0), axis=-1, keepdims=True)
        ghi = gate.astype(BF16).astype(F32)
        glo = gate - ghi
        le = lax.broadcasted_iota(jnp.int32, (rc, LANES), 1)
        ext = jnp.where(le == 0, ghi, jnp.where(le == 1, glo, 0.0))
        z_ref[pl.ds(r0, rc), 0:d] = zc.astype(BF16)
        z_ref[pl.ds(r0, rc), d:d + LANES] = ext.astype(BF16)
        return carry

    lax.fori_loop(0, cap // rc, chunk, 0)


def _moe_experts_kernel(tin_ref, se_ref, nv_ref, *refs, n_k, d):
    z_refs = refs[:MOE_SLOTS]
    w1_ref, w3_ref, w2_ref, o_ref, xs_ref, acc_ref = refs[MOE_SLOTS:]
    s = pl.program_id(0)
    k = pl.program_id(1)
    nv = nv_ref[s]
    half = MOE_SLOTS // 2

    @pl.when((nv > 0) & (k == 0))
    def _():
        for j in range(MOE_SLOTS):
            xs_ref[j * MOE_BLK:(j + 1) * MOE_BLK] = z_refs[j][...]
        acc_ref[...] = jnp.zeros_like(acc_ref)

    def compute(nrows):
        hb = xs_ref[0:nrows, 0:d]
        ext = xs_ref[0:nrows, d:d + LANES].astype(F32)
        le = lax.broadcasted_iota(jnp.int32, ext.shape, 1)
        gate = jnp.sum(jnp.where(le < 2, ext, 0.0), axis=-1, keepdims=True)
        a = _dot(hb, w1_ref[0, 0].astype(BF16))
        b = _dot(hb, w3_ref[0, 0].astype(BF16))
        act = a * _sigmoid(a) * b * gate
        acc_ref[0:nrows] += _dot(act.astype(BF16), w2_ref[0, 0].astype(BF16))

    @pl.when(nv > half)
    def _():
        compute(MOE_SLOTS * MOE_BLK)

    @pl.when((nv > 0) & (nv <= half))
    def _():
        compute(half * MOE_BLK)

    @pl.when(k == n_k - 1)
    def _():
        o_ref[...] = jnp.where(nv > 0, acc_ref[...], 0.0).astype(BF16)


def _moe_combine_kernel(tc_ref, *refs, nblk, cap):
    zb_refs = refs[:nblk]
    dest_ref, x_ref, mod_ref, o_ref, zbuf_ref = refs[nblk:]
    t, d = x_ref.shape
    for j in range(nblk):
        zbuf_ref[j * MOE_BLK:(j + 1) * MOE_BLK] = zb_refs[j][...]
    packed = dest_ref[...]
    lane = lax.broadcasted_iota(jnp.int32, packed.shape, 1)
    d1 = jnp.sum(jnp.where(lane == 0, packed, 0.0), axis=-1, keepdims=True)
    d2 = jnp.sum(jnp.where(lane == 1, packed, 0.0), axis=-1, keepdims=True)
    kc = 256
    acc = jnp.zeros((t, d), F32)
    for c0 in range(0, cap, kc):
        col = (lax.broadcasted_iota(jnp.int32, (t, kc), 1) + c0).astype(F32)
        sel = (jnp.where(col == d1, 1.0, 0.0) + jnp.where(col == d2, 1.0, 0.0)).astype(BF16)
        acc = acc + _dot(sel, zbuf_ref[c0:c0 + kc])
    o_ref[...] = _residual(x_ref[...], mod_ref[0, 5], acc)


def _moe_tables(cnt, *, n_tiles, n_e, nblk, n_steps):
    i32 = jnp.int32
    e_ids = jnp.arange(n_e, dtype=i32)
    t_ids = jnp.arange(n_tiles, dtype=i32)
    pc = (cnt + (MOE_BLK - 1)) // MOE_BLK
    lb = jnp.cumsum(pc, axis=1) - pc
    cum = jnp.cumsum(pc, axis=0)
    cum_ex = cum - pc
    nb_e = cum[-1]
    steps_e = (nb_e + (MOE_SLOTS - 1)) // MOE_SLOTS
    ends = jnp.cumsum(steps_e)
    step0 = ends - steps_e
    sidx = jnp.arange(n_steps, dtype=i32)
    step_e = jnp.minimum(jnp.sum((sidx[:, None] >= ends[None, :]).astype(i32), axis=1), n_e - 1)
    oh_e = (step_e[:, None] == e_ids[None, :]).astype(i32)
    by_step = lambda v: jnp.sum(oh_e * v[None, :], axis=1)
    by_step_tile = lambda m: jnp.sum(oh_e[:, None, :] * m[None, :, :], axis=2)
    rank = ((sidx - by_step(step0))[:, None] * MOE_SLOTS
            + jnp.arange(MOE_SLOTS, dtype=i32)[None, :])
    cum_s = by_step_tile(cum)
    tile = jnp.sum((cum_s[:, None, :] <= rank[:, :, None]).astype(i32), axis=2)
    nvalid = jnp.where(sidx < ends[-1], jnp.clip(by_step(nb_e) - rank[:, 0], 0, MOE_SLOTS), 0)
    ok = rank < (rank[:, :1] + nvalid[:, None])
    oh_t = (jnp.minimum(tile, n_tiles - 1)[:, :, None] == t_ids[None, None, :]).astype(i32)
    at_tile = lambda m: jnp.sum(oh_t * by_step_tile(m)[:, None, :], axis=2)
    src = jnp.sum(oh_t * t_ids, axis=2) * nblk + at_tile(lb) + rank - at_tile(cum_ex)
    tin = jnp.where(ok, src, 0).reshape(-1).astype(i32)
    lblk = jnp.arange(nblk, dtype=i32)[None, :]
    seg_end = lb + pc
    e_l = jnp.minimum(jnp.sum((seg_end[:, None, :] <= lblk[:, :, None]).astype(i32), axis=2), n_e - 1)
    oh_l = (e_l[:, :, None] == e_ids[None, None, :]).astype(i32)
    r_l = jnp.sum(oh_l * (cum_ex - lb)[:, None, :], axis=2) + lblk
    flat = (jnp.sum(oh_l * step0, axis=2) + r_l // MOE_SLOTS) * MOE_SLOTS + r_l % MOE_SLOTS
    tcomb = jnp.where(lblk < seg_end[:, -1:], flat, 0).reshape(-1).astype(i32)
    return tin, tcomb, step_e.astype(i32), nvalid.astype(i32)


def _moe_call(x, mod, g, router, w1, w3, w2, *, layer, nct_rows):
    n, d = x.shape
    _, n_e, _, f = w1.shape
    t = MOE_T
    assert n % t == 0 and nct_rows % t == 0
    n_tiles = n // t
    nct = nct_rows // t
    cap = 2 * t + n_e * MOE_BLK
    nblk = cap // MOE_BLK
    rows = MOE_SLOTS * MOE_BLK
    n_steps = -(-(n_tiles * nblk) // MOE_SLOTS) + n_e
    tf = 512
    n_k = f // tf
    router_p = jnp.pad(router.astype(F32), ((0, 0), (0, LANES - n_e)))
    ltri = jnp.tril(jnp.ones((t, t), F32), k=-1).astype(BF16)
    mod_spec = lambda off: pl.BlockSpec((1, 6, NB, d), lambda i, *_: (_is_lat(i, nct), 0, 0, 0))

    z, dest, cnt = pl.pallas_call(
        functools.partial(_moe_route_kernel, n_e=n_e, cap=cap),
        grid=(n_tiles,),
        in_specs=[
            pl.BlockSpec((t, d), lambda i: (i, 0)),
            mod_spec(0),
            pl.BlockSpec((1, d), lambda i: (0, 0)),
            pl.BlockSpec((d, LANES), lambda i: (0, 0)),
            pl.BlockSpec((t, t), lambda i: (0, 0)),
        ],
        out_specs=[
            pl.BlockSpec((cap, d + LANES), lambda i: (i, 0)),
            pl.BlockSpec((t, LANES), lambda i: (i, 0)),
            pl.BlockSpec((1, NB, LANES), lambda i: (i, 0, 0)),
        ],
        out_shape=[
            jax.ShapeDtypeStruct((n_tiles * cap, d + LANES), BF16),
            jax.ShapeDtypeStruct((n, LANES), F32),
            jax.ShapeDtypeStruct((n_tiles, NB, LANES), F32),
        ],
        scratch_shapes=[pltpu.VMEM((t, d), BF16)],
        compiler_params=_cparams(("arbitrary",)),
    )(x, mod, g, router_p, ltri)

    tin, tcomb, step_e, nvalid = _moe_tables(cnt[:, 0, :n_e].astype(jnp.int32), n_tiles=n_tiles,
                                            n_e=n_e, nblk=nblk, n_steps=n_steps)

    def kk(s, k, nv_r):
        return jnp.where(nv_r[s] > 0, k, n_k - 1)

    def z_spec(j):
        return pl.BlockSpec((MOE_BLK, d + LANES), lambda s, k, tin_r, se_r, nu_r: (tin_r[s * MOE_SLOTS + j], 0))

    zo = pl.pallas_call(
        functools.partial(_moe_experts_kernel, n_k=n_k, d=d),
        grid_spec=pltpu.PrefetchScalarGridSpec(
            num_scalar_prefetch=3,
            grid=(n_steps, n_k),
            in_specs=[z_spec(j) for j in range(MOE_SLOTS)] + [
                pl.BlockSpec((1, 1, d, tf), lambda s, k, tin_r, se_r, nu_r: (layer, se_r[s], 0, kk(s, k, nu_r))),
                pl.BlockSpec((1, 1, d, tf), lambda s, k, tin_r, se_r, nu_r: (layer, se_r[s], 0, kk(s, k, nu_r))),
                pl.BlockSpec((1, 1, tf, d), lambda s, k, tin_r, se_r, nu_r: (layer, se_r[s], kk(s, k, nu_r), 0)),
            ],
            out_specs=pl.BlockSpec((rows, d), lambda s, k, tin_r, se_r, nu_r: (s, 0)),
            scratch_shapes=[pltpu.VMEM((rows, d + LANES), BF16), pltpu.VMEM((rows, d), F32)],
        ),
        out_shape=jax.ShapeDtypeStruct((n_steps * rows, d), BF16),
        compiler_params=_cparams(("arbitrary", "arbitrary")),
    )(tin, step_e, nvalid, *([z] * MOE_SLOTS), w1, w3, w2)

    def zo_spec(j):
        return pl.BlockSpec((MOE_BLK, d), lambda i, tc_r: (tc_r[i * nblk + j], 0))

    return pl.pallas_call(
        functools.partial(_moe_combine_kernel, nblk=nblk, cap=cap),
        grid_spec=pltpu.PrefetchScalarGridSpec(
            num_scalar_prefetch=1,
            grid=(n_tiles,),
            in_specs=[zo_spec(j) for j in range(nblk)] + [
                pl.BlockSpec((t, LANES), lambda i, tc_r: (i, 0)),
                pl.BlockSpec((t, d), lambda i, tc_r: (i, 0)),
                mod_spec(0),
            ],
            out_specs=pl.BlockSpec((t, d), lambda i, tc_r: (i, 0)),
            scratch_shapes=[pltpu.VMEM((cap, d), BF16)],
        ),
        out_shape=jax.ShapeDtypeStruct((n, d), F32),
        compiler_params=_cparams(("arbitrary",)),
    )(tcomb, *([zo] * nblk), dest, x, mod)


def _dir_tile(d, s, nct, nt):
    bwd = jnp.where(s < nct, nct - 1 - s, nt - 1 + nct - s)
    return jnp.where(d == 0, s, bwd)


def _s5_scan_kernel(x_ref, mod_ref, g_ref, bb_ref, cc_ref, ab_ref, y_ref,
                    hb_ref, bu0_ref, bu1_ref, bu2_ref, st_ref):
    d = pl.program_id(0)
    s = pl.program_id(1)
    half = S5_JBLK * S5_STATE
    bufs = (bu0_ref, bu1_ref, bu2_ref)

    @pl.when(s == 0)
    def _():
        st_ref[...] = jnp.zeros_like(st_ref)

    h = _norm_mod(x_ref[...], g_ref[...], mod_ref[0, 0], mod_ref[0, 1])
    hb_ref[...] = h.astype(BF16)

    def project(j):
        bufs[j % 3][...] = _dot(hb_ref[:, j * LANES:(j + 1) * LANES], bb_ref[0, j])

    def scan(j):
        buf = bufs[j % 3]
        ar = ab_ref[0, j, 0]
        ai = ab_ref[0, j, 1]
        re = st_ref[j, 0]
        im = st_ref[j, 1]
        for t in range(TT):
            tt = t + d * (TT - 1 - 2 * t)
            r0 = pl.multiple_of(tt * NB, NB)
            bre = buf[pl.ds(r0, NB), 0:half]
            bim = buf[pl.ds(r0, NB), half:2 * half]
            re, im = ar * re - ai * im + bre, ar * im + ai * re + bim
            buf[pl.ds(r0, NB), 0:half] = re
            buf[pl.ds(r0, NB), half:2 * half] = im
        st_ref[j, 0] = re
        st_ref[j, 1] = im

    def readout(j):
        y_ref[0, j] = _dot(bufs[j % 3][...].astype(BF16), cc_ref[0, j])

    for i in range(S5_JBLK + 2):
        if i < S5_JBLK:
            project(i)
        if 0 <= i - 1 < S5_JBLK:
            scan(i - 1)
        if 0 <= i - 2 < S5_JBLK:
            readout(i - 2)


def _s5_scan_call(x, mod, g, bb, cc, ab, *, nct):
    n, d = x.shape
    nt = n // TM
    half = S5_JBLK * S5_STATE
    tile = lambda dd, s: _dir_tile(dd, s, nct, nt)
    return pl.pallas_call(
        _s5_scan_kernel,
        grid=(2, nt),
        in_specs=[
            pl.BlockSpec((TM, d), lambda dd, s: (tile(dd, s), 0)),
            pl.BlockSpec((1, 6, NB, d), lambda dd, s: (_is_lat(tile(dd, s), nct), 0, 0, 0)),
            pl.BlockSpec((1, d), lambda dd, s: (0, 0)),
            pl.BlockSpec((1, S5_JBLK, LANES, 2 * half), lambda dd, s: (dd, 0, 0, 0)),
            pl.BlockSpec((1, S5_JBLK, 2 * half, LANES), lambda dd, s: (dd, 0, 0, 0)),
            pl.BlockSpec((1, S5_JBLK, 2, NB, half), lambda dd, s: (dd, 0, 0, 0, 0)),
        ],
        out_specs=pl.BlockSpec((1, S5_JBLK, TM, LANES), lambda dd, s: (dd, 0, tile(dd, s), 0)),
        out_shape=jax.ShapeDtypeStruct((2, S5_JBLK, n, LANES), F32),
        scratch_shapes=[
            pltpu.VMEM((TM, d), BF16),
            pltpu.VMEM((TM, 2 * half), F32),
            pltpu.VMEM((TM, 2 * half), F32),
            pltpu.VMEM((TM, 2 * half), F32),
            pltpu.VMEM((S5_JBLK, 2, NB, half), F32),
        ],
        compiler_params=_cparams(("arbitrary", "arbitrary")),
    )(x, mod, g, bb, cc, ab)


def _gelu_tanh(x):
    c = math.sqrt(2.0 / math.pi)
    return 0.5 * x * (1.0 + jnp.tanh(c * (x + 0.044715 * (x * x * x))))


def _s5_glu_kernel(x_ref, mod_ref, g_ref, y_ref, dsk_ref, w_ref, b_ref, o_ref):
    x = x_ref[...]
    d = x.shape[1]
    h = _norm_mod(x, g_ref[...], mod_ref[0, 0], mod_ref[0, 1])
    y = jnp.concatenate([y_ref[0, j] + y_ref[1, j] for j in range(S5_JBLK)], axis=1)
    z = _gelu_tanh(y + dsk_ref[...] * h).astype(BF16)
    zz = _dot(z, w_ref[...]) + b_ref[...]
    out = zz[:, :d] * _sigmoid(zz[:, d:])
    o_ref[...] = _residual(x, mod_ref[0, 2], out)


def _s5_glu_call(x, mod, g, y, dsk, w, b, *, nct, tile0):
    n, d = x.shape
    nt = n // TM
    return pl.pallas_call(
        _s5_glu_kernel,
        grid=(nt - tile0,),
        in_specs=[
            pl.BlockSpec((TM, d), lambda i: (i + tile0, 0)),
            pl.BlockSpec((1, 6, NB, d), lambda i: (_is_lat(i + tile0, nct), 0, 0, 0)),
            pl.BlockSpec((1, d), lambda i: (0, 0)),
            pl.BlockSpec((2, S5_JBLK, TM, LANES), lambda i: (0, 0, i + tile0, 0)),
            pl.BlockSpec((1, d), lambda i: (0, 0)),
            pl.BlockSpec((d, 2 * d), lambda i: (0, 0)),
            pl.BlockSpec((1, 2 * d), lambda i: (0, 0)),
        ],
        out_specs=pl.BlockSpec((TM, d), lambda i: (i, 0)),
        out_shape=jax.ShapeDtypeStruct((n - tile0 * TM, d), F32),
        compiler_params=_cparams(("arbitrary",)),
    )(x, mod, g, y, dsk, w, b)


def _s5_params(lam_re, lam_im, log_dt, b_re, b_im, c_re, c_im):
    lam_re = jnp.minimum(lam_re.astype(F32), LAMBDA_RE_MAX)
    lam_im = lam_im.astype(F32)
    dt = jnp.exp(log_dt.astype(F32))[..., None]
    mag = jnp.exp(lam_re * dt)
    abar_re = mag * jnp.cos(lam_im * dt)
    abar_im = mag * jnp.sin(lam_im * dt)
    inv_den = 1.0 / (lam_re * lam_re + lam_im * lam_im)
    gam_re = ((abar_re - 1.0) * lam_re + abar_im * lam_im) * inv_den
    gam_im = (abar_im * lam_re - (abar_re - 1.0) * lam_im) * inv_den
    b_re = b_re.astype(F32)
    b_im = b_im.astype(F32)
    bb_re = gam_re[..., None] * b_re - gam_im[..., None] * b_im
    bb_im = gam_re[..., None] * b_im + gam_im[..., None] * b_re
    g_all = lam_re.shape[1]
    nj = g_all // S5_JBLK
    eye = jnp.eye(S5_JBLK, dtype=F32)

    def in_mat(m):
        m = m.reshape(2, nj, S5_JBLK, S5_STATE, S5_GROUP)
        m = jnp.einsum('djgnp,gh->djgphn', m, eye)
        return m.reshape(2, nj, S5_JBLK * S5_GROUP, S5_JBLK * S5_STATE)

    def out_mat(m):
        m = m.reshape(2, nj, S5_JBLK, S5_GROUP, S5_STATE)
        m = jnp.einsum('djgpn,gh->djgnhp', m, eye)
        return m.reshape(2, nj, S5_JBLK * S5_STATE, S5_JBLK * S5_GROUP)

    bb = jnp.concatenate([in_mat(bb_re), in_mat(bb_im)], axis=-1).astype(BF16)
    cc = jnp.concatenate([out_mat(c_re.astype(F32)), -out_mat(c_im.astype(F32))], axis=-2).astype(BF16)
    half = S5_JBLK * S5_STATE
    ab = jnp.stack([abar_re.reshape(2, nj, half), abar_im.reshape(2, nj, half)], axis=2)
    ab = jnp.broadcast_to(ab[:, :, :, None, :], (2, nj, 2, NB, half))
    return bb, cc, ab


def _cv_glu_kernel(x_ref, mod_ref, g_ref, w_ref, b_ref, o_ref):
    x = x_ref[...]
    d = x.shape[1]
    h = _norm_mod(x, g_ref[...], mod_ref[0, 0], mod_ref[0, 1]).astype(BF16)
    u = _dot(h, w_ref[...]) + b_ref[...]
    o_ref[...] = u[:, :d] * _sigmoid(u[:, d:])


def _cv_glu_call(x, mod, g, w, b, *, nct):
    n, d = x.shape
    nt = n // TM
    return pl.pallas_call(
        _cv_glu_kernel,
        grid=(nt,),
        in_specs=[
            pl.BlockSpec((TM, d), lambda i: (i, 0)),
            pl.BlockSpec((1, 6, NB, d), lambda i: (_is_lat(i, nct), 0, 0, 0)),
            pl.BlockSpec((1, d), lambda i: (0, 0)),
            pl.BlockSpec((d, 2 * d), lambda i: (0, 0)),
            pl.BlockSpec((1, 2 * d), lambda i: (0, 0)),
        ],
        out_specs=pl.BlockSpec((TM, d), lambda i: (i, 0)),
        out_shape=jax.ShapeDtypeStruct((n, d), F32),
        compiler_params=_cparams(("arbitrary",)),
    )(x, mod, g, w, b)


def _cv_conv_kernel(u_ref, up_ref, un_ref, x_ref, mod_ref, dw_ref, dwb_ref, lng_ref, lnb_ref,
                    w_ref, b_ref, o_ref, pad_ref, cv_ref, *, nct):
    i = pl.program_id(0)
    halo = CONV_PAD * NB
    prev_ok = (i > 0) & (i < nct)
    next_ok = i < nct - 1
    pad_ref[0:halo] = jnp.where(prev_ok, up_ref[HALO_ROWS - halo:HALO_ROWS], 0.0)
    pad_ref[halo:halo + TM] = u_ref[...]
    pad_ref[halo + TM:halo + TM + halo] = jnp.where(next_ok, un_ref[0:halo], 0.0)

    rb = 32
    lw = 512
    d = dwb_ref.shape[1]

    def body(c, _):
        r0 = pl.multiple_of(c * rb, rb)
        for l0 in range(0, d, lw):
            bias = jnp.broadcast_to(dwb_ref[:, l0:l0 + lw], (NB, lw))
            acc = [bias] * (rb // NB)
            for k in range(CONV_WIDTH):
                w = dw_ref[k, :, l0:l0 + lw]
                for g in range(rb // NB):
                    acc[g] = acc[g] + w * pad_ref[pl.ds(r0 + (k + g) * NB, NB), l0:l0 + lw]
            for g in range(rb // NB):
                cv_ref[pl.ds(r0 + g * NB, NB), l0:l0 + lw] = acc[g]
        return 0

    lax.fori_loop(0, TM // rb, body, 0)

    c = cv_ref[...]
    mu = jnp.mean(c, axis=-1, keepdims=True)
    cc = c - mu
    var = jnp.mean(cc * cc, axis=-1, keepdims=True)
    y = cc * lax.rsqrt(var + LN_EPS) * lng_ref[...] + lnb_ref[...]
    y = (y * _sigmoid(y)).astype(BF16)
    out = _dot(y, w_ref[...]) + b_ref[...]
    o_ref[...] = _residual(x_ref[...], mod_ref[0, 2], out)


def _cv_conv_call(u, x, mod, dw_w, dw_b, ln_g, ln_b, w, b, *, nct):
    n, d = x.shape
    nt = n // TM
    hb = TM // HALO_ROWS
    nhb = n // HALO_ROWS
    kern = functools.partial(_cv_conv_kernel, nct=nct)
    row = lambda i: (0, 0)
    return pl.pallas_call(
        kern,
        grid=(nt,),
        in_specs=[
            pl.BlockSpec((TM, d), lambda i: (i, 0)),
            pl.BlockSpec((HALO_ROWS, d), lambda i: (jnp.maximum(i * hb - 1, 0), 0)),
            pl.BlockSpec((HALO_ROWS, d), lambda i: (jnp.minimum((i + 1) * hb, nhb - 1), 0)),
            pl.BlockSpec((TM, d), lambda i: (i, 0)),
            pl.BlockSpec((1, 6, NB, d), lambda i: (_is_lat(i, nct), 0, 0, 0)),
            pl.BlockSpec((CONV_WIDTH, NB, d), lambda i: (0, 0, 0)),
            pl.BlockSpec((1, d), row),
            pl.BlockSpec((1, d), row),
            pl.BlockSpec((1, d), row),
            pl.BlockSpec((d, d), row),
            pl.BlockSpec((1, d), row),
        ],
        out_specs=pl.BlockSpec((TM, d), lambda i: (i, 0)),
        out_shape=jax.ShapeDtypeStruct((n, d), F32),
        scratch_shapes=[
            pltpu.VMEM((TM + 2 * CONV_PAD * NB, d), F32),
            pltpu.VMEM((TM, d), F32),
        ],
        compiler_params=_cparams(("arbitrary",)),
    )(u, u, u, x, mod, jnp.broadcast_to(dw_w[:, None, :], (CONV_WIDTH, NB, d)), dw_b, ln_g, ln_b, w, b)


def _head_ones():
    w = 2 * LANES
    r = lax.broadcasted_iota(jnp.int32, (w, w), 0) // RWKV_HEAD
    c = lax.broadcasted_iota(jnp.int32, (w, w), 1) // RWKV_HEAD
    return jnp.where(r == c, 1.0, 0.0).astype(BF16)


def _head_sum_bcast(x, ones_bd):
    hi = x.astype(BF16)
    lo = (x - hi.astype(F32)).astype(BF16)
    w = ones_bd.shape[0]
    cols = [_dot(hi[:, c0:c0 + w], ones_bd) + _dot(lo[:, c0:c0 + w], ones_bd)
            for c0 in range(0, x.shape[1], w)]
    return jnp.concatenate(cols, axis=1)


def _rw_proj_kernel(x_ref, xp_ref, xn_ref, mod_ref, g_ref, mu_ref, wr_ref, wk_ref, wv_ref,
                    w1_ref, w2_ref, w0_ref, a1_ref, a2_ref, a0_ref, g1_ref, g2_ref, kkw_ref,
                    r_ref, k_ref, v_ref, kk_ref, ew_ref, as_ref, gg_ref, hp_ref, *, nct, nt):
    i = pl.program_id(0)
    tm, d = x_ref.shape
    g = g_ref[...]
    sh = mod_ref[0, 0]
    sc = mod_ref[0, 1]
    h = _norm_mod(x_ref[...], g, sh, sc)
    hp = _norm_mod(xp_ref[...], g, sh, sc)
    hn = _norm_mod(xn_ref[...], g, sh, sc)
    prev_ok = (i != 0) & (i != nct)
    next_ok = (i != nct - 1) & (i != nt - 1)
    hp_ref[0:NB] = jnp.where(prev_ok, hp, 0.0)
    hp_ref[NB:NB + tm] = h
    hp_ref[NB + tm:NB + tm + NB] = jnp.where(next_ok, hn, 0.0)
    xx = 0.5 * (hp_ref[0:tm] + hp_ref[2 * NB:2 * NB + tm]) - h

    def mix(j):
        return (h + xx * mu_ref[j]).astype(BF16)

    r = _dot(mix(0), wr_ref[...])
    k = _dot(mix(2), wk_ref[...])
    v = _dot(mix(3), wv_ref[...])
    zw = w0_ref[...] + _dot(jnp.tanh(_dot(mix(1), w1_ref[...])).astype(BF16), w2_ref[...])
    ew = _sigmoid(zw) * math.exp(-0.5)
    asg = _sigmoid(a0_ref[...] + _dot(_dot(mix(4), a1_ref[...]).astype(BF16), a2_ref[...]))
    gg_ref[...] = _dot(_sigmoid(_dot(mix(5), g1_ref[...])).astype(BF16), g2_ref[...])

    kk = k * kkw_ref[...]
    kk = kk * lax.rsqrt(_head_sum_bcast(kk * kk, _head_ones()) + KK_EPS)

    for p in range(d // LANES):
        sl = slice(p * LANES, (p + 1) * LANES)
        r_ref[p] = r[:, sl]
        k_ref[p] = k[:, sl]
        v_ref[p] = v[:, sl]
        kk_ref[p] = kk[:, sl]
        for dr in range(2):
            sl2 = slice(dr * d + p * LANES, dr * d + (p + 1) * LANES)
            ew_ref[dr, p] = ew[:, sl2]
            as_ref[dr, p] = asg[:, sl2]


def _rw_proj_call(x, mod, g, mu, wr, wk, wv, w1, w2, w0, a1, a2, a0, g1, g2, kkw, *, nct):
    n, d = x.shape
    tm = RW_TM
    nt = n // tm
    nct = nct * (TM // tm)
    npair = d // LANES
    tb = tm // NB
    nb8 = n // NB
    kern = functools.partial(_rw_proj_kernel, nct=nct, nt=nt)
    full2 = lambda a: pl.BlockSpec(a.shape, lambda i: (0, 0))
    slab = pl.BlockSpec((npair, tm, LANES), lambda i: (0, i, 0))
    slab2 = pl.BlockSpec((2, npair, tm, LANES), lambda i: (0, 0, i, 0))
    slab_shape = jax.ShapeDtypeStruct((npair, n, LANES), F32)
    slab2_shape = jax.ShapeDtypeStruct((2, npair, n, LANES), F32)
    mu3 = mu.reshape(6, 1, d)
    return pl.pallas_call(
        kern,
        grid=(nt,),
        in_specs=[
            pl.BlockSpec((tm, d), lambda i: (i, 0)),
            pl.BlockSpec((NB, d), lambda i: (jnp.maximum(i * tb - 1, 0), 0)),
            pl.BlockSpec((NB, d), lambda i: (jnp.minimum((i + 1) * tb, nb8 - 1), 0)),
            pl.BlockSpec((1, 6, NB, d), lambda i: (_is_lat(i, nct), 0, 0, 0)),
            pl.BlockSpec((1, d), lambda i: (0, 0)),
            pl.BlockSpec((6, 1, d), lambda i: (0, 0, 0)),
            full2(wr), full2(wk), full2(wv),
            full2(w1), full2(w2), full2(w0), full2(a1), full2(a2), full2(a0),
            full2(g1), full2(g2), full2(kkw),
        ],
        out_specs=[slab, slab, slab, slab, slab2, slab2, pl.BlockSpec((tm, d), lambda i: (i, 0))],
        out_shape=[slab_shape, slab_shape, slab_shape, slab_shape, slab2_shape, slab2_shape,
                   jax.ShapeDtypeStruct((n, d), F32)],
        scratch_shapes=[pltpu.VMEM((tm + 2 * NB, d), F32)],
        compiler_params=_cparams(("arbitrary",)),
    )(x, x, x, mod, g, mu3, wr, wk, wv, w1, w2, w0, a1, a2, a0, g1, g2, kkw)


def _wkv_kernel(r_ref, k_ref, v_ref, kk_ref, ew_ref, as_ref, ka_ref, y_ref, st_ref, *, L, npair):
    d = pl.program_id(0)
    c = pl.program_id(1)

    @pl.when(c == 0)
    def _():
        st_ref[...] = jnp.zeros_like(st_ref)

    sgn = 1 - 2 * d
    l2 = 2 * L
    rr = lax.broadcasted_iota(jnp.int32, (l2, l2), 0)
    cc = lax.broadcasted_iota(jnp.int32, (l2, l2), 1)
    same = (rr >= L) == (cc >= L)
    diff = (rr - cc) * sgn
    strict_f = jnp.where(same & (diff > 0), 1.0, 0.0)
    incl_f = jnp.where(same & (diff >= 0), 1.0, 0.0)
    r1 = lax.broadcasted_iota(jnp.int32, (L, L), 0)
    c1 = lax.broadcasted_iota(jnp.int32, (L, L), 1)
    mcum = jnp.where((r1 - c1) * sgn >= 0, 1.0, 0.0).astype(BF16)
    head0 = lax.broadcasted_iota(jnp.int32, (L, LANES), 1) < RWKV_HEAD
    sr_i = lax.broadcasted_iota(jnp.int32, (LANES, LANES), 0)
    sc_i = lax.broadcasted_iota(jnp.int32, (LANES, LANES), 1)
    blockdiag = (sr_i < RWKV_HEAD) == (sc_i < RWKV_HEAD)
    n_dbl = int(round(math.log2(L)))

    def stack2(x):
        return jnp.concatenate([jnp.where(head0, x, 0.0), jnp.where(head0, 0.0, x)], axis=0)

    def unstack(xs):
        return xs[0:L] + xs[L:l2]

    def load(p, b):
        rows = pl.ds(b, L, stride=NB)
        return (r_ref[p, rows, :], k_ref[p, rows, :], v_ref[p, rows, :], kk_ref[p, rows, :],
                ew_ref[0, p, rows, :], as_ref[0, p, rows, :], st_ref[p * NB + b])

    def inst(ka, r, k, v, kk, ew, asg, s0):
        ldec = -ew
        h1, h2, h3 = _split3(ldec)
        cl3 = _dot(mcum, jnp.concatenate([h1, h2, h3], axis=1))
        yield
        cl = cl3[:, 0:LANES] + cl3[:, LANES:2 * LANES] + cl3[:, 2 * LANES:3 * LANES]
        ce = cl - ldec
        ecl = jnp.exp(cl)
        encl = jnp.exp(-cl)
        etot = jnp.exp(jnp.sum(ldec, axis=0, keepdims=True))
        at = -kk * jnp.exp(ce)
        rt = r * ecl
        bt = kk * asg * encl
        kt = k * (1.0 + (asg - 1.0) * ka) * encl

        ats = stack2(at)
        vsb = stack2(v).astype(BF16)
        lhs = jnp.concatenate([ats, stack2(rt)], axis=0).astype(BF16)
        rhs = jnp.concatenate([stack2(bt), stack2(kt)], axis=0).astype(BF16)
        pm = _dot_nt(lhs, rhs)
        yield
        nab = pm[0:l2, 0:l2] * strict_f
        nak = pm[0:l2, l2:2 * l2] * strict_f
        mrb = (pm[l2:2 * l2, 0:l2] * incl_f).astype(BF16)
        mrk = (pm[l2:2 * l2, l2:2 * l2] * incl_f).astype(BF16)

        xs = jnp.concatenate([ats, _dot(nak.astype(BF16), vsb)], axis=1)
        mkv = _dot(mrk, vsb)
        yield
        npow = nab
        for it in range(n_dbl):
            npb = npow.astype(BF16)
            xs = xs + _dot(npb, xs.astype(BF16))
            if it + 1 < n_dbl:
                npow = _dot(npb, npb)
            yield
        mx = _dot(mrb, xs.astype(BF16))
        at2 = unstack(xs[:, 0:LANES])
        ul = unstack(xs[:, LANES:2 * LANES])
        btb = bt.astype(BF16)
        emat = (jnp.where(blockdiag, _dot(at2.T.astype(BF16), btb), 0.0) * etot).astype(BF16)
        uv = jnp.concatenate([ul, v], axis=0)
        bk = jnp.concatenate([btb, kt.astype(BF16)], axis=0)
        dmat = jnp.where(blockdiag, _dot(uv.T.astype(BF16), bk), 0.0) * etot
        yield
        rt2 = rt + unstack(mx[:, 0:LANES])
        yl = unstack(mx[:, LANES:2 * LANES] + mkv)

        s0b = s0.astype(BF16)
        return _dot_nt(rt2.astype(BF16), s0b) + yl, s0 * etot + _dot(s0b, emat) + dmat

    def run_group(gens):
        outs = [None] * len(gens)
        live = list(enumerate(gens))
        while live:
            nxt = []
            for i, g in live:
                try:
                    next(g)
                    nxt.append((i, g))
                except StopIteration as stop:
                    outs[i] = stop.value
            live = nxt
        return outs

    def pairs(q, carry):
        work = [(q * WKV_PAIRS + j, b) for j in range(WKV_PAIRS) for b in range(NB)]
        outs = run_group([inst(ka_ref[p], *load(p, b)) for p, b in work])
        for (p, b), (y, s1) in zip(work, outs):
            y_ref[0, p, pl.ds(b, L, stride=NB), :] = y
            st_ref[p * NB + b] = s1
        return carry

    lax.fori_loop(0, npair // WKV_PAIRS, pairs, 0)


def _wkv_call(r, k, v, kk, ew, asg, ka, *, nct_rows):
    npair, n, _ = r.shape
    L = WKV_L
    rows = L * NB
    nc = n // rows
    ncc = nct_rows // rows
    kern = functools.partial(_wkv_kernel, L=L, npair=npair)

    def chunk(dd, c):
        bwd = jnp.where(c < ncc, ncc - 1 - c, nc - 1 + ncc - c)
        return jnp.where(dd == 0, c, bwd)

    slab = pl.BlockSpec((npair, rows, LANES), lambda dd, c: (0, chunk(dd, c), 0))
    slab2 = pl.BlockSpec((1, npair, rows, LANES), lambda dd, c: (dd, 0, chunk(dd, c), 0))
    return pl.pallas_call(
        kern,
        grid=(2, nc),
        in_specs=[slab, slab, slab, slab, slab2, slab2,
                  pl.BlockSpec((npair, 1, LANES), lambda dd, c: (0, 0, 0))],
        out_specs=slab2,
        out_shape=jax.ShapeDtypeStruct((2, npair, n, LANES), F32),
        scratch_shapes=[pltpu.VMEM((npair * NB, LANES, LANES), F32)],
        compiler_params=_cparams(("arbitrary", "arbitrary")),
    )(r, k, v, kk, ew, asg, ka)


def _rw_out_kernel(x_ref, mod_ref, y_ref, r_ref, k_ref, v_ref, as_ref, gg_ref, lng_ref, lnb_ref,
                   ka_ref, rk_ref, wo_ref, o_ref):
    x = x_ref[...]
    d = x.shape[1]
    npair = d // LANES
    cat = lambda f: jnp.concatenate([f(p) for p in range(npair)], axis=1)
    y = cat(lambda p: y_ref[0, p] + y_ref[1, p])
    r = cat(lambda p: r_ref[p])
    k = cat(lambda p: k_ref[p])
    v = cat(lambda p: v_ref[p])
    am = cat(lambda p: 0.5 * (as_ref[0, p] + as_ref[1, p]))
    ones_bd = _head_ones()
    inv = 1.0 / RWKV_HEAD
    mean = _head_sum_bcast(y, ones_bd) * inv
    yc = y - mean
    var = _head_sum_bcast(yc * yc, ones_bd) * inv
    yn = yc * lax.rsqrt(var + GN_EPS) * lng_ref[...] + lnb_ref[...]
    k_mean = k * (1.0 + (am - 1.0) * ka_ref[...])
    bonus = _head_sum_bcast(r * k_mean * rk_ref[...], ones_bd) * v
    z = ((yn + bonus) * gg_ref[...]).astype(BF16)
    o_ref[...] = _residual(x, mod_ref[0, 2], _dot(z, wo_ref[...]))


def _rw_out_call(x, mod, y, r, k, v, asg, gg, lng, lnb, ka, rk, wo, *, nct):
    n, d = x.shape
    nt = n // TM
    npair = d // LANES
    slab = pl.BlockSpec((npair, TM, LANES), lambda i: (0, i, 0))
    slab2 = pl.BlockSpec((2, npair, TM, LANES), lambda i: (0, 0, i, 0))
    row = pl.BlockSpec((1, d), lambda i: (0, 0))
    return pl.pallas_call(
        _rw_out_kernel,
        grid=(nt,),
        in_specs=[
            pl.BlockSpec((TM, d), lambda i: (i, 0)),
            pl.BlockSpec((1, 6, NB, d), lambda i: (_is_lat(i, nct), 0, 0, 0)),
            slab2, slab, slab, slab, slab2,
            pl.BlockSpec((TM, d), lambda i: (i, 0)),
            row, row, row, row,
            pl.BlockSpec((d, d), lambda i: (0, 0)),
        ],
        out_specs=pl.BlockSpec((TM, d), lambda i: (i, 0)),
        out_shape=jax.ShapeDtypeStruct((n, d), F32),
        compiler_params=_cparams(("arbitrary",)),
    )(x, mod, y, r, k, v, asg, gg, lng, lnb, ka, rk, wo)


def _final_kernel(x_ref, g_ref, o_ref, slab_ref):
    x = x_ref[...]
    d = x.shape[1]
    ms = jnp.mean(x * x, axis=-1, keepdims=True)
    y = x * lax.rsqrt(ms + RMS_EPS) * g_ref[...]
    for p in range(d // LANES):
        slab_ref[p] = y[:, p * LANES:(p + 1) * LANES]
    for b in range(NB):
        for p in range(d // LANES):
            o_ref[b, :, p * LANES:(p + 1) * LANES] = slab_ref[p, pl.ds(b, TT, stride=NB), :]


def _final_call(x, g, *, tile0):
    n, d = x.shape
    nt = n // TM
    return pl.pallas_call(
        _final_kernel,
        grid=(nt - tile0,),
        in_specs=[pl.BlockSpec((TM, d), lambda i: (i + tile0, 0)),
                  pl.BlockSpec((1, d), lambda i: (0, 0))],
        out_specs=pl.BlockSpec((NB, TT, d), lambda i: (0, i, 0)),
        out_shape=jax.ShapeDtypeStruct((NB, (nt - tile0) * TT, d), F32),
        scratch_shapes=[pltpu.VMEM((d // LANES, TM, LANES), F32)],
        compiler_params=_cparams(("arbitrary",)),
    )(x, g)


def _time_major_kernel(c_ref, x_ref, o_ref, slab_ref, *, nct):
    d = o_ref.shape[1]

    def emit(src_ref):
        for b in range(NB):
            for p in range(d // LANES):
                slab_ref[p, pl.ds(b, TT, stride=NB), :] = src_ref[b, :, p * LANES:(p + 1) * LANES]
        for p in range(d // LANES):
            o_ref[:, p * LANES:(p + 1) * LANES] = slab_ref[p]

    @pl.when(pl.program_id(0) < nct)
    def _():
        emit(c_ref)

    @pl.when(pl.program_id(0) >= nct)
    def _():
        emit(x_ref)


def _time_major_call(ctx, x):
    _, lc, d = ctx.shape
    seq = x.shape[1]
    nct = lc // TT
    nt = (lc + seq) // TT
    return pl.pallas_call(
        functools.partial(_time_major_kernel, nct=nct),
        grid=(nt,),
        in_specs=[pl.BlockSpec((NB, TT, d), lambda i: (0, jnp.minimum(i, nct - 1), 0)),
                  pl.BlockSpec((NB, TT, d), lambda i: (0, jnp.maximum(i - nct, 0), 0))],
        out_specs=pl.BlockSpec((TM, d), lambda i: (i, 0)),
        out_shape=jax.ShapeDtypeStruct((nt * TM, d), F32),
        scratch_shapes=[pltpu.VMEM((d // LANES, TM, LANES), F32)],
        compiler_params=_cparams(("arbitrary",)),
    )(ctx, x)


def _row(v):
    return v.reshape(1, -1).astype(F32)


def _s5_layer(x, mod, g, p, *, nct, tile0):
    bb, cc, ab = _s5_params(p['lam_re'], p['lam_im'], p['log_dt'], p['b_re'], p['b_im'],
                            p['c_re'], p['c_im'])
    y = _s5_scan_call(x, mod, g, bb, cc, ab, nct=nct)
    return _s5_glu_call(x, mod, g, y, _row(p['d']), p['glu_w'].astype(BF16), _row(p['glu_b']),
                        nct=nct, tile0=tile0)


def _conformer_layer(x, mod, g, p, *, nct):
    u = _cv_glu_call(x, mod, g, p['pw1_w'].astype(BF16), _row(p['pw1_b']), nct=nct)
    return _cv_conv_call(u, x, mod, p['dw_w'].astype(F32), _row(p['dw_b']), _row(p['ln_g']),
                         _row(p['ln_b']), p['pw2_w'].astype(BF16), _row(p['pw2_b']), nct=nct)


def _blockdiag2(m):
    z = jnp.zeros_like(m[0])
    return jnp.concatenate([jnp.concatenate([m[0], z], axis=1),
                            jnp.concatenate([z, m[1]], axis=1)], axis=0)


def _rwkv_layer(x, mod, g, p, *, nct):
    n, d = x.shape
    npair = d // LANES
    w1 = jnp.concatenate([p['w1'][0], p['w1'][1]], axis=1).astype(BF16)
    w2 = _blockdiag2(p['w2']).astype(BF16)
    a1 = jnp.concatenate([p['a1'][0], p['a1'][1]], axis=1).astype(BF16)
    a2 = _blockdiag2(p['a2']).astype(BF16)
    gl = p['g1'].shape[1]
    glp = -(-gl // LANES) * LANES
    g1 = jnp.pad(p['g1'], ((0, 0), (0, glp - gl))).astype(BF16)
    g2 = jnp.pad(p['g2'], ((0, glp - gl), (0, 0))).astype(BF16)
    r, k, v, kk, ew, asg, gg = _rw_proj_call(
        x, mod, g, p['mu'].astype(F32), p['w_r'].astype(BF16), p['w_k'].astype(BF16),
        p['w_v'].astype(BF16), w1, w2, _row(p['w0']), a1, a2, _row(p['a0']), g1, g2,
        _row(p['k_k']), nct=nct)
    ka_slab = p['k_a'].astype(F32).reshape(npair, 1, LANES)
    y = _wkv_call(r, k, v, kk, ew, asg, ka_slab, nct_rows=nct * TM)
    return _rw_out_call(x, mod, y, r, k, v, asg, gg, _row(p['lnx_g']), _row(p['lnx_b']),
                        _row(p['k_a']), _row(p['r_k']), p['w_o'].astype(BF16), nct=nct)


def _forward(x, c, ctx, c_ctx, ada_w, ada_b, norm_mix_g, norm_ffn_g, final_g, s5, cv, rw,
             ff_w1, ff_w3, ff_w2, moe_router, moe_w1, moe_w3, moe_w2):
    bsz, seq, d = x.shape
    lc = ctx.shape[1]
    depth = ada_w.shape[0]
    assert bsz == NB and lc % TT == 0 and seq % TT == 0
    assert TT == GRID_W, "one token tile must hold exactly one latent raster row per batch entry"
    nct = lc // TT
    t_all = lc + seq
    n = t_all * NB

    xs = _time_major_call(ctx.astype(F32), x.astype(F32))

    cvec = jnp.zeros((2 * NB, d), F32).at[:NB].set(c).at[NB].set(c_ctx)
    m_all = _ada_call(cvec, ada_w.astype(F32), ada_b.astype(F32))
    m_all = m_all.reshape(depth, 2 * NB, 6, d)
    m_lat = jnp.swapaxes(m_all[:, :NB], 1, 2)
    m_ctx = jnp.broadcast_to(m_all[:, NB][:, :, None, :], (depth, 6, NB, d))
    mods = jnp.stack([m_ctx, m_lat], axis=1)

    ctx_rows = nct * TM
    for i in range(depth):
        last = i == depth - 1
        mod = mods[i]
        g_mix = _row(norm_mix_g[i])
        j = i // 3
        if i % 3 == 0:
            xs = _s5_layer(xs, mod, g_mix, {k_: v_[j] for k_, v_ in s5.items()}, nct=nct,
                           tile0=nct if last else 0)
            ctx_rows = 0 if last else ctx_rows
        elif i % 3 == 1:
            xs = _conformer_layer(xs, mod, g_mix, {k_: v_[j] for k_, v_ in cv.items()}, nct=nct)
        else:
            xs = _rwkv_layer(xs, mod, g_mix, {k_: v_[j] for k_, v_ in rw.items()}, nct=nct)
        fi = i // 2
        g_ffn = _row(norm_ffn_g[i])
        if i % 2 == 0:
            xs = _ffn_call(xs, mod, g_ffn, ff_w1, ff_w3, ff_w2, layer=fi, nct_rows=ctx_rows)
        else:
            xs = _moe_call(xs, mod, g_ffn, moe_router[fi], moe_w1, moe_w3, moe_w2, layer=fi,
                           nct_rows=ctx_rows)

    return _final_call(xs, _row(final_g), tile0=ctx_rows // TM)


def kernel(x, c, ctx, c_ctx, ada_w, ada_b, norm_mix_g, norm_ffn_g, final_g, s5_lambda_re, s5_lambda_im, s5_log_dt, s5_b_re, s5_b_im, s5_c_re, s5_c_im, s5_d, s5_glu_w, s5_glu_b, cv_pw1_w, cv_pw1_b, cv_dw_w, cv_dw_b, cv_ln_g, cv_ln_b, cv_pw2_w, cv_pw2_b, rw_mu, rw_w_r, rw_w_k, rw_w_v, rw_w_o, rw_w0, rw_w1, rw_w2, rw_a0, rw_a1, rw_a2, rw_g1, rw_g2, rw_k_k, rw_k_a, rw_r_k, rw_lnx_g, rw_lnx_b, ff_w1, ff_w3, ff_w2, moe_router, moe_w1, moe_w3, moe_w2):
    s5 = dict(lam_re=s5_lambda_re, lam_im=s5_lambda_im, log_dt=s5_log_dt, b_re=s5_b_re, b_im=s5_b_im,
              c_re=s5_c_re, c_im=s5_c_im, d=s5_d, glu_w=s5_glu_w, glu_b=s5_glu_b)
    cv = dict(pw1_w=cv_pw1_w, pw1_b=cv_pw1_b, dw_w=cv_dw_w, dw_b=cv_dw_b, ln_g=cv_ln_g, ln_b=cv_ln_b,
              pw2_w=cv_pw2_w, pw2_b=cv_pw2_b)
    rw = dict(mu=rw_mu, w_r=rw_w_r, w_k=rw_w_k, w_v=rw_w_v, w_o=rw_w_o, w0=rw_w0, w1=rw_w1, w2=rw_w2,
              a0=rw_a0, a1=rw_a1, a2=rw_a2, g1=rw_g1, g2=rw_g2, k_k=rw_k_k, k_a=rw_k_a, r_k=rw_r_k,
              lnx_g=rw_lnx_g, lnx_b=rw_lnx_b)
    return _forward(x, c, ctx, c_ctx, ada_w, ada_b, norm_mix_g, norm_ffn_g, final_g, s5, cv, rw,
                    ff_w1, ff_w3, ff_w2, moe_router, moe_w1, moe_w3, moe_w2)
```

```python
import functools
import math

import jax
import jax.numpy as jnp
from jax import lax
from jax.experimental import pallas as pl
from jax.experimental.pallas import tpu as pltpu

F32 = jnp.float32
BF16 = jnp.bfloat16

NB = 8
LANES = 128
TM = 512
TT = TM // NB
GRID_W = 64
CONV_WIDTH = 31
CONV_PAD = CONV_WIDTH // 2
HALO_ROWS = 128
S5_GROUP = 16
S5_STATE = 64
S5_JBLK = 8
RWKV_HEAD = 64
WKV_L = 64
WKV_PAIRS = 2
RW_TM = 256
FFN_TM = 1024
MOE_T = 1024
MOE_BLK = 32
MOE_SLOTS = 32
RMS_EPS = 1e-6
LN_EPS = 1e-5
GN_EPS = 64e-5
KK_EPS = 1e-12
LAMBDA_RE_MAX = -1e-4
VMEM_LIMIT = 56 * 1024 * 1024


def _cparams(sem):
    return pltpu.CompilerParams(dimension_semantics=sem, vmem_limit_bytes=VMEM_LIMIT)


def _dot(a, b):
    return jnp.dot(a, b, preferred_element_type=F32)


def _dot_nt(a, b):
    return lax.dot_general(a, b, (((1,), (1,)), ((), ())), preferred_element_type=F32)


def _dot_x3(a, b):
    a1 = a.astype(BF16)
    a2 = (a - a1.astype(F32)).astype(BF16)
    b1 = b.astype(BF16)
    b2 = (b - b1.astype(F32)).astype(BF16)
    return _dot(a1, b1) + _dot(a1, b2) + _dot(a2, b1)


def _sigmoid(x):
    return jax.nn.sigmoid(x)


def _norm_mod(x, g, shift, scale):
    rows, d = x.shape
    ms = jnp.mean(x * x, axis=-1, keepdims=True)
    y = x * lax.rsqrt(ms + RMS_EPS) * g
    y3 = y.reshape(rows // NB, NB, d)
    return (y3 * (1.0 + scale)[None] + shift[None]).reshape(rows, d)


def _residual(x, gate, y):
    rows, d = x.shape
    y3 = y.reshape(rows // NB, NB, d) * gate[None]
    return x + y3.reshape(rows, d)


def _is_lat(i, nct):
    return jnp.where(i >= nct, 1, 0).astype(jnp.int32)


def _ada_kernel(s_ref, w_ref, b_ref, o_ref):
    s = s_ref[...]
    s = s * _sigmoid(s)
    o_ref[0] = _dot(s.astype(BF16), w_ref[0].astype(BF16)) + b_ref[0]


def _ada_call(cvec, ada_w, ada_b):
    depth, d, d6 = ada_w.shape
    tn = 1536
    return pl.pallas_call(
        _ada_kernel,
        grid=(depth, d6 // tn),
        in_specs=[
            pl.BlockSpec((2 * NB, d), lambda l, j: (0, 0)),
            pl.BlockSpec((1, d, tn), lambda l, j: (l, 0, j)),
            pl.BlockSpec((1, 1, tn), lambda l, j: (l, 0, j)),
        ],
        out_specs=pl.BlockSpec((1, 2 * NB, tn), lambda l, j: (l, 0, j)),
        out_shape=jax.ShapeDtypeStruct((depth, 2 * NB, d6), F32),
        compiler_params=_cparams(("arbitrary", "arbitrary")),
    )(cvec, ada_w, ada_b.reshape(depth, 1, d6))


def _ffn_kernel(x_ref, mod_ref, g_ref, w1_ref, w3_ref, w2_ref, o_ref, hb_ref, acc_ref, *, n_k):
    k = pl.program_id(1)

    @pl.when(k == 0)
    def _():
        h = _norm_mod(x_ref[...], g_ref[...], mod_ref[0, 3], mod_ref[0, 4])
        hb_ref[...] = h.astype(BF16)
        acc_ref[...] = jnp.zeros_like(acc_ref)

    hb = hb_ref[...]
    a = _dot(hb, w1_ref[0].astype(BF16))
    b = _dot(hb, w3_ref[0].astype(BF16))
    acc_ref[...] += _dot((a * _sigmoid(a) * b).astype(BF16), w2_ref[0].astype(BF16))

    @pl.when(k == n_k - 1)
    def _():
        o_ref[...] = _residual(x_ref[...], mod_ref[0, 5], acc_ref[...])


def _ffn_call(x, mod, g, w1, w3, w2, *, layer, nct_rows):
    n, d = x.shape
    f = w1.shape[2]
    tm = FFN_TM
    tf = 512
    n_k = f // tf
    nct = nct_rows // tm
    return pl.pallas_call(
        functools.partial(_ffn_kernel, n_k=n_k),
        grid=(n // tm, n_k),
        in_specs=[
            pl.BlockSpec((tm, d), lambda i, k: (i, 0)),
            pl.BlockSpec((1, 6, NB, d), lambda i, k: (_is_lat(i, nct), 0, 0, 0)),
            pl.BlockSpec((1, d), lambda i, k: (0, 0)),
            pl.BlockSpec((1, d, tf), lambda i, k: (layer, 0, k)),
            pl.BlockSpec((1, d, tf), lambda i, k: (layer, 0, k)),
            pl.BlockSpec((1, tf, d), lambda i, k: (layer, k, 0)),
        ],
        out_specs=pl.BlockSpec((tm, d), lambda i, k: (i, 0)),
        out_shape=jax.ShapeDtypeStruct((n, d), F32),
        scratch_shapes=[pltpu.VMEM((tm, d), BF16), pltpu.VMEM((tm, d), F32)],
        compiler_params=_cparams(("arbitrary", "arbitrary")),
    )(x, mod, g, w1, w3, w2)


def _top2(logits, n_e):
    lane = lax.broadcasted_iota(jnp.int32, logits.shape, 1).astype(F32)
    neg = jnp.float32(-3.0e38)
    lg = jnp.where(lane < n_e, logits, neg)
    m1 = jnp.max(lg, axis=-1, keepdims=True)
    i1 = jnp.min(jnp.where(lg == m1, lane, 1.0e9), axis=-1, keepdims=True)
    lg2 = jnp.where(lane == i1, neg, lg)
    m2 = jnp.max(lg2, axis=-1, keepdims=True)
    i2 = jnp.min(jnp.where(lg2 == m2, lane, 1.0e9), axis=-1, keepdims=True)
    e2 = jnp.exp(m2 - m1)
    den = 1.0 + e2
    return lane, i1, i2, 1.0 / den, e2 / den


def _moe_route_kernel(x_ref, mod_ref, g_ref, router_ref, ltri_ref, z_ref, dest_ref, cnt_ref,
                      hb_ref, *, n_e, cap):
    t, d = x_ref.shape
    h = _norm_mod(x_ref[...], g_ref[...], mod_ref[0, 3], mod_ref[0, 4])
    hb_ref[...] = h.astype(BF16)
    lane, i1, i2, p1, p2 = _top2(_dot_x3(h, router_ref[...]), n_e)
    sel1 = lane == i1
    sel2 = lane == i2
    c = jnp.where(sel1, 1.0, 0.0) + jnp.where(sel2, 1.0, 0.0)
    rank = _dot(ltri_ref[...], c.astype(BF16))
    cnt = jnp.sum(c, axis=0, keepdims=True)
    seg = jnp.floor((cnt + (MOE_BLK - 1)) * (1.0 / MOE_BLK)) * MOE_BLK
    ur = lax.broadcasted_iota(jnp.int32, (LANES, LANES), 0)
    uc = lax.broadcasted_iota(jnp.int32, (LANES, LANES), 1)
    before = jnp.where(ur < uc, 1.0, 0.0).astype(BF16)
    start = _dot(jnp.broadcast_to(seg, (NB, LANES)).astype(BF16), before)[0:1]
    pos = start + rank
    d1 = jnp.sum(jnp.where(sel1, pos, 0.0), axis=-1, keepdims=True)
    d2 = jnp.sum(jnp.where(sel2, pos, 0.0), axis=-1, keepdims=True)
    packed = (jnp.where(lane == 0.0, d1, 0.0) + jnp.where(lane == 1.0, d2, 0.0)
              + jnp.where(lane == 2.0, p1, 0.0) + jnp.where(lane == 3.0, p2, 0.0))
    dest_ref[...] = packed
    cnt_ref[0] = jnp.broadcast_to(cnt, (NB, LANES))
    packed_t = packed.T
    d1r = packed_t[0:1]
    d2r = packed_t[1:2]
    p1r = packed_t[2:3]
    p2r = packed_t[3:4]
    rc = 256

    def chunk(ci, carry):
        r0 = pl.multiple_of(ci * rc, rc)
        row = (lax.broadcasted_iota(jnp.int32, (rc, t), 0) + r0).astype(F32)
        m1 = row == d1r
        m2 = row == d2r
        perm = (jnp.where(m1, 1.0, 0.0) + jnp.where(m2, 1.0, 0.0)).astype(BF16)
        zc = _dot(perm, hb_ref[...])
        gate = jnp.sum(jnp.where(m1, p1r, 0.0) + jnp.where(m2, p2r, 0.0), axis=-1, keepdims=True)
        ghi = gate.astype(BF16).astype(F32)
        glo = gate - ghi
        le = lax.broadcasted_iota(jnp.int32, (rc, LANES), 1)
        ext = jnp.where(le == 0, ghi, jnp.where(le == 1, glo, 0.0))
        z_ref[pl.ds(r0, rc), 0:d] = zc.astype(BF16)
        z_ref[pl.ds(r0, rc), d:d + LANES] = ext.astype(BF16)
        return carry

    lax.fori_loop(0, cap // rc, chunk, 0)


def _moe_experts_kernel(tin_ref, se_ref, nv_ref, *refs, n_k, d):
    z_refs = refs[:MOE_SLOTS]
    w1_ref, w3_ref, w2_ref, o_ref, xs_ref, acc_ref = refs[MOE_SLOTS:]
    s = pl.program_id(0)
    k = pl.program_id(1)
    nv = nv_ref[s]

    @pl.when((nv > 0) & (k == 0))
    def _():
        for j in range(MOE_SLOTS):
            xs_ref[j * MOE_BLK:(j + 1) * MOE_BLK] = z_refs[j][...]
        acc_ref[...] = jnp.zeros_like(acc_ref)

    def compute(nrows):
        hb = xs_ref[0:nrows, 0:d]
        ext = xs_ref[0:nrows, d:d + LANES].astype(F32)
        le = lax.broadcasted_iota(jnp.int32, ext.shape, 1)
        gate = jnp.sum(jnp.where(le < 2, ext, 0.0), axis=-1, keepdims=True)
        a = _dot(hb, w1_ref[0, 0].astype(BF16))
        b = _dot(hb, w3_ref[0, 0].astype(BF16))
        act = a * _sigmoid(a) * b * gate
        acc_ref[0:nrows] += _dot(act.astype(BF16), w2_ref[0, 0].astype(BF16))

    quarter = MOE_SLOTS // 4
    for q in range(1, 5):
        @pl.when((nv > (q - 1) * quarter) & (nv <= q * quarter))
        def _():
            compute(q * quarter * MOE_BLK)

    @pl.when(k == n_k - 1)
    def _():
        o_ref[...] = jnp.where(nv > 0, acc_ref[...], 0.0).astype(BF16)


def _moe_combine_kernel(tc_ref, *refs, nblk, cap):
    zb_refs = refs[:nblk]
    dest_ref, x_ref, mod_ref, o_ref, zbuf_ref = refs[nblk:]
    t, d = x_ref.shape
    for j in range(nblk):
        zbuf_ref[j * MOE_BLK:(j + 1) * MOE_BLK] = zb_refs[j][...]
    packed = dest_ref[...]
    lane = lax.broadcasted_iota(jnp.int32, packed.shape, 1)
    d1 = jnp.sum(jnp.where(lane == 0, packed, 0.0), axis=-1, keepdims=True)
    d2 = jnp.sum(jnp.where(lane == 1, packed, 0.0), axis=-1, keepdims=True)
    kc = 256
    acc = jnp.zeros((t, d), F32)
    for c0 in range(0, cap, kc):
        col = (lax.broadcasted_iota(jnp.int32, (t, kc), 1) + c0).astype(F32)
        sel = (jnp.where(col == d1, 1.0, 0.0) + jnp.where(col == d2, 1.0, 0.0)).astype(BF16)
        acc = acc + _dot(sel, zbuf_ref[c0:c0 + kc])
    o_ref[...] = _residual(x_ref[...], mod_ref[0, 5], acc)


def _moe_tables(cnt, *, n_tiles, n_e, nblk, n_steps):
    i32 = jnp.int32
    e_ids = jnp.arange(n_e, dtype=i32)
    t_ids = jnp.arange(n_tiles, dtype=i32)
    pc = (cnt + (MOE_BLK - 1)) // MOE_BLK
    lb = jnp.cumsum(pc, axis=1) - pc
    cum = jnp.cumsum(pc, axis=0)
    cum_ex = cum - pc
    nb_e = cum[-1]
    steps_e = (nb_e + (MOE_SLOTS - 1)) // MOE_SLOTS
    ends = jnp.cumsum(steps_e)
    step0 = ends - steps_e
    sidx = jnp.arange(n_steps, dtype=i32)
    step_e = jnp.minimum(jnp.sum((sidx[:, None] >= ends[None, :]).astype(i32), axis=1), n_e - 1)
    oh_e = (step_e[:, None] == e_ids[None, :]).astype(i32)
    by_step = lambda v: jnp.sum(oh_e * v[None, :], axis=1)
    by_step_tile = lambda m: jnp.sum(oh_e[:, None, :] * m[None, :, :], axis=2)
    rank = ((sidx - by_step(step0))[:, None] * MOE_SLOTS
            + jnp.arange(MOE_SLOTS, dtype=i32)[None, :])
    cum_s = by_step_tile(cum)
    tile = jnp.sum((cum_s[:, None, :] <= rank[:, :, None]).astype(i32), axis=2)
    nvalid = jnp.where(sidx < ends[-1], jnp.clip(by_step(nb_e) - rank[:, 0], 0, MOE_SLOTS), 0)
    ok = rank < (rank[:, :1] + nvalid[:, None])
    oh_t = (jnp.minimum(tile, n_tiles - 1)[:, :, None] == t_ids[None, None, :]).astype(i32)
    at_tile = lambda m: jnp.sum(oh_t * by_step_tile(m)[:, None, :], axis=2)
    src = jnp.sum(oh_t * t_ids, axis=2) * nblk + at_tile(lb) + rank - at_tile(cum_ex)
    tin = jnp.where(ok, src, 0).reshape(-1).astype(i32)
    lblk = jnp.arange(nblk, dtype=i32)[None, :]
    seg_end = lb + pc
    e_l = jnp.minimum(jnp.sum((seg_end[:, None, :] <= lblk[:, :, None]).astype(i32), axis=2), n_e - 1)
    oh_l = (e_l[:, :, None] == e_ids[None, None, :]).astype(i32)
    r_l = jnp.sum(oh_l * (cum_ex - lb)[:, None, :], axis=2) + lblk
    flat = (jnp.sum(oh_l * step0, axis=2) + r_l // MOE_SLOTS) * MOE_SLOTS + r_l % MOE_SLOTS
    tcomb = jnp.where(lblk < seg_end[:, -1:], flat, 0).reshape(-1).astype(i32)
    return tin, tcomb, step_e.astype(i32), nvalid.astype(i32)


def _moe_call(x, mod, g, router, w1, w3, w2, *, layer, nct_rows):
    n, d = x.shape
    _, n_e, _, f = w1.shape
    t = MOE_T
    assert n % t == 0 and nct_rows % t == 0
    n_tiles = n // t
    nct = nct_rows // t
    cap = 2 * t + n_e * MOE_BLK
    nblk = cap // MOE_BLK
    rows = MOE_SLOTS * MOE_BLK
    n_steps = -(-(n_tiles * nblk) // MOE_SLOTS) + n_e
    tf = 512
    n_k = f // tf
    router_p = jnp.pad(router.astype(F32), ((0, 0), (0, LANES - n_e)))
    ltri = jnp.tril(jnp.ones((t, t), F32), k=-1).astype(BF16)
    mod_spec = lambda off: pl.BlockSpec((1, 6, NB, d), lambda i, *_: (_is_lat(i, nct), 0, 0, 0))

    z, dest, cnt = pl.pallas_call(
        functools.partial(_moe_route_kernel, n_e=n_e, cap=cap),
        grid=(n_tiles,),
        in_specs=[
            pl.BlockSpec((t, d), lambda i: (i, 0)),
            mod_spec(0),
            pl.BlockSpec((1, d), lambda i: (0, 0)),
            pl.BlockSpec((d, LANES), lambda i: (0, 0)),
            pl.BlockSpec((t, t), lambda i: (0, 0)),
        ],
        out_specs=[
            pl.BlockSpec((cap, d + LANES), lambda i: (i, 0)),
            pl.BlockSpec((t, LANES), lambda i: (i, 0)),
            pl.BlockSpec((1, NB, LANES), lambda i: (i, 0, 0)),
        ],
        out_shape=[
            jax.ShapeDtypeStruct((n_tiles * cap, d + LANES), BF16),
            jax.ShapeDtypeStruct((n, LANES), F32),
            jax.ShapeDtypeStruct((n_tiles, NB, LANES), F32),
        ],
        scratch_shapes=[pltpu.VMEM((t, d), BF16)],
        compiler_params=_cparams(("arbitrary",)),
    )(x, mod, g, router_p, ltri)

    tin, tcomb, step_e, nvalid = _moe_tables(cnt[:, 0, :n_e].astype(jnp.int32), n_tiles=n_tiles,
                                            n_e=n_e, nblk=nblk, n_steps=n_steps)

    def kk(s, k, nv_r):
        return jnp.where(nv_r[s] > 0, k, n_k - 1)

    def z_spec(j):
        return pl.BlockSpec((MOE_BLK, d + LANES), lambda s, k, tin_r, se_r, nu_r: (tin_r[s * MOE_SLOTS + j], 0))

    zo = pl.pallas_call(
        functools.partial(_moe_experts_kernel, n_k=n_k, d=d),
        grid_spec=pltpu.PrefetchScalarGridSpec(
            num_scalar_prefetch=3,
            grid=(n_steps, n_k),
            in_specs=[z_spec(j) for j in range(MOE_SLOTS)] + [
                pl.BlockSpec((1, 1, d, tf), lambda s, k, tin_r, se_r, nu_r: (layer, se_r[s], 0, kk(s, k, nu_r))),
                pl.BlockSpec((1, 1, d, tf), lambda s, k, tin_r, se_r, nu_r: (layer, se_r[s], 0, kk(s, k, nu_r))),
                pl.BlockSpec((1, 1, tf, d), lambda s, k, tin_r, se_r, nu_r: (layer, se_r[s], kk(s, k, nu_r), 0)),
            ],
            out_specs=pl.BlockSpec((rows, d), lambda s, k, tin_r, se_r, nu_r: (s, 0)),
            scratch_shapes=[pltpu.VMEM((rows, d + LANES), BF16), pltpu.VMEM((rows, d), F32)],
        ),
        out_shape=jax.ShapeDtypeStruct((n_steps * rows, d), BF16),
        compiler_params=_cparams(("arbitrary", "arbitrary")),
    )(tin, step_e, nvalid, *([z] * MOE_SLOTS), w1, w3, w2)

    def zo_spec(j):
        return pl.BlockSpec((MOE_BLK, d), lambda i, tc_r: (tc_r[i * nblk + j], 0))

    return pl.pallas_call(
        functools.partial(_moe_combine_kernel, nblk=nblk, cap=cap),
        grid_spec=pltpu.PrefetchScalarGridSpec(
            num_scalar_prefetch=1,
            grid=(n_tiles,),
            in_specs=[zo_spec(j) for j in range(nblk)] + [
                pl.BlockSpec((t, LANES), lambda i, tc_r: (i, 0)),
                pl.BlockSpec((t, d), lambda i, tc_r: (i, 0)),
                mod_spec(0),
            ],
            out_specs=pl.BlockSpec((t, d), lambda i, tc_r: (i, 0)),
            scratch_shapes=[pltpu.VMEM((cap, d), BF16)],
        ),
        out_shape=jax.ShapeDtypeStruct((n, d), F32),
        compiler_params=_cparams(("arbitrary",)),
    )(tcomb, *([zo] * nblk), dest, x, mod)


def _dir_tile(d, s, nct, nt):
    bwd = jnp.where(s < nct, nct - 1 - s, nt - 1 + nct - s)
    return jnp.where(d == 0, s, bwd)


def _s5_scan_kernel(x_ref, mod_ref, g_ref, bb_ref, cc_ref, ab_ref, y_ref,
                    hb_ref, bu0_ref, bu1_ref, bu2_ref, st_ref):
    d = pl.program_id(0)
    s = pl.program_id(1)
    half = S5_JBLK * S5_STATE
    bufs = (bu0_ref, bu1_ref, bu2_ref)

    @pl.when(s == 0)
    def _():
        st_ref[...] = jnp.zeros_like(st_ref)

    h = _norm_mod(x_ref[...], g_ref[...], mod_ref[0, 0], mod_ref[0, 1])
    hb_ref[...] = h.astype(BF16)

    def project(j):
        bufs[j % 3][...] = _dot(hb_ref[:, j * LANES:(j + 1) * LANES], bb_ref[0, j])

    def scan(j):
        buf = bufs[j % 3]
        ar = ab_ref[0, j, 0]
        ai = ab_ref[0, j, 1]
        re = st_ref[j, 0]
        im = st_ref[j, 1]
        for t in range(TT):
            tt = t + d * (TT - 1 - 2 * t)
            r0 = pl.multiple_of(tt * NB, NB)
            bre = buf[pl.ds(r0, NB), 0:half]
            bim = buf[pl.ds(r0, NB), half:2 * half]
            re, im = ar * re - ai * im + bre, ar * im + ai * re + bim
            buf[pl.ds(r0, NB), 0:half] = re
            buf[pl.ds(r0, NB), half:2 * half] = im
        st_ref[j, 0] = re
        st_ref[j, 1] = im

    def readout(j):
        y_ref[0, j] = _dot(bufs[j % 3][...].astype(BF16), cc_ref[0, j])

    for i in range(S5_JBLK + 2):
        if i < S5_JBLK:
            project(i)
        if 0 <= i - 1 < S5_JBLK:
            scan(i - 1)
        if 0 <= i - 2 < S5_JBLK:
            readout(i - 2)


def _s5_scan_call(x, mod, g, bb, cc, ab, *, nct):
    n, d = x.shape
    nt = n // TM
    half = S5_JBLK * S5_STATE
    tile = lambda dd, s: _dir_tile(dd, s, nct, nt)
    return pl.pallas_call(
        _s5_scan_kernel,
        grid=(2, nt),
        in_specs=[
            pl.BlockSpec((TM, d), lambda dd, s: (tile(dd, s), 0)),
            pl.BlockSpec((1, 6, NB, d), lambda dd, s: (_is_lat(tile(dd, s), nct), 0, 0, 0)),
            pl.BlockSpec((1, d), lambda dd, s: (0, 0)),
            pl.BlockSpec((1, S5_JBLK, LANES, 2 * half), lambda dd, s: (dd, 0, 0, 0)),
            pl.BlockSpec((1, S5_JBLK, 2 * half, LANES), lambda dd, s: (dd, 0, 0, 0)),
            pl.BlockSpec((1, S5_JBLK, 2, NB, half), lambda dd, s: (dd, 0, 0, 0, 0)),
        ],
        out_specs=pl.BlockSpec((1, S5_JBLK, TM, LANES), lambda dd, s: (dd, 0, tile(dd, s), 0)),
        out_shape=jax.ShapeDtypeStruct((2, S5_JBLK, n, LANES), F32),
        scratch_shapes=[
            pltpu.VMEM((TM, d), BF16),
            pltpu.VMEM((TM, 2 * half), F32),
            pltpu.VMEM((TM, 2 * half), F32),
            pltpu.VMEM((TM, 2 * half), F32),
            pltpu.VMEM((S5_JBLK, 2, NB, half), F32),
        ],
        compiler_params=_cparams(("arbitrary", "arbitrary")),
    )(x, mod, g, bb, cc, ab)


def _gelu_tanh(x):
    c = math.sqrt(2.0 / math.pi)
    return 0.5 * x * (1.0 + jnp.tanh(c * (x + 0.044715 * (x * x * x))))


def _s5_glu_kernel(x_ref, mod_ref, g_ref, y_ref, dsk_ref, w_ref, b_ref, o_ref):
    x = x_ref[...]
    d = x.shape[1]
    h = _norm_mod(x, g_ref[...], mod_ref[0, 0], mod_ref[0, 1])
    y = jnp.concatenate([y_ref[0, j] + y_ref[1, j] for j in range(S5_JBLK)], axis=1)
    z = _gelu_tanh(y + dsk_ref[...] * h).astype(BF16)
    zz = _dot(z, w_ref[...]) + b_ref[...]
    out = zz[:, :d] * _sigmoid(zz[:, d:])
    o_ref[...] = _residual(x, mod_ref[0, 2], out)


def _s5_glu_call(x, mod, g, y, dsk, w, b, *, nct, tile0):
    n, d = x.shape
    nt = n // TM
    return pl.pallas_call(
        _s5_glu_kernel,
        grid=(nt - tile0,),
        in_specs=[
            pl.BlockSpec((TM, d), lambda i: (i + tile0, 0)),
            pl.BlockSpec((1, 6, NB, d), lambda i: (_is_lat(i + tile0, nct), 0, 0, 0)),
            pl.BlockSpec((1, d), lambda i: (0, 0)),
            pl.BlockSpec((2, S5_JBLK, TM, LANES), lambda i: (0, 0, i + tile0, 0)),
            pl.BlockSpec((1, d), lambda i: (0, 0)),
            pl.BlockSpec((d, 2 * d), lambda i: (0, 0)),
            pl.BlockSpec((1, 2 * d), lambda i: (0, 0)),
        ],
        out_specs=pl.BlockSpec((TM, d), lambda i: (i, 0)),
        out_shape=jax.ShapeDtypeStruct((n - tile0 * TM, d), F32),
        compiler_params=_cparams(("arbitrary",)),
    )(x, mod, g, y, dsk, w, b)


def _s5_params(lam_re, lam_im, log_dt, b_re, b_im, c_re, c_im):
    lam_re = jnp.minimum(lam_re.astype(F32), LAMBDA_RE_MAX)
    lam_im = lam_im.astype(F32)
    dt = jnp.exp(log_dt.astype(F32))[..., None]
    mag = jnp.exp(lam_re * dt)
    abar_re = mag * jnp.cos(lam_im * dt)
    abar_im = mag * jnp.sin(lam_im * dt)
    inv_den = 1.0 / (lam_re * lam_re + lam_im * lam_im)
    gam_re = ((abar_re - 1.0) * lam_re + abar_im * lam_im) * inv_den
    gam_im = (abar_im * lam_re - (abar_re - 1.0) * lam_im) * inv_den
    b_re = b_re.astype(F32)
    b_im = b_im.astype(F32)
    bb_re = gam_re[..., None] * b_re - gam_im[..., None] * b_im
    bb_im = gam_re[..., None] * b_im + gam_im[..., None] * b_re
    g_all = lam_re.shape[1]
    nj = g_all // S5_JBLK
    eye = jnp.eye(S5_JBLK, dtype=F32)

    def in_mat(m):
        m = m.reshape(2, nj, S5_JBLK, S5_STATE, S5_GROUP)
        m = jnp.einsum('djgnp,gh->djgphn', m, eye)
        return m.reshape(2, nj, S5_JBLK * S5_GROUP, S5_JBLK * S5_STATE)

    def out_mat(m):
        m = m.reshape(2, nj, S5_JBLK, S5_GROUP, S5_STATE)
        m = jnp.einsum('djgpn,gh->djgnhp', m, eye)
        return m.reshape(2, nj, S5_JBLK * S5_STATE, S5_JBLK * S5_GROUP)

    bb = jnp.concatenate([in_mat(bb_re), in_mat(bb_im)], axis=-1).astype(BF16)
    cc = jnp.concatenate([out_mat(c_re.astype(F32)), -out_mat(c_im.astype(F32))], axis=-2).astype(BF16)
    half = S5_JBLK * S5_STATE
    ab = jnp.stack([abar_re.reshape(2, nj, half), abar_im.reshape(2, nj, half)], axis=2)
    ab = jnp.broadcast_to(ab[:, :, :, None, :], (2, nj, 2, NB, half))
    return bb, cc, ab


def _cv_glu_kernel(x_ref, mod_ref, g_ref, w_ref, b_ref, o_ref):
    x = x_ref[...]
    d = x.shape[1]
    h = _norm_mod(x, g_ref[...], mod_ref[0, 0], mod_ref[0, 1]).astype(BF16)
    u = _dot(h, w_ref[...]) + b_ref[...]
    o_ref[...] = u[:, :d] * _sigmoid(u[:, d:])


def _cv_glu_call(x, mod, g, w, b, *, nct):
    n, d = x.shape
    nt = n // TM
    return pl.pallas_call(
        _cv_glu_kernel,
        grid=(nt,),
        in_specs=[
            pl.BlockSpec((TM, d), lambda i: (i, 0)),
            pl.BlockSpec((1, 6, NB, d), lambda i: (_is_lat(i, nct), 0, 0, 0)),
            pl.BlockSpec((1, d), lambda i: (0, 0)),
            pl.BlockSpec((d, 2 * d), lambda i: (0, 0)),
            pl.BlockSpec((1, 2 * d), lambda i: (0, 0)),
        ],
        out_specs=pl.BlockSpec((TM, d), lambda i: (i, 0)),
        out_shape=jax.ShapeDtypeStruct((n, d), F32),
        compiler_params=_cparams(("arbitrary",)),
    )(x, mod, g, w, b)


def _cv_conv_kernel(u_ref, up_ref, un_ref, x_ref, mod_ref, dw_ref, dwb_ref, lng_ref, lnb_ref,
                    w_ref, b_ref, o_ref, pad_ref, cv_ref, *, nct):
    i = pl.program_id(0)
    halo = CONV_PAD * NB
    prev_ok = (i > 0) & (i < nct)
    next_ok = i < nct - 1
    pad_ref[0:halo] = jnp.where(prev_ok, up_ref[HALO_ROWS - halo:HALO_ROWS], 0.0)
    pad_ref[halo:halo + TM] = u_ref[...]
    pad_ref[halo + TM:halo + TM + halo] = jnp.where(next_ok, un_ref[0:halo], 0.0)

    rb = 32
    lw = 512
    d = dwb_ref.shape[1]

    def body(c, _):
        r0 = pl.multiple_of(c * rb, rb)
        for l0 in range(0, d, lw):
            bias = jnp.broadcast_to(dwb_ref[:, l0:l0 + lw], (NB, lw))
            acc = [bias] * (rb // NB)
            for k in range(CONV_WIDTH):
                w = dw_ref[k, :, l0:l0 + lw]
                for g in range(rb // NB):
                    acc[g] = acc[g] + w * pad_ref[pl.ds(r0 + (k + g) * NB, NB), l0:l0 + lw]
            for g in range(rb // NB):
                cv_ref[pl.ds(r0 + g * NB, NB), l0:l0 + lw] = acc[g]
        return 0

    lax.fori_loop(0, TM // rb, body, 0)

    c = cv_ref[...]
    mu = jnp.mean(c, axis=-1, keepdims=True)
    cc = c - mu
    var = jnp.mean(cc * cc, axis=-1, keepdims=True)
    y = cc * lax.rsqrt(var + LN_EPS) * lng_ref[...] + lnb_ref[...]
    y = (y * _sigmoid(y)).astype(BF16)
    out = _dot(y, w_ref[...]) + b_ref[...]
    o_ref[...] = _residual(x_ref[...], mod_ref[0, 2], out)


def _cv_conv_call(u, x, mod, dw_w, dw_b, ln_g, ln_b, w, b, *, nct):
    n, d = x.shape
    nt = n // TM
    hb = TM // HALO_ROWS
    nhb = n // HALO_ROWS
    kern = functools.partial(_cv_conv_kernel, nct=nct)
    row = lambda i: (0, 0)
    return pl.pallas_call(
        kern,
        grid=(nt,),
        in_specs=[
            pl.BlockSpec((TM, d), lambda i: (i, 0)),
            pl.BlockSpec((HALO_ROWS, d), lambda i: (jnp.maximum(i * hb - 1, 0), 0)),
            pl.BlockSpec((HALO_ROWS, d), lambda i: (jnp.minimum((i + 1) * hb, nhb - 1), 0)),
            pl.BlockSpec((TM, d), lambda i: (i, 0)),
            pl.BlockSpec((1, 6, NB, d), lambda i: (_is_lat(i, nct), 0, 0, 0)),
            pl.BlockSpec((CONV_WIDTH, NB, d), lambda i: (0, 0, 0)),
            pl.BlockSpec((1, d), row),
            pl.BlockSpec((1, d), row),
            pl.BlockSpec((1, d), row),
            pl.BlockSpec((d, d), row),
            pl.BlockSpec((1, d), row),
        ],
        out_specs=pl.BlockSpec((TM, d), lambda i: (i, 0)),
        out_shape=jax.ShapeDtypeStruct((n, d), F32),
        scratch_shapes=[
            pltpu.VMEM((TM + 2 * CONV_PAD * NB, d), F32),
            pltpu.VMEM((TM, d), F32),
        ],
        compiler_params=_cparams(("arbitrary",)),
    )(u, u, u, x, mod, jnp.broadcast_to(dw_w[:, None, :], (CONV_WIDTH, NB, d)), dw_b, ln_g, ln_b, w, b)


def _head_ones():
    w = 2 * LANES
    r = lax.broadcasted_iota(jnp.int32, (w, w), 0) // RWKV_HEAD
    c = lax.broadcasted_iota(jnp.int32, (w, w), 1) // RWKV_HEAD
    return jnp.where(r == c, 1.0, 0.0).astype(BF16)


def _head_sum_bcast(x, ones_bd):
    hi = x.astype(BF16)
    lo = (x - hi.astype(F32)).astype(BF16)
    w = ones_bd.shape[0]
    cols = [_dot(hi[:, c0:c0 + w], ones_bd) + _dot(lo[:, c0:c0 + w], ones_bd)
            for c0 in range(0, x.shape[1], w)]
    return jnp.concatenate(cols, axis=1)


def _rw_proj_kernel(x_ref, xp_ref, xn_ref, mod_ref, g_ref, mu_ref, wr_ref, wk_ref, wv_ref,
                    w1_ref, w2_ref, w0_ref, a1_ref, a2_ref, a0_ref, g1_ref, g2_ref, kkw_ref,
                    r_ref, k_ref, v_ref, kk_ref, ew_ref, as_ref, gg_ref, hp_ref, *, nct, nt):
    i = pl.program_id(0)
    tm, d = x_ref.shape
    g = g_ref[...]
    sh = mod_ref[0, 0]
    sc = mod_ref[0, 1]
    h = _norm_mod(x_ref[...], g, sh, sc)
    hp = _norm_mod(xp_ref[...], g, sh, sc)
    hn = _norm_mod(xn_ref[...], g, sh, sc)
    prev_ok = (i != 0) & (i != nct)
    next_ok = (i != nct - 1) & (i != nt - 1)
    hp_ref[0:NB] = jnp.where(prev_ok, hp, 0.0)
    hp_ref[NB:NB + tm] = h
    hp_ref[NB + tm:NB + tm + NB] = jnp.where(next_ok, hn, 0.0)
    xx = 0.5 * (hp_ref[0:tm] + hp_ref[2 * NB:2 * NB + tm]) - h

    def mix(j):
        return (h + xx * mu_ref[j]).astype(BF16)

    r = _dot(mix(0), wr_ref[...])
    k = _dot(mix(2), wk_ref[...])
    v = _dot(mix(3), wv_ref[...])
    zw = w0_ref[...] + _dot(jnp.tanh(_dot(mix(1), w1_ref[...])).astype(BF16), w2_ref[...])
    ew = _sigmoid(zw) * math.exp(-0.5)
    asg = _sigmoid(a0_ref[...] + _dot(_dot(mix(4), a1_ref[...]).astype(BF16), a2_ref[...]))
    gg_ref[...] = _dot(_sigmoid(_dot(mix(5), g1_ref[...])).astype(BF16), g2_ref[...])

    kk = k * kkw_ref[...]
    kk = kk * lax.rsqrt(_head_sum_bcast(kk * kk, _head_ones()) + KK_EPS)

    for p in range(d // LANES):
        sl = slice(p * LANES, (p + 1) * LANES)
        r_ref[p] = r[:, sl]
        k_ref[p] = k[:, sl]
        v_ref[p] = v[:, sl]
        kk_ref[p] = kk[:, sl]
        for dr in range(2):
            sl2 = slice(dr * d + p * LANES, dr * d + (p + 1) * LANES)
            ew_ref[dr, p] = ew[:, sl2]
            as_ref[dr, p] = asg[:, sl2]


def _rw_proj_call(x, mod, g, mu, wr, wk, wv, w1, w2, w0, a1, a2, a0, g1, g2, kkw, *, nct):
    n, d = x.shape
    tm = RW_TM
    nt = n // tm
    nct = nct * (TM // tm)
    npair = d // LANES
    tb = tm // NB
    nb8 = n // NB
    kern = functools.partial(_rw_proj_kernel, nct=nct, nt=nt)
    full2 = lambda a: pl.BlockSpec(a.shape, lambda i: (0, 0))
    slab = pl.BlockSpec((npair, tm, LANES), lambda i: (0, i, 0))
    slab2 = pl.BlockSpec((2, npair, tm, LANES), lambda i: (0, 0, i, 0))
    slab_shape = jax.ShapeDtypeStruct((npair, n, LANES), F32)
    slab2_shape = jax.ShapeDtypeStruct((2, npair, n, LANES), F32)
    mu3 = mu.reshape(6, 1, d)
    return pl.pallas_call(
        kern,
        grid=(nt,),
        in_specs=[
            pl.BlockSpec((tm, d), lambda i: (i, 0)),
            pl.BlockSpec((NB, d), lambda i: (jnp.maximum(i * tb - 1, 0), 0)),
            pl.BlockSpec((NB, d), lambda i: (jnp.minimum((i + 1) * tb, nb8 - 1), 0)),
            pl.BlockSpec((1, 6, NB, d), lambda i: (_is_lat(i, nct), 0, 0, 0)),
            pl.BlockSpec((1, d), lambda i: (0, 0)),
            pl.BlockSpec((6, 1, d), lambda i: (0, 0, 0)),
            full2(wr), full2(wk), full2(wv),
            full2(w1), full2(w2), full2(w0), full2(a1), full2(a2), full2(a0),
            full2(g1), full2(g2), full2(kkw),
        ],
        out_specs=[slab, slab, slab, slab, slab2, slab2, pl.BlockSpec((tm, d), lambda i: (i, 0))],
        out_shape=[slab_shape, slab_shape, slab_shape, slab_shape, slab2_shape, slab2_shape,
                   jax.ShapeDtypeStruct((n, d), F32)],
        scratch_shapes=[pltpu.VMEM((tm + 2 * NB, d), F32)],
        compiler_params=_cparams(("arbitrary",)),
    )(x, x, x, mod, g, mu3, wr, wk, wv, w1, w2, w0, a1, a2, a0, g1, g2, kkw)


def _wkv_kernel(r_ref, k_ref, v_ref, kk_ref, ew_ref, as_ref, ka_ref, y_ref, st_ref, *, L, npair):
    d = pl.program_id(0)
    c = pl.program_id(1)

    @pl.when(c == 0)
    def _():
        st_ref[...] = jnp.zeros_like(st_ref)

    sgn = 1 - 2 * d
    l2 = 2 * L
    rr = lax.broadcasted_iota(jnp.int32, (l2, l2), 0)
    cc = lax.broadcasted_iota(jnp.int32, (l2, l2), 1)
    same = (rr >= L) == (cc >= L)
    diff = (rr - cc) * sgn
    strict_f = jnp.where(same & (diff > 0), 1.0, 0.0)
    incl_f = jnp.where(same & (diff >= 0), 1.0, 0.0)
    r1 = lax.broadcasted_iota(jnp.int32, (L, L), 0)
    c1 = lax.broadcasted_iota(jnp.int32, (L, L), 1)
    mcum = jnp.where((r1 - c1) * sgn >= 0, 1.0, 0.0).astype(BF16)
    head0 = lax.broadcasted_iota(jnp.int32, (L, LANES), 1) < RWKV_HEAD
    sr_i = lax.broadcasted_iota(jnp.int32, (LANES, LANES), 0)
    sc_i = lax.broadcasted_iota(jnp.int32, (LANES, LANES), 1)
    blockdiag = (sr_i < RWKV_HEAD) == (sc_i < RWKV_HEAD)
    n_dbl = int(round(math.log2(L)))

    def stack2(x):
        return jnp.concatenate([jnp.where(head0, x, 0.0), jnp.where(head0, 0.0, x)], axis=0)

    def unstack(xs):
        return xs[0:L] + xs[L:l2]

    def load(p, b):
        rows = pl.ds(b, L, stride=NB)
        return (r_ref[p, rows, :], k_ref[p, rows, :], v_ref[p, rows, :], kk_ref[p, rows, :],
                ew_ref[0, p, rows, :], as_ref[0, p, rows, :], st_ref[p * NB + b])

    def inst(ka, r, k, v, kk, ew, asg, s0):
        ldec = -ew
        hi = ldec.astype(BF16)
        lo = (ldec - hi.astype(F32)).astype(BF16)
        cl2 = _dot(mcum, jnp.concatenate([hi, lo], axis=1))
        yield
        cl = cl2[:, 0:LANES] + cl2[:, LANES:2 * LANES]
        ce = cl - ldec
        ecl = jnp.exp(cl)
        encl = jnp.exp(-cl)
        etot = jnp.exp(jnp.sum(ldec, axis=0, keepdims=True))
        at = -kk * jnp.exp(ce)
        rt = r * ecl
        bt = kk * asg * encl
        kt = k * (1.0 + (asg - 1.0) * ka) * encl

        ats = stack2(at)
        vsb = stack2(v).astype(BF16)
        lhs = jnp.concatenate([ats, stack2(rt)], axis=0).astype(BF16)
        rhs = jnp.concatenate([stack2(bt), stack2(kt)], axis=0).astype(BF16)
        pm = _dot_nt(lhs, rhs)
        yield
        nab = pm[0:l2, 0:l2] * strict_f
        nak = pm[0:l2, l2:2 * l2] * strict_f
        mrb = (pm[l2:2 * l2, 0:l2] * incl_f).astype(BF16)
        mrk = (pm[l2:2 * l2, l2:2 * l2] * incl_f).astype(BF16)

        xs = jnp.concatenate([ats, _dot(nak.astype(BF16), vsb)], axis=1)
        mkv = _dot(mrk, vsb)
        yield
        npow = nab
        for it in range(n_dbl):
            npb = npow.astype(BF16)
            xs = xs + _dot(npb, xs.astype(BF16))
            if it + 1 < n_dbl:
                npow = _dot(npb, npb)
            yield
        mx = _dot(mrb, xs.astype(BF16))
        at2 = unstack(xs[:, 0:LANES])
        ul = unstack(xs[:, LANES:2 * LANES])
        btb = bt.astype(BF16)
        emat = (jnp.where(blockdiag, _dot(at2.T.astype(BF16), btb), 0.0) * etot).astype(BF16)
        uv = jnp.concatenate([ul, v], axis=0)
        bk = jnp.concatenate([btb, kt.astype(BF16)], axis=0)
        dmat = jnp.where(blockdiag, _dot(uv.T.astype(BF16), bk), 0.0) * etot
        yield
        rt2 = rt + unstack(mx[:, 0:LANES])
        yl = unstack(mx[:, LANES:2 * LANES] + mkv)

        s0b = s0.astype(BF16)
        return _dot_nt(rt2.astype(BF16), s0b) + yl, s0 * etot + _dot(s0b, emat) + dmat

    def run_group(gens):
        outs = [None] * len(gens)
        live = list(enumerate(gens))
        while live:
            nxt = []
            for i, g in live:
                try:
                    next(g)
                    nxt.append((i, g))
                except StopIteration as stop:
                    outs[i] = stop.value
            live = nxt
        return outs

    def pairs(q, carry):
        work = [(q * WKV_PAIRS + j, b) for j in range(WKV_PAIRS) for b in range(NB)]
        outs = run_group([inst(ka_ref[p], *load(p, b)) for p, b in work])
        for (p, b), (y, s1) in zip(work, outs):
            y_ref[0, p, pl.ds(b, L, stride=NB), :] = y
            st_ref[p * NB + b] = s1
        return carry

    lax.fori_loop(0, npair // WKV_PAIRS, pairs, 0)


def _wkv_call(r, k, v, kk, ew, asg, ka, *, nct_rows):
    npair, n, _ = r.shape
    L = WKV_L
    rows = L * NB
    nc = n // rows
    ncc = nct_rows // rows
    kern = functools.partial(_wkv_kernel, L=L, npair=npair)

    def chunk(dd, c):
        bwd = jnp.where(c < ncc, ncc - 1 - c, nc - 1 + ncc - c)
        return jnp.where(dd == 0, c, bwd)

    slab = pl.BlockSpec((npair, rows, LANES), lambda dd, c: (0, chunk(dd, c), 0))
    slab2 = pl.BlockSpec((1, npair, rows, LANES), lambda dd, c: (dd, 0, chunk(dd, c), 0))
    return pl.pallas_call(
        kern,
        grid=(2, nc),
        in_specs=[slab, slab, slab, slab, slab2, slab2,
                  pl.BlockSpec((npair, 1, LANES), lambda dd, c: (0, 0, 0))],
        out_specs=slab2,
        out_shape=jax.ShapeDtypeStruct((2, npair, n, LANES), F32),
        scratch_shapes=[pltpu.VMEM((npair * NB, LANES, LANES), F32)],
        compiler_params=_cparams(("arbitrary", "arbitrary")),
    )(r, k, v, kk, ew, asg, ka)


def _rw_out_kernel(x_ref, mod_ref, y_ref, r_ref, k_ref, v_ref, as_ref, gg_ref, lng_ref, lnb_ref,
                   ka_ref, rk_ref, wo_ref, o_ref):
    x = x_ref[...]
    d = x.shape[1]
    npair = d // LANES
    cat = lambda f: jnp.concatenate([f(p) for p in range(npair)], axis=1)
    y = cat(lambda p: y_ref[0, p] + y_ref[1, p])
    r = cat(lambda p: r_ref[p])
    k = cat(lambda p: k_ref[p])
    v = cat(lambda p: v_ref[p])
    am = cat(lambda p: 0.5 * (as_ref[0, p] + as_ref[1, p]))
    ones_bd = _head_ones()
    inv = 1.0 / RWKV_HEAD
    mean = _head_sum_bcast(y, ones_bd) * inv
    yc = y - mean
    var = _head_sum_bcast(yc * yc, ones_bd) * inv
    yn = yc * lax.rsqrt(var + GN_EPS) * lng_ref[...] + lnb_ref[...]
    k_mean = k * (1.0 + (am - 1.0) * ka_ref[...])
    bonus = _head_sum_bcast(r * k_mean * rk_ref[...], ones_bd) * v
    z = ((yn + bonus) * gg_ref[...]).astype(BF16)
    o_ref[...] = _residual(x, mod_ref[0, 2], _dot(z, wo_ref[...]))


def _rw_out_call(x, mod, y, r, k, v, asg, gg, lng, lnb, ka, rk, wo, *, nct):
    n, d = x.shape
    nt = n // TM
    npair = d // LANES
    slab = pl.BlockSpec((npair, TM, LANES), lambda i: (0, i, 0))
    slab2 = pl.BlockSpec((2, npair, TM, LANES), lambda i: (0, 0, i, 0))
    row = pl.BlockSpec((1, d), lambda i: (0, 0))
    return pl.pallas_call(
        _rw_out_kernel,
        grid=(nt,),
        in_specs=[
            pl.BlockSpec((TM, d), lambda i: (i, 0)),
            pl.BlockSpec((1, 6, NB, d), lambda i: (_is_lat(i, nct), 0, 0, 0)),
            slab2, slab, slab, slab, slab2,
            pl.BlockSpec((TM, d), lambda i: (i, 0)),
            row, row, row, row,
            pl.BlockSpec((d, d), lambda i: (0, 0)),
        ],
        out_specs=pl.BlockSpec((TM, d), lambda i: (i, 0)),
        out_shape=jax.ShapeDtypeStruct((n, d), F32),
        compiler_params=_cparams(("arbitrary",)),
    )(x, mod, y, r, k, v, asg, gg, lng, lnb, ka, rk, wo)


def _final_kernel(x_ref, g_ref, o_ref, slab_ref):
    x = x_ref[...]
    d = x.shape[1]
    ms = jnp.mean(x * x, axis=-1, keepdims=True)
    y = x * lax.rsqrt(ms + RMS_EPS) * g_ref[...]
    for p in range(d // LANES):
        slab_ref[p] = y[:, p * LANES:(p + 1) * LANES]
    for b in range(NB):
        for p in range(d // LANES):
            o_ref[b, :, p * LANES:(p + 1) * LANES] = slab_ref[p, pl.ds(b, TT, stride=NB), :]


def _final_call(x, g, *, tile0):
    n, d = x.shape
    nt = n // TM
    return pl.pallas_call(
        _final_kernel,
        grid=(nt - tile0,),
        in_specs=[pl.BlockSpec((TM, d), lambda i: (i + tile0, 0)),
                  pl.BlockSpec((1, d), lambda i: (0, 0))],
        out_specs=pl.BlockSpec((NB, TT, d), lambda i: (0, i, 0)),
        out_shape=jax.ShapeDtypeStruct((NB, (nt - tile0) * TT, d), F32),
        scratch_shapes=[pltpu.VMEM((d // LANES, TM, LANES), F32)],
        compiler_params=_cparams(("arbitrary",)),
    )(x, g)


def _time_major_kernel(c_ref, x_ref, o_ref, slab_ref, *, nct):
    d = o_ref.shape[1]

    def emit(src_ref):
        for b in range(NB):
            for p in range(d // LANES):
                slab_ref[p, pl.ds(b, TT, stride=NB), :] = src_ref[b, :, p * LANES:(p + 1) * LANES]
        for p in range(d // LANES):
            o_ref[:, p * LANES:(p + 1) * LANES] = slab_ref[p]

    @pl.when(pl.program_id(0) < nct)
    def _():
        emit(c_ref)

    @pl.when(pl.program_id(0) >= nct)
    def _():
        emit(x_ref)


def _time_major_call(ctx, x):
    _, lc, d = ctx.shape
    seq = x.shape[1]
    nct = lc // TT
    nt = (lc + seq) // TT
    return pl.pallas_call(
        functools.partial(_time_major_kernel, nct=nct),
        grid=(nt,),
        in_specs=[pl.BlockSpec((NB, TT, d), lambda i: (0, jnp.minimum(i, nct - 1), 0)),
                  pl.BlockSpec((NB, TT, d), lambda i: (0, jnp.maximum(i - nct, 0), 0))],
        out_specs=pl.BlockSpec((TM, d), lambda i: (i, 0)),
        out_shape=jax.ShapeDtypeStruct((nt * TM, d), F32),
        scratch_shapes=[pltpu.VMEM((d // LANES, TM, LANES), F32)],
        compiler_params=_cparams(("arbitrary",)),
    )(ctx, x)


def _row(v):
    return v.reshape(1, -1).astype(F32)


def _s5_layer(x, mod, g, p, prep, *, nct, tile0):
    bb, cc, ab = prep
    y = _s5_scan_call(x, mod, g, bb, cc, ab, nct=nct)
    return _s5_glu_call(x, mod, g, y, _row(p['d']), p['glu_w'].astype(BF16), _row(p['glu_b']),
                        nct=nct, tile0=tile0)


def _conformer_layer(x, mod, g, p, *, nct):
    u = _cv_glu_call(x, mod, g, p['pw1_w'].astype(BF16), _row(p['pw1_b']), nct=nct)
    return _cv_conv_call(u, x, mod, p['dw_w'].astype(F32), _row(p['dw_b']), _row(p['ln_g']),
                         _row(p['ln_b']), p['pw2_w'].astype(BF16), _row(p['pw2_b']), nct=nct)


def _blockdiag2(m):
    z = jnp.zeros_like(m[0])
    return jnp.concatenate([jnp.concatenate([m[0], z], axis=1),
                            jnp.concatenate([z, m[1]], axis=1)], axis=0)


def _rwkv_layer(x, mod, g, p, *, nct):
    n, d = x.shape
    npair = d // LANES
    w1 = jnp.concatenate([p['w1'][0], p['w1'][1]], axis=1).astype(BF16)
    w2 = _blockdiag2(p['w2']).astype(BF16)
    a1 = jnp.concatenate([p['a1'][0], p['a1'][1]], axis=1).astype(BF16)
    a2 = _blockdiag2(p['a2']).astype(BF16)
    gl = p['g1'].shape[1]
    glp = -(-gl // LANES) * LANES
    g1 = jnp.pad(p['g1'], ((0, 0), (0, glp - gl))).astype(BF16)
    g2 = jnp.pad(p['g2'], ((0, glp - gl), (0, 0))).astype(BF16)
    r, k, v, kk, ew, asg, gg = _rw_proj_call(
        x, mod, g, p['mu'].astype(F32), p['w_r'].astype(BF16), p['w_k'].astype(BF16),
        p['w_v'].astype(BF16), w1, w2, _row(p['w0']), a1, a2, _row(p['a0']), g1, g2,
        _row(p['k_k']), nct=nct)
    ka_slab = p['k_a'].astype(F32).reshape(npair, 1, LANES)
    y = _wkv_call(r, k, v, kk, ew, asg, ka_slab, nct_rows=nct * TM)
    return _rw_out_call(x, mod, y, r, k, v, asg, gg, _row(p['lnx_g']), _row(p['lnx_b']),
                        _row(p['k_a']), _row(p['r_k']), p['w_o'].astype(BF16), nct=nct)


def _forward(x, c, ctx, c_ctx, ada_w, ada_b, norm_mix_g, norm_ffn_g, final_g, s5, cv, rw,
             ff_w1, ff_w3, ff_w2, moe_router, moe_w1, moe_w3, moe_w2):
    bsz, seq, d = x.shape
    lc = ctx.shape[1]
    depth = ada_w.shape[0]
    assert bsz == NB and lc % TT == 0 and seq % TT == 0
    assert TT == GRID_W, "one token tile must hold exactly one latent raster row per batch entry"
    nct = lc // TT
    t_all = lc + seq
    n = t_all * NB

    xs = _time_major_call(ctx.astype(F32), x.astype(F32))

    cvec = jnp.zeros((2 * NB, d), F32).at[:NB].set(c).at[NB].set(c_ctx)
    m_all = _ada_call(cvec, ada_w.astype(F32), ada_b.astype(F32))
    m_all = m_all.reshape(depth, 2 * NB, 6, d)
    m_lat = jnp.swapaxes(m_all[:, :NB], 1, 2)
    m_ctx = jnp.broadcast_to(m_all[:, NB][:, :, None, :], (depth, 6, NB, d))
    mods = jnp.stack([m_ctx, m_lat], axis=1)

    s5_prep = jax.vmap(_s5_params)(s5['lam_re'], s5['lam_im'], s5['log_dt'], s5['b_re'], s5['b_im'],
                                   s5['c_re'], s5['c_im'])
    ctx_rows = nct * TM
    for i in range(depth):
        last = i == depth - 1
        mod = mods[i]
        g_mix = _row(norm_mix_g[i])
        j = i // 3
        if i % 3 == 0:
            xs = _s5_layer(xs, mod, g_mix, {k_: v_[j] for k_, v_ in s5.items()},
                           tuple(a[j] for a in s5_prep), nct=nct, tile0=nct if last else 0)
            ctx_rows = 0 if last else ctx_rows
        elif i % 3 == 1:
            xs = _conformer_layer(xs, mod, g_mix, {k_: v_[j] for k_, v_ in cv.items()}, nct=nct)
        else:
            xs = _rwkv_layer(xs, mod, g_mix, {k_: v_[j] for k_, v_ in rw.items()}, nct=nct)
        fi = i // 2
        g_ffn = _row(norm_ffn_g[i])
        if i % 2 == 0:
            xs = _ffn_call(xs, mod, g_ffn, ff_w1, ff_w3, ff_w2, layer=fi, nct_rows=ctx_rows)
        else:
            xs = _moe_call(xs, mod, g_ffn, moe_router[fi], moe_w1, moe_w3, moe_w2, layer=fi,
                           nct_rows=ctx_rows)

    return _final_call(xs, _row(final_g), tile0=ctx_rows // TM)


def kernel(x, c, ctx, c_ctx, ada_w, ada_b, norm_mix_g, norm_ffn_g, final_g, s5_lambda_re, s5_lambda_im, s5_log_dt, s5_b_re, s5_b_im, s5_c_re, s5_c_im, s5_d, s5_glu_w, s5_glu_b, cv_pw1_w, cv_pw1_b, cv_dw_w, cv_dw_b, cv_ln_g, cv_ln_b, cv_pw2_w, cv_pw2_b, rw_mu, rw_w_r, rw_w_k, rw_w_v, rw_w_o, rw_w0, rw_w1, rw_w2, rw_a0, rw_a1, rw_a2, rw_g1, rw_g2, rw_k_k, rw_k_a, rw_r_k, rw_lnx_g, rw_lnx_b, ff_w1, ff_w3, ff_w2, moe_router, moe_w1, moe_w3, moe_w2):
    s5 = dict(lam_re=s5_lambda_re, lam_im=s5_lambda_im, log_dt=s5_log_dt, b_re=s5_b_re, b_im=s5_b_im,
              c_re=s5_c_re, c_im=s5_c_im, d=s5_d, glu_w=s5_glu_w, glu_b=s5_glu_b)
    cv = dict(pw1_w=cv_pw1_w, pw1_b=cv_pw1_b, dw_w=cv_dw_w, dw_b=cv_dw_b, ln_g=cv_ln_g, ln_b=cv_ln_b,
              pw2_w=cv_pw2_w, pw2_b=cv_pw2_b)
    rw = dict(mu=rw_mu, w_r=rw_w_r, w_k=rw_w_k, w_v=rw_w_v, w_o=rw_w_o, w0=rw_w0, w1=rw_w1, w2=rw_w2,
              a0=rw_a0, a1=rw_a1, a2=rw_a2, g1=rw_g1, g2=rw_g2, k_k=rw_k_k, k_a=rw_k_a, r_k=rw_r_k,
              lnx_g=rw_lnx_g, lnx_b=rw_lnx_b)
    return _forward(x, c, ctx, c_ctx, ada_w, ada_b, norm_mix_g, norm_ffn_g, final_g, s5, cv, rw,
                    ff_w1, ff_w3, ff_w2, moe_router, moe_w1, moe_w3, moe_w2)
```

```python
import functools
import math

import jax
import jax.numpy as jnp
from jax import lax
from jax.experimental import pallas as pl
from jax.experimental.pallas import tpu as pltpu

F32 = jnp.float32
BF16 = jnp.bfloat16

NB = 8
LANES = 128
TM = 512
TT = TM // NB
GRID_W = 64
CONV_WIDTH = 31
CONV_PAD = CONV_WIDTH // 2
HALO_ROWS = 128
S5_GROUP = 16
S5_STATE = 64
S5_JBLK = 8
RWKV_HEAD = 64
WKV_L = 64
WKV_PAIRS = 2
RW_TM = 256
FFN_TM = 1024
MOE_T = 1024
MOE_BLK = 32
MOE_SLOTS = 32
RMS_EPS = 1e-6
LN_EPS = 1e-5
GN_EPS = 64e-5
KK_EPS = 1e-12
LAMBDA_RE_MAX = -1e-4
VMEM_LIMIT = 56 * 1024 * 1024


def _cparams(sem):
    return pltpu.CompilerParams(dimension_semantics=sem, vmem_limit_bytes=VMEM_LIMIT)


def _dot(a, b):
    return jnp.dot(a, b, preferred_element_type=F32)


def _dot_nt(a, b):
    return lax.dot_general(a, b, (((1,), (1,)), ((), ())), preferred_element_type=F32)


def _dot_x3(a, b):
    a1 = a.astype(BF16)
    a2 = (a - a1.astype(F32)).astype(BF16)
    b1 = b.astype(BF16)
    b2 = (b - b1.astype(F32)).astype(BF16)
    return _dot(a1, b1) + _dot(a1, b2) + _dot(a2, b1)


def _sigmoid(x):
    return jax.nn.sigmoid(x)


def _norm_mod(x, g, shift, scale):
    rows, d = x.shape
    ms = jnp.mean(x * x, axis=-1, keepdims=True)
    y = x * lax.rsqrt(ms + RMS_EPS) * g
    y3 = y.reshape(rows // NB, NB, d)
    return (y3 * (1.0 + scale)[None] + shift[None]).reshape(rows, d)


def _residual(x, gate, y):
    rows, d = x.shape
    y3 = y.reshape(rows // NB, NB, d) * gate[None]
    return x + y3.reshape(rows, d)


def _is_lat(i, nct):
    return jnp.where(i >= nct, 1, 0).astype(jnp.int32)


def _ada_kernel(s_ref, w_ref, b_ref, o_ref):
    s = s_ref[...]
    s = s * _sigmoid(s)
    o_ref[0] = _dot(s.astype(BF16), w_ref[0].astype(BF16)) + b_ref[0]


def _ada_call(cvec, ada_w, ada_b):
    depth, d, d6 = ada_w.shape
    tn = 1536
    return pl.pallas_call(
        _ada_kernel,
        grid=(depth, d6 // tn),
        in_specs=[
            pl.BlockSpec((2 * NB, d), lambda l, j: (0, 0)),
            pl.BlockSpec((1, d, tn), lambda l, j: (l, 0, j)),
            pl.BlockSpec((1, 1, tn), lambda l, j: (l, 0, j)),
        ],
        out_specs=pl.BlockSpec((1, 2 * NB, tn), lambda l, j: (l, 0, j)),
        out_shape=jax.ShapeDtypeStruct((depth, 2 * NB, d6), F32),
        compiler_params=_cparams(("arbitrary", "arbitrary")),
    )(cvec, ada_w, ada_b.reshape(depth, 1, d6))


def _ffn_kernel(x_ref, mod_ref, g_ref, w1_ref, w3_ref, w2_ref, o_ref, hb_ref, acc_ref, *, n_k):
    k = pl.program_id(1)

    @pl.when(k == 0)
    def _():
        h = _norm_mod(x_ref[...], g_ref[...], mod_ref[0, 3], mod_ref[0, 4])
        hb_ref[...] = h.astype(BF16)
        acc_ref[...] = jnp.zeros_like(acc_ref)

    hb = hb_ref[...]
    a = _dot(hb, w1_ref[0].astype(BF16))
    b = _dot(hb, w3_ref[0].astype(BF16))
    acc_ref[...] += _dot((a * _sigmoid(a) * b).astype(BF16), w2_ref[0].astype(BF16))

    @pl.when(k == n_k - 1)
    def _():
        o_ref[...] = _residual(x_ref[...], mod_ref[0, 5], acc_ref[...])


def _ffn_call(x, mod, g, w1, w3, w2, *, layer, nct_rows):
    n, d = x.shape
    f = w1.shape[2]
    tm = FFN_TM
    tf = 512
    n_k = f // tf
    nct = nct_rows // tm
    return pl.pallas_call(
        functools.partial(_ffn_kernel, n_k=n_k),
        grid=(n // tm, n_k),
        in_specs=[
            pl.BlockSpec((tm, d), lambda i, k: (i, 0)),
            pl.BlockSpec((1, 6, NB, d), lambda i, k: (_is_lat(i, nct), 0, 0, 0)),
            pl.BlockSpec((1, d), lambda i, k: (0, 0)),
            pl.BlockSpec((1, d, tf), lambda i, k: (layer, 0, k)),
            pl.BlockSpec((1, d, tf), lambda i, k: (layer, 0, k)),
            pl.BlockSpec((1, tf, d), lambda i, k: (layer, k, 0)),
        ],
        out_specs=pl.BlockSpec((tm, d), lambda i, k: (i, 0)),
        out_shape=jax.ShapeDtypeStruct((n, d), F32),
        scratch_shapes=[pltpu.VMEM((tm, d), BF16), pltpu.VMEM((tm, d), F32)],
        compiler_params=_cparams(("arbitrary", "arbitrary")),
    )(x, mod, g, w1, w3, w2)


def _top2(logits, n_e):
    lane = lax.broadcasted_iota(jnp.int32, logits.shape, 1).astype(F32)
    neg = jnp.float32(-3.0e38)
    lg = jnp.where(lane < n_e, logits, neg)
    m1 = jnp.max(lg, axis=-1, keepdims=True)
    i1 = jnp.min(jnp.where(lg == m1, lane, 1.0e9), axis=-1, keepdims=True)
    lg2 = jnp.where(lane == i1, neg, lg)
    m2 = jnp.max(lg2, axis=-1, keepdims=True)
    i2 = jnp.min(jnp.where(lg2 == m2, lane, 1.0e9), axis=-1, keepdims=True)
    e2 = jnp.exp(m2 - m1)
    den = 1.0 + e2
    return lane, i1, i2, 1.0 / den, e2 / den


def _moe_route_kernel(x_ref, mod_ref, g_ref, router_ref, ltri_ref, z_ref, dest_ref, cnt_ref,
                      hb_ref, *, n_e, cap):
    t, d = x_ref.shape
    h = _norm_mod(x_ref[...], g_ref[...], mod_ref[0, 3], mod_ref[0, 4])
    hb_ref[...] = h.astype(BF16)
    lane, i1, i2, p1, p2 = _top2(_dot_x3(h, router_ref[...]), n_e)
    sel1 = lane == i1
    sel2 = lane == i2
    c = jnp.where(sel1, 1.0, 0.0) + jnp.where(sel2, 1.0, 0.0)
    rank = _dot(ltri_ref[...], c.astype(BF16))
    cnt = jnp.sum(c, axis=0, keepdims=True)
    seg = jnp.floor((cnt + (MOE_BLK - 1)) * (1.0 / MOE_BLK)) * MOE_BLK
    ur = lax.broadcasted_iota(jnp.int32, (LANES, LANES), 0)
    uc = lax.broadcasted_iota(jnp.int32, (LANES, LANES), 1)
    before = jnp.where(ur < uc, 1.0, 0.0).astype(BF16)
    start = _dot(jnp.broadcast_to(seg, (NB, LANES)).astype(BF16), before)[0:1]
    pos = start + rank
    d1 = jnp.sum(jnp.where(sel1, pos, 0.0), axis=-1, keepdims=True)
    d2 = jnp.sum(jnp.where(sel2, pos, 0.0), axis=-1, keepdims=True)
    packed = (jnp.where(lane == 0.0, d1, 0.0) + jnp.where(lane == 1.0, d2, 0.0)
              + jnp.where(lane == 2.0, p1, 0.0) + jnp.where(lane == 3.0, p2, 0.0))
    dest_ref[...] = packed
    cnt_ref[0] = jnp.broadcast_to(cnt, (NB, LANES))
    packed_t = packed.T
    d1r = packed_t[0:1]
    d2r = packed_t[1:2]
    p1r = packed_t[2:3]
    p2r = packed_t[3:4]
    rc = 256

    def chunk(ci, carry):
        r0 = pl.multiple_of(ci * rc, rc)
        row = (lax.broadcasted_iota(jnp.int32, (rc, t), 0) + r0).astype(F32)
        m1 = row == d1r
        m2 = row == d2r
        perm = (jnp.where(m1, 1.0, 0.0) + jnp.where(m2, 1.0, 0.0)).astype(BF16)
        zc = _dot(perm, hb_ref[...])
        gate = jnp.sum(jnp.where(m1, p1r, 0.0) + jnp.where(m2, p2r, 0.0), axis=-1, keepdims=True)
        ghi = gate.astype(BF16).astype(F32)
        glo = gate - ghi
        le = lax.broadcasted_iota(jnp.int32, (rc, LANES), 1)
        ext = jnp.where(le == 0, ghi, jnp.where(le == 1, glo, 0.0))
        z_ref[pl.ds(r0, rc), 0:d] = zc.astype(BF16)
        z_ref[pl.ds(r0, rc), d:d + LANES] = ext.astype(BF16)
        return carry

    lax.fori_loop(0, cap // rc, chunk, 0)


def _moe_experts_kernel(tin_ref, se_ref, nv_ref, *refs, n_k, d):
    z_refs = refs[:MOE_SLOTS]
    w1_ref, w3_ref, w2_ref, o_ref, xs_ref, acc_ref = refs[MOE_SLOTS:]
    s = pl.program_id(0)
    k = pl.program_id(1)
    nv = nv_ref[s]

    @pl.when((nv > 0) & (k == 0))
    def _():
        for j in range(MOE_SLOTS):
            xs_ref[j * MOE_BLK:(j + 1) * MOE_BLK] = z_refs[j][...]
        acc_ref[...] = jnp.zeros_like(acc_ref)

    def compute(nrows):
        hb = xs_ref[0:nrows, 0:d]
        ext = xs_ref[0:nrows, d:d + LANES].astype(F32)
        le = lax.broadcasted_iota(jnp.int32, ext.shape, 1)
        gate = jnp.sum(jnp.where(le < 2, ext, 0.0), axis=-1, keepdims=True)
        a = _dot(hb, w1_ref[0, 0].astype(BF16))
        b = _dot(hb, w3_ref[0, 0].astype(BF16))
        act = a * _sigmoid(a) * b * gate
        acc_ref[0:nrows] += _dot(act.astype(BF16), w2_ref[0, 0].astype(BF16))

    quarter = MOE_SLOTS // 4
    for q in range(1, 5):
        @pl.when((nv > (q - 1) * quarter) & (nv <= q * quarter))
        def _():
            compute(q * quarter * MOE_BLK)

    @pl.when(k == n_k - 1)
    def _():
        o_ref[...] = jnp.where(nv > 0, acc_ref[...], 0.0).astype(BF16)


def _moe_combine_kernel(tc_ref, *refs, nblk, cap):
    zb_refs = refs[:nblk]
    dest_ref, x_ref, mod_ref, o_ref, zbuf_ref = refs[nblk:]
    t, d = x_ref.shape
    for j in range(nblk):
        zbuf_ref[j * MOE_BLK:(j + 1) * MOE_BLK] = zb_refs[j][...]
    packed = dest_ref[...]
    lane = lax.broadcasted_iota(jnp.int32, packed.shape, 1)
    d1 = jnp.sum(jnp.where(lane == 0, packed, 0.0), axis=-1, keepdims=True)
    d2 = jnp.sum(jnp.where(lane == 1, packed, 0.0), axis=-1, keepdims=True)
    kc = 256
    acc = jnp.zeros((t, d), F32)
    for c0 in range(0, cap, kc):
        col = (lax.broadcasted_iota(jnp.int32, (t, kc), 1) + c0).astype(F32)
        sel = (jnp.where(col == d1, 1.0, 0.0) + jnp.where(col == d2, 1.0, 0.0)).astype(BF16)
        acc = acc + _dot(sel, zbuf_ref[c0:c0 + kc])
    o_ref[...] = _residual(x_ref[...], mod_ref[0, 5], acc)


def _moe_tables(cnt, *, n_tiles, n_e, nblk, n_steps):
    i32 = jnp.int32
    e_ids = jnp.arange(n_e, dtype=i32)
    t_ids = jnp.arange(n_tiles, dtype=i32)
    pc = (cnt + (MOE_BLK - 1)) // MOE_BLK
    lb = jnp.cumsum(pc, axis=1) - pc
    cum = jnp.cumsum(pc, axis=0)
    cum_ex = cum - pc
    nb_e = cum[-1]
    steps_e = (nb_e + (MOE_SLOTS - 1)) // MOE_SLOTS
    ends = jnp.cumsum(steps_e)
    step0 = ends - steps_e
    sidx = jnp.arange(n_steps, dtype=i32)
    step_e = jnp.minimum(jnp.sum((sidx[:, None] >= ends[None, :]).astype(i32), axis=1), n_e - 1)
    oh_e = (step_e[:, None] == e_ids[None, :]).astype(i32)
    by_step = lambda v: jnp.sum(oh_e * v[None, :], axis=1)
    by_step_tile = lambda m: jnp.sum(oh_e[:, None, :] * m[None, :, :], axis=2)
    rank = ((sidx - by_step(step0))[:, None] * MOE_SLOTS
            + jnp.arange(MOE_SLOTS, dtype=i32)[None, :])
    cum_s = by_step_tile(cum)
    tile = jnp.sum((cum_s[:, None, :] <= rank[:, :, None]).astype(i32), axis=2)
    nvalid = jnp.where(sidx < ends[-1], jnp.clip(by_step(nb_e) - rank[:, 0], 0, MOE_SLOTS), 0)
    ok = rank < (rank[:, :1] + nvalid[:, None])
    oh_t = (jnp.minimum(tile, n_tiles - 1)[:, :, None] == t_ids[None, None, :]).astype(i32)
    at_tile = lambda m: jnp.sum(oh_t * by_step_tile(m)[:, None, :], axis=2)
    src = jnp.sum(oh_t * t_ids, axis=2) * nblk + at_tile(lb) + rank - at_tile(cum_ex)
    tin = jnp.where(ok, src, 0).reshape(-1).astype(i32)
    lblk = jnp.arange(nblk, dtype=i32)[None, :]
    seg_end = lb + pc
    e_l = jnp.minimum(jnp.sum((seg_end[:, None, :] <= lblk[:, :, None]).astype(i32), axis=2), n_e - 1)
    oh_l = (e_l[:, :, None] == e_ids[None, None, :]).astype(i32)
    r_l = jnp.sum(oh_l * (cum_ex - lb)[:, None, :], axis=2) + lblk
    flat = (jnp.sum(oh_l * step0, axis=2) + r_l // MOE_SLOTS) * MOE_SLOTS + r_l % MOE_SLOTS
    tcomb = jnp.where(lblk < seg_end[:, -1:], flat, 0).reshape(-1).astype(i32)
    return tin, tcomb, step_e.astype(i32), nvalid.astype(i32)


def _moe_call(x, mod, g, router, w1, w3, w2, *, layer, nct_rows):
    n, d = x.shape
    _, n_e, _, f = w1.shape
    t = MOE_T
    assert n % t == 0 and nct_rows % t == 0
    n_tiles = n // t
    nct = nct_rows // t
    cap = 2 * t + n_e * MOE_BLK
    nblk = cap // MOE_BLK
    rows = MOE_SLOTS * MOE_BLK
    n_steps = -(-(n_tiles * nblk) // MOE_SLOTS) + n_e
    tf = 512
    n_k = f // tf
    router_p = jnp.pad(router.astype(F32), ((0, 0), (0, LANES - n_e)))
    ltri = jnp.tril(jnp.ones((t, t), F32), k=-1).astype(BF16)
    mod_spec = lambda off: pl.BlockSpec((1, 6, NB, d), lambda i, *_: (_is_lat(i, nct), 0, 0, 0))

    z, dest, cnt = pl.pallas_call(
        functools.partial(_moe_route_kernel, n_e=n_e, cap=cap),
        grid=(n_tiles,),
        in_specs=[
            pl.BlockSpec((t, d), lambda i: (i, 0)),
            mod_spec(0),
            pl.BlockSpec((1, d), lambda i: (0, 0)),
            pl.BlockSpec((d, LANES), lambda i: (0, 0)),
            pl.BlockSpec((t, t), lambda i: (0, 0)),
        ],
        out_specs=[
            pl.BlockSpec((cap, d + LANES), lambda i: (i, 0)),
            pl.BlockSpec((t, LANES), lambda i: (i, 0)),
            pl.BlockSpec((1, NB, LANES), lambda i: (i, 0, 0)),
        ],
        out_shape=[
            jax.ShapeDtypeStruct((n_tiles * cap, d + LANES), BF16),
            jax.ShapeDtypeStruct((n, LANES), F32),
            jax.ShapeDtypeStruct((n_tiles, NB, LANES), F32),
        ],
        scratch_shapes=[pltpu.VMEM((t, d), BF16)],
        compiler_params=_cparams(("arbitrary",)),
    )(x, mod, g, router_p, ltri)

    tin, tcomb, step_e, nvalid = _moe_tables(cnt[:, 0, :n_e].astype(jnp.int32), n_tiles=n_tiles,
                                            n_e=n_e, nblk=nblk, n_steps=n_steps)

    def kk(s, k, nv_r):
        return jnp.where(nv_r[s] > 0, k, n_k - 1)

    def z_spec(j):
        return pl.BlockSpec((MOE_BLK, d + LANES), lambda s, k, tin_r, se_r, nu_r: (tin_r[s * MOE_SLOTS + j], 0))

    zo = pl.pallas_call(
        functools.partial(_moe_experts_kernel, n_k=n_k, d=d),
        grid_spec=pltpu.PrefetchScalarGridSpec(
            num_scalar_prefetch=3,
            grid=(n_steps, n_k),
            in_specs=[z_spec(j) for j in range(MOE_SLOTS)] + [
                pl.BlockSpec((1, 1, d, tf), lambda s, k, tin_r, se_r, nu_r: (layer, se_r[s], 0, kk(s, k, nu_r))),
                pl.BlockSpec((1, 1, d, tf), lambda s, k, tin_r, se_r, nu_r: (layer, se_r[s], 0, kk(s, k, nu_r))),
                pl.BlockSpec((1, 1, tf, d), lambda s, k, tin_r, se_r, nu_r: (layer, se_r[s], kk(s, k, nu_r), 0)),
            ],
            out_specs=pl.BlockSpec((rows, d), lambda s, k, tin_r, se_r, nu_r: (s, 0)),
            scratch_shapes=[pltpu.VMEM((rows, d + LANES), BF16), pltpu.VMEM((rows, d), F32)],
        ),
        out_shape=jax.ShapeDtypeStruct((n_steps * rows, d), BF16),
        compiler_params=_cparams(("arbitrary", "arbitrary")),
    )(tin, step_e, nvalid, *([z] * MOE_SLOTS), w1, w3, w2)

    def zo_spec(j):
        return pl.BlockSpec((MOE_BLK, d), lambda i, tc_r: (tc_r[i * nblk + j], 0))

    return pl.pallas_call(
        functools.partial(_moe_combine_kernel, nblk=nblk, cap=cap),
        grid_spec=pltpu.PrefetchScalarGridSpec(
            num_scalar_prefetch=1,
            grid=(n_tiles,),
            in_specs=[zo_spec(j) for j in range(nblk)] + [
                pl.BlockSpec((t, LANES), lambda i, tc_r: (i, 0)),
                pl.BlockSpec((t, d), lambda i, tc_r: (i, 0)),
                mod_spec(0),
            ],
            out_specs=pl.BlockSpec((t, d), lambda i, tc_r: (i, 0)),
            scratch_shapes=[pltpu.VMEM((cap, d), BF16)],
        ),
        out_shape=jax.ShapeDtypeStruct((n, d), F32),
        compiler_params=_cparams(("arbitrary",)),
    )(tcomb, *([zo] * nblk), dest, x, mod)


def _dir_tile(d, s, nct, nt):
    bwd = jnp.where(s < nct, nct - 1 - s, nt - 1 + nct - s)
    return jnp.where(d == 0, s, bwd)


def _s5_scan_kernel(x_ref, mod_ref, g_ref, bb_ref, cc_ref, ab_ref, y_ref,
                    hb_ref, bu0_ref, bu1_ref, bu2_ref, st_ref):
    d = pl.program_id(0)
    s = pl.program_id(1)
    half = S5_JBLK * S5_STATE
    bufs = (bu0_ref, bu1_ref, bu2_ref)

    @pl.when(s == 0)
    def _():
        st_ref[...] = jnp.zeros_like(st_ref)

    h = _norm_mod(x_ref[...], g_ref[...], mod_ref[0, 0], mod_ref[0, 1])
    hb_ref[...] = h.astype(BF16)

    def project(j):
        bufs[j % 3][...] = _dot(hb_ref[:, j * LANES:(j + 1) * LANES], bb_ref[0, j])

    def scan(j):
        buf = bufs[j % 3]
        ar = ab_ref[0, j, 0]
        ai = ab_ref[0, j, 1]
        re = st_ref[j, 0]
        im = st_ref[j, 1]
        for t in range(TT):
            tt = t + d * (TT - 1 - 2 * t)
            r0 = pl.multiple_of(tt * NB, NB)
            bre = buf[pl.ds(r0, NB), 0:half]
            bim = buf[pl.ds(r0, NB), half:2 * half]
            re, im = ar * re - ai * im + bre, ar * im + ai * re + bim
            buf[pl.ds(r0, NB), 0:half] = re
            buf[pl.ds(r0, NB), half:2 * half] = im
        st_ref[j, 0] = re
        st_ref[j, 1] = im

    def readout(j):
        y_ref[0, j] = _dot(bufs[j % 3][...].astype(BF16), cc_ref[0, j])

    for i in range(S5_JBLK + 2):
        if i < S5_JBLK:
            project(i)
        if 0 <= i - 1 < S5_JBLK:
            scan(i - 1)
        if 0 <= i - 2 < S5_JBLK:
            readout(i - 2)


def _s5_scan_call(x, mod, g, bb, cc, ab, *, nct):
    n, d = x.shape
    nt = n // TM
    half = S5_JBLK * S5_STATE
    tile = lambda dd, s: _dir_tile(dd, s, nct, nt)
    return pl.pallas_call(
        _s5_scan_kernel,
        grid=(2, nt),
        in_specs=[
            pl.BlockSpec((TM, d), lambda dd, s: (tile(dd, s), 0)),
            pl.BlockSpec((1, 6, NB, d), lambda dd, s: (_is_lat(tile(dd, s), nct), 0, 0, 0)),
            pl.BlockSpec((1, d), lambda dd, s: (0, 0)),
            pl.BlockSpec((1, S5_JBLK, LANES, 2 * half), lambda dd, s: (dd, 0, 0, 0)),
            pl.BlockSpec((1, S5_JBLK, 2 * half, LANES), lambda dd, s: (dd, 0, 0, 0)),
            pl.BlockSpec((1, S5_JBLK, 2, NB, half), lambda dd, s: (dd, 0, 0, 0, 0)),
        ],
        out_specs=pl.BlockSpec((1, S5_JBLK, TM, LANES), lambda dd, s: (dd, 0, tile(dd, s), 0)),
        out_shape=jax.ShapeDtypeStruct((2, S5_JBLK, n, LANES), F32),
        scratch_shapes=[
            pltpu.VMEM((TM, d), BF16),
            pltpu.VMEM((TM, 2 * half), F32),
            pltpu.VMEM((TM, 2 * half), F32),
            pltpu.VMEM((TM, 2 * half), F32),
            pltpu.VMEM((S5_JBLK, 2, NB, half), F32),
        ],
        compiler_params=_cparams(("arbitrary", "arbitrary")),
    )(x, mod, g, bb, cc, ab)


def _gelu_tanh(x):
    c = math.sqrt(2.0 / math.pi)
    return 0.5 * x * (1.0 + jnp.tanh(c * (x + 0.044715 * (x * x * x))))


def _s5_glu_kernel(x_ref, mod_ref, g_ref, y_ref, dsk_ref, w_ref, b_ref, o_ref):
    x = x_ref[...]
    d = x.shape[1]
    h = _norm_mod(x, g_ref[...], mod_ref[0, 0], mod_ref[0, 1])
    y = jnp.concatenate([y_ref[0, j] + y_ref[1, j] for j in range(S5_JBLK)], axis=1)
    z = _gelu_tanh(y + dsk_ref[...] * h).astype(BF16)
    zz = _dot(z, w_ref[...]) + b_ref[...]
    out = zz[:, :d] * _sigmoid(zz[:, d:])
    o_ref[...] = _residual(x, mod_ref[0, 2], out)


def _s5_glu_call(x, mod, g, y, dsk, w, b, *, nct, tile0):
    n, d = x.shape
    nt = n // TM
    return pl.pallas_call(
        _s5_glu_kernel,
        grid=(nt - tile0,),
        in_specs=[
            pl.BlockSpec((TM, d), lambda i: (i + tile0, 0)),
            pl.BlockSpec((1, 6, NB, d), lambda i: (_is_lat(i + tile0, nct), 0, 0, 0)),
            pl.BlockSpec((1, d), lambda i: (0, 0)),
            pl.BlockSpec((2, S5_JBLK, TM, LANES), lambda i: (0, 0, i + tile0, 0)),
            pl.BlockSpec((1, d), lambda i: (0, 0)),
            pl.BlockSpec((d, 2 * d), lambda i: (0, 0)),
            pl.BlockSpec((1, 2 * d), lambda i: (0, 0)),
        ],
        out_specs=pl.BlockSpec((TM, d), lambda i: (i, 0)),
        out_shape=jax.ShapeDtypeStruct((n - tile0 * TM, d), F32),
        compiler_params=_cparams(("arbitrary",)),
    )(x, mod, g, y, dsk, w, b)


def _s5_params(lam_re, lam_im, log_dt, b_re, b_im, c_re, c_im):
    lam_re = jnp.minimum(lam_re.astype(F32), LAMBDA_RE_MAX)
    lam_im = lam_im.astype(F32)
    dt = jnp.exp(log_dt.astype(F32))[..., None]
    mag = jnp.exp(lam_re * dt)
    abar_re = mag * jnp.cos(lam_im * dt)
    abar_im = mag * jnp.sin(lam_im * dt)
    inv_den = 1.0 / (lam_re * lam_re + lam_im * lam_im)
    gam_re = ((abar_re - 1.0) * lam_re + abar_im * lam_im) * inv_den
    gam_im = (abar_im * lam_re - (abar_re - 1.0) * lam_im) * inv_den
    b_re = b_re.astype(F32)
    b_im = b_im.astype(F32)
    bb_re = gam_re[..., None] * b_re - gam_im[..., None] * b_im
    bb_im = gam_re[..., None] * b_im + gam_im[..., None] * b_re
    g_all = lam_re.shape[1]
    nj = g_all // S5_JBLK
    eye = jnp.eye(S5_JBLK, dtype=F32)

    def in_mat(m):
        m = m.reshape(2, nj, S5_JBLK, S5_STATE, S5_GROUP)
        m = jnp.einsum('djgnp,gh->djgphn', m, eye)
        return m.reshape(2, nj, S5_JBLK * S5_GROUP, S5_JBLK * S5_STATE)

    def out_mat(m):
        m = m.reshape(2, nj, S5_JBLK, S5_GROUP, S5_STATE)
        m = jnp.einsum('djgpn,gh->djgnhp', m, eye)
        return m.reshape(2, nj, S5_JBLK * S5_STATE, S5_JBLK * S5_GROUP)

    bb = jnp.concatenate([in_mat(bb_re), in_mat(bb_im)], axis=-1).astype(BF16)
    cc = jnp.concatenate([out_mat(c_re.astype(F32)), -out_mat(c_im.astype(F32))], axis=-2).astype(BF16)
    half = S5_JBLK * S5_STATE
    ab = jnp.stack([abar_re.reshape(2, nj, half), abar_im.reshape(2, nj, half)], axis=2)
    ab = jnp.broadcast_to(ab[:, :, :, None, :], (2, nj, 2, NB, half))
    return bb, cc, ab


def _cv_glu_kernel(x_ref, mod_ref, g_ref, w_ref, b_ref, o_ref):
    x = x_ref[...]
    d = x.shape[1]
    h = _norm_mod(x, g_ref[...], mod_ref[0, 0], mod_ref[0, 1]).astype(BF16)
    u = _dot(h, w_ref[...]) + b_ref[...]
    o_ref[...] = u[:, :d] * _sigmoid(u[:, d:])


def _cv_glu_call(x, mod, g, w, b, *, nct):
    n, d = x.shape
    nt = n // TM
    return pl.pallas_call(
        _cv_glu_kernel,
        grid=(nt,),
        in_specs=[
            pl.BlockSpec((TM, d), lambda i: (i, 0)),
            pl.BlockSpec((1, 6, NB, d), lambda i: (_is_lat(i, nct), 0, 0, 0)),
            pl.BlockSpec((1, d), lambda i: (0, 0)),
            pl.BlockSpec((d, 2 * d), lambda i: (0, 0)),
            pl.BlockSpec((1, 2 * d), lambda i: (0, 0)),
        ],
        out_specs=pl.BlockSpec((TM, d), lambda i: (i, 0)),
        out_shape=jax.ShapeDtypeStruct((n, d), F32),
        compiler_params=_cparams(("arbitrary",)),
    )(x, mod, g, w, b)


def _cv_conv_kernel(u_ref, up_ref, un_ref, x_ref, mod_ref, dw_ref, dwb_ref, lng_ref, lnb_ref,
                    w_ref, b_ref, o_ref, pad_ref, cv_ref, *, nct):
    i = pl.program_id(0)
    halo = CONV_PAD * NB
    prev_ok = (i > 0) & (i < nct)
    next_ok = i < nct - 1
    pad_ref[0:halo] = jnp.where(prev_ok, up_ref[HALO_ROWS - halo:HALO_ROWS], 0.0)
    pad_ref[halo:halo + TM] = u_ref[...]
    pad_ref[halo + TM:halo + TM + halo] = jnp.where(next_ok, un_ref[0:halo], 0.0)

    rb = 32
    lw = 512
    d = dwb_ref.shape[1]

    def body(c, _):
        r0 = pl.multiple_of(c * rb, rb)
        for l0 in range(0, d, lw):
            bias = jnp.broadcast_to(dwb_ref[:, l0:l0 + lw], (NB, lw))
            acc = [bias] * (rb // NB)
            for k in range(CONV_WIDTH):
                w = dw_ref[k, :, l0:l0 + lw]
                for g in range(rb // NB):
                    acc[g] = acc[g] + w * pad_ref[pl.ds(r0 + (k + g) * NB, NB), l0:l0 + lw]
            for g in range(rb // NB):
                cv_ref[pl.ds(r0 + g * NB, NB), l0:l0 + lw] = acc[g]
        return 0

    lax.fori_loop(0, TM // rb, body, 0)

    c = cv_ref[...]
    mu = jnp.mean(c, axis=-1, keepdims=True)
    cc = c - mu
    var = jnp.mean(cc * cc, axis=-1, keepdims=True)
    y = cc * lax.rsqrt(var + LN_EPS) * lng_ref[...] + lnb_ref[...]
    y = (y * _sigmoid(y)).astype(BF16)
    out = _dot(y, w_ref[...]) + b_ref[...]
    o_ref[...] = _residual(x_ref[...], mod_ref[0, 2], out)


def _cv_conv_call(u, x, mod, dw_w, dw_b, ln_g, ln_b, w, b, *, nct):
    n, d = x.shape
    nt = n // TM
    hb = TM // HALO_ROWS
    nhb = n // HALO_ROWS
    kern = functools.partial(_cv_conv_kernel, nct=nct)
    row = lambda i: (0, 0)
    return pl.pallas_call(
        kern,
        grid=(nt,),
        in_specs=[
            pl.BlockSpec((TM, d), lambda i: (i, 0)),
            pl.BlockSpec((HALO_ROWS, d), lambda i: (jnp.maximum(i * hb - 1, 0), 0)),
            pl.BlockSpec((HALO_ROWS, d), lambda i: (jnp.minimum((i + 1) * hb, nhb - 1), 0)),
            pl.BlockSpec((TM, d), lambda i: (i, 0)),
            pl.BlockSpec((1, 6, NB, d), lambda i: (_is_lat(i, nct), 0, 0, 0)),
            pl.BlockSpec((CONV_WIDTH, NB, d), lambda i: (0, 0, 0)),
            pl.BlockSpec((1, d), row),
            pl.BlockSpec((1, d), row),
            pl.BlockSpec((1, d), row),
            pl.BlockSpec((d, d), row),
            pl.BlockSpec((1, d), row),
        ],
        out_specs=pl.BlockSpec((TM, d), lambda i: (i, 0)),
        out_shape=jax.ShapeDtypeStruct((n, d), F32),
        scratch_shapes=[
            pltpu.VMEM((TM + 2 * CONV_PAD * NB, d), F32),
            pltpu.VMEM((TM, d), F32),
        ],
        compiler_params=_cparams(("arbitrary",)),
    )(u, u, u, x, mod, jnp.broadcast_to(dw_w[:, None, :], (CONV_WIDTH, NB, d)), dw_b, ln_g, ln_b, w, b)


def _head_ones():
    w = 2 * LANES
    r = lax.broadcasted_iota(jnp.int32, (w, w), 0) // RWKV_HEAD
    c = lax.broadcasted_iota(jnp.int32, (w, w), 1) // RWKV_HEAD
    return jnp.where(r == c, 1.0, 0.0).astype(BF16)


def _head_sum_bcast(x, ones_bd):
    hi = x.astype(BF16)
    lo = (x - hi.astype(F32)).astype(BF16)
    w = ones_bd.shape[0]
    cols = [_dot(hi[:, c0:c0 + w], ones_bd) + _dot(lo[:, c0:c0 + w], ones_bd)
            for c0 in range(0, x.shape[1], w)]
    return jnp.concatenate(cols, axis=1)


def _rw_proj_kernel(x_ref, xp_ref, xn_ref, mod_ref, g_ref, mu_ref, wr_ref, wk_ref, wv_ref,
                    w1_ref, w2_ref, w0_ref, a1_ref, a2_ref, a0_ref, g1_ref, g2_ref, kkw_ref,
                    ka_ref, rk_ref,
                    r_ref, k_ref, v_ref, kk_ref, ew_ref, as_ref, gg_ref, bg_ref, hp_ref, *, nct, nt):
    i = pl.program_id(0)
    tm, d = x_ref.shape
    g = g_ref[...]
    sh = mod_ref[0, 0]
    sc = mod_ref[0, 1]
    h = _norm_mod(x_ref[...], g, sh, sc)
    hp = _norm_mod(xp_ref[...], g, sh, sc)
    hn = _norm_mod(xn_ref[...], g, sh, sc)
    prev_ok = (i != 0) & (i != nct)
    next_ok = (i != nct - 1) & (i != nt - 1)
    hp_ref[0:NB] = jnp.where(prev_ok, hp, 0.0)
    hp_ref[NB:NB + tm] = h
    hp_ref[NB + tm:NB + tm + NB] = jnp.where(next_ok, hn, 0.0)
    xx = 0.5 * (hp_ref[0:tm] + hp_ref[2 * NB:2 * NB + tm]) - h

    def mix(j):
        return (h + xx * mu_ref[j]).astype(BF16)

    r = _dot(mix(0), wr_ref[...])
    k = _dot(mix(2), wk_ref[...])
    v = _dot(mix(3), wv_ref[...])
    zw = w0_ref[...] + _dot(jnp.tanh(_dot(mix(1), w1_ref[...])).astype(BF16), w2_ref[...])
    ew = _sigmoid(zw) * math.exp(-0.5)
    asg = _sigmoid(a0_ref[...] + _dot(_dot(mix(4), a1_ref[...]).astype(BF16), a2_ref[...]))
    gg = _dot(_sigmoid(_dot(mix(5), g1_ref[...])).astype(BF16), g2_ref[...])
    gg_ref[...] = gg

    ones_bd = _head_ones()
    kk = k * kkw_ref[...]
    kk = kk * lax.rsqrt(_head_sum_bcast(kk * kk, ones_bd) + KK_EPS)
    k_mean = k * (1.0 + (0.5 * (asg[:, :d] + asg[:, d:]) - 1.0) * ka_ref[...])
    bg_ref[...] = _head_sum_bcast(r * k_mean * rk_ref[...], ones_bd) * v * gg

    for p in range(d // LANES):
        sl = slice(p * LANES, (p + 1) * LANES)
        r_ref[p] = r[:, sl]
        k_ref[p] = k[:, sl]
        v_ref[p] = v[:, sl]
        kk_ref[p] = kk[:, sl]
        for dr in range(2):
            sl2 = slice(dr * d + p * LANES, dr * d + (p + 1) * LANES)
            ew_ref[dr, p] = ew[:, sl2]
            as_ref[dr, p] = asg[:, sl2]


def _rw_proj_call(x, mod, g, mu, wr, wk, wv, w1, w2, w0, a1, a2, a0, g1, g2, kkw, ka, rk, *, nct):
    n, d = x.shape
    tm = RW_TM
    nt = n // tm
    nct = nct * (TM // tm)
    npair = d // LANES
    tb = tm // NB
    nb8 = n // NB
    kern = functools.partial(_rw_proj_kernel, nct=nct, nt=nt)
    full2 = lambda a: pl.BlockSpec(a.shape, lambda i: (0, 0))
    slab = pl.BlockSpec((npair, tm, LANES), lambda i: (0, i, 0))
    slab2 = pl.BlockSpec((2, npair, tm, LANES), lambda i: (0, 0, i, 0))
    slab_shape = jax.ShapeDtypeStruct((npair, n, LANES), F32)
    slab2_shape = jax.ShapeDtypeStruct((2, npair, n, LANES), F32)
    mu3 = mu.reshape(6, 1, d)
    return pl.pallas_call(
        kern,
        grid=(nt,),
        in_specs=[
            pl.BlockSpec((tm, d), lambda i: (i, 0)),
            pl.BlockSpec((NB, d), lambda i: (jnp.maximum(i * tb - 1, 0), 0)),
            pl.BlockSpec((NB, d), lambda i: (jnp.minimum((i + 1) * tb, nb8 - 1), 0)),
            pl.BlockSpec((1, 6, NB, d), lambda i: (_is_lat(i, nct), 0, 0, 0)),
            pl.BlockSpec((1, d), lambda i: (0, 0)),
            pl.BlockSpec((6, 1, d), lambda i: (0, 0, 0)),
            full2(wr), full2(wk), full2(wv),
            full2(w1), full2(w2), full2(w0), full2(a1), full2(a2), full2(a0),
            full2(g1), full2(g2), full2(kkw), full2(ka), full2(rk),
        ],
        out_specs=[slab, slab, slab, slab, slab2, slab2, pl.BlockSpec((tm, d), lambda i: (i, 0)),
                   pl.BlockSpec((tm, d), lambda i: (i, 0))],
        out_shape=[slab_shape, slab_shape, slab_shape, slab_shape, slab2_shape, slab2_shape,
                   jax.ShapeDtypeStruct((n, d), F32), jax.ShapeDtypeStruct((n, d), F32)],
        scratch_shapes=[pltpu.VMEM((tm + 2 * NB, d), F32)],
        compiler_params=_cparams(("arbitrary",)),
    )(x, x, x, mod, g, mu3, wr, wk, wv, w1, w2, w0, a1, a2, a0, g1, g2, kkw, ka, rk)


def _wkv_kernel(r_ref, k_ref, v_ref, kk_ref, ew_ref, as_ref, ka_ref, y_ref, st_ref, *, L, npair):
    d = pl.program_id(0)
    c = pl.program_id(1)

    @pl.when(c == 0)
    def _():
        st_ref[...] = jnp.zeros_like(st_ref)

    sgn = 1 - 2 * d
    l2 = 2 * L
    rr = lax.broadcasted_iota(jnp.int32, (l2, l2), 0)
    cc = lax.broadcasted_iota(jnp.int32, (l2, l2), 1)
    same = (rr >= L) == (cc >= L)
    diff = (rr - cc) * sgn
    strict_f = jnp.where(same & (diff > 0), 1.0, 0.0)
    incl_f = jnp.where(same & (diff >= 0), 1.0, 0.0)
    r1 = lax.broadcasted_iota(jnp.int32, (L, L), 0)
    c1 = lax.broadcasted_iota(jnp.int32, (L, L), 1)
    mcum = jnp.where((r1 - c1) * sgn >= 0, 1.0, 0.0).astype(BF16)
    head0 = lax.broadcasted_iota(jnp.int32, (L, LANES), 1) < RWKV_HEAD
    sr_i = lax.broadcasted_iota(jnp.int32, (LANES, LANES), 0)
    sc_i = lax.broadcasted_iota(jnp.int32, (LANES, LANES), 1)
    blockdiag = (sr_i < RWKV_HEAD) == (sc_i < RWKV_HEAD)
    n_dbl = int(round(math.log2(L)))

    def stack2(x):
        return jnp.concatenate([jnp.where(head0, x, 0.0), jnp.where(head0, 0.0, x)], axis=0)

    def unstack(xs):
        return xs[0:L] + xs[L:l2]

    def load(p, b):
        rows = pl.ds(b, L, stride=NB)
        return (r_ref[p, rows, :], k_ref[p, rows, :], v_ref[p, rows, :], kk_ref[p, rows, :],
                ew_ref[0, p, rows, :], as_ref[0, p, rows, :], st_ref[p * NB + b])

    def inst(ka, r, k, v, kk, ew, asg, s0):
        ldec = -ew
        hi = ldec.astype(BF16)
        lo = (ldec - hi.astype(F32)).astype(BF16)
        cl2 = _dot(mcum, jnp.concatenate([hi, lo], axis=1))
        yield
        cl = cl2[:, 0:LANES] + cl2[:, LANES:2 * LANES]
        ce = cl - ldec
        ecl = jnp.exp(cl)
        encl = jnp.exp(-cl)
        etot = jnp.exp(jnp.sum(ldec, axis=0, keepdims=True))
        at = -kk * jnp.exp(ce)
        rt = r * ecl
        bt = kk * asg * encl
        kt = k * (1.0 + (asg - 1.0) * ka) * encl

        ats = stack2(at)
        vsb = stack2(v).astype(BF16)
        lhs = jnp.concatenate([ats, stack2(rt)], axis=0).astype(BF16)
        rhs = jnp.concatenate([stack2(bt), stack2(kt)], axis=0).astype(BF16)
        pm = _dot_nt(lhs, rhs)
        yield
        nab = pm[0:l2, 0:l2] * strict_f
        nak = pm[0:l2, l2:2 * l2] * strict_f
        mrb = (pm[l2:2 * l2, 0:l2] * incl_f).astype(BF16)
        mrk = (pm[l2:2 * l2, l2:2 * l2] * incl_f).astype(BF16)

        xs = jnp.concatenate([ats, _dot(nak.astype(BF16), vsb)], axis=1)
        mkv = _dot(mrk, vsb)
        yield
        npow = nab
        for it in range(n_dbl):
            npb = npow.astype(BF16)
            xs = xs + (_dot_x3(npow, xs) if it == 0 else _dot(npb, xs.astype(BF16)))
            if it + 1 < n_dbl:
                npow = _dot(npb, npb)
            yield
        mx = _dot(mrb, xs.astype(BF16))
        at2 = unstack(xs[:, 0:LANES])
        ul = unstack(xs[:, LANES:2 * LANES])
        btb = bt.astype(BF16)
        emat = (jnp.where(blockdiag, _dot(at2.T.astype(BF16), btb), 0.0) * etot).astype(BF16)
        uv = jnp.concatenate([ul, v], axis=0)
        bk = jnp.concatenate([btb, kt.astype(BF16)], axis=0)
        dmat = jnp.where(blockdiag, _dot(uv.T.astype(BF16), bk), 0.0) * etot
        yield
        rt2 = rt + unstack(mx[:, 0:LANES])
        yl = unstack(mx[:, LANES:2 * LANES] + mkv)

        s0b = s0.astype(BF16)
        return _dot_nt(rt2.astype(BF16), s0b) + yl, s0 * etot + _dot(s0b, emat) + dmat

    def run_group(gens):
        outs = [None] * len(gens)
        live = list(enumerate(gens))
        while live:
            nxt = []
            for i, g in live:
                try:
                    next(g)
                    nxt.append((i, g))
                except StopIteration as stop:
                    outs[i] = stop.value
            live = nxt
        return outs

    def pairs(q, carry):
        work = [(q * WKV_PAIRS + j, b) for j in range(WKV_PAIRS) for b in range(NB)]
        outs = run_group([inst(ka_ref[p], *load(p, b)) for p, b in work])
        for (p, b), (y, s1) in zip(work, outs):
            y_ref[0, p, pl.ds(b, L, stride=NB), :] = y
            st_ref[p * NB + b] = s1
        return carry

    lax.fori_loop(0, npair // WKV_PAIRS, pairs, 0)


def _wkv_call(r, k, v, kk, ew, asg, ka, *, nct_rows):
    npair, n, _ = r.shape
    L = WKV_L
    rows = L * NB
    nc = n // rows
    ncc = nct_rows // rows
    kern = functools.partial(_wkv_kernel, L=L, npair=npair)

    def chunk(dd, c):
        bwd = jnp.where(c < ncc, ncc - 1 - c, nc - 1 + ncc - c)
        return jnp.where(dd == 0, c, bwd)

    slab = pl.BlockSpec((npair, rows, LANES), lambda dd, c: (0, chunk(dd, c), 0))
    slab2 = pl.BlockSpec((1, npair, rows, LANES), lambda dd, c: (dd, 0, chunk(dd, c), 0))
    return pl.pallas_call(
        kern,
        grid=(2, nc),
        in_specs=[slab, slab, slab, slab, slab2, slab2,
                  pl.BlockSpec((npair, 1, LANES), lambda dd, c: (0, 0, 0))],
        out_specs=slab2,
        out_shape=jax.ShapeDtypeStruct((2, npair, n, LANES), F32),
        scratch_shapes=[pltpu.VMEM((npair * NB, LANES, LANES), F32)],
        compiler_params=_cparams(("arbitrary", "arbitrary")),
    )(r, k, v, kk, ew, asg, ka)


def _rw_out_kernel(x_ref, mod_ref, y_ref, gg_ref, bg_ref, lng_ref, lnb_ref, wo_ref, o_ref):
    x = x_ref[...]
    d = x.shape[1]
    y = jnp.concatenate([y_ref[0, p] + y_ref[1, p] for p in range(d // LANES)], axis=1)
    ones_bd = _head_ones()
    inv = 1.0 / RWKV_HEAD
    mean = _head_sum_bcast(y, ones_bd) * inv
    yc = y - mean
    var = _head_sum_bcast(yc * yc, ones_bd) * inv
    yn = yc * lax.rsqrt(var + GN_EPS) * lng_ref[...] + lnb_ref[...]
    z = (yn * gg_ref[...] + bg_ref[...]).astype(BF16)
    o_ref[...] = _residual(x, mod_ref[0, 2], _dot(z, wo_ref[...]))


def _rw_out_call(x, mod, y, gg, bg, lng, lnb, wo, *, nct):
    n, d = x.shape
    nt = n // TM
    npair = d // LANES
    slab2 = pl.BlockSpec((2, npair, TM, LANES), lambda i: (0, 0, i, 0))
    row = pl.BlockSpec((1, d), lambda i: (0, 0))
    return pl.pallas_call(
        _rw_out_kernel,
        grid=(nt,),
        in_specs=[
            pl.BlockSpec((TM, d), lambda i: (i, 0)),
            pl.BlockSpec((1, 6, NB, d), lambda i: (_is_lat(i, nct), 0, 0, 0)),
            slab2,
            pl.BlockSpec((TM, d), lambda i: (i, 0)),
            pl.BlockSpec((TM, d), lambda i: (i, 0)),
            row, row,
            pl.BlockSpec((d, d), lambda i: (0, 0)),
        ],
        out_specs=pl.BlockSpec((TM, d), lambda i: (i, 0)),
        out_shape=jax.ShapeDtypeStruct((n, d), F32),
        compiler_params=_cparams(("arbitrary",)),
    )(x, mod, y, gg, bg, lng, lnb, wo)


def _final_kernel(x_ref, g_ref, o_ref, slab_ref):
    x = x_ref[...]
    d = x.shape[1]
    ms = jnp.mean(x * x, axis=-1, keepdims=True)
    y = x * lax.rsqrt(ms + RMS_EPS) * g_ref[...]
    for p in range(d // LANES):
        slab_ref[p] = y[:, p * LANES:(p + 1) * LANES]
    for b in range(NB):
        for p in range(d // LANES):
            o_ref[b, :, p * LANES:(p + 1) * LANES] = slab_ref[p, pl.ds(b, TT, stride=NB), :]


def _final_call(x, g, *, tile0):
    n, d = x.shape
    nt = n // TM
    return pl.pallas_call(
        _final_kernel,
        grid=(nt - tile0,),
        in_specs=[pl.BlockSpec((TM, d), lambda i: (i + tile0, 0)),
                  pl.BlockSpec((1, d), lambda i: (0, 0))],
        out_specs=pl.BlockSpec((NB, TT, d), lambda i: (0, i, 0)),
        out_shape=jax.ShapeDtypeStruct((NB, (nt - tile0) * TT, d), F32),
        scratch_shapes=[pltpu.VMEM((d // LANES, TM, LANES), F32)],
        compiler_params=_cparams(("arbitrary",)),
    )(x, g)


def _time_major_kernel(c_ref, x_ref, o_ref, slab_ref, *, nct):
    d = o_ref.shape[1]

    def emit(src_ref):
        for b in range(NB):
            for p in range(d // LANES):
                slab_ref[p, pl.ds(b, TT, stride=NB), :] = src_ref[b, :, p * LANES:(p + 1) * LANES]
        for p in range(d // LANES):
            o_ref[:, p * LANES:(p + 1) * LANES] = slab_ref[p]

    @pl.when(pl.program_id(0) < nct)
    def _():
        emit(c_ref)

    @pl.when(pl.program_id(0) >= nct)
    def _():
        emit(x_ref)


def _time_major_call(ctx, x):
    _, lc, d = ctx.shape
    seq = x.shape[1]
    nct = lc // TT
    nt = (lc + seq) // TT
    return pl.pallas_call(
        functools.partial(_time_major_kernel, nct=nct),
        grid=(nt,),
        in_specs=[pl.BlockSpec((NB, TT, d), lambda i: (0, jnp.minimum(i, nct - 1), 0)),
                  pl.BlockSpec((NB, TT, d), lambda i: (0, jnp.maximum(i - nct, 0), 0))],
        out_specs=pl.BlockSpec((TM, d), lambda i: (i, 0)),
        out_shape=jax.ShapeDtypeStruct((nt * TM, d), F32),
        scratch_shapes=[pltpu.VMEM((d // LANES, TM, LANES), F32)],
        compiler_params=_cparams(("arbitrary",)),
    )(ctx, x)


def _row(v):
    return v.reshape(1, -1).astype(F32)


def _s5_layer(x, mod, g, p, prep, *, nct, tile0):
    bb, cc, ab = prep
    y = _s5_scan_call(x, mod, g, bb, cc, ab, nct=nct)
    return _s5_glu_call(x, mod, g, y, _row(p['d']), p['glu_w'].astype(BF16), _row(p['glu_b']),
                        nct=nct, tile0=tile0)


def _conformer_layer(x, mod, g, p, *, nct):
    u = _cv_glu_call(x, mod, g, p['pw1_w'].astype(BF16), _row(p['pw1_b']), nct=nct)
    return _cv_conv_call(u, x, mod, p['dw_w'].astype(F32), _row(p['dw_b']), _row(p['ln_g']),
                         _row(p['ln_b']), p['pw2_w'].astype(BF16), _row(p['pw2_b']), nct=nct)


def _blockdiag2(m):
    z = jnp.zeros_like(m[0])
    return jnp.concatenate([jnp.concatenate([m[0], z], axis=1),
                            jnp.concatenate([z, m[1]], axis=1)], axis=0)


def _rwkv_layer(x, mod, g, p, *, nct):
    n, d = x.shape
    npair = d // LANES
    w1 = jnp.concatenate([p['w1'][0], p['w1'][1]], axis=1).astype(BF16)
    w2 = _blockdiag2(p['w2']).astype(BF16)
    a1 = jnp.concatenate([p['a1'][0], p['a1'][1]], axis=1).astype(BF16)
    a2 = _blockdiag2(p['a2']).astype(BF16)
    gl = p['g1'].shape[1]
    glp = -(-gl // LANES) * LANES
    g1 = jnp.pad(p['g1'], ((0, 0), (0, glp - gl))).astype(BF16)
    g2 = jnp.pad(p['g2'], ((0, glp - gl), (0, 0))).astype(BF16)
    r, k, v, kk, ew, asg, gg, bg = _rw_proj_call(
        x, mod, g, p['mu'].astype(F32), p['w_r'].astype(BF16), p['w_k'].astype(BF16),
        p['w_v'].astype(BF16), w1, w2, _row(p['w0']), a1, a2, _row(p['a0']), g1, g2,
        _row(p['k_k']), _row(p['k_a']), _row(p['r_k']), nct=nct)
    ka_slab = p['k_a'].astype(F32).reshape(npair, 1, LANES)
    y = _wkv_call(r, k, v, kk, ew, asg, ka_slab, nct_rows=nct * TM)
    return _rw_out_call(x, mod, y, gg, bg, _row(p['lnx_g']), _row(p['lnx_b']),
                        p['w_o'].astype(BF16), nct=nct)


def _forward(x, c, ctx, c_ctx, ada_w, ada_b, norm_mix_g, norm_ffn_g, final_g, s5, cv, rw,
             ff_w1, ff_w3, ff_w2, moe_router, moe_w1, moe_w3, moe_w2):
    bsz, seq, d = x.shape
    lc = ctx.shape[1]
    depth = ada_w.shape[0]
    assert bsz == NB and lc % TT == 0 and seq % TT == 0
    assert TT == GRID_W, "one token tile must hold exactly one latent raster row per batch entry"
    nct = lc // TT
    t_all = lc + seq
    n = t_all * NB

    xs = _time_major_call(ctx.astype(F32), x.astype(F32))

    cvec = jnp.zeros((2 * NB, d), F32).at[:NB].set(c).at[NB].set(c_ctx)
    m_all = _ada_call(cvec, ada_w.astype(F32), ada_b.astype(F32))
    m_all = m_all.reshape(depth, 2 * NB, 6, d)
    m_lat = jnp.swapaxes(m_all[:, :NB], 1, 2)
    m_ctx = jnp.broadcast_to(m_all[:, NB][:, :, None, :], (depth, 6, NB, d))
    mods = jnp.stack([m_ctx, m_lat], axis=1)

    s5_prep = jax.vmap(_s5_params)(s5['lam_re'], s5['lam_im'], s5['log_dt'], s5['b_re'], s5['b_im'],
                                   s5['c_re'], s5['c_im'])
    ctx_rows = nct * TM
    for i in range(depth):
        last = i == depth - 1
        mod = mods[i]
        g_mix = _row(norm_mix_g[i])
        j = i // 3
        if i % 3 == 0:
            xs = _s5_layer(xs, mod, g_mix, {k_: v_[j] for k_, v_ in s5.items()},
                           tuple(a[j] for a in s5_prep), nct=nct, tile0=nct if last else 0)
            ctx_rows = 0 if last else ctx_rows
        elif i % 3 == 1:
            xs = _conformer_layer(xs, mod, g_mix, {k_: v_[j] for k_, v_ in cv.items()}, nct=nct)
        else:
            xs = _rwkv_layer(xs, mod, g_mix, {k_: v_[j] for k_, v_ in rw.items()}, nct=nct)
        fi = i // 2
        g_ffn = _row(norm_ffn_g[i])
        if i % 2 == 0:
            xs = _ffn_call(xs, mod, g_ffn, ff_w1, ff_w3, ff_w2, layer=fi, nct_rows=ctx_rows)
        else:
            xs = _moe_call(xs, mod, g_ffn, moe_router[fi], moe_w1, moe_w3, moe_w2, layer=fi,
                           nct_rows=ctx_rows)

    return _final_call(xs, _row(final_g), tile0=ctx_rows // TM)


def kernel(x, c, ctx, c_ctx, ada_w, ada_b, norm_mix_g, norm_ffn_g, final_g, s5_lambda_re, s5_lambda_im, s5_log_dt, s5_b_re, s5_b_im, s5_c_re, s5_c_im, s5_d, s5_glu_w, s5_glu_b, cv_pw1_w, cv_pw1_b, cv_dw_w, cv_dw_b, cv_ln_g, cv_ln_b, cv_pw2_w, cv_pw2_b, rw_mu, rw_w_r, rw_w_k, rw_w_v, rw_w_o, rw_w0, rw_w1, rw_w2, rw_a0, rw_a1, rw_a2, rw_g1, rw_g2, rw_k_k, rw_k_a, rw_r_k, rw_lnx_g, rw_lnx_b, ff_w1, ff_w3, ff_w2, moe_router, moe_w1, moe_w3, moe_w2):
    s5 = dict(lam_re=s5_lambda_re, lam_im=s5_lambda_im, log_dt=s5_log_dt, b_re=s5_b_re, b_im=s5_b_im,
              c_re=s5_c_re, c_im=s5_c_im, d=s5_d, glu_w=s5_glu_w, glu_b=s5_glu_b)
    cv = dict(pw1_w=cv_pw1_w, pw1_b=cv_pw1_b, dw_w=cv_dw_w, dw_b=cv_dw_b, ln_g=cv_ln_g, ln_b=cv_ln_b,
              pw2_w=cv_pw2_w, pw2_b=cv_pw2_b)
    rw = dict(mu=rw_mu, w_r=rw_w_r, w_k=rw_w_k, w_v=rw_w_v, w_o=rw_w_o, w0=rw_w0, w1=rw_w1, w2=rw_w2,
              a0=rw_a0, a1=rw_a1, a2=rw_a2, g1=rw_g1, g2=rw_g2, k_k=rw_k_k, k_a=rw_k_a, r_k=rw_r_k,
              lnx_g=rw_lnx_g, lnx_b=rw_lnx_b)
    return _forward(x, c, ctx, c_ctx, ada_w, ada_b, norm_mix_g, norm_ffn_g, final_g, s5, cv, rw,
                    ff_w1, ff_w3, ff_w2, moe_router, moe_w1, moe_w3, moe_w2)
```

```python
import functools
import math

import jax
import jax.numpy as jnp
from jax import lax
from jax.experimental import pallas as pl
from jax.experimental.pallas import tpu as pltpu

F32 = jnp.float32
BF16 = jnp.bfloat16

NB = 8
LANES = 128
TM = 512
TT = TM // NB
GRID_W = 64
CONV_WIDTH = 31
CONV_PAD = CONV_WIDTH // 2
HALO_ROWS = 128
S5_GROUP = 16
S5_STATE = 64
S5_JBLK = 8
RWKV_HEAD = 64
WKV_L = 64
WKV_PAIRS = 2
RW_TM = 256
FFN_TM = 1024
MOE_T = 1024
MOE_BLK = 32
MOE_SLOTS = 32
RMS_EPS = 1e-6
LN_EPS = 1e-5
GN_EPS = 64e-5
KK_EPS = 1e-12
LAMBDA_RE_MAX = -1e-4
VMEM_LIMIT = 56 * 1024 * 1024


def _cparams(sem):
    return pltpu.CompilerParams(dimension_semantics=sem, vmem_limit_bytes=VMEM_LIMIT)


def _dot(a, b):
    return jnp.dot(a, b, preferred_element_type=F32)


def _dot_nt(a, b):
    return lax.dot_general(a, b, (((1,), (1,)), ((), ())), preferred_element_type=F32)


def _dot_x3(a, b):
    a1 = a.astype(BF16)
    a2 = (a - a1.astype(F32)).astype(BF16)
    b1 = b.astype(BF16)
    b2 = (b - b1.astype(F32)).astype(BF16)
    return _dot(a1, b1) + _dot(a1, b2) + _dot(a2, b1)


def _sigmoid(x):
    return jax.nn.sigmoid(x)


def _norm_mod(x, g, shift, scale):
    rows, d = x.shape
    ms = jnp.mean(x * x, axis=-1, keepdims=True)
    y = x * lax.rsqrt(ms + RMS_EPS) * g
    y3 = y.reshape(rows // NB, NB, d)
    return (y3 * (1.0 + scale)[None] + shift[None]).reshape(rows, d)


def _residual(x, gate, y):
    rows, d = x.shape
    y3 = y.reshape(rows // NB, NB, d) * gate[None]
    return x + y3.reshape(rows, d)


def _is_lat(i, nct):
    return jnp.where(i >= nct, 1, 0).astype(jnp.int32)


def _ada_kernel(s_ref, w_ref, b_ref, o_ref):
    s = s_ref[...]
    s = s * _sigmoid(s)
    o_ref[0] = _dot(s.astype(BF16), w_ref[0].astype(BF16)) + b_ref[0]


def _ada_call(cvec, ada_w, ada_b):
    depth, d, d6 = ada_w.shape
    tn = 1536
    return pl.pallas_call(
        _ada_kernel,
        grid=(depth, d6 // tn),
        in_specs=[
            pl.BlockSpec((2 * NB, d), lambda l, j: (0, 0)),
            pl.BlockSpec((1, d, tn), lambda l, j: (l, 0, j)),
            pl.BlockSpec((1, 1, tn), lambda l, j: (l, 0, j)),
        ],
        out_specs=pl.BlockSpec((1, 2 * NB, tn), lambda l, j: (l, 0, j)),
        out_shape=jax.ShapeDtypeStruct((depth, 2 * NB, d6), F32),
        compiler_params=_cparams(("arbitrary", "arbitrary")),
    )(cvec, ada_w, ada_b.reshape(depth, 1, d6))


def _ffn_kernel(x_ref, mod_ref, g_ref, w1_ref, w3_ref, w2_ref, o_ref, hb_ref, acc_ref, *, n_k):
    k = pl.program_id(1)

    @pl.when(k == 0)
    def _():
        h = _norm_mod(x_ref[...], g_ref[...], mod_ref[0, 3], mod_ref[0, 4])
        hb_ref[...] = h.astype(BF16)
        acc_ref[...] = jnp.zeros_like(acc_ref)

    hb = hb_ref[...]
    a = _dot(hb, w1_ref[0].astype(BF16))
    b = _dot(hb, w3_ref[0].astype(BF16))
    acc_ref[...] += _dot((a * _sigmoid(a) * b).astype(BF16), w2_ref[0].astype(BF16))

    @pl.when(k == n_k - 1)
    def _():
        o_ref[...] = _residual(x_ref[...], mod_ref[0, 5], acc_ref[...])


def _ffn_call(x, mod, g, w1, w3, w2, *, layer, nct_rows):
    n, d = x.shape
    f = w1.shape[2]
    tm = FFN_TM
    tf = 512
    n_k = f // tf
    nct = nct_rows // tm
    return pl.pallas_call(
        functools.partial(_ffn_kernel, n_k=n_k),
        grid=(n // tm, n_k),
        in_specs=[
            pl.BlockSpec((tm, d), lambda i, k: (i, 0)),
            pl.BlockSpec((1, 6, NB, d), lambda i, k: (_is_lat(i, nct), 0, 0, 0)),
            pl.BlockSpec((1, d), lambda i, k: (0, 0)),
            pl.BlockSpec((1, d, tf), lambda i, k: (layer, 0, k)),
            pl.BlockSpec((1, d, tf), lambda i, k: (layer, 0, k)),
            pl.BlockSpec((1, tf, d), lambda i, k: (layer, k, 0)),
        ],
        out_specs=pl.BlockSpec((tm, d), lambda i, k: (i, 0)),
        out_shape=jax.ShapeDtypeStruct((n, d), F32),
        scratch_shapes=[pltpu.VMEM((tm, d), BF16), pltpu.VMEM((tm, d), F32)],
        compiler_params=_cparams(("arbitrary", "arbitrary")),
    )(x, mod, g, w1, w3, w2)


def _top2(logits, n_e):
    lane = lax.broadcasted_iota(jnp.int32, logits.shape, 1).astype(F32)
    neg = jnp.float32(-3.0e38)
    lg = jnp.where(lane < n_e, logits, neg)
    m1 = jnp.max(lg, axis=-1, keepdims=True)
    i1 = jnp.min(jnp.where(lg == m1, lane, 1.0e9), axis=-1, keepdims=True)
    lg2 = jnp.where(lane == i1, neg, lg)
    m2 = jnp.max(lg2, axis=-1, keepdims=True)
    i2 = jnp.min(jnp.where(lg2 == m2, lane, 1.0e9), axis=-1, keepdims=True)
    e2 = jnp.exp(m2 - m1)
    den = 1.0 + e2
    return lane, i1, i2, 1.0 / den, e2 / den


def _moe_route_kernel(x_ref, mod_ref, g_ref, router_ref, ltri_ref, z_ref, dest_ref, cnt_ref,
                      hb_ref, *, n_e, cap):
    t, d = x_ref.shape
    h = _norm_mod(x_ref[...], g_ref[...], mod_ref[0, 3], mod_ref[0, 4])
    hb_ref[...] = h.astype(BF16)
    lane, i1, i2, p1, p2 = _top2(_dot_x3(h, router_ref[...]), n_e)
    sel1 = lane == i1
    sel2 = lane == i2
    c = jnp.where(sel1, 1.0, 0.0) + jnp.where(sel2, 1.0, 0.0)
    rank = _dot(ltri_ref[...], c.astype(BF16))
    cnt = jnp.sum(c, axis=0, keepdims=True)
    seg = jnp.floor((cnt + (MOE_BLK - 1)) * (1.0 / MOE_BLK)) * MOE_BLK
    ur = lax.broadcasted_iota(jnp.int32, (LANES, LANES), 0)
    uc = lax.broadcasted_iota(jnp.int32, (LANES, LANES), 1)
    before = jnp.where(ur < uc, 1.0, 0.0).astype(BF16)
    start = _dot(jnp.broadcast_to(seg, (NB, LANES)).astype(BF16), before)[0:1]
    pos = start + rank
    d1 = jnp.sum(jnp.where(sel1, pos, 0.0), axis=-1, keepdims=True)
    d2 = jnp.sum(jnp.where(sel2, pos, 0.0), axis=-1, keepdims=True)
    packed = (jnp.where(lane == 0.0, d1, 0.0) + jnp.where(lane == 1.0, d2, 0.0)
              + jnp.where(lane == 2.0, p1, 0.0) + jnp.where(lane == 3.0, p2, 0.0))
    dest_ref[...] = packed
    cnt_ref[0] = jnp.broadcast_to(cnt, (NB, LANES))
    packed_t = packed.T
    d1r = packed_t[0:1]
    d2r = packed_t[1:2]
    p1r = packed_t[2:3]
    p2r = packed_t[3:4]
    rc = 256

    def chunk(ci, carry):
        r0 = pl.multiple_of(ci * rc, rc)
        row = (lax.broadcasted_iota(jnp.int32, (rc, t), 0) + r0).astype(F32)
        m1 = row == d1r
        m2 = row == d2r
        perm = (jnp.where(m1, 1.0, 0.0) + jnp.where(m2, 1.0, 0.0)).astype(BF16)
        zc = _dot(perm, hb_ref[...])
        gate = jnp.sum(jnp.where(m1, p1r, 0.0) + jnp.where(m2, p2r, 0.0), axis=-1, keepdims=True)
        ghi = gate.astype(BF16).astype(F32)
        glo = gate - ghi
        le = lax.broadcasted_iota(jnp.int32, (rc, LANES), 1)
        ext = jnp.where(le == 0, ghi, jnp.where(le == 1, glo, 0.0))
        z_ref[pl.ds(r0, rc), 0:d] = zc.astype(BF16)
        z_ref[pl.ds(r0, rc), d:d + LANES] = ext.astype(BF16)
        return carry

    lax.fori_loop(0, cap // rc, chunk, 0)


def _moe_experts_kernel(tin_ref, se_ref, nv_ref, *refs, n_k, d):
    z_refs = refs[:MOE_SLOTS]
    w1_ref, w3_ref, w2_ref, o_ref, xs_ref, acc_ref = refs[MOE_SLOTS:]
    s = pl.program_id(0)
    k = pl.program_id(1)
    nv = nv_ref[s]

    @pl.when((nv > 0) & (k == 0))
    def _():
        for j in range(MOE_SLOTS):
            xs_ref[j * MOE_BLK:(j + 1) * MOE_BLK] = z_refs[j][...]
        acc_ref[...] = jnp.zeros_like(acc_ref)

    def compute(nrows):
        hb = xs_ref[0:nrows, 0:d]
        ext = xs_ref[0:nrows, d:d + LANES].astype(F32)
        le = lax.broadcasted_iota(jnp.int32, ext.shape, 1)
        gate = jnp.sum(jnp.where(le < 2, ext, 0.0), axis=-1, keepdims=True)
        a = _dot(hb, w1_ref[0, 0].astype(BF16))
        b = _dot(hb, w3_ref[0, 0].astype(BF16))
        act = a * _sigmoid(a) * b * gate
        acc_ref[0:nrows] += _dot(act.astype(BF16), w2_ref[0, 0].astype(BF16))

    quarter = MOE_SLOTS // 4
    for q in range(1, 5):
        @pl.when((nv > (q - 1) * quarter) & (nv <= q * quarter))
        def _():
            compute(q * quarter * MOE_BLK)

    @pl.when(k == n_k - 1)
    def _():
        o_ref[...] = jnp.where(nv > 0, acc_ref[...], 0.0).astype(BF16)


def _moe_combine_kernel(tc_ref, *refs, nblk, cap):
    zb_refs = refs[:nblk]
    dest_ref, x_ref, mod_ref, o_ref, zbuf_ref = refs[nblk:]
    t, d = x_ref.shape
    for j in range(nblk):
        zbuf_ref[j * MOE_BLK:(j + 1) * MOE_BLK] = zb_refs[j][...]
    packed = dest_ref[...]
    lane = lax.broadcasted_iota(jnp.int32, packed.shape, 1)
    d1 = jnp.sum(jnp.where(lane == 0, packed, 0.0), axis=-1, keepdims=True)
    d2 = jnp.sum(jnp.where(lane == 1, packed, 0.0), axis=-1, keepdims=True)
    kc = 256
    acc = jnp.zeros((t, d), F32)
    for c0 in range(0, cap, kc):
        col = (lax.broadcasted_iota(jnp.int32, (t, kc), 1) + c0).astype(F32)
        sel = (jnp.where(col == d1, 1.0, 0.0) + jnp.where(col == d2, 1.0, 0.0)).astype(BF16)
        acc = acc + _dot(sel, zbuf_ref[c0:c0 + kc])
    o_ref[...] = _residual(x_ref[...], mod_ref[0, 5], acc)


def _moe_tables(cnt, *, n_tiles, n_e, nblk, n_steps):
    i32 = jnp.int32
    e_ids = jnp.arange(n_e, dtype=i32)
    t_ids = jnp.arange(n_tiles, dtype=i32)
    pc = (cnt + (MOE_BLK - 1)) // MOE_BLK
    lb = jnp.cumsum(pc, axis=1) - pc
    cum = jnp.cumsum(pc, axis=0)
    cum_ex = cum - pc
    nb_e = cum[-1]
    steps_e = (nb_e + (MOE_SLOTS - 1)) // MOE_SLOTS
    ends = jnp.cumsum(steps_e)
    step0 = ends - steps_e
    sidx = jnp.arange(n_steps, dtype=i32)
    step_e = jnp.minimum(jnp.sum((sidx[:, None] >= ends[None, :]).astype(i32), axis=1), n_e - 1)
    oh_e = (step_e[:, None] == e_ids[None, :]).astype(i32)
    by_step = lambda v: jnp.sum(oh_e * v[None, :], axis=1)
    by_step_tile = lambda m: jnp.sum(oh_e[:, None, :] * m[None, :, :], axis=2)
    rank = ((sidx - by_step(step0))[:, None] * MOE_SLOTS
            + jnp.arange(MOE_SLOTS, dtype=i32)[None, :])
    cum_s = by_step_tile(cum)
    tile = jnp.sum((cum_s[:, None, :] <= rank[:, :, None]).astype(i32), axis=2)
    nvalid = jnp.where(sidx < ends[-1], jnp.clip(by_step(nb_e) - rank[:, 0], 0, MOE_SLOTS), 0)
    ok = rank < (rank[:, :1] + nvalid[:, None])
    oh_t = (jnp.minimum(tile, n_tiles - 1)[:, :, None] == t_ids[None, None, :]).astype(i32)
    at_tile = lambda m: jnp.sum(oh_t * by_step_tile(m)[:, None, :], axis=2)
    src = jnp.sum(oh_t * t_ids, axis=2) * nblk + at_tile(lb) + rank - at_tile(cum_ex)
    tin = jnp.where(ok, src, 0).reshape(-1).astype(i32)
    lblk = jnp.arange(nblk, dtype=i32)[None, :]
    seg_end = lb + pc
    e_l = jnp.minimum(jnp.sum((seg_end[:, None, :] <= lblk[:, :, None]).astype(i32), axis=2), n_e - 1)
    oh_l = (e_l[:, :, None] == e_ids[None, None, :]).astype(i32)
    r_l = jnp.sum(oh_l * (cum_ex - lb)[:, None, :], axis=2) + lblk
    flat = (jnp.sum(oh_l * step0, axis=2) + r_l // MOE_SLOTS) * MOE_SLOTS + r_l % MOE_SLOTS
    tcomb = jnp.where(lblk < seg_end[:, -1:], flat, 0).reshape(-1).astype(i32)
    return tin, tcomb, step_e.astype(i32), nvalid.astype(i32)


def _moe_call(x, mod, g, router, w1, w3, w2, *, layer, nct_rows):
    n, d = x.shape
    _, n_e, _, f = w1.shape
    t = MOE_T
    assert n % t == 0 and nct_rows % t == 0
    n_tiles = n // t
    nct = nct_rows // t
    cap = 2 * t + n_e * MOE_BLK
    nblk = cap // MOE_BLK
    rows = MOE_SLOTS * MOE_BLK
    n_steps = -(-(n_tiles * nblk) // MOE_SLOTS) + n_e
    tf = 512
    n_k = f // tf
    router_p = jnp.pad(router.astype(F32), ((0, 0), (0, LANES - n_e)))
    ltri = jnp.tril(jnp.ones((t, t), F32), k=-1).astype(BF16)
    mod_spec = lambda off: pl.BlockSpec((1, 6, NB, d), lambda i, *_: (_is_lat(i, nct), 0, 0, 0))

    z, dest, cnt = pl.pallas_call(
        functools.partial(_moe_route_kernel, n_e=n_e, cap=cap),
        grid=(n_tiles,),
        in_specs=[
            pl.BlockSpec((t, d), lambda i: (i, 0)),
            mod_spec(0),
            pl.BlockSpec((1, d), lambda i: (0, 0)),
            pl.BlockSpec((d, LANES), lambda i: (0, 0)),
            pl.BlockSpec((t, t), lambda i: (0, 0)),
        ],
        out_specs=[
            pl.BlockSpec((cap, d + LANES), lambda i: (i, 0)),
            pl.BlockSpec((t, LANES), lambda i: (i, 0)),
            pl.BlockSpec((1, NB, LANES), lambda i: (i, 0, 0)),
        ],
        out_shape=[
            jax.ShapeDtypeStruct((n_tiles * cap, d + LANES), BF16),
            jax.ShapeDtypeStruct((n, LANES), F32),
            jax.ShapeDtypeStruct((n_tiles, NB, LANES), F32),
        ],
        scratch_shapes=[pltpu.VMEM((t, d), BF16)],
        compiler_params=_cparams(("arbitrary",)),
    )(x, mod, g, router_p, ltri)

    tin, tcomb, step_e, nvalid = _moe_tables(cnt[:, 0, :n_e].astype(jnp.int32), n_tiles=n_tiles,
                                            n_e=n_e, nblk=nblk, n_steps=n_steps)

    def kk(s, k, nv_r):
        return jnp.where(nv_r[s] > 0, k, n_k - 1)

    def z_spec(j):
        return pl.BlockSpec((MOE_BLK, d + LANES), lambda s, k, tin_r, se_r, nu_r: (tin_r[s * MOE_SLOTS + j], 0))

    zo = pl.pallas_call(
        functools.partial(_moe_experts_kernel, n_k=n_k, d=d),
        grid_spec=pltpu.PrefetchScalarGridSpec(
            num_scalar_prefetch=3,
            grid=(n_steps, n_k),
            in_specs=[z_spec(j) for j in range(MOE_SLOTS)] + [
                pl.BlockSpec((1, 1, d, tf), lambda s, k, tin_r, se_r, nu_r: (layer, se_r[s], 0, kk(s, k, nu_r))),
                pl.BlockSpec((1, 1, d, tf), lambda s, k, tin_r, se_r, nu_r: (layer, se_r[s], 0, kk(s, k, nu_r))),
                pl.BlockSpec((1, 1, tf, d), lambda s, k, tin_r, se_r, nu_r: (layer, se_r[s], kk(s, k, nu_r), 0)),
            ],
            out_specs=pl.BlockSpec((rows, d), lambda s, k, tin_r, se_r, nu_r: (s, 0)),
            scratch_shapes=[pltpu.VMEM((rows, d + LANES), BF16), pltpu.VMEM((rows, d), F32)],
        ),
        out_shape=jax.ShapeDtypeStruct((n_steps * rows, d), BF16),
        compiler_params=_cparams(("arbitrary", "arbitrary")),
    )(tin, step_e, nvalid, *([z] * MOE_SLOTS), w1, w3, w2)

    def zo_spec(j):
        return pl.BlockSpec((MOE_BLK, d), lambda i, tc_r: (tc_r[i * nblk + j], 0))

    return pl.pallas_call(
        functools.partial(_moe_combine_kernel, nblk=nblk, cap=cap),
        grid_spec=pltpu.PrefetchScalarGridSpec(
            num_scalar_prefetch=1,
            grid=(n_tiles,),
            in_specs=[zo_spec(j) for j in range(nblk)] + [
                pl.BlockSpec((t, LANES), lambda i, tc_r: (i, 0)),
                pl.BlockSpec((t, d), lambda i, tc_r: (i, 0)),
                mod_spec(0),
            ],
            out_specs=pl.BlockSpec((t, d), lambda i, tc_r: (i, 0)),
            scratch_shapes=[pltpu.VMEM((cap, d), BF16)],
        ),
        out_shape=jax.ShapeDtypeStruct((n, d), F32),
        compiler_params=_cparams(("arbitrary",)),
    )(tcomb, *([zo] * nblk), dest, x, mod)


def _dir_tile(d, s, nct, nt):
    bwd = jnp.where(s < nct, nct - 1 - s, nt - 1 + nct - s)
    return jnp.where(d == 0, s, bwd)


def _s5_scan_kernel(x_ref, mod_ref, g_ref, bb_ref, cc_ref, ab_ref, y_ref,
                    hb_ref, bu0_ref, bu1_ref, bu2_ref, st_ref):
    d = pl.program_id(0)
    s = pl.program_id(1)
    half = S5_JBLK * S5_STATE
    bufs = (bu0_ref, bu1_ref, bu2_ref)

    @pl.when(s == 0)
    def _():
        st_ref[...] = jnp.zeros_like(st_ref)

    h = _norm_mod(x_ref[...], g_ref[...], mod_ref[0, 0], mod_ref[0, 1])
    hb_ref[...] = h.astype(BF16)

    def project(j):
        bufs[j % 3][...] = _dot(hb_ref[:, j * LANES:(j + 1) * LANES], bb_ref[0, j])

    def scan(j):
        buf = bufs[j % 3]
        ar = ab_ref[0, j, 0]
        ai = ab_ref[0, j, 1]
        re = st_ref[j, 0]
        im = st_ref[j, 1]
        for t in range(TT):
            tt = t + d * (TT - 1 - 2 * t)
            r0 = pl.multiple_of(tt * NB, NB)
            bre = buf[pl.ds(r0, NB), 0:half]
            bim = buf[pl.ds(r0, NB), half:2 * half]
            re, im = ar * re - ai * im + bre, ar * im + ai * re + bim
            buf[pl.ds(r0, NB), 0:half] = re
            buf[pl.ds(r0, NB), half:2 * half] = im
        st_ref[j, 0] = re
        st_ref[j, 1] = im

    def readout(j):
        y_ref[0, j] = _dot(bufs[j % 3][...].astype(BF16), cc_ref[0, j])

    for i in range(S5_JBLK + 2):
        if i < S5_JBLK:
            project(i)
        if 0 <= i - 1 < S5_JBLK:
            scan(i - 1)
        if 0 <= i - 2 < S5_JBLK:
            readout(i - 2)


def _s5_scan_call(x, mod, g, bb, cc, ab, *, nct):
    n, d = x.shape
    nt = n // TM
    half = S5_JBLK * S5_STATE
    tile = lambda dd, s: _dir_tile(dd, s, nct, nt)
    return pl.pallas_call(
        _s5_scan_kernel,
        grid=(2, nt),
        in_specs=[
            pl.BlockSpec((TM, d), lambda dd, s: (tile(dd, s), 0)),
            pl.BlockSpec((1, 6, NB, d), lambda dd, s: (_is_lat(tile(dd, s), nct), 0, 0, 0)),
            pl.BlockSpec((1, d), lambda dd, s: (0, 0)),
            pl.BlockSpec((1, S5_JBLK, LANES, 2 * half), lambda dd, s: (dd, 0, 0, 0)),
            pl.BlockSpec((1, S5_JBLK, 2 * half, LANES), lambda dd, s: (dd, 0, 0, 0)),
            pl.BlockSpec((1, S5_JBLK, 2, NB, half), lambda dd, s: (dd, 0, 0, 0, 0)),
        ],
        out_specs=pl.BlockSpec((1, S5_JBLK, TM, LANES), lambda dd, s: (dd, 0, tile(dd, s), 0)),
        out_shape=jax.ShapeDtypeStruct((2, S5_JBLK, n, LANES), F32),
        scratch_shapes=[
            pltpu.VMEM((TM, d), BF16),
            pltpu.VMEM((TM, 2 * half), F32),
            pltpu.VMEM((TM, 2 * half), F32),
            pltpu.VMEM((TM, 2 * half), F32),
            pltpu.VMEM((S5_JBLK, 2, NB, half), F32),
        ],
        compiler_params=_cparams(("arbitrary", "arbitrary")),
    )(x, mod, g, bb, cc, ab)


def _gelu_tanh(x):
    c = math.sqrt(2.0 / math.pi)
    return 0.5 * x * (1.0 + jnp.tanh(c * (x + 0.044715 * (x * x * x))))


def _s5_glu_kernel(x_ref, mod_ref, g_ref, y_ref, dsk_ref, w_ref, b_ref, o_ref):
    x = x_ref[...]
    d = x.shape[1]
    h = _norm_mod(x, g_ref[...], mod_ref[0, 0], mod_ref[0, 1])
    y = jnp.concatenate([y_ref[0, j] + y_ref[1, j] for j in range(S5_JBLK)], axis=1)
    z = _gelu_tanh(y + dsk_ref[...] * h).astype(BF16)
    zz = _dot(z, w_ref[...]) + b_ref[...]
    out = zz[:, :d] * _sigmoid(zz[:, d:])
    o_ref[...] = _residual(x, mod_ref[0, 2], out)


def _s5_glu_call(x, mod, g, y, dsk, w, b, *, nct, tile0):
    n, d = x.shape
    nt = n // TM
    return pl.pallas_call(
        _s5_glu_kernel,
        grid=(nt - tile0,),
        in_specs=[
            pl.BlockSpec((TM, d), lambda i: (i + tile0, 0)),
            pl.BlockSpec((1, 6, NB, d), lambda i: (_is_lat(i + tile0, nct), 0, 0, 0)),
            pl.BlockSpec((1, d), lambda i: (0, 0)),
            pl.BlockSpec((2, S5_JBLK, TM, LANES), lambda i: (0, 0, i + tile0, 0)),
            pl.BlockSpec((1, d), lambda i: (0, 0)),
            pl.BlockSpec((d, 2 * d), lambda i: (0, 0)),
            pl.BlockSpec((1, 2 * d), lambda i: (0, 0)),
        ],
        out_specs=pl.BlockSpec((TM, d), lambda i: (i, 0)),
        out_shape=jax.ShapeDtypeStruct((n - tile0 * TM, d), F32),
        compiler_params=_cparams(("arbitrary",)),
    )(x, mod, g, y, dsk, w, b)


def _s5_params(lam_re, lam_im, log_dt, b_re, b_im, c_re, c_im):
    lam_re = jnp.minimum(lam_re.astype(F32), LAMBDA_RE_MAX)
    lam_im = lam_im.astype(F32)
    dt = jnp.exp(log_dt.astype(F32))[..., None]
    mag = jnp.exp(lam_re * dt)
    abar_re = mag * jnp.cos(lam_im * dt)
    abar_im = mag * jnp.sin(lam_im * dt)
    inv_den = 1.0 / (lam_re * lam_re + lam_im * lam_im)
    gam_re = ((abar_re - 1.0) * lam_re + abar_im * lam_im) * inv_den
    gam_im = (abar_im * lam_re - (abar_re - 1.0) * lam_im) * inv_den
    b_re = b_re.astype(F32)
    b_im = b_im.astype(F32)
    bb_re = gam_re[..., None] * b_re - gam_im[..., None] * b_im
    bb_im = gam_re[..., None] * b_im + gam_im[..., None] * b_re
    g_all = lam_re.shape[1]
    nj = g_all // S5_JBLK
    eye = jnp.eye(S5_JBLK, dtype=F32)

    def in_mat(m):
        m = m.reshape(2, nj, S5_JBLK, S5_STATE, S5_GROUP)
        m = jnp.einsum('djgnp,gh->djgphn', m, eye)
        return m.reshape(2, nj, S5_JBLK * S5_GROUP, S5_JBLK * S5_STATE)

    def out_mat(m):
        m = m.reshape(2, nj, S5_JBLK, S5_GROUP, S5_STATE)
        m = jnp.einsum('djgpn,gh->djgnhp', m, eye)
        return m.reshape(2, nj, S5_JBLK * S5_STATE, S5_JBLK * S5_GROUP)

    bb = jnp.concatenate([in_mat(bb_re), in_mat(bb_im)], axis=-1).astype(BF16)
    cc = jnp.concatenate([out_mat(c_re.astype(F32)), -out_mat(c_im.astype(F32))], axis=-2).astype(BF16)
    half = S5_JBLK * S5_STATE
    ab = jnp.stack([abar_re.reshape(2, nj, half), abar_im.reshape(2, nj, half)], axis=2)
    ab = jnp.broadcast_to(ab[:, :, :, None, :], (2, nj, 2, NB, half))
    return bb, cc, ab


def _cv_glu_kernel(x_ref, mod_ref, g_ref, w_ref, b_ref, o_ref):
    x = x_ref[...]
    d = x.shape[1]
    h = _norm_mod(x, g_ref[...], mod_ref[0, 0], mod_ref[0, 1]).astype(BF16)
    u = _dot(h, w_ref[...]) + b_ref[...]
    o_ref[...] = u[:, :d] * _sigmoid(u[:, d:])


def _cv_glu_call(x, mod, g, w, b, *, nct):
    n, d = x.shape
    nt = n // TM
    return pl.pallas_call(
        _cv_glu_kernel,
        grid=(nt,),
        in_specs=[
            pl.BlockSpec((TM, d), lambda i: (i, 0)),
            pl.BlockSpec((1, 6, NB, d), lambda i: (_is_lat(i, nct), 0, 0, 0)),
            pl.BlockSpec((1, d), lambda i: (0, 0)),
            pl.BlockSpec((d, 2 * d), lambda i: (0, 0)),
            pl.BlockSpec((1, 2 * d), lambda i: (0, 0)),
        ],
        out_specs=pl.BlockSpec((TM, d), lambda i: (i, 0)),
        out_shape=jax.ShapeDtypeStruct((n, d), F32),
        compiler_params=_cparams(("arbitrary",)),
    )(x, mod, g, w, b)


def _cv_conv_kernel(u_ref, up_ref, un_ref, x_ref, mod_ref, dw_ref, dwb_ref, lng_ref, lnb_ref,
                    w_ref, b_ref, o_ref, pad_ref, cv_ref, *, nct):
    i = pl.program_id(0)
    halo = CONV_PAD * NB
    prev_ok = (i > 0) & (i < nct)
    next_ok = i < nct - 1
    pad_ref[0:halo] = jnp.where(prev_ok, up_ref[HALO_ROWS - halo:HALO_ROWS], 0.0)
    pad_ref[halo:halo + TM] = u_ref[...]
    pad_ref[halo + TM:halo + TM + halo] = jnp.where(next_ok, un_ref[0:halo], 0.0)

    rb = 32
    lw = 512
    d = dwb_ref.shape[1]

    def body(c, _):
        r0 = pl.multiple_of(c * rb, rb)
        for l0 in range(0, d, lw):
            bias = jnp.broadcast_to(dwb_ref[:, l0:l0 + lw], (NB, lw))
            acc = [bias] * (rb // NB)
            for k in range(CONV_WIDTH):
                w = dw_ref[k, :, l0:l0 + lw]
                for g in range(rb // NB):
                    acc[g] = acc[g] + w * pad_ref[pl.ds(r0 + (k + g) * NB, NB), l0:l0 + lw]
            for g in range(rb // NB):
                cv_ref[pl.ds(r0 + g * NB, NB), l0:l0 + lw] = acc[g]
        return 0

    lax.fori_loop(0, TM // rb, body, 0)

    c = cv_ref[...]
    mu = jnp.mean(c, axis=-1, keepdims=True)
    cc = c - mu
    var = jnp.mean(cc * cc, axis=-1, keepdims=True)
    y = cc * lax.rsqrt(var + LN_EPS) * lng_ref[...] + lnb_ref[...]
    y = (y * _sigmoid(y)).astype(BF16)
    out = _dot(y, w_ref[...]) + b_ref[...]
    o_ref[...] = _residual(x_ref[...], mod_ref[0, 2], out)


def _cv_conv_call(u, x, mod, dw_w, dw_b, ln_g, ln_b, w, b, *, nct):
    n, d = x.shape
    nt = n // TM
    hb = TM // HALO_ROWS
    nhb = n // HALO_ROWS
    kern = functools.partial(_cv_conv_kernel, nct=nct)
    row = lambda i: (0, 0)
    return pl.pallas_call(
        kern,
        grid=(nt,),
        in_specs=[
            pl.BlockSpec((TM, d), lambda i: (i, 0)),
            pl.BlockSpec((HALO_ROWS, d), lambda i: (jnp.maximum(i * hb - 1, 0), 0)),
            pl.BlockSpec((HALO_ROWS, d), lambda i: (jnp.minimum((i + 1) * hb, nhb - 1), 0)),
            pl.BlockSpec((TM, d), lambda i: (i, 0)),
            pl.BlockSpec((1, 6, NB, d), lambda i: (_is_lat(i, nct), 0, 0, 0)),
            pl.BlockSpec((CONV_WIDTH, NB, d), lambda i: (0, 0, 0)),
            pl.BlockSpec((1, d), row),
            pl.BlockSpec((1, d), row),
            pl.BlockSpec((1, d), row),
            pl.BlockSpec((d, d), row),
            pl.BlockSpec((1, d), row),
        ],
        out_specs=pl.BlockSpec((TM, d), lambda i: (i, 0)),
        out_shape=jax.ShapeDtypeStruct((n, d), F32),
        scratch_shapes=[
            pltpu.VMEM((TM + 2 * CONV_PAD * NB, d), F32),
            pltpu.VMEM((TM, d), F32),
        ],
        compiler_params=_cparams(("arbitrary",)),
    )(u, u, u, x, mod, jnp.broadcast_to(dw_w[:, None, :], (CONV_WIDTH, NB, d)), dw_b, ln_g, ln_b, w, b)


def _head_ones():
    w = 2 * LANES
    r = lax.broadcasted_iota(jnp.int32, (w, w), 0) // RWKV_HEAD
    c = lax.broadcasted_iota(jnp.int32, (w, w), 1) // RWKV_HEAD
    return jnp.where(r == c, 1.0, 0.0).astype(BF16)


def _head_sum_bcast(x, ones_bd):
    hi = x.astype(BF16)
    lo = (x - hi.astype(F32)).astype(BF16)
    w = ones_bd.shape[0]
    cols = [_dot(hi[:, c0:c0 + w], ones_bd) + _dot(lo[:, c0:c0 + w], ones_bd)
            for c0 in range(0, x.shape[1], w)]
    return jnp.concatenate(cols, axis=1)


def _rw_proj_kernel(x_ref, xp_ref, xn_ref, mod_ref, g_ref, mu_ref, wr_ref, wk_ref, wv_ref,
                    w1_ref, w2_ref, w0_ref, a1_ref, a2_ref, a0_ref, g1_ref, g2_ref, kkw_ref,
                    ka_ref, rk_ref,
                    r_ref, k_ref, v_ref, kk_ref, ew_ref, as_ref, gg_ref, bg_ref, hp_ref, *, nct, nt):
    i = pl.program_id(0)
    tm, d = x_ref.shape
    g = g_ref[...]
    sh = mod_ref[0, 0]
    sc = mod_ref[0, 1]
    h = _norm_mod(x_ref[...], g, sh, sc)
    hp = _norm_mod(xp_ref[...], g, sh, sc)
    hn = _norm_mod(xn_ref[...], g, sh, sc)
    prev_ok = (i != 0) & (i != nct)
    next_ok = (i != nct - 1) & (i != nt - 1)
    hp_ref[0:NB] = jnp.where(prev_ok, hp, 0.0)
    hp_ref[NB:NB + tm] = h
    hp_ref[NB + tm:NB + tm + NB] = jnp.where(next_ok, hn, 0.0)
    xx = 0.5 * (hp_ref[0:tm] + hp_ref[2 * NB:2 * NB + tm]) - h

    def mix(j):
        return (h + xx * mu_ref[j]).astype(BF16)

    r = _dot(mix(0), wr_ref[...])
    k = _dot(mix(2), wk_ref[...])
    v = _dot(mix(3), wv_ref[...])
    zw = w0_ref[...] + _dot(jnp.tanh(_dot(mix(1), w1_ref[...])).astype(BF16), w2_ref[...])
    ew = _sigmoid(zw) * math.exp(-0.5)
    asg = _sigmoid(a0_ref[...] + _dot(_dot(mix(4), a1_ref[...]).astype(BF16), a2_ref[...]))
    gg = _dot(_sigmoid(_dot(mix(5), g1_ref[...])).astype(BF16), g2_ref[...])
    gg_ref[...] = gg

    ones_bd = _head_ones()
    kk = k * kkw_ref[...]
    kk = kk * lax.rsqrt(_head_sum_bcast(kk * kk, ones_bd) + KK_EPS)
    k_mean = k * (1.0 + (0.5 * (asg[:, :d] + asg[:, d:]) - 1.0) * ka_ref[...])
    bg_ref[...] = _head_sum_bcast(r * k_mean * rk_ref[...], ones_bd) * v * gg

    for p in range(d // LANES):
        sl = slice(p * LANES, (p + 1) * LANES)
        r_ref[p] = r[:, sl]
        k_ref[p] = k[:, sl]
        v_ref[p] = v[:, sl]
        kk_ref[p] = kk[:, sl]
        for dr in range(2):
            sl2 = slice(dr * d + p * LANES, dr * d + (p + 1) * LANES)
            ew_ref[dr, p] = ew[:, sl2]
            as_ref[dr, p] = asg[:, sl2]


def _rw_proj_call(x, mod, g, mu, wr, wk, wv, w1, w2, w0, a1, a2, a0, g1, g2, kkw, ka, rk, *, nct):
    n, d = x.shape
    tm = RW_TM
    nt = n // tm
    nct = nct * (TM // tm)
    npair = d // LANES
    tb = tm // NB
    nb8 = n // NB
    kern = functools.partial(_rw_proj_kernel, nct=nct, nt=nt)
    full2 = lambda a: pl.BlockSpec(a.shape, lambda i: (0, 0))
    slab = pl.BlockSpec((npair, tm, LANES), lambda i: (0, i, 0))
    slab2 = pl.BlockSpec((2, npair, tm, LANES), lambda i: (0, 0, i, 0))
    slab_shape = jax.ShapeDtypeStruct((npair, n, LANES), F32)
    slab2_shape = jax.ShapeDtypeStruct((2, npair, n, LANES), F32)
    mu3 = mu.reshape(6, 1, d)
    return pl.pallas_call(
        kern,
        grid=(nt,),
        in_specs=[
            pl.BlockSpec((tm, d), lambda i: (i, 0)),
            pl.BlockSpec((NB, d), lambda i: (jnp.maximum(i * tb - 1, 0), 0)),
            pl.BlockSpec((NB, d), lambda i: (jnp.minimum((i + 1) * tb, nb8 - 1), 0)),
            pl.BlockSpec((1, 6, NB, d), lambda i: (_is_lat(i, nct), 0, 0, 0)),
            pl.BlockSpec((1, d), lambda i: (0, 0)),
            pl.BlockSpec((6, 1, d), lambda i: (0, 0, 0)),
            full2(wr), full2(wk), full2(wv),
            full2(w1), full2(w2), full2(w0), full2(a1), full2(a2), full2(a0),
            full2(g1), full2(g2), full2(kkw), full2(ka), full2(rk),
        ],
        out_specs=[slab, slab, slab, slab, slab2, slab2, pl.BlockSpec((tm, d), lambda i: (i, 0)),
                   pl.BlockSpec((tm, d), lambda i: (i, 0))],
        out_shape=[slab_shape, slab_shape, slab_shape, slab_shape, slab2_shape, slab2_shape,
                   jax.ShapeDtypeStruct((n, d), F32), jax.ShapeDtypeStruct((n, d), F32)],
        scratch_shapes=[pltpu.VMEM((tm + 2 * NB, d), F32)],
        compiler_params=_cparams(("arbitrary",)),
    )(x, x, x, mod, g, mu3, wr, wk, wv, w1, w2, w0, a1, a2, a0, g1, g2, kkw, ka, rk)


def _wkv_kernel(r_ref, k_ref, v_ref, kk_ref, ew_ref, as_ref, ka_ref, y_ref, st_ref, *, L, npair):
    d = pl.program_id(0)
    c = pl.program_id(1)

    @pl.when(c == 0)
    def _():
        st_ref[...] = jnp.zeros_like(st_ref)

    sgn = 1 - 2 * d
    l2 = 2 * L
    rr = lax.broadcasted_iota(jnp.int32, (l2, l2), 0)
    cc = lax.broadcasted_iota(jnp.int32, (l2, l2), 1)
    same = (rr >= L) == (cc >= L)
    diff = (rr - cc) * sgn
    strict_f = jnp.where(same & (diff > 0), 1.0, 0.0)
    incl_f = jnp.where(same & (diff >= 0), 1.0, 0.0)
    r1 = lax.broadcasted_iota(jnp.int32, (L, L), 0)
    c1 = lax.broadcasted_iota(jnp.int32, (L, L), 1)
    mcum = jnp.where((r1 - c1) * sgn >= 0, 1.0, 0.0).astype(BF16)
    head0 = lax.broadcasted_iota(jnp.int32, (L, LANES), 1) < RWKV_HEAD
    sr_i = lax.broadcasted_iota(jnp.int32, (LANES, LANES), 0)
    sc_i = lax.broadcasted_iota(jnp.int32, (LANES, LANES), 1)
    blockdiag = (sr_i < RWKV_HEAD) == (sc_i < RWKV_HEAD)
    n_dbl = int(round(math.log2(L)))

    def stack2(x):
        return jnp.concatenate([jnp.where(head0, x, 0.0), jnp.where(head0, 0.0, x)], axis=0)

    def unstack(xs):
        return xs[0:L] + xs[L:l2]

    def load(p, b):
        rows = pl.ds(b, L, stride=NB)
        return (r_ref[p, rows, :], k_ref[p, rows, :], v_ref[p, rows, :], kk_ref[p, rows, :],
                ew_ref[0, p, rows, :], as_ref[0, p, rows, :], st_ref[p * NB + b])

    def inst(ka, r, k, v, kk, ew, asg, s0):
        ldec = -ew
        hi = ldec.astype(BF16)
        lo = (ldec - hi.astype(F32)).astype(BF16)
        cl2 = _dot(mcum, jnp.concatenate([hi, lo], axis=1))
        yield
        cl = cl2[:, 0:LANES] + cl2[:, LANES:2 * LANES]
        ce = cl - ldec
        ecl = jnp.exp(cl)
        encl = jnp.exp(-cl)
        etot = jnp.exp(jnp.sum(ldec, axis=0, keepdims=True))
        at = -kk * jnp.exp(ce)
        rt = r * ecl
        bt = kk * asg * encl
        kt = k * (1.0 + (asg - 1.0) * ka) * encl

        ats = stack2(at)
        vsb = stack2(v).astype(BF16)
        lhs = jnp.concatenate([ats, stack2(rt)], axis=0).astype(BF16)
        rhs = jnp.concatenate([stack2(bt), stack2(kt)], axis=0).astype(BF16)
        pm = _dot_nt(lhs, rhs)
        yield
        nab = pm[0:l2, 0:l2] * strict_f
        nak = pm[0:l2, l2:2 * l2] * strict_f
        mrb = (pm[l2:2 * l2, 0:l2] * incl_f).astype(BF16)
        mrk = (pm[l2:2 * l2, l2:2 * l2] * incl_f).astype(BF16)

        xs = jnp.concatenate([ats, _dot(nak.astype(BF16), vsb)], axis=1)
        mkv = _dot(mrk, vsb)
        yield
        npow = nab
        for it in range(n_dbl):
            npb = npow.astype(BF16)
            xs = xs + _dot(npb, xs.astype(BF16))
            if it + 1 < n_dbl:
                npow = _dot(npb, npb)
            yield
        mx = _dot(mrb, xs.astype(BF16))
        at2 = unstack(xs[:, 0:LANES])
        ul = unstack(xs[:, LANES:2 * LANES])
        btb = bt.astype(BF16)
        emat = (jnp.where(blockdiag, _dot(at2.T.astype(BF16), btb), 0.0) * etot).astype(BF16)
        uv = jnp.concatenate([ul, v], axis=0)
        bk = jnp.concatenate([btb, kt.astype(BF16)], axis=0)
        dmat = jnp.where(blockdiag, _dot(uv.T.astype(BF16), bk), 0.0) * etot
        yield
        rt2 = rt + unstack(mx[:, 0:LANES])
        yl = unstack(mx[:, LANES:2 * LANES] + mkv)

        s0b = s0.astype(BF16)
        return _dot_nt(rt2.astype(BF16), s0b) + yl, s0 * etot + _dot(s0b, emat) + dmat

    def run_group(gens):
        outs = [None] * len(gens)
        live = list(enumerate(gens))
        while live:
            nxt = []
            for i, g in live:
                try:
                    next(g)
                    nxt.append((i, g))
                except StopIteration as stop:
                    outs[i] = stop.value
            live = nxt
        return outs

    def pairs(q, carry):
        work = [(q * WKV_PAIRS + j, b) for j in range(WKV_PAIRS) for b in range(NB)]
        outs = run_group([inst(ka_ref[p], *load(p, b)) for p, b in work])
        for (p, b), (y, s1) in zip(work, outs):
            y_ref[0, p, pl.ds(b, L, stride=NB), :] = y
            st_ref[p * NB + b] = s1
        return carry

    lax.fori_loop(0, npair // WKV_PAIRS, pairs, 0)


def _wkv_call(r, k, v, kk, ew, asg, ka, *, nct_rows):
    npair, n, _ = r.shape
    L = WKV_L
    rows = L * NB
    nc = n // rows
    ncc = nct_rows // rows
    kern = functools.partial(_wkv_kernel, L=L, npair=npair)

    def chunk(dd, c):
        bwd = jnp.where(c < ncc, ncc - 1 - c, nc - 1 + ncc - c)
        return jnp.where(dd == 0, c, bwd)

    slab = pl.BlockSpec((npair, rows, LANES), lambda dd, c: (0, chunk(dd, c), 0))
    slab2 = pl.BlockSpec((1, npair, rows, LANES), lambda dd, c: (dd, 0, chunk(dd, c), 0))
    return pl.pallas_call(
        kern,
        grid=(2, nc),
        in_specs=[slab, slab, slab, slab, slab2, slab2,
                  pl.BlockSpec((npair, 1, LANES), lambda dd, c: (0, 0, 0))],
        out_specs=slab2,
        out_shape=jax.ShapeDtypeStruct((2, npair, n, LANES), F32),
        scratch_shapes=[pltpu.VMEM((npair * NB, LANES, LANES), F32)],
        compiler_params=_cparams(("arbitrary", "arbitrary")),
    )(r, k, v, kk, ew, asg, ka)


def _rw_out_kernel(x_ref, mod_ref, y_ref, gg_ref, bg_ref, lng_ref, lnb_ref, wo_ref, o_ref):
    x = x_ref[...]
    d = x.shape[1]
    y = jnp.concatenate([y_ref[0, p] + y_ref[1, p] for p in range(d // LANES)], axis=1)
    ones_bd = _head_ones()
    inv = 1.0 / RWKV_HEAD
    mean = _head_sum_bcast(y, ones_bd) * inv
    yc = y - mean
    var = _head_sum_bcast(yc * yc, ones_bd) * inv
    yn = yc * lax.rsqrt(var + GN_EPS) * lng_ref[...] + lnb_ref[...]
    z = (yn * gg_ref[...] + bg_ref[...]).astype(BF16)
    o_ref[...] = _residual(x, mod_ref[0, 2], _dot(z, wo_ref[...]))


def _rw_out_call(x, mod, y, gg, bg, lng, lnb, wo, *, nct):
    n, d = x.shape
    nt = n // TM
    npair = d // LANES
    slab2 = pl.BlockSpec((2, npair, TM, LANES), lambda i: (0, 0, i, 0))
    row = pl.BlockSpec((1, d), lambda i: (0, 0))
    return pl.pallas_call(
        _rw_out_kernel,
        grid=(nt,),
        in_specs=[
            pl.BlockSpec((TM, d), lambda i: (i, 0)),
            pl.BlockSpec((1, 6, NB, d), lambda i: (_is_lat(i, nct), 0, 0, 0)),
            slab2,
            pl.BlockSpec((TM, d), lambda i: (i, 0)),
            pl.BlockSpec((TM, d), lambda i: (i, 0)),
            row, row,
            pl.BlockSpec((d, d), lambda i: (0, 0)),
        ],
        out_specs=pl.BlockSpec((TM, d), lambda i: (i, 0)),
        out_shape=jax.ShapeDtypeStruct((n, d), F32),
        compiler_params=_cparams(("arbitrary",)),
    )(x, mod, y, gg, bg, lng, lnb, wo)


def _final_kernel(x_ref, g_ref, o_ref, slab_ref):
    x = x_ref[...]
    d = x.shape[1]
    ms = jnp.mean(x * x, axis=-1, keepdims=True)
    y = x * lax.rsqrt(ms + RMS_EPS) * g_ref[...]
    for p in range(d // LANES):
        slab_ref[p] = y[:, p * LANES:(p + 1) * LANES]
    for b in range(NB):
        for p in range(d // LANES):
            o_ref[b, :, p * LANES:(p + 1) * LANES] = slab_ref[p, pl.ds(b, TT, stride=NB), :]


def _final_call(x, g, *, tile0):
    n, d = x.shape
    nt = n // TM
    return pl.pallas_call(
        _final_kernel,
        grid=(nt - tile0,),
        in_specs=[pl.BlockSpec((TM, d), lambda i: (i + tile0, 0)),
                  pl.BlockSpec((1, d), lambda i: (0, 0))],
        out_specs=pl.BlockSpec((NB, TT, d), lambda i: (0, i, 0)),
        out_shape=jax.ShapeDtypeStruct((NB, (nt - tile0) * TT, d), F32),
        scratch_shapes=[pltpu.VMEM((d // LANES, TM, LANES), F32)],
        compiler_params=_cparams(("arbitrary",)),
    )(x, g)


def _time_major_kernel(c_ref, x_ref, o_ref, slab_ref, *, nct):
    d = o_ref.shape[1]

    def emit(src_ref):
        for b in range(NB):
            for p in range(d // LANES):
                slab_ref[p, pl.ds(b, TT, stride=NB), :] = src_ref[b, :, p * LANES:(p + 1) * LANES]
        for p in range(d // LANES):
            o_ref[:, p * LANES:(p + 1) * LANES] = slab_ref[p]

    @pl.when(pl.program_id(0) < nct)
    def _():
        emit(c_ref)

    @pl.when(pl.program_id(0) >= nct)
    def _():
        emit(x_ref)


def _time_major_call(ctx, x):
    _, lc, d = ctx.shape
    seq = x.shape[1]
    nct = lc // TT
    nt = (lc + seq) // TT
    return pl.pallas_call(
        functools.partial(_time_major_kernel, nct=nct),
        grid=(nt,),
        in_specs=[pl.BlockSpec((NB, TT, d), lambda i: (0, jnp.minimum(i, nct - 1), 0)),
                  pl.BlockSpec((NB, TT, d), lambda i: (0, jnp.maximum(i - nct, 0), 0))],
        out_specs=pl.BlockSpec((TM, d), lambda i: (i, 0)),
        out_shape=jax.ShapeDtypeStruct((nt * TM, d), F32),
        scratch_shapes=[pltpu.VMEM((d // LANES, TM, LANES), F32)],
        compiler_params=_cparams(("arbitrary",)),
    )(ctx, x)


def _row(v):
    return v.reshape(1, -1).astype(F32)


def _s5_layer(x, mod, g, p, prep, *, nct, tile0):
    bb, cc, ab = prep
    y = _s5_scan_call(x, mod, g, bb, cc, ab, nct=nct)
    return _s5_glu_call(x, mod, g, y, _row(p['d']), p['glu_w'].astype(BF16), _row(p['glu_b']),
                        nct=nct, tile0=tile0)


def _conformer_layer(x, mod, g, p, *, nct):
    u = _cv_glu_call(x, mod, g, p['pw1_w'].astype(BF16), _row(p['pw1_b']), nct=nct)
    return _cv_conv_call(u, x, mod, p['dw_w'].astype(F32), _row(p['dw_b']), _row(p['ln_g']),
                         _row(p['ln_b']), p['pw2_w'].astype(BF16), _row(p['pw2_b']), nct=nct)


def _blockdiag2(m):
    z = jnp.zeros_like(m[0])
    return jnp.concatenate([jnp.concatenate([m[0], z], axis=1),
                            jnp.concatenate([z, m[1]], axis=1)], axis=0)


def _rwkv_layer(x, mod, g, p, *, nct):
    n, d = x.shape
    npair = d // LANES
    w1 = jnp.concatenate([p['w1'][0], p['w1'][1]], axis=1).astype(BF16)
    w2 = _blockdiag2(p['w2']).astype(BF16)
    a1 = jnp.concatenate([p['a1'][0], p['a1'][1]], axis=1).astype(BF16)
    a2 = _blockdiag2(p['a2']).astype(BF16)
    gl = p['g1'].shape[1]
    glp = -(-gl // LANES) * LANES
    g1 = jnp.pad(p['g1'], ((0, 0), (0, glp - gl))).astype(BF16)
    g2 = jnp.pad(p['g2'], ((0, glp - gl), (0, 0))).astype(BF16)
    r, k, v, kk, ew, asg, gg, bg = _rw_proj_call(
        x, mod, g, p['mu'].astype(F32), p['w_r'].astype(BF16), p['w_k'].astype(BF16),
        p['w_v'].astype(BF16), w1, w2, _row(p['w0']), a1, a2, _row(p['a0']), g1, g2,
        _row(p['k_k']), _row(p['k_a']), _row(p['r_k']), nct=nct)
    ka_slab = p['k_a'].astype(F32).reshape(npair, 1, LANES)
    y = _wkv_call(r, k, v, kk, ew, asg, ka_slab, nct_rows=nct * TM)
    return _rw_out_call(x, mod, y, gg, bg, _row(p['lnx_g']), _row(p['lnx_b']),
                        p['w_o'].astype(BF16), nct=nct)


def _forward(x, c, ctx, c_ctx, ada_w, ada_b, norm_mix_g, norm_ffn_g, final_g, s5, cv, rw,
             ff_w1, ff_w3, ff_w2, moe_router, moe_w1, moe_w3, moe_w2):
    bsz, seq, d = x.shape
    lc = ctx.shape[1]
    depth = ada_w.shape[0]
    assert bsz == NB and lc % TT == 0 and seq % TT == 0
    assert TT == GRID_W, "one token tile must hold exactly one latent raster row per batch entry"
    nct = lc // TT
    t_all = lc + seq
    n = t_all * NB

    xs = _time_major_call(ctx.astype(F32), x.astype(F32))

    cvec = jnp.zeros((2 * NB, d), F32).at[:NB].set(c).at[NB].set(c_ctx)
    m_all = _ada_call(cvec, ada_w.astype(F32), ada_b.astype(F32))
    m_all = m_all.reshape(depth, 2 * NB, 6, d)
    m_lat = jnp.swapaxes(m_all[:, :NB], 1, 2)
    m_ctx = jnp.broadcast_to(m_all[:, NB][:, :, None, :], (depth, 6, NB, d))
    mods = jnp.stack([m_ctx, m_lat], axis=1)

    s5_prep = jax.vmap(_s5_params)(s5['lam_re'], s5['lam_im'], s5['log_dt'], s5['b_re'], s5['b_im'],
                                   s5['c_re'], s5['c_im'])
    ctx_rows = nct * TM
    for i in range(depth):
        last = i == depth - 1
        mod = mods[i]
        g_mix = _row(norm_mix_g[i])
        j = i // 3
        if i % 3 == 0:
            xs = _s5_layer(xs, mod, g_mix, {k_: v_[j] for k_, v_ in s5.items()},
                           tuple(a[j] for a in s5_prep), nct=nct, tile0=nct if last else 0)
            ctx_rows = 0 if last else ctx_rows
        elif i % 3 == 1:
            xs = _conformer_layer(xs, mod, g_mix, {k_: v_[j] for k_, v_ in cv.items()}, nct=nct)
        else:
            xs = _rwkv_layer(xs, mod, g_mix, {k_: v_[j] for k_, v_ in rw.items()}, nct=nct)
        fi = i // 2
        g_ffn = _row(norm_ffn_g[i])
        if i % 2 == 0:
            xs = _ffn_call(xs, mod, g_ffn, ff_w1, ff_w3, ff_w2, layer=fi, nct_rows=ctx_rows)
        else:
            xs = _moe_call(xs, mod, g_ffn, moe_router[fi], moe_w1, moe_w3, moe_w2, layer=fi,
                           nct_rows=ctx_rows)

    return _final_call(xs, _row(final_g), tile0=ctx_rows // TM)


def kernel(x, c, ctx, c_ctx, ada_w, ada_b, norm_mix_g, norm_ffn_g, final_g, s5_lambda_re, s5_lambda_im, s5_log_dt, s5_b_re, s5_b_im, s5_c_re, s5_c_im, s5_d, s5_glu_w, s5_glu_b, cv_pw1_w, cv_pw1_b, cv_dw_w, cv_dw_b, cv_ln_g, cv_ln_b, cv_pw2_w, cv_pw2_b, rw_mu, rw_w_r, rw_w_k, rw_w_v, rw_w_o, rw_w0, rw_w1, rw_w2, rw_a0, rw_a1, rw_a2, rw_g1, rw_g2, rw_k_k, rw_k_a, rw_r_k, rw_lnx_g, rw_lnx_b, ff_w1, ff_w3, ff_w2, moe_router, moe_w1, moe_w3, moe_w2):
    s5 = dict(lam_re=s5_lambda_re, lam_im=s5_lambda_im, log_dt=s5_log_dt, b_re=s5_b_re, b_im=s5_b_im,
              c_re=s5_c_re, c_im=s5_c_im, d=s5_d, glu_w=s5_glu_w, glu_b=s5_glu_b)
    cv = dict(pw1_w=cv_pw1_w, pw1_b=cv_pw1_b, dw_w=cv_dw_w, dw_b=cv_dw_b, ln_g=cv_ln_g, ln_b=cv_ln_b,
              pw2_w=cv_pw2_w, pw2_b=cv_pw2_b)
    rw = dict(mu=rw_mu, w_r=rw_w_r, w_k=rw_w_k, w_v=rw_w_v, w_o=rw_w_o, w0=rw_w0, w1=rw_w1, w2=rw_w2,
              a0=rw_a0, a1=rw_a1, a2=rw_a2, g1=rw_g1, g2=rw_g2, k_k=rw_k_k, k_a=rw_k_a, r_k=rw_r_k,
              lnx_g=rw_lnx_g, lnx_b=rw_lnx_b)
    return _forward(x, c, ctx, c_ctx, ada_w, ada_b, norm_mix_g, norm_ffn_g, final_g, s5, cv, rw,
                    ff_w1, ff_w3, ff_w2, moe_router, moe_w1, moe_w3, moe_w2)
```

```python
import functools
import math

import jax
import jax.numpy as jnp
from jax import lax
from jax.experimental import pallas as pl
from jax.experimental.pallas import tpu as pltpu

F32 = jnp.float32
BF16 = jnp.bfloat16

NB = 8
LANES = 128
TM = 512
TT = TM // NB
GRID_W = 64
CONV_WIDTH = 31
CONV_PAD = CONV_WIDTH // 2
HALO_ROWS = 128
S5_GROUP = 16
S5_STATE = 64
S5_JBLK = 8
RWKV_HEAD = 64
WKV_L = 64
WKV_PAIRS = 2
RW_TM = 256
FFN_TM = 1024
MOE_T = 1024
MOE_BLK = 32
MOE_SLOTS = 32
RMS_EPS = 1e-6
LN_EPS = 1e-5
GN_EPS = 64e-5
KK_EPS = 1e-12
LAMBDA_RE_MAX = -1e-4
VMEM_LIMIT = 56 * 1024 * 1024


def _cparams(sem):
    return pltpu.CompilerParams(dimension_semantics=sem, vmem_limit_bytes=VMEM_LIMIT)


def _dot(a, b):
    return jnp.dot(a, b, preferred_element_type=F32)


def _dot_nt(a, b):
    return lax.dot_general(a, b, (((1,), (1,)), ((), ())), preferred_element_type=F32)


def _dot_x3(a, b):
    a1 = a.astype(BF16)
    a2 = (a - a1.astype(F32)).astype(BF16)
    b1 = b.astype(BF16)
    b2 = (b - b1.astype(F32)).astype(BF16)
    return _dot(a1, b1) + _dot(a1, b2) + _dot(a2, b1)


def _sigmoid(x):
    return jax.nn.sigmoid(x)


def _norm_mod(x, g, shift, scale):
    rows, d = x.shape
    ms = jnp.mean(x * x, axis=-1, keepdims=True)
    y = x * lax.rsqrt(ms + RMS_EPS) * g
    y3 = y.reshape(rows // NB, NB, d)
    return (y3 * (1.0 + scale)[None] + shift[None]).reshape(rows, d)


def _residual(x, gate, y):
    rows, d = x.shape
    y3 = y.reshape(rows // NB, NB, d) * gate[None]
    return x + y3.reshape(rows, d)


def _is_lat(i, nct):
    return jnp.where(i >= nct, 1, 0).astype(jnp.int32)


def _ada_kernel(s_ref, w_ref, b_ref, o_ref):
    s = s_ref[...]
    s = s * _sigmoid(s)
    o_ref[0] = _dot(s.astype(BF16), w_ref[0].astype(BF16)) + b_ref[0]


def _ada_call(cvec, ada_w, ada_b):
    depth, d, d6 = ada_w.shape
    tn = 1536
    return pl.pallas_call(
        _ada_kernel,
        grid=(depth, d6 // tn),
        in_specs=[
            pl.BlockSpec((2 * NB, d), lambda l, j: (0, 0)),
            pl.BlockSpec((1, d, tn), lambda l, j: (l, 0, j)),
            pl.BlockSpec((1, 1, tn), lambda l, j: (l, 0, j)),
        ],
        out_specs=pl.BlockSpec((1, 2 * NB, tn), lambda l, j: (l, 0, j)),
        out_shape=jax.ShapeDtypeStruct((depth, 2 * NB, d6), F32),
        compiler_params=_cparams(("arbitrary", "arbitrary")),
    )(cvec, ada_w, ada_b.reshape(depth, 1, d6))


def _ffn_kernel(x_ref, mod_ref, g_ref, w1_ref, w3_ref, w2_ref, o_ref, hb_ref, acc_ref, *, n_k):
    k = pl.program_id(1)

    @pl.when(k == 0)
    def _():
        h = _norm_mod(x_ref[...], g_ref[...], mod_ref[0, 3], mod_ref[0, 4])
        hb_ref[...] = h.astype(BF16)
        acc_ref[...] = jnp.zeros_like(acc_ref)

    hb = hb_ref[...]
    a = _dot(hb, w1_ref[0].astype(BF16))
    b = _dot(hb, w3_ref[0].astype(BF16))
    acc_ref[...] += _dot((a * _sigmoid(a) * b).astype(BF16), w2_ref[0].astype(BF16))

    @pl.when(k == n_k - 1)
    def _():
        o_ref[...] = _residual(x_ref[...], mod_ref[0, 5], acc_ref[...])


def _ffn_call(x, mod, g, w1, w3, w2, *, layer, nct_rows):
    n, d = x.shape
    f = w1.shape[2]
    tm = FFN_TM
    tf = 512
    n_k = f // tf
    nct = nct_rows // tm
    return pl.pallas_call(
        functools.partial(_ffn_kernel, n_k=n_k),
        grid=(n // tm, n_k),
        in_specs=[
            pl.BlockSpec((tm, d), lambda i, k: (i, 0)),
            pl.BlockSpec((1, 6, NB, d), lambda i, k: (_is_lat(i, nct), 0, 0, 0)),
            pl.BlockSpec((1, d), lambda i, k: (0, 0)),
            pl.BlockSpec((1, d, tf), lambda i, k: (layer, 0, k)),
            pl.BlockSpec((1, d, tf), lambda i, k: (layer, 0, k)),
            pl.BlockSpec((1, tf, d), lambda i, k: (layer, k, 0)),
        ],
        out_specs=pl.BlockSpec((tm, d), lambda i, k: (i, 0)),
        out_shape=jax.ShapeDtypeStruct((n, d), F32),
        scratch_shapes=[pltpu.VMEM((tm, d), BF16), pltpu.VMEM((tm, d), F32)],
        compiler_params=_cparams(("arbitrary", "arbitrary")),
    )(x, mod, g, w1, w3, w2)


def _top2(logits, n_e):
    lane = lax.broadcasted_iota(jnp.int32, logits.shape, 1).astype(F32)
    neg = jnp.float32(-3.0e38)
    lg = jnp.where(lane < n_e, logits, neg)
    m1 = jnp.max(lg, axis=-1, keepdims=True)
    i1 = jnp.min(jnp.where(lg == m1, lane, 1.0e9), axis=-1, keepdims=True)
    lg2 = jnp.where(lane == i1, neg, lg)
    m2 = jnp.max(lg2, axis=-1, keepdims=True)
    i2 = jnp.min(jnp.where(lg2 == m2, lane, 1.0e9), axis=-1, keepdims=True)
    e2 = jnp.exp(m2 - m1)
    den = 1.0 + e2
    return lane, i1, i2, 1.0 / den, e2 / den


def _moe_route_kernel(x_ref, mod_ref, g_ref, router_ref, ltri_ref, z_ref, dest_ref, cnt_ref,
                      hb_ref, *, n_e, cap):
    t, d = x_ref.shape
    h = _norm_mod(x_ref[...], g_ref[...], mod_ref[0, 3], mod_ref[0, 4])
    hb_ref[...] = h.astype(BF16)
    lane, i1, i2, p1, p2 = _top2(_dot_x3(h, router_ref[...]), n_e)
    sel1 = lane == i1
    sel2 = lane == i2
    c = jnp.where(sel1, 1.0, 0.0) + jnp.where(sel2, 1.0, 0.0)
    rank = _dot(ltri_ref[...], c.astype(BF16))
    cnt = jnp.sum(c, axis=0, keepdims=True)
    seg = jnp.floor((cnt + (MOE_BLK - 1)) * (1.0 / MOE_BLK)) * MOE_BLK
    ur = lax.broadcasted_iota(jnp.int32, (LANES, LANES), 0)
    uc = lax.broadcasted_iota(jnp.int32, (LANES, LANES), 1)
    before = jnp.where(ur < uc, 1.0, 0.0).astype(BF16)
    start = _dot(jnp.broadcast_to(seg, (NB, LANES)).astype(BF16), before)[0:1]
    pos = start + rank
    d1 = jnp.sum(jnp.where(sel1, pos, 0.0), axis=-1, keepdims=True)
    d2 = jnp.sum(jnp.where(sel2, pos, 0.0), axis=-1, keepdims=True)
    packed = (jnp.where(lane == 0.0, d1, 0.0) + jnp.where(lane == 1.0, d2, 0.0)
              + jnp.where(lane == 2.0, p1, 0.0) + jnp.where(lane == 3.0, p2, 0.0))
    dest_ref[...] = packed
    cnt_ref[0] = jnp.broadcast_to(cnt, (NB, LANES))
    packed_t = packed.T
    d1r = packed_t[0:1]
    d2r = packed_t[1:2]
    p1r = packed_t[2:3]
    p2r = packed_t[3:4]
    rc = 256

    def chunk(ci, carry):
        r0 = pl.multiple_of(ci * rc, rc)
        row = (lax.broadcasted_iota(jnp.int32, (rc, t), 0) + r0).astype(F32)
        m1 = row == d1r
        m2 = row == d2r
        perm = (jnp.where(m1, 1.0, 0.0) + jnp.where(m2, 1.0, 0.0)).astype(BF16)
        zc = _dot(perm, hb_ref[...])
        gate = jnp.sum(jnp.where(m1, p1r, 0.0) + jnp.where(m2, p2r, 0.0), axis=-1, keepdims=True)
        ghi = gate.astype(BF16).astype(F32)
        glo = gate - ghi
        le = lax.broadcasted_iota(jnp.int32, (rc, LANES), 1)
        ext = jnp.where(le == 0, ghi, jnp.where(le == 1, glo, 0.0))
        z_ref[pl.ds(r0, rc), 0:d] = zc.astype(BF16)
        z_ref[pl.ds(r0, rc), d:d + LANES] = ext.astype(BF16)
        return carry

    lax.fori_loop(0, cap // rc, chunk, 0)


def _moe_experts_kernel(tin_ref, se_ref, nv_ref, *refs, n_k, d):
    z_refs = refs[:MOE_SLOTS]
    w1_ref, w3_ref, w2_ref, o_ref, xs_ref, acc_ref = refs[MOE_SLOTS:]
    s = pl.program_id(0)
    k = pl.program_id(1)
    nv = nv_ref[s]

    @pl.when((nv > 0) & (k == 0))
    def _():
        for j in range(MOE_SLOTS):
            xs_ref[j * MOE_BLK:(j + 1) * MOE_BLK] = z_refs[j][...]
        acc_ref[...] = jnp.zeros_like(acc_ref)

    def compute(nrows):
        hb = xs_ref[0:nrows, 0:d]
        ext = xs_ref[0:nrows, d:d + LANES].astype(F32)
        le = lax.broadcasted_iota(jnp.int32, ext.shape, 1)
        gate = jnp.sum(jnp.where(le < 2, ext, 0.0), axis=-1, keepdims=True)
        a = _dot(hb, w1_ref[0, 0].astype(BF16))
        b = _dot(hb, w3_ref[0, 0].astype(BF16))
        act = a * _sigmoid(a) * b * gate
        acc_ref[0:nrows] += _dot(act.astype(BF16), w2_ref[0, 0].astype(BF16))

    quarter = MOE_SLOTS // 4
    for q in range(1, 5):
        @pl.when((nv > (q - 1) * quarter) & (nv <= q * quarter))
        def _():
            compute(q * quarter * MOE_BLK)

    @pl.when(k == n_k - 1)
    def _():
        o_ref[...] = jnp.where(nv > 0, acc_ref[...], 0.0).astype(BF16)


def _moe_combine_kernel(tc_ref, *refs, nblk, cap, final):
    zb_refs = refs[:nblk]
    if final:
        dest_ref, x_ref, mod_ref, fg_ref, o_ref, zbuf_ref, slab_ref = refs[nblk:]
    else:
        dest_ref, x_ref, mod_ref, o_ref, zbuf_ref = refs[nblk:]
    t, d = x_ref.shape
    for j in range(nblk):
        zbuf_ref[j * MOE_BLK:(j + 1) * MOE_BLK] = zb_refs[j][...]
    packed = dest_ref[...]
    lane = lax.broadcasted_iota(jnp.int32, packed.shape, 1)
    d1 = jnp.sum(jnp.where(lane == 0, packed, 0.0), axis=-1, keepdims=True)
    d2 = jnp.sum(jnp.where(lane == 1, packed, 0.0), axis=-1, keepdims=True)
    kc = 256
    acc = jnp.zeros((t, d), F32)
    for c0 in range(0, cap, kc):
        col = (lax.broadcasted_iota(jnp.int32, (t, kc), 1) + c0).astype(F32)
        sel = (jnp.where(col == d1, 1.0, 0.0) + jnp.where(col == d2, 1.0, 0.0)).astype(BF16)
        acc = acc + _dot(sel, zbuf_ref[c0:c0 + kc])
    y = _residual(x_ref[...], mod_ref[0, 5], acc)
    if not final:
        o_ref[...] = y
        return
    y = y * lax.rsqrt(jnp.mean(y * y, axis=-1, keepdims=True) + RMS_EPS) * fg_ref[...]
    for p in range(d // LANES):
        slab_ref[p] = y[:, p * LANES:(p + 1) * LANES]
    for b in range(NB):
        for p in range(d // LANES):
            o_ref[b, :, p * LANES:(p + 1) * LANES] = slab_ref[p, pl.ds(b, t // NB, stride=NB), :]


def _moe_tables(cnt, *, n_tiles, n_e, nblk, n_steps):
    i32 = jnp.int32
    e_ids = jnp.arange(n_e, dtype=i32)
    t_ids = jnp.arange(n_tiles, dtype=i32)
    pc = (cnt + (MOE_BLK - 1)) // MOE_BLK
    lb = jnp.cumsum(pc, axis=1) - pc
    cum = jnp.cumsum(pc, axis=0)
    cum_ex = cum - pc
    nb_e = cum[-1]
    steps_e = (nb_e + (MOE_SLOTS - 1)) // MOE_SLOTS
    ends = jnp.cumsum(steps_e)
    step0 = ends - steps_e
    sidx = jnp.arange(n_steps, dtype=i32)
    step_e = jnp.minimum(jnp.sum((sidx[:, None] >= ends[None, :]).astype(i32), axis=1), n_e - 1)
    oh_e = (step_e[:, None] == e_ids[None, :]).astype(i32)
    by_step = lambda v: jnp.sum(oh_e * v[None, :], axis=1)
    by_step_tile = lambda m: jnp.sum(oh_e[:, None, :] * m[None, :, :], axis=2)
    rank = ((sidx - by_step(step0))[:, None] * MOE_SLOTS
            + jnp.arange(MOE_SLOTS, dtype=i32)[None, :])
    cum_s = by_step_tile(cum)
    tile = jnp.sum((cum_s[:, None, :] <= rank[:, :, None]).astype(i32), axis=2)
    nvalid = jnp.where(sidx < ends[-1], jnp.clip(by_step(nb_e) - rank[:, 0], 0, MOE_SLOTS), 0)
    ok = rank < (rank[:, :1] + nvalid[:, None])
    oh_t = (jnp.minimum(tile, n_tiles - 1)[:, :, None] == t_ids[None, None, :]).astype(i32)
    at_tile = lambda m: jnp.sum(oh_t * by_step_tile(m)[:, None, :], axis=2)
    src = jnp.sum(oh_t * t_ids, axis=2) * nblk + at_tile(lb) + rank - at_tile(cum_ex)
    tin = jnp.where(ok, src, 0).reshape(-1).astype(i32)
    lblk = jnp.arange(nblk, dtype=i32)[None, :]
    seg_end = lb + pc
    e_l = jnp.minimum(jnp.sum((seg_end[:, None, :] <= lblk[:, :, None]).astype(i32), axis=2), n_e - 1)
    oh_l = (e_l[:, :, None] == e_ids[None, None, :]).astype(i32)
    r_l = jnp.sum(oh_l * (cum_ex - lb)[:, None, :], axis=2) + lblk
    flat = (jnp.sum(oh_l * step0, axis=2) + r_l // MOE_SLOTS) * MOE_SLOTS + r_l % MOE_SLOTS
    tcomb = jnp.where(lblk < seg_end[:, -1:], flat, 0).reshape(-1).astype(i32)
    return tin, tcomb, step_e.astype(i32), nvalid.astype(i32)


def _moe_call(x, mod, g, router, w1, w3, w2, *, layer, nct_rows, final_g=None):
    n, d = x.shape
    _, n_e, _, f = w1.shape
    t = MOE_T
    assert n % t == 0 and nct_rows % t == 0
    n_tiles = n // t
    nct = nct_rows // t
    cap = 2 * t + n_e * MOE_BLK
    nblk = cap // MOE_BLK
    rows = MOE_SLOTS * MOE_BLK
    n_steps = -(-(n_tiles * nblk) // MOE_SLOTS) + n_e
    tf = 512
    n_k = f // tf
    router_p = jnp.pad(router.astype(F32), ((0, 0), (0, LANES - n_e)))
    ltri = jnp.tril(jnp.ones((t, t), F32), k=-1).astype(BF16)
    mod_spec = lambda off: pl.BlockSpec((1, 6, NB, d), lambda i, *_: (_is_lat(i, nct), 0, 0, 0))

    z, dest, cnt = pl.pallas_call(
        functools.partial(_moe_route_kernel, n_e=n_e, cap=cap),
        grid=(n_tiles,),
        in_specs=[
            pl.BlockSpec((t, d), lambda i: (i, 0)),
            mod_spec(0),
            pl.BlockSpec((1, d), lambda i: (0, 0)),
            pl.BlockSpec((d, LANES), lambda i: (0, 0)),
            pl.BlockSpec((t, t), lambda i: (0, 0)),
        ],
        out_specs=[
            pl.BlockSpec((cap, d + LANES), lambda i: (i, 0)),
            pl.BlockSpec((t, LANES), lambda i: (i, 0)),
            pl.BlockSpec((1, NB, LANES), lambda i: (i, 0, 0)),
        ],
        out_shape=[
            jax.ShapeDtypeStruct((n_tiles * cap, d + LANES), BF16),
            jax.ShapeDtypeStruct((n, LANES), F32),
            jax.ShapeDtypeStruct((n_tiles, NB, LANES), F32),
        ],
        scratch_shapes=[pltpu.VMEM((t, d), BF16)],
        compiler_params=_cparams(("arbitrary",)),
    )(x, mod, g, router_p, ltri)

    tin, tcomb, step_e, nvalid = _moe_tables(cnt[:, 0, :n_e].astype(jnp.int32), n_tiles=n_tiles,
                                            n_e=n_e, nblk=nblk, n_steps=n_steps)

    def kk(s, k, nv_r):
        return jnp.where(nv_r[s] > 0, k, n_k - 1)

    def z_spec(j):
        return pl.BlockSpec((MOE_BLK, d + LANES), lambda s, k, tin_r, se_r, nu_r: (tin_r[s * MOE_SLOTS + j], 0))

    zo = pl.pallas_call(
        functools.partial(_moe_experts_kernel, n_k=n_k, d=d),
        grid_spec=pltpu.PrefetchScalarGridSpec(
            num_scalar_prefetch=3,
            grid=(n_steps, n_k),
            in_specs=[z_spec(j) for j in range(MOE_SLOTS)] + [
                pl.BlockSpec((1, 1, d, tf), lambda s, k, tin_r, se_r, nu_r: (layer, se_r[s], 0, kk(s, k, nu_r))),
                pl.BlockSpec((1, 1, d, tf), lambda s, k, tin_r, se_r, nu_r: (layer, se_r[s], 0, kk(s, k, nu_r))),
                pl.BlockSpec((1, 1, tf, d), lambda s, k, tin_r, se_r, nu_r: (layer, se_r[s], kk(s, k, nu_r), 0)),
            ],
            out_specs=pl.BlockSpec((rows, d), lambda s, k, tin_r, se_r, nu_r: (s, 0)),
            scratch_shapes=[pltpu.VMEM((rows, d + LANES), BF16), pltpu.VMEM((rows, d), F32)],
        ),
        out_shape=jax.ShapeDtypeStruct((n_steps * rows, d), BF16),
        compiler_params=_cparams(("arbitrary", "arbitrary")),
    )(tin, step_e, nvalid, *([z] * MOE_SLOTS), w1, w3, w2)

    def zo_spec(j):
        return pl.BlockSpec((MOE_BLK, d), lambda i, tc_r: (tc_r[i * nblk + j], 0))

    final = final_g is not None
    extra_in = [pl.BlockSpec((1, d), lambda i, tc_r: (0, 0))] if final else []
    extra_args = (final_g,) if final else ()
    if final:
        out_spec = pl.BlockSpec((NB, t // NB, d), lambda i, tc_r: (0, i, 0))
        out_shape = jax.ShapeDtypeStruct((NB, n // NB, d), F32)
        scratch = [pltpu.VMEM((cap, d), BF16), pltpu.VMEM((d // LANES, t, LANES), F32)]
    else:
        out_spec = pl.BlockSpec((t, d), lambda i, tc_r: (i, 0))
        out_shape = jax.ShapeDtypeStruct((n, d), F32)
        scratch = [pltpu.VMEM((cap, d), BF16)]
    return pl.pallas_call(
        functools.partial(_moe_combine_kernel, nblk=nblk, cap=cap, final=final),
        grid_spec=pltpu.PrefetchScalarGridSpec(
            num_scalar_prefetch=1,
            grid=(n_tiles,),
            in_specs=[zo_spec(j) for j in range(nblk)] + [
                pl.BlockSpec((t, LANES), lambda i, tc_r: (i, 0)),
                pl.BlockSpec((t, d), lambda i, tc_r: (i, 0)),
                mod_spec(0),
            ] + extra_in,
            out_specs=out_spec,
            scratch_shapes=scratch,
        ),
        out_shape=out_shape,
        compiler_params=_cparams(("arbitrary",)),
    )(tcomb, *([zo] * nblk), dest, x, mod, *extra_args)


def _dir_tile(d, s, nct, nt):
    bwd = jnp.where(s < nct, nct - 1 - s, nt - 1 + nct - s)
    return jnp.where(d == 0, s, bwd)


def _s5_scan_kernel(x_ref, mod_ref, g_ref, bb_ref, cc_ref, ab_ref, y_ref,
                    hb_ref, bu0_ref, bu1_ref, bu2_ref, st_ref):
    d = pl.program_id(0)
    s = pl.program_id(1)
    half = S5_JBLK * S5_STATE
    bufs = (bu0_ref, bu1_ref, bu2_ref)

    @pl.when(s == 0)
    def _():
        st_ref[...] = jnp.zeros_like(st_ref)

    h = _norm_mod(x_ref[...], g_ref[...], mod_ref[0, 0], mod_ref[0, 1])
    hb_ref[...] = h.astype(BF16)

    def project(j):
        bufs[j % 3][...] = _dot(hb_ref[:, j * LANES:(j + 1) * LANES], bb_ref[0, j])

    def scan(j):
        buf = bufs[j % 3]
        ar = ab_ref[0, j, 0]
        ai = ab_ref[0, j, 1]
        re = st_ref[j, 0]
        im = st_ref[j, 1]
        for t in range(TT):
            tt = t + d * (TT - 1 - 2 * t)
            r0 = pl.multiple_of(tt * NB, NB)
            bre = buf[pl.ds(r0, NB), 0:half]
            bim = buf[pl.ds(r0, NB), half:2 * half]
            re, im = ar * re - ai * im + bre, ar * im + ai * re + bim
            buf[pl.ds(r0, NB), 0:half] = re
            buf[pl.ds(r0, NB), half:2 * half] = im
        st_ref[j, 0] = re
        st_ref[j, 1] = im

    def readout(j):
        y_ref[0, j] = _dot(bufs[j % 3][...].astype(BF16), cc_ref[0, j])

    for i in range(S5_JBLK + 2):
        if i < S5_JBLK:
            project(i)
        if 0 <= i - 1 < S5_JBLK:
            scan(i - 1)
        if 0 <= i - 2 < S5_JBLK:
            readout(i - 2)


def _s5_scan_call(x, mod, g, bb, cc, ab, *, nct):
    n, d = x.shape
    nt = n // TM
    half = S5_JBLK * S5_STATE
    tile = lambda dd, s: _dir_tile(dd, s, nct, nt)
    return pl.pallas_call(
        _s5_scan_kernel,
        grid=(2, nt),
        in_specs=[
            pl.BlockSpec((TM, d), lambda dd, s: (tile(dd, s), 0)),
            pl.BlockSpec((1, 6, NB, d), lambda dd, s: (_is_lat(tile(dd, s), nct), 0, 0, 0)),
            pl.BlockSpec((1, d), lambda dd, s: (0, 0)),
            pl.BlockSpec((1, S5_JBLK, LANES, 2 * half), lambda dd, s: (dd, 0, 0, 0)),
            pl.BlockSpec((1, S5_JBLK, 2 * half, LANES), lambda dd, s: (dd, 0, 0, 0)),
            pl.BlockSpec((1, S5_JBLK, 2, NB, half), lambda dd, s: (dd, 0, 0, 0, 0)),
        ],
        out_specs=pl.BlockSpec((1, S5_JBLK, TM, LANES), lambda dd, s: (dd, 0, tile(dd, s), 0)),
        out_shape=jax.ShapeDtypeStruct((2, S5_JBLK, n, LANES), F32),
        scratch_shapes=[
            pltpu.VMEM((TM, d), BF16),
            pltpu.VMEM((TM, 2 * half), F32),
            pltpu.VMEM((TM, 2 * half), F32),
            pltpu.VMEM((TM, 2 * half), F32),
            pltpu.VMEM((S5_JBLK, 2, NB, half), F32),
        ],
        compiler_params=_cparams(("arbitrary", "arbitrary")),
    )(x, mod, g, bb, cc, ab)


def _gelu_tanh(x):
    c = math.sqrt(2.0 / math.pi)
    return 0.5 * x * (1.0 + jnp.tanh(c * (x + 0.044715 * (x * x * x))))


def _s5_glu_kernel(x_ref, mod_ref, g_ref, y_ref, dsk_ref, w_ref, b_ref, o_ref):
    x = x_ref[...]
    d = x.shape[1]
    h = _norm_mod(x, g_ref[...], mod_ref[0, 0], mod_ref[0, 1])
    y = jnp.concatenate([y_ref[0, j] + y_ref[1, j] for j in range(S5_JBLK)], axis=1)
    z = _gelu_tanh(y + dsk_ref[...] * h).astype(BF16)
    zz = _dot(z, w_ref[...]) + b_ref[...]
    out = zz[:, :d] * _sigmoid(zz[:, d:])
    o_ref[...] = _residual(x, mod_ref[0, 2], out)


def _s5_glu_call(x, mod, g, y, dsk, w, b, *, nct, tile0):
    n, d = x.shape
    nt = n // TM
    return pl.pallas_call(
        _s5_glu_kernel,
        grid=(nt - tile0,),
        in_specs=[
            pl.BlockSpec((TM, d), lambda i: (i + tile0, 0)),
            pl.BlockSpec((1, 6, NB, d), lambda i: (_is_lat(i + tile0, nct), 0, 0, 0)),
            pl.BlockSpec((1, d), lambda i: (0, 0)),
            pl.BlockSpec((2, S5_JBLK, TM, LANES), lambda i: (0, 0, i + tile0, 0)),
            pl.BlockSpec((1, d), lambda i: (0, 0)),
            pl.BlockSpec((d, 2 * d), lambda i: (0, 0)),
            pl.BlockSpec((1, 2 * d), lambda i: (0, 0)),
        ],
        out_specs=pl.BlockSpec((TM, d), lambda i: (i, 0)),
        out_shape=jax.ShapeDtypeStruct((n - tile0 * TM, d), F32),
        compiler_params=_cparams(("arbitrary",)),
    )(x, mod, g, y, dsk, w, b)


def _s5_params(lam_re, lam_im, log_dt, b_re, b_im, c_re, c_im):
    lam_re = jnp.minimum(lam_re.astype(F32), LAMBDA_RE_MAX)
    lam_im = lam_im.astype(F32)
    dt = jnp.exp(log_dt.astype(F32))[..., None]
    mag = jnp.exp(lam_re * dt)
    abar_re = mag * jnp.cos(lam_im * dt)
    abar_im = mag * jnp.sin(lam_im * dt)
    inv_den = 1.0 / (lam_re * lam_re + lam_im * lam_im)
    gam_re = ((abar_re - 1.0) * lam_re + abar_im * lam_im) * inv_den
    gam_im = (abar_im * lam_re - (abar_re - 1.0) * lam_im) * inv_den
    b_re = b_re.astype(F32)
    b_im = b_im.astype(F32)
    bb_re = gam_re[..., None] * b_re - gam_im[..., None] * b_im
    bb_im = gam_re[..., None] * b_im + gam_im[..., None] * b_re
    g_all = lam_re.shape[1]
    nj = g_all // S5_JBLK
    eye = jnp.eye(S5_JBLK, dtype=F32)

    def in_mat(m):
        m = m.reshape(2, nj, S5_JBLK, S5_STATE, S5_GROUP)
        m = jnp.einsum('djgnp,gh->djgphn', m, eye)
        return m.reshape(2, nj, S5_JBLK * S5_GROUP, S5_JBLK * S5_STATE)

    def out_mat(m):
        m = m.reshape(2, nj, S5_JBLK, S5_GROUP, S5_STATE)
        m = jnp.einsum('djgpn,gh->djgnhp', m, eye)
        return m.reshape(2, nj, S5_JBLK * S5_STATE, S5_JBLK * S5_GROUP)

    bb = jnp.concatenate([in_mat(bb_re), in_mat(bb_im)], axis=-1).astype(BF16)
    cc = jnp.concatenate([out_mat(c_re.astype(F32)), -out_mat(c_im.astype(F32))], axis=-2).astype(BF16)
    half = S5_JBLK * S5_STATE
    ab = jnp.stack([abar_re.reshape(2, nj, half), abar_im.reshape(2, nj, half)], axis=2)
    ab = jnp.broadcast_to(ab[:, :, :, None, :], (2, nj, 2, NB, half))
    return bb, cc, ab


def _cv_glu_kernel(x_ref, mod_ref, g_ref, w_ref, b_ref, o_ref):
    x = x_ref[...]
    d = x.shape[1]
    h = _norm_mod(x, g_ref[...], mod_ref[0, 0], mod_ref[0, 1]).astype(BF16)
    u = _dot(h, w_ref[...]) + b_ref[...]
    o_ref[...] = u[:, :d] * _sigmoid(u[:, d:])


def _cv_glu_call(x, mod, g, w, b, *, nct):
    n, d = x.shape
    nt = n // TM
    return pl.pallas_call(
        _cv_glu_kernel,
        grid=(nt,),
        in_specs=[
            pl.BlockSpec((TM, d), lambda i: (i, 0)),
            pl.BlockSpec((1, 6, NB, d), lambda i: (_is_lat(i, nct), 0, 0, 0)),
            pl.BlockSpec((1, d), lambda i: (0, 0)),
            pl.BlockSpec((d, 2 * d), lambda i: (0, 0)),
            pl.BlockSpec((1, 2 * d), lambda i: (0, 0)),
        ],
        out_specs=pl.BlockSpec((TM, d), lambda i: (i, 0)),
        out_shape=jax.ShapeDtypeStruct((n, d), F32),
        compiler_params=_cparams(("arbitrary",)),
    )(x, mod, g, w, b)


def _cv_conv_kernel(u_ref, up_ref, un_ref, x_ref, mod_ref, dw_ref, dwb_ref, lng_ref, lnb_ref,
                    w_ref, b_ref, o_ref, pad_ref, cv_ref, *, nct):
    i = pl.program_id(0)
    halo = CONV_PAD * NB
    prev_ok = (i > 0) & (i < nct)
    next_ok = i < nct - 1
    pad_ref[0:halo] = jnp.where(prev_ok, up_ref[HALO_ROWS - halo:HALO_ROWS], 0.0)
    pad_ref[halo:halo + TM] = u_ref[...]
    pad_ref[halo + TM:halo + TM + halo] = jnp.where(next_ok, un_ref[0:halo], 0.0)

    rb = 32
    lw = 512
    d = dwb_ref.shape[1]

    def body(c, _):
        r0 = pl.multiple_of(c * rb, rb)
        for l0 in range(0, d, lw):
            bias = jnp.broadcast_to(dwb_ref[:, l0:l0 + lw], (NB, lw))
            acc = [bias] * (rb // NB)
            for k in range(CONV_WIDTH):
                w = dw_ref[k, :, l0:l0 + lw]
                for g in range(rb // NB):
                    acc[g] = acc[g] + w * pad_ref[pl.ds(r0 + (k + g) * NB, NB), l0:l0 + lw]
            for g in range(rb // NB):
                cv_ref[pl.ds(r0 + g * NB, NB), l0:l0 + lw] = acc[g]
        return 0

    lax.fori_loop(0, TM // rb, body, 0)

    c = cv_ref[...]
    mu = jnp.mean(c, axis=-1, keepdims=True)
    cc = c - mu
    var = jnp.mean(cc * cc, axis=-1, keepdims=True)
    y = cc * lax.rsqrt(var + LN_EPS) * lng_ref[...] + lnb_ref[...]
    y = (y * _sigmoid(y)).astype(BF16)
    out = _dot(y, w_ref[...]) + b_ref[...]
    o_ref[...] = _residual(x_ref[...], mod_ref[0, 2], out)


def _cv_conv_call(u, x, mod, dw_w, dw_b, ln_g, ln_b, w, b, *, nct):
    n, d = x.shape
    nt = n // TM
    hb = TM // HALO_ROWS
    nhb = n // HALO_ROWS
    kern = functools.partial(_cv_conv_kernel, nct=nct)
    row = lambda i: (0, 0)
    return pl.pallas_call(
        kern,
        grid=(nt,),
        in_specs=[
            pl.BlockSpec((TM, d), lambda i: (i, 0)),
            pl.BlockSpec((HALO_ROWS, d), lambda i: (jnp.maximum(i * hb - 1, 0), 0)),
            pl.BlockSpec((HALO_ROWS, d), lambda i: (jnp.minimum((i + 1) * hb, nhb - 1), 0)),
            pl.BlockSpec((TM, d), lambda i: (i, 0)),
            pl.BlockSpec((1, 6, NB, d), lambda i: (_is_lat(i, nct), 0, 0, 0)),
            pl.BlockSpec((CONV_WIDTH, NB, d), lambda i: (0, 0, 0)),
            pl.BlockSpec((1, d), row),
            pl.BlockSpec((1, d), row),
            pl.BlockSpec((1, d), row),
            pl.BlockSpec((d, d), row),
            pl.BlockSpec((1, d), row),
        ],
        out_specs=pl.BlockSpec((TM, d), lambda i: (i, 0)),
        out_shape=jax.ShapeDtypeStruct((n, d), F32),
        scratch_shapes=[
            pltpu.VMEM((TM + 2 * CONV_PAD * NB, d), F32),
            pltpu.VMEM((TM, d), F32),
        ],
        compiler_params=_cparams(("arbitrary",)),
    )(u, u, u, x, mod, jnp.broadcast_to(dw_w[:, None, :], (CONV_WIDTH, NB, d)), dw_b, ln_g, ln_b, w, b)


def _head_ones():
    w = 2 * LANES
    r = lax.broadcasted_iota(jnp.int32, (w, w), 0) // RWKV_HEAD
    c = lax.broadcasted_iota(jnp.int32, (w, w), 1) // RWKV_HEAD
    return jnp.where(r == c, 1.0, 0.0).astype(BF16)


def _head_sum_bcast(x, ones_bd):
    hi = x.astype(BF16)
    lo = (x - hi.astype(F32)).astype(BF16)
    w = ones_bd.shape[0]
    cols = [_dot(hi[:, c0:c0 + w], ones_bd) + _dot(lo[:, c0:c0 + w], ones_bd)
            for c0 in range(0, x.shape[1], w)]
    return jnp.concatenate(cols, axis=1)


def _rw_proj_kernel(x_ref, xp_ref, xn_ref, mod_ref, g_ref, mu_ref, wr_ref, wk_ref, wv_ref,
                    w1_ref, w2_ref, w0_ref, a1_ref, a2_ref, a0_ref, g1_ref, g2_ref, kkw_ref,
                    ka_ref, rk_ref,
                    r_ref, k_ref, v_ref, kk_ref, ew_ref, as_ref, gg_ref, bg_ref, hp_ref, *, nct, nt):
    i = pl.program_id(0)
    tm, d = x_ref.shape
    g = g_ref[...]
    sh = mod_ref[0, 0]
    sc = mod_ref[0, 1]
    h = _norm_mod(x_ref[...], g, sh, sc)
    hp = _norm_mod(xp_ref[...], g, sh, sc)
    hn = _norm_mod(xn_ref[...], g, sh, sc)
    prev_ok = (i != 0) & (i != nct)
    next_ok = (i != nct - 1) & (i != nt - 1)
    hp_ref[0:NB] = jnp.where(prev_ok, hp, 0.0)
    hp_ref[NB:NB + tm] = h
    hp_ref[NB + tm:NB + tm + NB] = jnp.where(next_ok, hn, 0.0)
    xx = 0.5 * (hp_ref[0:tm] + hp_ref[2 * NB:2 * NB + tm]) - h

    def mix(j):
        return (h + xx * mu_ref[j]).astype(BF16)

    r = _dot(mix(0), wr_ref[...])
    k = _dot(mix(2), wk_ref[...])
    v = _dot(mix(3), wv_ref[...])
    zw = w0_ref[...] + _dot(jnp.tanh(_dot(mix(1), w1_ref[...])).astype(BF16), w2_ref[...])
    ew = _sigmoid(zw) * math.exp(-0.5)
    asg = _sigmoid(a0_ref[...] + _dot(_dot(mix(4), a1_ref[...]).astype(BF16), a2_ref[...]))
    gg = _dot(_sigmoid(_dot(mix(5), g1_ref[...])).astype(BF16), g2_ref[...])
    gg_ref[...] = gg

    ones_bd = _head_ones()
    kk = k * kkw_ref[...]
    kk = kk * lax.rsqrt(_head_sum_bcast(kk * kk, ones_bd) + KK_EPS)
    k_mean = k * (1.0 + (0.5 * (asg[:, :d] + asg[:, d:]) - 1.0) * ka_ref[...])
    bg_ref[...] = _head_sum_bcast(r * k_mean * rk_ref[...], ones_bd) * v * gg

    for p in range(d // LANES):
        sl = slice(p * LANES, (p + 1) * LANES)
        r_ref[p] = r[:, sl]
        k_ref[p] = k[:, sl]
        v_ref[p] = v[:, sl]
        kk_ref[p] = kk[:, sl]
        for dr in range(2):
            sl2 = slice(dr * d + p * LANES, dr * d + (p + 1) * LANES)
            ew_ref[dr, p] = ew[:, sl2]
            as_ref[dr, p] = asg[:, sl2]


def _rw_proj_call(x, mod, g, mu, wr, wk, wv, w1, w2, w0, a1, a2, a0, g1, g2, kkw, ka, rk, *, nct):
    n, d = x.shape
    tm = RW_TM
    nt = n // tm
    nct = nct * (TM // tm)
    npair = d // LANES
    tb = tm // NB
    nb8 = n // NB
    kern = functools.partial(_rw_proj_kernel, nct=nct, nt=nt)
    full2 = lambda a: pl.BlockSpec(a.shape, lambda i: (0, 0))
    slab = pl.BlockSpec((npair, tm, LANES), lambda i: (0, i, 0))
    slab2 = pl.BlockSpec((2, npair, tm, LANES), lambda i: (0, 0, i, 0))
    slab_shape = jax.ShapeDtypeStruct((npair, n, LANES), F32)
    slab2_shape = jax.ShapeDtypeStruct((2, npair, n, LANES), F32)
    mu3 = mu.reshape(6, 1, d)
    return pl.pallas_call(
        kern,
        grid=(nt,),
        in_specs=[
            pl.BlockSpec((tm, d), lambda i: (i, 0)),
            pl.BlockSpec((NB, d), lambda i: (jnp.maximum(i * tb - 1, 0), 0)),
            pl.BlockSpec((NB, d), lambda i: (jnp.minimum((i + 1) * tb, nb8 - 1), 0)),
            pl.BlockSpec((1, 6, NB, d), lambda i: (_is_lat(i, nct), 0, 0, 0)),
            pl.BlockSpec((1, d), lambda i: (0, 0)),
            pl.BlockSpec((6, 1, d), lambda i: (0, 0, 0)),
            full2(wr), full2(wk), full2(wv),
            full2(w1), full2(w2), full2(w0), full2(a1), full2(a2), full2(a0),
            full2(g1), full2(g2), full2(kkw), full2(ka), full2(rk),
        ],
        out_specs=[slab, slab, slab, slab, slab2, slab2, pl.BlockSpec((tm, d), lambda i: (i, 0)),
                   pl.BlockSpec((tm, d), lambda i: (i, 0))],
        out_shape=[slab_shape, slab_shape, slab_shape, slab_shape, slab2_shape, slab2_shape,
                   jax.ShapeDtypeStruct((n, d), F32), jax.ShapeDtypeStruct((n, d), F32)],
        scratch_shapes=[pltpu.VMEM((tm + 2 * NB, d), F32)],
        compiler_params=_cparams(("arbitrary",)),
    )(x, x, x, mod, g, mu3, wr, wk, wv, w1, w2, w0, a1, a2, a0, g1, g2, kkw, ka, rk)


def _wkv_kernel(r_ref, k_ref, v_ref, kk_ref, ew_ref, as_ref, ka_ref, y_ref, st_ref, *, L, npair):
    d = pl.program_id(0)
    c = pl.program_id(1)

    @pl.when(c == 0)
    def _():
        st_ref[...] = jnp.zeros_like(st_ref)

    sgn = 1 - 2 * d
    l2 = 2 * L
    rr = lax.broadcasted_iota(jnp.int32, (l2, l2), 0)
    cc = lax.broadcasted_iota(jnp.int32, (l2, l2), 1)
    same = (rr >= L) == (cc >= L)
    diff = (rr - cc) * sgn
    strict_f = jnp.where(same & (diff > 0), 1.0, 0.0)
    incl_f = jnp.where(same & (diff >= 0), 1.0, 0.0)
    r1 = lax.broadcasted_iota(jnp.int32, (L, L), 0)
    c1 = lax.broadcasted_iota(jnp.int32, (L, L), 1)
    mcum = jnp.where((r1 - c1) * sgn >= 0, 1.0, 0.0).astype(BF16)
    head0 = lax.broadcasted_iota(jnp.int32, (L, LANES), 1) < RWKV_HEAD
    sr_i = lax.broadcasted_iota(jnp.int32, (LANES, LANES), 0)
    sc_i = lax.broadcasted_iota(jnp.int32, (LANES, LANES), 1)
    blockdiag = (sr_i < RWKV_HEAD) == (sc_i < RWKV_HEAD)
    n_dbl = int(round(math.log2(L)))

    def stack2(x):
        return jnp.concatenate([jnp.where(head0, x, 0.0), jnp.where(head0, 0.0, x)], axis=0)

    def unstack(xs):
        return xs[0:L] + xs[L:l2]

    def load(p, b):
        rows = pl.ds(b, L, stride=NB)
        return (r_ref[p, rows, :], k_ref[p, rows, :], v_ref[p, rows, :], kk_ref[p, rows, :],
                ew_ref[0, p, rows, :], as_ref[0, p, rows, :], st_ref[p * NB + b])

    def inst(ka, r, k, v, kk, ew, asg, s0):
        ldec = -ew
        hi = ldec.astype(BF16)
        lo = (ldec - hi.astype(F32)).astype(BF16)
        cl2 = _dot(mcum, jnp.concatenate([hi, lo], axis=1))
        yield
        cl = cl2[:, 0:LANES] + cl2[:, LANES:2 * LANES]
        ce = cl - ldec
        ecl = jnp.exp(cl)
        encl = jnp.exp(-cl)
        etot = jnp.exp(jnp.sum(ldec, axis=0, keepdims=True))
        at = -kk * jnp.exp(ce)
        rt = r * ecl
        bt = kk * asg * encl
        kt = k * (1.0 + (asg - 1.0) * ka) * encl

        ats = stack2(at)
        vsb = stack2(v).astype(BF16)
        lhs = jnp.concatenate([ats, stack2(rt)], axis=0).astype(BF16)
        rhs = jnp.concatenate([stack2(bt), stack2(kt)], axis=0).astype(BF16)
        pm = _dot_nt(lhs, rhs)
        yield
        nab = pm[0:l2, 0:l2] * strict_f
        nak = pm[0:l2, l2:2 * l2] * strict_f
        mrb = (pm[l2:2 * l2, 0:l2] * incl_f).astype(BF16)
        mrk = (pm[l2:2 * l2, l2:2 * l2] * incl_f).astype(BF16)

        xs = jnp.concatenate([ats, _dot(nak.astype(BF16), vsb)], axis=1)
        mkv = _dot(mrk, vsb)
        yield
        npow = nab
        for it in range(n_dbl):
            npb = npow.astype(BF16)
            xs = xs + _dot(npb, xs.astype(BF16))
            if it + 1 < n_dbl:
                npow = _dot(npb, npb)
            yield
        mx = _dot(mrb, xs.astype(BF16))
        at2 = unstack(xs[:, 0:LANES])
        ul = unstack(xs[:, LANES:2 * LANES])
        btb = bt.astype(BF16)
        emat = (jnp.where(blockdiag, _dot(at2.T.astype(BF16), btb), 0.0) * etot).astype(BF16)
        uv = jnp.concatenate([ul, v], axis=0)
        bk = jnp.concatenate([btb, kt.astype(BF16)], axis=0)
        dmat = jnp.where(blockdiag, _dot(uv.T.astype(BF16), bk), 0.0) * etot
        yield
        rt2 = rt + unstack(mx[:, 0:LANES])
        yl = unstack(mx[:, LANES:2 * LANES] + mkv)

        s0b = s0.astype(BF16)
        return _dot_nt(rt2.astype(BF16), s0b) + yl, s0 * etot + _dot(s0b, emat) + dmat

    def run_group(gens):
        outs = [None] * len(gens)
        live = list(enumerate(gens))
        while live:
            nxt = []
            for i, g in live:
                try:
                    next(g)
                    nxt.append((i, g))
                except StopIteration as stop:
                    outs[i] = stop.value
            live = nxt
        return outs

    def pairs(q, carry):
        work = [(q * WKV_PAIRS + j, b) for j in range(WKV_PAIRS) for b in range(NB)]
        outs = run_group([inst(ka_ref[p], *load(p, b)) for p, b in work])
        for (p, b), (y, s1) in zip(work, outs):
            y_ref[0, p, pl.ds(b, L, stride=NB), :] = y
            st_ref[p * NB + b] = s1
        return carry

    lax.fori_loop(0, npair // WKV_PAIRS, pairs, 0)


def _wkv_call(r, k, v, kk, ew, asg, ka, *, nct_rows):
    npair, n, _ = r.shape
    L = WKV_L
    rows = L * NB
    nc = n // rows
    ncc = nct_rows // rows
    kern = functools.partial(_wkv_kernel, L=L, npair=npair)

    def chunk(dd, c):
        bwd = jnp.where(c < ncc, ncc - 1 - c, nc - 1 + ncc - c)
        return jnp.where(dd == 0, c, bwd)

    slab = pl.BlockSpec((npair, rows, LANES), lambda dd, c: (0, chunk(dd, c), 0))
    slab2 = pl.BlockSpec((1, npair, rows, LANES), lambda dd, c: (dd, 0, chunk(dd, c), 0))
    return pl.pallas_call(
        kern,
        grid=(2, nc),
        in_specs=[slab, slab, slab, slab, slab2, slab2,
                  pl.BlockSpec((npair, 1, LANES), lambda dd, c: (0, 0, 0))],
        out_specs=slab2,
        out_shape=jax.ShapeDtypeStruct((2, npair, n, LANES), F32),
        scratch_shapes=[pltpu.VMEM((npair * NB, LANES, LANES), F32)],
        compiler_params=_cparams(("arbitrary", "arbitrary")),
    )(r, k, v, kk, ew, asg, ka)


def _rw_out_kernel(x_ref, mod_ref, y_ref, gg_ref, bg_ref, lng_ref, lnb_ref, wo_ref, o_ref):
    x = x_ref[...]
    d = x.shape[1]
    y = jnp.concatenate([y_ref[0, p] + y_ref[1, p] for p in range(d // LANES)], axis=1)
    ones_bd = _head_ones()
    inv = 1.0 / RWKV_HEAD
    mean = _head_sum_bcast(y, ones_bd) * inv
    yc = y - mean
    var = _head_sum_bcast(yc * yc, ones_bd) * inv
    yn = yc * lax.rsqrt(var + GN_EPS) * lng_ref[...] + lnb_ref[...]
    z = (yn * gg_ref[...] + bg_ref[...]).astype(BF16)
    o_ref[...] = _residual(x, mod_ref[0, 2], _dot(z, wo_ref[...]))


def _rw_out_call(x, mod, y, gg, bg, lng, lnb, wo, *, nct):
    n, d = x.shape
    nt = n // TM
    npair = d // LANES
    slab2 = pl.BlockSpec((2, npair, TM, LANES), lambda i: (0, 0, i, 0))
    row = pl.BlockSpec((1, d), lambda i: (0, 0))
    return pl.pallas_call(
        _rw_out_kernel,
        grid=(nt,),
        in_specs=[
            pl.BlockSpec((TM, d), lambda i: (i, 0)),
            pl.BlockSpec((1, 6, NB, d), lambda i: (_is_lat(i, nct), 0, 0, 0)),
            slab2,
            pl.BlockSpec((TM, d), lambda i: (i, 0)),
            pl.BlockSpec((TM, d), lambda i: (i, 0)),
            row, row,
            pl.BlockSpec((d, d), lambda i: (0, 0)),
        ],
        out_specs=pl.BlockSpec((TM, d), lambda i: (i, 0)),
        out_shape=jax.ShapeDtypeStruct((n, d), F32),
        compiler_params=_cparams(("arbitrary",)),
    )(x, mod, y, gg, bg, lng, lnb, wo)


def _final_kernel(x_ref, g_ref, o_ref, slab_ref):
    x = x_ref[...]
    d = x.shape[1]
    ms = jnp.mean(x * x, axis=-1, keepdims=True)
    y = x * lax.rsqrt(ms + RMS_EPS) * g_ref[...]
    for p in range(d // LANES):
        slab_ref[p] = y[:, p * LANES:(p + 1) * LANES]
    for b in range(NB):
        for p in range(d // LANES):
            o_ref[b, :, p * LANES:(p + 1) * LANES] = slab_ref[p, pl.ds(b, TT, stride=NB), :]


def _final_call(x, g, *, tile0):
    n, d = x.shape
    nt = n // TM
    return pl.pallas_call(
        _final_kernel,
        grid=(nt - tile0,),
        in_specs=[pl.BlockSpec((TM, d), lambda i: (i + tile0, 0)),
                  pl.BlockSpec((1, d), lambda i: (0, 0))],
        out_specs=pl.BlockSpec((NB, TT, d), lambda i: (0, i, 0)),
        out_shape=jax.ShapeDtypeStruct((NB, (nt - tile0) * TT, d), F32),
        scratch_shapes=[pltpu.VMEM((d // LANES, TM, LANES), F32)],
        compiler_params=_cparams(("arbitrary",)),
    )(x, g)


def _time_major_kernel(c_ref, x_ref, o_ref, slab_ref, *, nct):
    d = o_ref.shape[1]

    def emit(src_ref):
        for b in range(NB):
            for p in range(d // LANES):
                slab_ref[p, pl.ds(b, TT, stride=NB), :] = src_ref[b, :, p * LANES:(p + 1) * LANES]
        for p in range(d // LANES):
            o_ref[:, p * LANES:(p + 1) * LANES] = slab_ref[p]

    @pl.when(pl.program_id(0) < nct)
    def _():
        emit(c_ref)

    @pl.when(pl.program_id(0) >= nct)
    def _():
        emit(x_ref)


def _time_major_call(ctx, x):
    _, lc, d = ctx.shape
    seq = x.shape[1]
    nct = lc // TT
    nt = (lc + seq) // TT
    return pl.pallas_call(
        functools.partial(_time_major_kernel, nct=nct),
        grid=(nt,),
        in_specs=[pl.BlockSpec((NB, TT, d), lambda i: (0, jnp.minimum(i, nct - 1), 0)),
                  pl.BlockSpec((NB, TT, d), lambda i: (0, jnp.maximum(i - nct, 0), 0))],
        out_specs=pl.BlockSpec((TM, d), lambda i: (i, 0)),
        out_shape=jax.ShapeDtypeStruct((nt * TM, d), F32),
        scratch_shapes=[pltpu.VMEM((d // LANES, TM, LANES), F32)],
        compiler_params=_cparams(("arbitrary",)),
    )(ctx, x)


def _row(v):
    return v.reshape(1, -1).astype(F32)


def _s5_layer(x, mod, g, p, prep, *, nct, tile0):
    bb, cc, ab = prep
    y = _s5_scan_call(x, mod, g, bb, cc, ab, nct=nct)
    return _s5_glu_call(x, mod, g, y, _row(p['d']), p['glu_w'].astype(BF16), _row(p['glu_b']),
                        nct=nct, tile0=tile0)


def _conformer_layer(x, mod, g, p, *, nct):
    u = _cv_glu_call(x, mod, g, p['pw1_w'].astype(BF16), _row(p['pw1_b']), nct=nct)
    return _cv_conv_call(u, x, mod, p['dw_w'].astype(F32), _row(p['dw_b']), _row(p['ln_g']),
                         _row(p['ln_b']), p['pw2_w'].astype(BF16), _row(p['pw2_b']), nct=nct)


def _blockdiag2(m):
    z = jnp.zeros_like(m[0])
    return jnp.concatenate([jnp.concatenate([m[0], z], axis=1),
                            jnp.concatenate([z, m[1]], axis=1)], axis=0)


def _rwkv_layer(x, mod, g, p, *, nct):
    n, d = x.shape
    npair = d // LANES
    w1 = jnp.concatenate([p['w1'][0], p['w1'][1]], axis=1).astype(BF16)
    w2 = _blockdiag2(p['w2']).astype(BF16)
    a1 = jnp.concatenate([p['a1'][0], p['a1'][1]], axis=1).astype(BF16)
    a2 = _blockdiag2(p['a2']).astype(BF16)
    gl = p['g1'].shape[1]
    glp = -(-gl // LANES) * LANES
    g1 = jnp.pad(p['g1'], ((0, 0), (0, glp - gl))).astype(BF16)
    g2 = jnp.pad(p['g2'], ((0, glp - gl), (0, 0))).astype(BF16)
    r, k, v, kk, ew, asg, gg, bg = _rw_proj_call(
        x, mod, g, p['mu'].astype(F32), p['w_r'].astype(BF16), p['w_k'].astype(BF16),
        p['w_v'].astype(BF16), w1, w2, _row(p['w0']), a1, a2, _row(p['a0']), g1, g2,
        _row(p['k_k']), _row(p['k_a']), _row(p['r_k']), nct=nct)
    ka_slab = p['k_a'].astype(F32).reshape(npair, 1, LANES)
    y = _wkv_call(r, k, v, kk, ew, asg, ka_slab, nct_rows=nct * TM)
    return _rw_out_call(x, mod, y, gg, bg, _row(p['lnx_g']), _row(p['lnx_b']),
                        p['w_o'].astype(BF16), nct=nct)


def _forward(x, c, ctx, c_ctx, ada_w, ada_b, norm_mix_g, norm_ffn_g, final_g, s5, cv, rw,
             ff_w1, ff_w3, ff_w2, moe_router, moe_w1, moe_w3, moe_w2):
    bsz, seq, d = x.shape
    lc = ctx.shape[1]
    depth = ada_w.shape[0]
    assert bsz == NB and lc % TT == 0 and seq % TT == 0
    assert TT == GRID_W, "one token tile must hold exactly one latent raster row per batch entry"
    nct = lc // TT
    t_all = lc + seq
    n = t_all * NB

    xs = _time_major_call(ctx.astype(F32), x.astype(F32))

    cvec = jnp.zeros((2 * NB, d), F32).at[:NB].set(c).at[NB].set(c_ctx)
    m_all = _ada_call(cvec, ada_w.astype(F32), ada_b.astype(F32))
    m_all = m_all.reshape(depth, 2 * NB, 6, d)
    m_lat = jnp.swapaxes(m_all[:, :NB], 1, 2)
    m_ctx = jnp.broadcast_to(m_all[:, NB][:, :, None, :], (depth, 6, NB, d))
    mods = jnp.stack([m_ctx, m_lat], axis=1)

    s5_prep = jax.vmap(_s5_params)(s5['lam_re'], s5['lam_im'], s5['log_dt'], s5['b_re'], s5['b_im'],
                                   s5['c_re'], s5['c_im'])
    ctx_rows = nct * TM
    for i in range(depth):
        last = i == depth - 1
        mod = mods[i]
        g_mix = _row(norm_mix_g[i])
        j = i // 3
        if i % 3 == 0:
            xs = _s5_layer(xs, mod, g_mix, {k_: v_[j] for k_, v_ in s5.items()},
                           tuple(a[j] for a in s5_prep), nct=nct, tile0=nct if last else 0)
            ctx_rows = 0 if last else ctx_rows
        elif i % 3 == 1:
            xs = _conformer_layer(xs, mod, g_mix, {k_: v_[j] for k_, v_ in cv.items()}, nct=nct)
        else:
            xs = _rwkv_layer(xs, mod, g_mix, {k_: v_[j] for k_, v_ in rw.items()}, nct=nct)
        fi = i // 2
        g_ffn = _row(norm_ffn_g[i])
        if i % 2 == 0:
            xs = _ffn_call(xs, mod, g_ffn, ff_w1, ff_w3, ff_w2, layer=fi, nct_rows=ctx_rows)
        else:
            fused_final = last and ctx_rows == 0
            xs = _moe_call(xs, mod, g_ffn, moe_router[fi], moe_w1, moe_w3, moe_w2, layer=fi,
                           nct_rows=ctx_rows, final_g=_row(final_g) if fused_final else None)
            if fused_final:
                return xs

    return _final_call(xs, _row(final_g), tile0=ctx_rows // TM)


def kernel(x, c, ctx, c_ctx, ada_w, ada_b, norm_mix_g, norm_ffn_g, final_g, s5_lambda_re, s5_lambda_im, s5_log_dt, s5_b_re, s5_b_im, s5_c_re, s5_c_im, s5_d, s5_glu_w, s5_glu_b, cv_pw1_w, cv_pw1_b, cv_dw_w, cv_dw_b, cv_ln_g, cv_ln_b, cv_pw2_w, cv_pw2_b, rw_mu, rw_w_r, rw_w_k, rw_w_v, rw_w_o, rw_w0, rw_w1, rw_w2, rw_a0, rw_a1, rw_a2, rw_g1, rw_g2, rw_k_k, rw_k_a, rw_r_k, rw_lnx_g, rw_lnx_b, ff_w1, ff_w3, ff_w2, moe_router, moe_w1, moe_w3, moe_w2):
    s5 = dict(lam_re=s5_lambda_re, lam_im=s5_lambda_im, log_dt=s5_log_dt, b_re=s5_b_re, b_im=s5_b_im,
              c_re=s5_c_re, c_im=s5_c_im, d=s5_d, glu_w=s5_glu_w, glu_b=s5_glu_b)
    cv = dict(pw1_w=cv_pw1_w, pw1_b=cv_pw1_b, dw_w=cv_dw_w, dw_b=cv_dw_b, ln_g=cv_ln_g, ln_b=cv_ln_b,
              pw2_w=cv_pw2_w, pw2_b=cv_pw2_b)
    rw = dict(mu=rw_mu, w_r=rw_w_r, w_k=rw_w_k, w_v=rw_w_v, w_o=rw_w_o, w0=rw_w0, w1=rw_w1, w2=rw_w2,
              a0=rw_a0, a1=rw_a1, a2=rw_a2, g1=rw_g1, g2=rw_g2, k_k=rw_k_k, k_a=rw_k_a, r_k=rw_r_k,
              lnx_g=rw_lnx_g, lnx_b=rw_lnx_b)
    return _forward(x, c, ctx, c_ctx, ada_w, ada_b, norm_mix_g, norm_ffn_g, final_g, s5, cv, rw,
                    ff_w1, ff_w3, ff_w2, moe_router, moe_w1, moe_w3, moe_w2)
```

```python
import functools
import math

import jax
import jax.numpy as jnp
from jax import lax
from jax.experimental import pallas as pl
from jax.experimental.pallas import tpu as pltpu

F32 = jnp.float32
BF16 = jnp.bfloat16

NB = 8
LANES = 128
TM = 512
TT = TM // NB
GRID_W = 64
CONV_WIDTH = 31
CONV_PAD = CONV_WIDTH // 2
HALO_ROWS = 128
S5_GROUP = 16
S5_STATE = 64
S5_JBLK = 8
RWKV_HEAD = 64
WKV_L = 64
WKV_PAIRS = 2
RW_TM = 256
FFN_TM = 1024
MOE_T = 1024
MOE_BLK = 32
MOE_SLOTS = 32
RMS_EPS = 1e-6
LN_EPS = 1e-5
GN_EPS = 64e-5
KK_EPS = 1e-12
LAMBDA_RE_MAX = -1e-4
VMEM_LIMIT = 56 * 1024 * 1024


def _cparams(sem):
    return pltpu.CompilerParams(dimension_semantics=sem, vmem_limit_bytes=VMEM_LIMIT)


def _dot(a, b):
    return jnp.dot(a, b, preferred_element_type=F32)


def _dot_nt(a, b):
    return lax.dot_general(a, b, (((1,), (1,)), ((), ())), preferred_element_type=F32)


def _dot_x3(a, b):
    a1 = a.astype(BF16)
    a2 = (a - a1.astype(F32)).astype(BF16)
    b1 = b.astype(BF16)
    b2 = (b - b1.astype(F32)).astype(BF16)
    return _dot(a1, b1) + _dot(a1, b2) + _dot(a2, b1)


def _sigmoid(x):
    return jax.nn.sigmoid(x)


def _norm_mod(x, g, shift, scale):
    rows, d = x.shape
    ms = jnp.mean(x * x, axis=-1, keepdims=True)
    y = x * lax.rsqrt(ms + RMS_EPS) * g
    y3 = y.reshape(rows // NB, NB, d)
    return (y3 * (1.0 + scale)[None] + shift[None]).reshape(rows, d)


def _residual(x, gate, y):
    rows, d = x.shape
    y3 = y.reshape(rows // NB, NB, d) * gate[None]
    return x + y3.reshape(rows, d)


def _is_lat(i, nct):
    return jnp.where(i >= nct, 1, 0).astype(jnp.int32)


def _ada_kernel(s_ref, w_ref, b_ref, o_ref):
    s = s_ref[...]
    s = s * _sigmoid(s)
    o_ref[0] = _dot(s.astype(BF16), w_ref[0].astype(BF16)) + b_ref[0]


def _ada_call(cvec, ada_w, ada_b):
    depth, d, d6 = ada_w.shape
    tn = 1536
    return pl.pallas_call(
        _ada_kernel,
        grid=(depth, d6 // tn),
        in_specs=[
            pl.BlockSpec((2 * NB, d), lambda l, j: (0, 0)),
            pl.BlockSpec((1, d, tn), lambda l, j: (l, 0, j)),
            pl.BlockSpec((1, 1, tn), lambda l, j: (l, 0, j)),
        ],
        out_specs=pl.BlockSpec((1, 2 * NB, tn), lambda l, j: (l, 0, j)),
        out_shape=jax.ShapeDtypeStruct((depth, 2 * NB, d6), F32),
        compiler_params=_cparams(("arbitrary", "arbitrary")),
    )(cvec, ada_w, ada_b.reshape(depth, 1, d6))


def _ffn_kernel(x_ref, mod_ref, g_ref, w1_ref, w3_ref, w2_ref, o_ref, hb_ref, acc_ref, *, n_k):
    k = pl.program_id(1)

    @pl.when(k == 0)
    def _():
        h = _norm_mod(x_ref[...], g_ref[...], mod_ref[0, 3], mod_ref[0, 4])
        hb_ref[...] = h.astype(BF16)
        acc_ref[...] = jnp.zeros_like(acc_ref)

    hb = hb_ref[...]
    a = _dot(hb, w1_ref[0].astype(BF16))
    b = _dot(hb, w3_ref[0].astype(BF16))
    acc_ref[...] += _dot((a * _sigmoid(a) * b).astype(BF16), w2_ref[0].astype(BF16))

    @pl.when(k == n_k - 1)
    def _():
        o_ref[...] = _residual(x_ref[...], mod_ref[0, 5], acc_ref[...])


def _ffn_call(x, mod, g, w1, w3, w2, *, layer, nct_rows):
    n, d = x.shape
    f = w1.shape[2]
    tm = FFN_TM
    tf = 512
    n_k = f // tf
    nct = nct_rows // tm
    return pl.pallas_call(
        functools.partial(_ffn_kernel, n_k=n_k),
        grid=(n // tm, n_k),
        in_specs=[
            pl.BlockSpec((tm, d), lambda i, k: (i, 0)),
            pl.BlockSpec((1, 6, NB, d), lambda i, k: (_is_lat(i, nct), 0, 0, 0)),
            pl.BlockSpec((1, d), lambda i, k: (0, 0)),
            pl.BlockSpec((1, d, tf), lambda i, k: (layer, 0, k)),
            pl.BlockSpec((1, d, tf), lambda i, k: (layer, 0, k)),
            pl.BlockSpec((1, tf, d), lambda i, k: (layer, k, 0)),
        ],
        out_specs=pl.BlockSpec((tm, d), lambda i, k: (i, 0)),
        out_shape=jax.ShapeDtypeStruct((n, d), F32),
        scratch_shapes=[pltpu.VMEM((tm, d), BF16), pltpu.VMEM((tm, d), F32)],
        compiler_params=_cparams(("arbitrary", "arbitrary")),
    )(x, mod, g, w1, w3, w2)


def _top2(logits, n_e):
    lane = lax.broadcasted_iota(jnp.int32, logits.shape, 1).astype(F32)
    neg = jnp.float32(-3.0e38)
    lg = jnp.where(lane < n_e, logits, neg)
    m1 = jnp.max(lg, axis=-1, keepdims=True)
    i1 = jnp.min(jnp.where(lg == m1, lane, 1.0e9), axis=-1, keepdims=True)
    lg2 = jnp.where(lane == i1, neg, lg)
    m2 = jnp.max(lg2, axis=-1, keepdims=True)
    i2 = jnp.min(jnp.where(lg2 == m2, lane, 1.0e9), axis=-1, keepdims=True)
    e2 = jnp.exp(m2 - m1)
    den = 1.0 + e2
    return lane, i1, i2, 1.0 / den, e2 / den


def _moe_route_kernel(x_ref, mod_ref, g_ref, router_ref, ltri_ref, z_ref, dest_ref, cnt_ref,
                      hb_ref, *, n_e, cap):
    t, d = x_ref.shape
    h = _norm_mod(x_ref[...], g_ref[...], mod_ref[0, 3], mod_ref[0, 4])
    hb_ref[...] = h.astype(BF16)
    lane, i1, i2, p1, p2 = _top2(_dot_x3(h, router_ref[...]), n_e)
    sel1 = lane == i1
    sel2 = lane == i2
    c = jnp.where(sel1, 1.0, 0.0) + jnp.where(sel2, 1.0, 0.0)
    rank = _dot(ltri_ref[...], c.astype(BF16))
    cnt = jnp.sum(c, axis=0, keepdims=True)
    seg = jnp.floor((cnt + (MOE_BLK - 1)) * (1.0 / MOE_BLK)) * MOE_BLK
    ur = lax.broadcasted_iota(jnp.int32, (LANES, LANES), 0)
    uc = lax.broadcasted_iota(jnp.int32, (LANES, LANES), 1)
    before = jnp.where(ur < uc, 1.0, 0.0).astype(BF16)
    start = _dot(jnp.broadcast_to(seg, (NB, LANES)).astype(BF16), before)[0:1]
    pos = start + rank
    d1 = jnp.sum(jnp.where(sel1, pos, 0.0), axis=-1, keepdims=True)
    d2 = jnp.sum(jnp.where(sel2, pos, 0.0), axis=-1, keepdims=True)
    packed = (jnp.where(lane == 0.0, d1, 0.0) + jnp.where(lane == 1.0, d2, 0.0)
              + jnp.where(lane == 2.0, p1, 0.0) + jnp.where(lane == 3.0, p2, 0.0))
    dest_ref[...] = packed
    cnt_ref[0] = jnp.broadcast_to(cnt, (NB, LANES))
    packed_t = packed.T
    d1r = packed_t[0:1]
    d2r = packed_t[1:2]
    p1r = packed_t[2:3]
    p2r = packed_t[3:4]
    rc = 768

    def chunk(ci, carry):
        r0 = pl.multiple_of(ci * rc, rc)
        row = (lax.broadcasted_iota(jnp.int32, (rc, t), 0) + r0).astype(F32)
        m1 = row == d1r
        m2 = row == d2r
        perm = (jnp.where(m1, 1.0, 0.0) + jnp.where(m2, 1.0, 0.0)).astype(BF16)
        zc = _dot(perm, hb_ref[...])
        gate = jnp.sum(jnp.where(m1, p1r, 0.0) + jnp.where(m2, p2r, 0.0), axis=-1, keepdims=True)
        ghi = gate.astype(BF16).astype(F32)
        glo = gate - ghi
        le = lax.broadcasted_iota(jnp.int32, (rc, LANES), 1)
        ext = jnp.where(le == 0, ghi, jnp.where(le == 1, glo, 0.0))
        z_ref[pl.ds(r0, rc), 0:d] = zc.astype(BF16)
        z_ref[pl.ds(r0, rc), d:d + LANES] = ext.astype(BF16)
        return carry

    lax.fori_loop(0, cap // rc, chunk, 0)


def _moe_experts_kernel(tin_ref, se_ref, nv_ref, *refs, n_k, d):
    z_refs = refs[:MOE_SLOTS]
    w1_ref, w3_ref, w2_ref, o_ref, xs_ref, acc_ref = refs[MOE_SLOTS:]
    s = pl.program_id(0)
    k = pl.program_id(1)
    nv = nv_ref[s]

    @pl.when((nv > 0) & (k == 0))
    def _():
        for j in range(MOE_SLOTS):
            xs_ref[j * MOE_BLK:(j + 1) * MOE_BLK] = z_refs[j][...]
        acc_ref[...] = jnp.zeros_like(acc_ref)

    def compute(nrows):
        hb = xs_ref[0:nrows, 0:d]
        ext = xs_ref[0:nrows, d:d + LANES].astype(F32)
        le = lax.broadcasted_iota(jnp.int32, ext.shape, 1)
        gate = jnp.sum(jnp.where(le < 2, ext, 0.0), axis=-1, keepdims=True)
        a = _dot(hb, w1_ref[0, 0].astype(BF16))
        b = _dot(hb, w3_ref[0, 0].astype(BF16))
        act = a * _sigmoid(a) * b * gate
        acc_ref[0:nrows] += _dot(act.astype(BF16), w2_ref[0, 0].astype(BF16))

    quarter = MOE_SLOTS // 4
    for q in range(1, 5):
        @pl.when((nv > (q - 1) * quarter) & (nv <= q * quarter))
        def _():
            compute(q * quarter * MOE_BLK)

    @pl.when(k == n_k - 1)
    def _():
        o_ref[...] = jnp.where(nv > 0, acc_ref[...], 0.0).astype(BF16)


def _moe_combine_kernel(tc_ref, *refs, nblk, cap, final):
    zb_refs = refs[:nblk]
    if final:
        dest_ref, x_ref, mod_ref, fg_ref, o_ref, zbuf_ref, slab_ref = refs[nblk:]
    else:
        dest_ref, x_ref, mod_ref, o_ref, zbuf_ref = refs[nblk:]
    t, d = x_ref.shape
    for j in range(nblk):
        zbuf_ref[j * MOE_BLK:(j + 1) * MOE_BLK] = zb_refs[j][...]
    packed = dest_ref[...]
    lane = lax.broadcasted_iota(jnp.int32, packed.shape, 1)
    d1 = jnp.sum(jnp.where(lane == 0, packed, 0.0), axis=-1, keepdims=True)
    d2 = jnp.sum(jnp.where(lane == 1, packed, 0.0), axis=-1, keepdims=True)
    kc = 768
    acc = jnp.zeros((t, d), F32)
    for c0 in range(0, cap, kc):
        col = (lax.broadcasted_iota(jnp.int32, (t, kc), 1) + c0).astype(F32)
        sel = (jnp.where(col == d1, 1.0, 0.0) + jnp.where(col == d2, 1.0, 0.0)).astype(BF16)
        acc = acc + _dot(sel, zbuf_ref[c0:c0 + kc])
    y = _residual(x_ref[...], mod_ref[0, 5], acc)
    if not final:
        o_ref[...] = y
        return
    y = y * lax.rsqrt(jnp.mean(y * y, axis=-1, keepdims=True) + RMS_EPS) * fg_ref[...]
    for p in range(d // LANES):
        slab_ref[p] = y[:, p * LANES:(p + 1) * LANES]
    for b in range(NB):
        for p in range(d // LANES):
            o_ref[b, :, p * LANES:(p + 1) * LANES] = slab_ref[p, pl.ds(b, t // NB, stride=NB), :]


def _moe_tables(cnt, *, n_tiles, n_e, nblk, n_steps):
    i32 = jnp.int32
    e_ids = jnp.arange(n_e, dtype=i32)
    t_ids = jnp.arange(n_tiles, dtype=i32)
    pc = (cnt + (MOE_BLK - 1)) // MOE_BLK
    lb = jnp.cumsum(pc, axis=1) - pc
    cum = jnp.cumsum(pc, axis=0)
    cum_ex = cum - pc
    nb_e = cum[-1]
    steps_e = (nb_e + (MOE_SLOTS - 1)) // MOE_SLOTS
    ends = jnp.cumsum(steps_e)
    step0 = ends - steps_e
    sidx = jnp.arange(n_steps, dtype=i32)
    step_e = jnp.minimum(jnp.sum((sidx[:, None] >= ends[None, :]).astype(i32), axis=1), n_e - 1)
    oh_e = (step_e[:, None] == e_ids[None, :]).astype(i32)
    by_step = lambda v: jnp.sum(oh_e * v[None, :], axis=1)
    by_step_tile = lambda m: jnp.sum(oh_e[:, None, :] * m[None, :, :], axis=2)
    rank = ((sidx - by_step(step0))[:, None] * MOE_SLOTS
            + jnp.arange(MOE_SLOTS, dtype=i32)[None, :])
    cum_s = by_step_tile(cum)
    tile = jnp.sum((cum_s[:, None, :] <= rank[:, :, None]).astype(i32), axis=2)
    nvalid = jnp.where(sidx < ends[-1], jnp.clip(by_step(nb_e) - rank[:, 0], 0, MOE_SLOTS), 0)
    ok = rank < (rank[:, :1] + nvalid[:, None])
    oh_t = (jnp.minimum(tile, n_tiles - 1)[:, :, None] == t_ids[None, None, :]).astype(i32)
    at_tile = lambda m: jnp.sum(oh_t * by_step_tile(m)[:, None, :], axis=2)
    src = jnp.sum(oh_t * t_ids, axis=2) * nblk + at_tile(lb) + rank - at_tile(cum_ex)
    tin = jnp.where(ok, src, 0).reshape(-1).astype(i32)
    lblk = jnp.arange(nblk, dtype=i32)[None, :]
    seg_end = lb + pc
    e_l = jnp.minimum(jnp.sum((seg_end[:, None, :] <= lblk[:, :, None]).astype(i32), axis=2), n_e - 1)
    oh_l = (e_l[:, :, None] == e_ids[None, None, :]).astype(i32)
    r_l = jnp.sum(oh_l * (cum_ex - lb)[:, None, :], axis=2) + lblk
    flat = (jnp.sum(oh_l * step0, axis=2) + r_l // MOE_SLOTS) * MOE_SLOTS + r_l % MOE_SLOTS
    tcomb = jnp.where(lblk < seg_end[:, -1:], flat, 0).reshape(-1).astype(i32)
    return tin, tcomb, step_e.astype(i32), nvalid.astype(i32)


def _moe_call(x, mod, g, router, w1, w3, w2, *, layer, nct_rows, final_g=None):
    n, d = x.shape
    _, n_e, _, f = w1.shape
    t = MOE_T
    assert n % t == 0 and nct_rows % t == 0
    n_tiles = n // t
    nct = nct_rows // t
    cap = 2 * t + n_e * MOE_BLK
    nblk = cap // MOE_BLK
    rows = MOE_SLOTS * MOE_BLK
    n_steps = -(-(n_tiles * nblk) // MOE_SLOTS) + n_e
    tf = 512
    n_k = f // tf
    router_p = jnp.pad(router.astype(F32), ((0, 0), (0, LANES - n_e)))
    ltri = jnp.tril(jnp.ones((t, t), F32), k=-1).astype(BF16)
    mod_spec = lambda off: pl.BlockSpec((1, 6, NB, d), lambda i, *_: (_is_lat(i, nct), 0, 0, 0))

    z, dest, cnt = pl.pallas_call(
        functools.partial(_moe_route_kernel, n_e=n_e, cap=cap),
        grid=(n_tiles,),
        in_specs=[
            pl.BlockSpec((t, d), lambda i: (i, 0)),
            mod_spec(0),
            pl.BlockSpec((1, d), lambda i: (0, 0)),
            pl.BlockSpec((d, LANES), lambda i: (0, 0)),
            pl.BlockSpec((t, t), lambda i: (0, 0)),
        ],
        out_specs=[
            pl.BlockSpec((cap, d + LANES), lambda i: (i, 0)),
            pl.BlockSpec((t, LANES), lambda i: (i, 0)),
            pl.BlockSpec((1, NB, LANES), lambda i: (i, 0, 0)),
        ],
        out_shape=[
            jax.ShapeDtypeStruct((n_tiles * cap, d + LANES), BF16),
            jax.ShapeDtypeStruct((n, LANES), F32),
            jax.ShapeDtypeStruct((n_tiles, NB, LANES), F32),
        ],
        scratch_shapes=[pltpu.VMEM((t, d), BF16)],
        compiler_params=_cparams(("arbitrary",)),
    )(x, mod, g, router_p, ltri)

    tin, tcomb, step_e, nvalid = _moe_tables(cnt[:, 0, :n_e].astype(jnp.int32), n_tiles=n_tiles,
                                            n_e=n_e, nblk=nblk, n_steps=n_steps)

    def kk(s, k, nv_r):
        return jnp.where(nv_r[s] > 0, k, n_k - 1)

    def z_spec(j):
        return pl.BlockSpec((MOE_BLK, d + LANES), lambda s, k, tin_r, se_r, nu_r: (tin_r[s * MOE_SLOTS + j], 0))

    zo = pl.pallas_call(
        functools.partial(_moe_experts_kernel, n_k=n_k, d=d),
        grid_spec=pltpu.PrefetchScalarGridSpec(
            num_scalar_prefetch=3,
            grid=(n_steps, n_k),
            in_specs=[z_spec(j) for j in range(MOE_SLOTS)] + [
                pl.BlockSpec((1, 1, d, tf), lambda s, k, tin_r, se_r, nu_r: (layer, se_r[s], 0, kk(s, k, nu_r))),
                pl.BlockSpec((1, 1, d, tf), lambda s, k, tin_r, se_r, nu_r: (layer, se_r[s], 0, kk(s, k, nu_r))),
                pl.BlockSpec((1, 1, tf, d), lambda s, k, tin_r, se_r, nu_r: (layer, se_r[s], kk(s, k, nu_r), 0)),
            ],
            out_specs=pl.BlockSpec((rows, d), lambda s, k, tin_r, se_r, nu_r: (s, 0)),
            scratch_shapes=[pltpu.VMEM((rows, d + LANES), BF16), pltpu.VMEM((rows, d), F32)],
        ),
        out_shape=jax.ShapeDtypeStruct((n_steps * rows, d), BF16),
        compiler_params=_cparams(("arbitrary", "arbitrary")),
    )(tin, step_e, nvalid, *([z] * MOE_SLOTS), w1, w3, w2)

    def zo_spec(j):
        return pl.BlockSpec((MOE_BLK, d), lambda i, tc_r: (tc_r[i * nblk + j], 0))

    final = final_g is not None
    extra_in = [pl.BlockSpec((1, d), lambda i, tc_r: (0, 0))] if final else []
    extra_args = (final_g,) if final else ()
    if final:
        out_spec = pl.BlockSpec((NB, t // NB, d), lambda i, tc_r: (0, i, 0))
        out_shape = jax.ShapeDtypeStruct((NB, n // NB, d), F32)
        scratch = [pltpu.VMEM((cap, d), BF16), pltpu.VMEM((d // LANES, t, LANES), F32)]
    else:
        out_spec = pl.BlockSpec((t, d), lambda i, tc_r: (i, 0))
        out_shape = jax.ShapeDtypeStruct((n, d), F32)
        scratch = [pltpu.VMEM((cap, d), BF16)]
    return pl.pallas_call(
        functools.partial(_moe_combine_kernel, nblk=nblk, cap=cap, final=final),
        grid_spec=pltpu.PrefetchScalarGridSpec(
            num_scalar_prefetch=1,
            grid=(n_tiles,),
            in_specs=[zo_spec(j) for j in range(nblk)] + [
                pl.BlockSpec((t, LANES), lambda i, tc_r: (i, 0)),
                pl.BlockSpec((t, d), lambda i, tc_r: (i, 0)),
                mod_spec(0),
            ] + extra_in,
            out_specs=out_spec,
            scratch_shapes=scratch,
        ),
        out_shape=out_shape,
        compiler_params=_cparams(("arbitrary",)),
    )(tcomb, *([zo] * nblk), dest, x, mod, *extra_args)


def _dir_tile(d, s, nct, nt):
    bwd = jnp.where(s < nct, nct - 1 - s, nt - 1 + nct - s)
    return jnp.where(d == 0, s, bwd)


def _s5_scan_kernel(x_ref, mod_ref, g_ref, bb_ref, cc_ref, ab_ref, y_ref,
                    hb_ref, bu0_ref, bu1_ref, bu2_ref, st_ref):
    d = pl.program_id(0)
    s = pl.program_id(1)
    half = S5_JBLK * S5_STATE
    bufs = (bu0_ref, bu1_ref, bu2_ref)

    @pl.when(s == 0)
    def _():
        st_ref[...] = jnp.zeros_like(st_ref)

    h = _norm_mod(x_ref[...], g_ref[...], mod_ref[0, 0], mod_ref[0, 1])
    hb_ref[...] = h.astype(BF16)

    def project(j):
        bufs[j % 3][...] = _dot(hb_ref[:, j * LANES:(j + 1) * LANES], bb_ref[0, j])

    def scan(j):
        buf = bufs[j % 3]
        ar = ab_ref[0, j, 0]
        ai = ab_ref[0, j, 1]
        re = st_ref[j, 0]
        im = st_ref[j, 1]
        for t in range(TT):
            tt = t + d * (TT - 1 - 2 * t)
            r0 = pl.multiple_of(tt * NB, NB)
            bre = buf[pl.ds(r0, NB), 0:half]
            bim = buf[pl.ds(r0, NB), half:2 * half]
            re, im = ar * re - ai * im + bre, ar * im + ai * re + bim
            buf[pl.ds(r0, NB), 0:half] = re
            buf[pl.ds(r0, NB), half:2 * half] = im
        st_ref[j, 0] = re
        st_ref[j, 1] = im

    def readout(j):
        y_ref[0, j] = _dot(bufs[j % 3][...].astype(BF16), cc_ref[0, j])

    for i in range(S5_JBLK + 2):
        if i < S5_JBLK:
            project(i)
        if 0 <= i - 1 < S5_JBLK:
            scan(i - 1)
        if 0 <= i - 2 < S5_JBLK:
            readout(i - 2)


def _s5_scan_call(x, mod, g, bb, cc, ab, *, nct):
    n, d = x.shape
    nt = n // TM
    half = S5_JBLK * S5_STATE
    tile = lambda dd, s: _dir_tile(dd, s, nct, nt)
    return pl.pallas_call(
        _s5_scan_kernel,
        grid=(2, nt),
        in_specs=[
            pl.BlockSpec((TM, d), lambda dd, s: (tile(dd, s), 0)),
            pl.BlockSpec((1, 6, NB, d), lambda dd, s: (_is_lat(tile(dd, s), nct), 0, 0, 0)),
            pl.BlockSpec((1, d), lambda dd, s: (0, 0)),
            pl.BlockSpec((1, S5_JBLK, LANES, 2 * half), lambda dd, s: (dd, 0, 0, 0)),
            pl.BlockSpec((1, S5_JBLK, 2 * half, LANES), lambda dd, s: (dd, 0, 0, 0)),
            pl.BlockSpec((1, S5_JBLK, 2, NB, half), lambda dd, s: (dd, 0, 0, 0, 0)),
        ],
        out_specs=pl.BlockSpec((1, S5_JBLK, TM, LANES), lambda dd, s: (dd, 0, tile(dd, s), 0)),
        out_shape=jax.ShapeDtypeStruct((2, S5_JBLK, n, LANES), F32),
        scratch_shapes=[
            pltpu.VMEM((TM, d), BF16),
            pltpu.VMEM((TM, 2 * half), F32),
            pltpu.VMEM((TM, 2 * half), F32),
            pltpu.VMEM((TM, 2 * half), F32),
            pltpu.VMEM((S5_JBLK, 2, NB, half), F32),
        ],
        compiler_params=_cparams(("arbitrary", "arbitrary")),
    )(x, mod, g, bb, cc, ab)


def _gelu_tanh(x):
    c = math.sqrt(2.0 / math.pi)
    return 0.5 * x * (1.0 + jnp.tanh(c * (x + 0.044715 * (x * x * x))))


def _s5_glu_kernel(x_ref, mod_ref, g_ref, y_ref, dsk_ref, w_ref, b_ref, o_ref):
    x = x_ref[...]
    d = x.shape[1]
    h = _norm_mod(x, g_ref[...], mod_ref[0, 0], mod_ref[0, 1])
    y = jnp.concatenate([y_ref[0, j] + y_ref[1, j] for j in range(S5_JBLK)], axis=1)
    z = _gelu_tanh(y + dsk_ref[...] * h).astype(BF16)
    zz = _dot(z, w_ref[...]) + b_ref[...]
    out = zz[:, :d] * _sigmoid(zz[:, d:])
    o_ref[...] = _residual(x, mod_ref[0, 2], out)


def _s5_glu_call(x, mod, g, y, dsk, w, b, *, nct, tile0):
    n, d = x.shape
    nt = n // TM
    return pl.pallas_call(
        _s5_glu_kernel,
        grid=(nt - tile0,),
        in_specs=[
            pl.BlockSpec((TM, d), lambda i: (i + tile0, 0)),
            pl.BlockSpec((1, 6, NB, d), lambda i: (_is_lat(i + tile0, nct), 0, 0, 0)),
            pl.BlockSpec((1, d), lambda i: (0, 0)),
            pl.BlockSpec((2, S5_JBLK, TM, LANES), lambda i: (0, 0, i + tile0, 0)),
            pl.BlockSpec((1, d), lambda i: (0, 0)),
            pl.BlockSpec((d, 2 * d), lambda i: (0, 0)),
            pl.BlockSpec((1, 2 * d), lambda i: (0, 0)),
        ],
        out_specs=pl.BlockSpec((TM, d), lambda i: (i, 0)),
        out_shape=jax.ShapeDtypeStruct((n - tile0 * TM, d), F32),
        compiler_params=_cparams(("arbitrary",)),
    )(x, mod, g, y, dsk, w, b)


def _s5_params(lam_re, lam_im, log_dt, b_re, b_im, c_re, c_im):
    lam_re = jnp.minimum(lam_re.astype(F32), LAMBDA_RE_MAX)
    lam_im = lam_im.astype(F32)
    dt = jnp.exp(log_dt.astype(F32))[..., None]
    mag = jnp.exp(lam_re * dt)
    abar_re = mag * jnp.cos(lam_im * dt)
    abar_im = mag * jnp.sin(lam_im * dt)
    inv_den = 1.0 / (lam_re * lam_re + lam_im * lam_im)
    gam_re = ((abar_re - 1.0) * lam_re + abar_im * lam_im) * inv_den
    gam_im = (abar_im * lam_re - (abar_re - 1.0) * lam_im) * inv_den
    b_re = b_re.astype(F32)
    b_im = b_im.astype(F32)
    bb_re = gam_re[..., None] * b_re - gam_im[..., None] * b_im
    bb_im = gam_re[..., None] * b_im + gam_im[..., None] * b_re
    g_all = lam_re.shape[1]
    nj = g_all // S5_JBLK
    eye = jnp.eye(S5_JBLK, dtype=F32)

    def in_mat(m):
        m = m.reshape(2, nj, S5_JBLK, S5_STATE, S5_GROUP)
        m = jnp.einsum('djgnp,gh->djgphn', m, eye)
        return m.reshape(2, nj, S5_JBLK * S5_GROUP, S5_JBLK * S5_STATE)

    def out_mat(m):
        m = m.reshape(2, nj, S5_JBLK, S5_GROUP, S5_STATE)
        m = jnp.einsum('djgpn,gh->djgnhp', m, eye)
        return m.reshape(2, nj, S5_JBLK * S5_STATE, S5_JBLK * S5_GROUP)

    bb = jnp.concatenate([in_mat(bb_re), in_mat(bb_im)], axis=-1).astype(BF16)
    cc = jnp.concatenate([out_mat(c_re.astype(F32)), -out_mat(c_im.astype(F32))], axis=-2).astype(BF16)
    half = S5_JBLK * S5_STATE
    ab = jnp.stack([abar_re.reshape(2, nj, half), abar_im.reshape(2, nj, half)], axis=2)
    ab = jnp.broadcast_to(ab[:, :, :, None, :], (2, nj, 2, NB, half))
    return bb, cc, ab


def _cv_glu_kernel(x_ref, mod_ref, g_ref, w_ref, b_ref, o_ref):
    x = x_ref[...]
    d = x.shape[1]
    h = _norm_mod(x, g_ref[...], mod_ref[0, 0], mod_ref[0, 1]).astype(BF16)
    u = _dot(h, w_ref[...]) + b_ref[...]
    o_ref[...] = u[:, :d] * _sigmoid(u[:, d:])


def _cv_glu_call(x, mod, g, w, b, *, nct):
    n, d = x.shape
    nt = n // TM
    return pl.pallas_call(
        _cv_glu_kernel,
        grid=(nt,),
        in_specs=[
            pl.BlockSpec((TM, d), lambda i: (i, 0)),
            pl.BlockSpec((1, 6, NB, d), lambda i: (_is_lat(i, nct), 0, 0, 0)),
            pl.BlockSpec((1, d), lambda i: (0, 0)),
            pl.BlockSpec((d, 2 * d), lambda i: (0, 0)),
            pl.BlockSpec((1, 2 * d), lambda i: (0, 0)),
        ],
        out_specs=pl.BlockSpec((TM, d), lambda i: (i, 0)),
        out_shape=jax.ShapeDtypeStruct((n, d), F32),
        compiler_params=_cparams(("arbitrary",)),
    )(x, mod, g, w, b)


def _cv_conv_kernel(u_ref, up_ref, un_ref, x_ref, mod_ref, dw_ref, dwb_ref, lng_ref, lnb_ref,
                    w_ref, b_ref, o_ref, pad_ref, cv_ref, *, nct):
    i = pl.program_id(0)
    halo = CONV_PAD * NB
    prev_ok = (i > 0) & (i < nct)
    next_ok = i < nct - 1
    pad_ref[0:halo] = jnp.where(prev_ok, up_ref[HALO_ROWS - halo:HALO_ROWS], 0.0)
    pad_ref[halo:halo + TM] = u_ref[...]
    pad_ref[halo + TM:halo + TM + halo] = jnp.where(next_ok, un_ref[0:halo], 0.0)

    rb = 32
    lw = 512
    d = dwb_ref.shape[1]

    def body(c, _):
        r0 = pl.multiple_of(c * rb, rb)
        for l0 in range(0, d, lw):
            bias = jnp.broadcast_to(dwb_ref[:, l0:l0 + lw], (NB, lw))
            acc = [bias] * (rb // NB)
            for k in range(CONV_WIDTH):
                w = dw_ref[k, :, l0:l0 + lw]
                for g in range(rb // NB):
                    acc[g] = acc[g] + w * pad_ref[pl.ds(r0 + (k + g) * NB, NB), l0:l0 + lw]
            for g in range(rb // NB):
                cv_ref[pl.ds(r0 + g * NB, NB), l0:l0 + lw] = acc[g]
        return 0

    lax.fori_loop(0, TM // rb, body, 0)

    c = cv_ref[...]
    mu = jnp.mean(c, axis=-1, keepdims=True)
    cc = c - mu
    var = jnp.mean(cc * cc, axis=-1, keepdims=True)
    y = cc * lax.rsqrt(var + LN_EPS) * lng_ref[...] + lnb_ref[...]
    y = (y * _sigmoid(y)).astype(BF16)
    out = _dot(y, w_ref[...]) + b_ref[...]
    o_ref[...] = _residual(x_ref[...], mod_ref[0, 2], out)


def _cv_conv_call(u, x, mod, dw_w, dw_b, ln_g, ln_b, w, b, *, nct):
    n, d = x.shape
    nt = n // TM
    hb = TM // HALO_ROWS
    nhb = n // HALO_ROWS
    kern = functools.partial(_cv_conv_kernel, nct=nct)
    row = lambda i: (0, 0)
    return pl.pallas_call(
        kern,
        grid=(nt,),
        in_specs=[
            pl.BlockSpec((TM, d), lambda i: (i, 0)),
            pl.BlockSpec((HALO_ROWS, d), lambda i: (jnp.maximum(i * hb - 1, 0), 0)),
            pl.BlockSpec((HALO_ROWS, d), lambda i: (jnp.minimum((i + 1) * hb, nhb - 1), 0)),
            pl.BlockSpec((TM, d), lambda i: (i, 0)),
            pl.BlockSpec((1, 6, NB, d), lambda i: (_is_lat(i, nct), 0, 0, 0)),
            pl.BlockSpec((CONV_WIDTH, NB, d), lambda i: (0, 0, 0)),
            pl.BlockSpec((1, d), row),
            pl.BlockSpec((1, d), row),
            pl.BlockSpec((1, d), row),
            pl.BlockSpec((d, d), row),
            pl.BlockSpec((1, d), row),
        ],
        out_specs=pl.BlockSpec((TM, d), lambda i: (i, 0)),
        out_shape=jax.ShapeDtypeStruct((n, d), F32),
        scratch_shapes=[
            pltpu.VMEM((TM + 2 * CONV_PAD * NB, d), F32),
            pltpu.VMEM((TM, d), F32),
        ],
        compiler_params=_cparams(("arbitrary",)),
    )(u, u, u, x, mod, jnp.broadcast_to(dw_w[:, None, :], (CONV_WIDTH, NB, d)), dw_b, ln_g, ln_b, w, b)


def _head_ones():
    w = 2 * LANES
    r = lax.broadcasted_iota(jnp.int32, (w, w), 0) // RWKV_HEAD
    c = lax.broadcasted_iota(jnp.int32, (w, w), 1) // RWKV_HEAD
    return jnp.where(r == c, 1.0, 0.0).astype(BF16)


def _head_sum_bcast(x, ones_bd):
    hi = x.astype(BF16)
    lo = (x - hi.astype(F32)).astype(BF16)
    w = ones_bd.shape[0]
    cols = [_dot(hi[:, c0:c0 + w], ones_bd) + _dot(lo[:, c0:c0 + w], ones_bd)
            for c0 in range(0, x.shape[1], w)]
    return jnp.concatenate(cols, axis=1)


def _rw_proj_kernel(x_ref, xp_ref, xn_ref, mod_ref, g_ref, mu_ref, wr_ref, wk_ref, wv_ref,
                    w1_ref, w2_ref, w0_ref, a1_ref, a2_ref, a0_ref, g1_ref, g2_ref, kkw_ref,
                    ka_ref, rk_ref,
                    r_ref, k_ref, v_ref, kk_ref, ew_ref, as_ref, gg_ref, bg_ref, hp_ref, *, nct, nt):
    i = pl.program_id(0)
    tm, d = x_ref.shape
    g = g_ref[...]
    sh = mod_ref[0, 0]
    sc = mod_ref[0, 1]
    h = _norm_mod(x_ref[...], g, sh, sc)
    hp = _norm_mod(xp_ref[...], g, sh, sc)
    hn = _norm_mod(xn_ref[...], g, sh, sc)
    prev_ok = (i != 0) & (i != nct)
    next_ok = (i != nct - 1) & (i != nt - 1)
    hp_ref[0:NB] = jnp.where(prev_ok, hp, 0.0)
    hp_ref[NB:NB + tm] = h
    hp_ref[NB + tm:NB + tm + NB] = jnp.where(next_ok, hn, 0.0)
    xx = 0.5 * (hp_ref[0:tm] + hp_ref[2 * NB:2 * NB + tm]) - h

    def mix(j):
        return (h + xx * mu_ref[j]).astype(BF16)

    r = _dot(mix(0), wr_ref[...])
    k = _dot(mix(2), wk_ref[...])
    v = _dot(mix(3), wv_ref[...])
    zw = w0_ref[...] + _dot(jnp.tanh(_dot(mix(1), w1_ref[...])).astype(BF16), w2_ref[...])
    ew = _sigmoid(zw) * math.exp(-0.5)
    asg = _sigmoid(a0_ref[...] + _dot(_dot(mix(4), a1_ref[...]).astype(BF16), a2_ref[...]))
    gg = _dot(_sigmoid(_dot(mix(5), g1_ref[...])).astype(BF16), g2_ref[...])
    gg_ref[...] = gg

    ones_bd = _head_ones()
    kk = k * kkw_ref[...]
    kk = kk * lax.rsqrt(_head_sum_bcast(kk * kk, ones_bd) + KK_EPS)
    k_mean = k * (1.0 + (0.5 * (asg[:, :d] + asg[:, d:]) - 1.0) * ka_ref[...])
    bg_ref[...] = _head_sum_bcast(r * k_mean * rk_ref[...], ones_bd) * v * gg

    for p in range(d // LANES):
        sl = slice(p * LANES, (p + 1) * LANES)
        r_ref[p] = r[:, sl]
        k_ref[p] = k[:, sl]
        v_ref[p] = v[:, sl]
        kk_ref[p] = kk[:, sl]
        for dr in range(2):
            sl2 = slice(dr * d + p * LANES, dr * d + (p + 1) * LANES)
            ew_ref[dr, p] = ew[:, sl2]
            as_ref[dr, p] = asg[:, sl2]


def _rw_proj_call(x, mod, g, mu, wr, wk, wv, w1, w2, w0, a1, a2, a0, g1, g2, kkw, ka, rk, *, nct):
    n, d = x.shape
    tm = RW_TM
    nt = n // tm
    nct = nct * (TM // tm)
    npair = d // LANES
    tb = tm // NB
    nb8 = n // NB
    kern = functools.partial(_rw_proj_kernel, nct=nct, nt=nt)
    full2 = lambda a: pl.BlockSpec(a.shape, lambda i: (0, 0))
    slab = pl.BlockSpec((npair, tm, LANES), lambda i: (0, i, 0))
    slab2 = pl.BlockSpec((2, npair, tm, LANES), lambda i: (0, 0, i, 0))
    slab_shape = jax.ShapeDtypeStruct((npair, n, LANES), F32)
    slab2_shape = jax.ShapeDtypeStruct((2, npair, n, LANES), F32)
    mu3 = mu.reshape(6, 1, d)
    return pl.pallas_call(
        kern,
        grid=(nt,),
        in_specs=[
            pl.BlockSpec((tm, d), lambda i: (i, 0)),
            pl.BlockSpec((NB, d), lambda i: (jnp.maximum(i * tb - 1, 0), 0)),
            pl.BlockSpec((NB, d), lambda i: (jnp.minimum((i + 1) * tb, nb8 - 1), 0)),
            pl.BlockSpec((1, 6, NB, d), lambda i: (_is_lat(i, nct), 0, 0, 0)),
            pl.BlockSpec((1, d), lambda i: (0, 0)),
            pl.BlockSpec((6, 1, d), lambda i: (0, 0, 0)),
            full2(wr), full2(wk), full2(wv),
            full2(w1), full2(w2), full2(w0), full2(a1), full2(a2), full2(a0),
            full2(g1), full2(g2), full2(kkw), full2(ka), full2(rk),
        ],
        out_specs=[slab, slab, slab, slab, slab2, slab2, pl.BlockSpec((tm, d), lambda i: (i, 0)),
                   pl.BlockSpec((tm, d), lambda i: (i, 0))],
        out_shape=[slab_shape, slab_shape, slab_shape, slab_shape, slab2_shape, slab2_shape,
                   jax.ShapeDtypeStruct((n, d), F32), jax.ShapeDtypeStruct((n, d), F32)],
        scratch_shapes=[pltpu.VMEM((tm + 2 * NB, d), F32)],
        compiler_params=_cparams(("arbitrary",)),
    )(x, x, x, mod, g, mu3, wr, wk, wv, w1, w2, w0, a1, a2, a0, g1, g2, kkw, ka, rk)


def _wkv_kernel(r_ref, k_ref, v_ref, kk_ref, ew_ref, as_ref, ka_ref, y_ref, st_ref, *, L, npair):
    d = pl.program_id(0)
    c = pl.program_id(1)

    @pl.when(c == 0)
    def _():
        st_ref[...] = jnp.zeros_like(st_ref)

    sgn = 1 - 2 * d
    l2 = 2 * L
    rr = lax.broadcasted_iota(jnp.int32, (l2, l2), 0)
    cc = lax.broadcasted_iota(jnp.int32, (l2, l2), 1)
    same = (rr >= L) == (cc >= L)
    diff = (rr - cc) * sgn
    strict_f = jnp.where(same & (diff > 0), 1.0, 0.0)
    incl_f = jnp.where(same & (diff >= 0), 1.0, 0.0)
    r1 = lax.broadcasted_iota(jnp.int32, (L, L), 0)
    c1 = lax.broadcasted_iota(jnp.int32, (L, L), 1)
    mcum = jnp.where((r1 - c1) * sgn >= 0, 1.0, 0.0).astype(BF16)
    head0 = lax.broadcasted_iota(jnp.int32, (L, LANES), 1) < RWKV_HEAD
    sr_i = lax.broadcasted_iota(jnp.int32, (LANES, LANES), 0)
    sc_i = lax.broadcasted_iota(jnp.int32, (LANES, LANES), 1)
    blockdiag = (sr_i < RWKV_HEAD) == (sc_i < RWKV_HEAD)
    n_dbl = int(round(math.log2(L)))

    def stack2(x):
        return jnp.concatenate([jnp.where(head0, x, 0.0), jnp.where(head0, 0.0, x)], axis=0)

    def unstack(xs):
        return xs[0:L] + xs[L:l2]

    def load(p, b):
        rows = pl.ds(b, L, stride=NB)
        return (r_ref[p, rows, :], k_ref[p, rows, :], v_ref[p, rows, :], kk_ref[p, rows, :],
                ew_ref[0, p, rows, :], as_ref[0, p, rows, :], st_ref[p * NB + b])

    def inst(ka, r, k, v, kk, ew, asg, s0):
        ldec = -ew
        hi = ldec.astype(BF16)
        lo = (ldec - hi.astype(F32)).astype(BF16)
        cl2 = _dot(mcum, jnp.concatenate([hi, lo], axis=1))
        yield
        cl = cl2[:, 0:LANES] + cl2[:, LANES:2 * LANES]
        ce = cl - ldec
        ecl = jnp.exp(cl)
        encl = jnp.exp(-cl)
        etot = jnp.exp(jnp.sum(ldec, axis=0, keepdims=True))
        at = -kk * jnp.exp(ce)
        rt = r * ecl
        bt = kk * asg * encl
        kt = k * (1.0 + (asg - 1.0) * ka) * encl

        ats = stack2(at)
        vsb = stack2(v).astype(BF16)
        lhs = jnp.concatenate([ats, stack2(rt)], axis=0).astype(BF16)
        rhs = jnp.concatenate([stack2(bt), stack2(kt)], axis=0).astype(BF16)
        pm = _dot_nt(lhs, rhs)
        yield
        nab = pm[0:l2, 0:l2] * strict_f
        nak = pm[0:l2, l2:2 * l2] * strict_f
        mrb = (pm[l2:2 * l2, 0:l2] * incl_f).astype(BF16)
        mrk = (pm[l2:2 * l2, l2:2 * l2] * incl_f).astype(BF16)

        xs = jnp.concatenate([ats, _dot(nak.astype(BF16), vsb)], axis=1)
        mkv = _dot(mrk, vsb)
        yield
        npow = nab
        for it in range(n_dbl):
            npb = npow.astype(BF16)
            xs = xs + _dot(npb, xs.astype(BF16))
            if it + 1 < n_dbl:
                npow = _dot(npb, npb)
            yield
        mx = _dot(mrb, xs.astype(BF16))
        at2 = unstack(xs[:, 0:LANES])
        ul = unstack(xs[:, LANES:2 * LANES])
        btb = bt.astype(BF16)
        emat = (jnp.where(blockdiag, _dot(at2.T.astype(BF16), btb), 0.0) * etot).astype(BF16)
        uv = jnp.concatenate([ul, v], axis=0)
        bk = jnp.concatenate([btb, kt.astype(BF16)], axis=0)
        dmat = jnp.where(blockdiag, _dot(uv.T.astype(BF16), bk), 0.0) * etot
        yield
        rt2 = rt + unstack(mx[:, 0:LANES])
        yl = unstack(mx[:, LANES:2 * LANES] + mkv)

        s0b = s0.astype(BF16)
        return _dot_nt(rt2.astype(BF16), s0b) + yl, s0 * etot + _dot(s0b, emat) + dmat

    def run_group(gens):
        outs = [None] * len(gens)
        live = list(enumerate(gens))
        while live:
            nxt = []
            for i, g in live:
                try:
                    next(g)
                    nxt.append((i, g))
                except StopIteration as stop:
                    outs[i] = stop.value
            live = nxt
        return outs

    def pairs(q, carry):
        work = [(q * WKV_PAIRS + j, b) for j in range(WKV_PAIRS) for b in range(NB)]
        outs = run_group([inst(ka_ref[p], *load(p, b)) for p, b in work])
        for (p, b), (y, s1) in zip(work, outs):
            y_ref[0, p, pl.ds(b, L, stride=NB), :] = y
            st_ref[p * NB + b] = s1
        return carry

    lax.fori_loop(0, npair // WKV_PAIRS, pairs, 0)


def _wkv_call(r, k, v, kk, ew, asg, ka, *, nct_rows):
    npair, n, _ = r.shape
    L = WKV_L
    rows = L * NB
    nc = n // rows
    ncc = nct_rows // rows
    kern = functools.partial(_wkv_kernel, L=L, npair=npair)

    def chunk(dd, c):
        bwd = jnp.where(c < ncc, ncc - 1 - c, nc - 1 + ncc - c)
        return jnp.where(dd == 0, c, bwd)

    slab = pl.BlockSpec((npair, rows, LANES), lambda dd, c: (0, chunk(dd, c), 0))
    slab2 = pl.BlockSpec((1, npair, rows, LANES), lambda dd, c: (dd, 0, chunk(dd, c), 0))
    return pl.pallas_call(
        kern,
        grid=(2, nc),
        in_specs=[slab, slab, slab, slab, slab2, slab2,
                  pl.BlockSpec((npair, 1, LANES), lambda dd, c: (0, 0, 0))],
        out_specs=slab2,
        out_shape=jax.ShapeDtypeStruct((2, npair, n, LANES), F32),
        scratch_shapes=[pltpu.VMEM((npair * NB, LANES, LANES), F32)],
        compiler_params=_cparams(("arbitrary", "arbitrary")),
    )(r, k, v, kk, ew, asg, ka)


def _rw_out_kernel(x_ref, mod_ref, y_ref, gg_ref, bg_ref, lng_ref, lnb_ref, wo_ref, o_ref):
    x = x_ref[...]
    d = x.shape[1]
    y = jnp.concatenate([y_ref[0, p] + y_ref[1, p] for p in range(d // LANES)], axis=1)
    ones_bd = _head_ones()
    inv = 1.0 / RWKV_HEAD
    mean = _head_sum_bcast(y, ones_bd) * inv
    yc = y - mean
    var = _head_sum_bcast(yc * yc, ones_bd) * inv
    yn = yc * lax.rsqrt(var + GN_EPS) * lng_ref[...] + lnb_ref[...]
    z = (yn * gg_ref[...] + bg_ref[...]).astype(BF16)
    o_ref[...] = _residual(x, mod_ref[0, 2], _dot(z, wo_ref[...]))


def _rw_out_call(x, mod, y, gg, bg, lng, lnb, wo, *, nct):
    n, d = x.shape
    nt = n // TM
    npair = d // LANES
    slab2 = pl.BlockSpec((2, npair, TM, LANES), lambda i: (0, 0, i, 0))
    row = pl.BlockSpec((1, d), lambda i: (0, 0))
    return pl.pallas_call(
        _rw_out_kernel,
        grid=(nt,),
        in_specs=[
            pl.BlockSpec((TM, d), lambda i: (i, 0)),
            pl.BlockSpec((1, 6, NB, d), lambda i: (_is_lat(i, nct), 0, 0, 0)),
            slab2,
            pl.BlockSpec((TM, d), lambda i: (i, 0)),
            pl.BlockSpec((TM, d), lambda i: (i, 0)),
            row, row,
            pl.BlockSpec((d, d), lambda i: (0, 0)),
        ],
        out_specs=pl.BlockSpec((TM, d), lambda i: (i, 0)),
        out_shape=jax.ShapeDtypeStruct((n, d), F32),
        compiler_params=_cparams(("arbitrary",)),
    )(x, mod, y, gg, bg, lng, lnb, wo)


def _final_kernel(x_ref, g_ref, o_ref, slab_ref):
    x = x_ref[...]
    d = x.shape[1]
    ms = jnp.mean(x * x, axis=-1, keepdims=True)
    y = x * lax.rsqrt(ms + RMS_EPS) * g_ref[...]
    for p in range(d // LANES):
        slab_ref[p] = y[:, p * LANES:(p + 1) * LANES]
    for b in range(NB):
        for p in range(d // LANES):
            o_ref[b, :, p * LANES:(p + 1) * LANES] = slab_ref[p, pl.ds(b, TT, stride=NB), :]


def _final_call(x, g, *, tile0):
    n, d = x.shape
    nt = n // TM
    return pl.pallas_call(
        _final_kernel,
        grid=(nt - tile0,),
        in_specs=[pl.BlockSpec((TM, d), lambda i: (i + tile0, 0)),
                  pl.BlockSpec((1, d), lambda i: (0, 0))],
        out_specs=pl.BlockSpec((NB, TT, d), lambda i: (0, i, 0)),
        out_shape=jax.ShapeDtypeStruct((NB, (nt - tile0) * TT, d), F32),
        scratch_shapes=[pltpu.VMEM((d // LANES, TM, LANES), F32)],
        compiler_params=_cparams(("arbitrary",)),
    )(x, g)


def _time_major_kernel(c_ref, x_ref, o_ref, slab_ref, *, nct):
    d = o_ref.shape[1]

    def emit(src_ref):
        for b in range(NB):
            for p in range(d // LANES):
                slab_ref[p, pl.ds(b, TT, stride=NB), :] = src_ref[b, :, p * LANES:(p + 1) * LANES]
        for p in range(d // LANES):
            o_ref[:, p * LANES:(p + 1) * LANES] = slab_ref[p]

    @pl.when(pl.program_id(0) < nct)
    def _():
        emit(c_ref)

    @pl.when(pl.program_id(0) >= nct)
    def _():
        emit(x_ref)


def _time_major_call(ctx, x):
    _, lc, d = ctx.shape
    seq = x.shape[1]
    nct = lc // TT
    nt = (lc + seq) // TT
    return pl.pallas_call(
        functools.partial(_time_major_kernel, nct=nct),
        grid=(nt,),
        in_specs=[pl.BlockSpec((NB, TT, d), lambda i: (0, jnp.minimum(i, nct - 1), 0)),
                  pl.BlockSpec((NB, TT, d), lambda i: (0, jnp.maximum(i - nct, 0), 0))],
        out_specs=pl.BlockSpec((TM, d), lambda i: (i, 0)),
        out_shape=jax.ShapeDtypeStruct((nt * TM, d), F32),
        scratch_shapes=[pltpu.VMEM((d // LANES, TM, LANES), F32)],
        compiler_params=_cparams(("arbitrary",)),
    )(ctx, x)


def _row(v):
    return v.reshape(1, -1).astype(F32)


def _s5_layer(x, mod, g, p, prep, *, nct, tile0):
    bb, cc, ab = prep
    y = _s5_scan_call(x, mod, g, bb, cc, ab, nct=nct)
    return _s5_glu_call(x, mod, g, y, _row(p['d']), p['glu_w'].astype(BF16), _row(p['glu_b']),
                        nct=nct, tile0=tile0)


def _conformer_layer(x, mod, g, p, *, nct):
    u = _cv_glu_call(x, mod, g, p['pw1_w'].astype(BF16), _row(p['pw1_b']), nct=nct)
    return _cv_conv_call(u, x, mod, p['dw_w'].astype(F32), _row(p['dw_b']), _row(p['ln_g']),
                         _row(p['ln_b']), p['pw2_w'].astype(BF16), _row(p['pw2_b']), nct=nct)


def _blockdiag2(m):
    z = jnp.zeros_like(m[0])
    return jnp.concatenate([jnp.concatenate([m[0], z], axis=1),
                            jnp.concatenate([z, m[1]], axis=1)], axis=0)


def _rwkv_layer(x, mod, g, p, *, nct):
    n, d = x.shape
    npair = d // LANES
    w1 = jnp.concatenate([p['w1'][0], p['w1'][1]], axis=1).astype(BF16)
    w2 = _blockdiag2(p['w2']).astype(BF16)
    a1 = jnp.concatenate([p['a1'][0], p['a1'][1]], axis=1).astype(BF16)
    a2 = _blockdiag2(p['a2']).astype(BF16)
    gl = p['g1'].shape[1]
    glp = -(-gl // LANES) * LANES
    g1 = jnp.pad(p['g1'], ((0, 0), (0, glp - gl))).astype(BF16)
    g2 = jnp.pad(p['g2'], ((0, glp - gl), (0, 0))).astype(BF16)
    r, k, v, kk, ew, asg, gg, bg = _rw_proj_call(
        x, mod, g, p['mu'].astype(F32), p['w_r'].astype(BF16), p['w_k'].astype(BF16),
        p['w_v'].astype(BF16), w1, w2, _row(p['w0']), a1, a2, _row(p['a0']), g1, g2,
        _row(p['k_k']), _row(p['k_a']), _row(p['r_k']), nct=nct)
    ka_slab = p['k_a'].astype(F32).reshape(npair, 1, LANES)
    y = _wkv_call(r, k, v, kk, ew, asg, ka_slab, nct_rows=nct * TM)
    return _rw_out_call(x, mod, y, gg, bg, _row(p['lnx_g']), _row(p['lnx_b']),
                        p['w_o'].astype(BF16), nct=nct)


def _forward(x, c, ctx, c_ctx, ada_w, ada_b, norm_mix_g, norm_ffn_g, final_g, s5, cv, rw,
             ff_w1, ff_w3, ff_w2, moe_router, moe_w1, moe_w3, moe_w2):
    bsz, seq, d = x.shape
    lc = ctx.shape[1]
    depth = ada_w.shape[0]
    assert bsz == NB and lc % TT == 0 and seq % TT == 0
    assert TT == GRID_W, "one token tile must hold exactly one latent raster row per batch entry"
    nct = lc // TT
    t_all = lc + seq
    n = t_all * NB

    xs = _time_major_call(ctx.astype(F32), x.astype(F32))

    cvec = jnp.zeros((2 * NB, d), F32).at[:NB].set(c).at[NB].set(c_ctx)
    m_all = _ada_call(cvec, ada_w.astype(F32), ada_b.astype(F32))
    m_all = m_all.reshape(depth, 2 * NB, 6, d)
    m_lat = jnp.swapaxes(m_all[:, :NB], 1, 2)
    m_ctx = jnp.broadcast_to(m_all[:, NB][:, :, None, :], (depth, 6, NB, d))
    mods = jnp.stack([m_ctx, m_lat], axis=1)

    s5_prep = jax.vmap(_s5_params)(s5['lam_re'], s5['lam_im'], s5['log_dt'], s5['b_re'], s5['b_im'],
                                   s5['c_re'], s5['c_im'])
    ctx_rows = nct * TM
    for i in range(depth):
        last = i == depth - 1
        mod = mods[i]
        g_mix = _row(norm_mix_g[i])
        j = i // 3
        if i % 3 == 0:
            xs = _s5_layer(xs, mod, g_mix, {k_: v_[j] for k_, v_ in s5.items()},
                           tuple(a[j] for a in s5_prep), nct=nct, tile0=nct if last else 0)
            ctx_rows = 0 if last else ctx_rows
        elif i % 3 == 1:
            xs = _conformer_layer(xs, mod, g_mix, {k_: v_[j] for k_, v_ in cv.items()}, nct=nct)
        else:
            xs = _rwkv_layer(xs, mod, g_mix, {k_: v_[j] for k_, v_ in rw.items()}, nct=nct)
        fi = i // 2
        g_ffn = _row(norm_ffn_g[i])
        if i % 2 == 0:
            xs = _ffn_call(xs, mod, g_ffn, ff_w1, ff_w3, ff_w2, layer=fi, nct_rows=ctx_rows)
        else:
            fused_final = last and ctx_rows == 0
            xs = _moe_call(xs, mod, g_ffn, moe_router[fi], moe_w1, moe_w3, moe_w2, layer=fi,
                           nct_rows=ctx_rows, final_g=_row(final_g) if fused_final else None)
            if fused_final:
                return xs

    return _final_call(xs, _row(final_g), tile0=ctx_rows // TM)


def kernel(x, c, ctx, c_ctx, ada_w, ada_b, norm_mix_g, norm_ffn_g, final_g, s5_lambda_re, s5_lambda_im, s5_log_dt, s5_b_re, s5_b_im, s5_c_re, s5_c_im, s5_d, s5_glu_w, s5_glu_b, cv_pw1_w, cv_pw1_b, cv_dw_w, cv_dw_b, cv_ln_g, cv_ln_b, cv_pw2_w, cv_pw2_b, rw_mu, rw_w_r, rw_w_k, rw_w_v, rw_w_o, rw_w0, rw_w1, rw_w2, rw_a0, rw_a1, rw_a2, rw_g1, rw_g2, rw_k_k, rw_k_a, rw_r_k, rw_lnx_g, rw_lnx_b, ff_w1, ff_w3, ff_w2, moe_router, moe_w1, moe_w3, moe_w2):
    s5 = dict(lam_re=s5_lambda_re, lam_im=s5_lambda_im, log_dt=s5_log_dt, b_re=s5_b_re, b_im=s5_b_im,
              c_re=s5_c_re, c_im=s5_c_im, d=s5_d, glu_w=s5_glu_w, glu_b=s5_glu_b)
    cv = dict(pw1_w=cv_pw1_w, pw1_b=cv_pw1_b, dw_w=cv_dw_w, dw_b=cv_dw_b, ln_g=cv_ln_g, ln_b=cv_ln_b,
              pw2_w=cv_pw2_w, pw2_b=cv_pw2_b)
    rw = dict(mu=rw_mu, w_r=rw_w_r, w_k=rw_w_k, w_v=rw_w_v, w_o=rw_w_o, w0=rw_w0, w1=rw_w1, w2=rw_w2,
              a0=rw_a0, a1=rw_a1, a2=rw_a2, g1=rw_g1, g2=rw_g2, k_k=rw_k_k, k_a=rw_k_a, r_k=rw_r_k,
              lnx_g=rw_lnx_g, lnx_b=rw_lnx_b)
    return _forward(x, c, ctx, c_ctx, ada_w, ada_b, norm_mix_g, norm_ffn_g, final_g, s5, cv, rw,
                    ff_w1, ff_w3, ff_w2, moe_router, moe_w1, moe_w3, moe_w2)
```
